```python
import math
import jax, jax.numpy as jnp
from jax import lax
import numpy as np

D_MODEL = 1024
BATCH = 4
SEQ = 4096
DEPTH = 2

CHUNK = 64
PLE_DIM = 256
RMS_EPS = 1e-6

LRU_WIDTH = D_MODEL // 2
LRU_BLOCKS = 8
LRU_BLOCK = LRU_WIDTH // LRU_BLOCKS
CONV_WIDTH = 4
LRU_C = 8.0

ATT_HEADS = 8
ATT_HEAD_DIM = 64
ATT_WIDTH = ATT_HEADS * ATT_HEAD_DIM
IDX_HEADS = 8
IDX_DIM = 64
MAX_TOP_K = 256
Q_BLOCK = 128

N_BUCKETS = 32
MAX_DISTANCE = 128

EV_SPLIT = (LRU_WIDTH, LRU_WIDTH, ATT_WIDTH, ATT_WIDTH, ATT_WIDTH, IDX_HEADS * IDX_DIM, IDX_DIM, IDX_HEADS)
EV_IN = LRU_WIDTH * 2 + ATT_WIDTH * 3 + IDX_HEADS * IDX_DIM + IDX_DIM + IDX_HEADS
EV_OUT = LRU_WIDTH + ATT_WIDTH

RET_HEADS = 8
RET_QK_DIM = D_MODEL // RET_HEADS
RET_V_DIM = 2 * D_MODEL // RET_HEADS
RET_QK_WIDTH = RET_HEADS * RET_QK_DIM
RET_V_WIDTH = RET_HEADS * RET_V_DIM
OD_SPLIT = (RET_QK_WIDTH, RET_QK_WIDTH, RET_V_WIDTH, RET_V_WIDTH)
OD_IN = 2 * RET_QK_WIDTH + 2 * RET_V_WIDTH
ROPE_BASE = 10000.0
GN_EPS = 1e-5

N_GROUPS = 4
EXPERTS_PER_GROUP = 8
N_EXPERTS = N_GROUPS * EXPERTS_PER_GROUP
TOP_K_EXPERTS = 2
EXPERT_FF = D_MODEL // 2

kernel_name = "hybrid_rglru_dsa_retention_hmoe"


def rms_norm(x, g):
    xf = x.astype(jnp.float32)
    y = xf * lax.rsqrt(jnp.mean(xf * xf, axis=-1, keepdims=True) + RMS_EPS)
    return (y * g.astype(jnp.float32)).astype(x.dtype)


def _split(z, sizes):
    offs = []
    acc = 0
    for s in sizes[:-1]:
        acc += s
        offs.append(acc)
    return jnp.split(z, offs, axis=-1)


def t5_bucket(rel):
    half = N_BUCKETS // 2
    max_exact = half // 2
    ret = (rel > 0).astype(jnp.int32) * half
    n = jnp.abs(rel)
    nf = jnp.maximum(n, 1).astype(jnp.float32)
    large = max_exact + (jnp.log(nf / max_exact) / math.log(MAX_DISTANCE / max_exact) * (half - max_exact)).astype(jnp.int32)
    large = jnp.minimum(large, half - 1)
    return ret + jnp.where(n < max_exact, n, large)


def causal_depthwise_conv(x, w, b):
    c = x.shape[-1]
    y = lax.conv_general_dilated(x, w[:, None, :].astype(x.dtype), window_strides=(1,), padding=[(CONV_WIDTH - 1, 0)], dimension_numbers=("NWC", "WIO", "NWC"), feature_group_count=c)
    return y + b.astype(x.dtype)


def rglru_mixer(xa, ga, conv_w, conv_b, wa, ba, wx, bx, lam):
    b_, s_, _ = xa.shape
    xc = causal_depthwise_conv(xa, conv_w, conv_b)
    xb = xc.reshape(b_, s_, LRU_BLOCKS, LRU_BLOCK)
    r = jax.nn.sigmoid((jnp.einsum("bshi,hij->bshj", xb, wa).reshape(b_, s_, LRU_WIDTH) + ba).astype(jnp.float32))
    gi = jax.nn.sigmoid((jnp.einsum("bshi,hij->bshj", xb, wx).reshape(b_, s_, LRU_WIDTH) + bx).astype(jnp.float32))
    log_a = -LRU_C * r * jax.nn.softplus(-lam.astype(jnp.float32))
    a = jnp.exp(log_a)
    u = jnp.sqrt(-jnp.expm1(2.0 * log_a)) * (gi * xc.astype(jnp.float32))

    def combine(left, right):
        a1, b1 = left
        a2, b2 = right
        return a1 * a2, a2 * b1 + b2

    _, hseq = lax.associative_scan(combine, (a, u), axis=1)
    y = hseq * jax.nn.gelu(ga.astype(jnp.float32))
    return y.astype(xa.dtype)


def dsa_mixer(q, k, v, iq, ik, iw, q_g, k_g, rel_bias):
    b_, s_ = q.shape[:2]
    top_k = min(MAX_TOP_K, s_ // 4)
    nb = s_ // Q_BLOCK
    q = rms_norm(q, q_g)
    k = rms_norm(k, k_g)
    iw = iw * (IDX_HEADS ** -0.5) * (IDX_DIM ** -0.5)
    key_chunk = jnp.arange(s_, dtype=jnp.int32) // CHUNK
    scale = ATT_HEAD_DIM ** -0.5

    def to_blocks(t):
        return jnp.moveaxis(t.reshape(b_, nb, Q_BLOCK, *t.shape[2:]), 1, 0)

    def block_fn(args):
        blk, qb, iqb, iwb = args
        qpos = blk * Q_BLOCK + jnp.arange(Q_BLOCK, dtype=jnp.int32)
        qchunk = qpos // CHUNK
        admissible = key_chunk[None, :] <= qchunk[:, None]
        dots = jnp.einsum("bqhd,bsd->bqsh", iqb, ik)
        score = jnp.einsum("bqsh,bqh->bqs", jax.nn.relu(dots).astype(jnp.float32), iwb.astype(jnp.float32))
        score = jnp.where(admissible[None], score, -jnp.inf)
        _, idx = lax.top_k(score, top_k)
        kg = jax.vmap(lambda kk, ii: kk[ii])(k, idx)
        vg = jax.vmap(lambda vv, ii: vv[ii])(v, idx)
        valid = (idx // CHUNK) <= qchunk[None, :, None]
        bias = rel_bias[t5_bucket(idx - qpos[None, :, None])]
        logits = jnp.einsum("bqhd,bqkhd->bqhk", qb, kg).astype(jnp.float32) * scale + jnp.moveaxis(bias, -1, 2).astype(jnp.float32)
        logits = jnp.where(valid[:, :, None, :], logits, -jnp.inf)
        probs = jax.nn.softmax(logits, axis=-1).astype(v.dtype)
        return jnp.einsum("bqhk,bqkhd->bqhd", probs, vg)

    out = lax.map(block_fn, (jnp.arange(nb, dtype=jnp.int32), to_blocks(q), to_blocks(iq), to_blocks(iw)))
    return jnp.moveaxis(out, 0, 1).reshape(b_, s_, ATT_WIDTH)


def apply_rotary(x):
    s_, d = x.shape[1], x.shape[-1]
    half = d // 2
    inv = ROPE_BASE ** (-jnp.arange(half, dtype=jnp.float32) / half)
    ang = jnp.arange(s_, dtype=jnp.float32)[:, None] * inv[None, :]
    cos = jnp.cos(ang)[None, :, None, :]
    sin = jnp.sin(ang)[None, :, None, :]
    x1, x2 = x[..., :half], x[..., half:]
    return jnp.concatenate([x1 * cos - x2 * sin, x2 * cos + x1 * sin], axis=-1)


def retention_mixer(q, k, v, g, gn_g):
    b_, s_, h_, dk = q.shape
    dv = v.shape[-1]
    nc = s_ // CHUNK
    qf = apply_rotary(q.astype(jnp.float32))
    kf = apply_rotary(k.astype(jnp.float32)) * (dk ** -0.5)
    vf = v.astype(jnp.float32)
    log_g = jnp.log(1.0 - 2.0 ** (-5.0 - jnp.arange(h_, dtype=jnp.float32)))
    pos = jnp.arange(CHUNK, dtype=jnp.float32)
    diff = pos[:, None] - pos[None, :]
    causal = diff >= 0
    decay_intra = jnp.where(causal[None], jnp.exp(jnp.where(causal, diff, 0.0)[None] * log_g[:, None, None]), 0.0)
    q_decay = jnp.exp((pos + 1.0)[:, None] * log_g[None, :])
    k_decay = jnp.exp((CHUNK - 1.0 - pos)[:, None] * log_g[None, :])
    chunk_decay = jnp.exp(CHUNK * log_g)

    def chunks(t):
        return jnp.moveaxis(t.reshape(b_, nc, CHUNK, *t.shape[2:]), 1, 0)

    def step(state, inp):
        qc, kc, vc = inp
        inner = jnp.einsum("bihd,bjhd->bhij", qc, kc) * decay_intra[None]
        o_inner = jnp.einsum("bhij,bjhe->bihe", inner, vc)
        o_cross = jnp.einsum("bihd,bhde->bihe", qc * q_decay[None, :, :, None], state)
        new_state = state * chunk_decay[None, :, None, None] + jnp.einsum("bjhd,bjhe->bhde", kc * k_decay[None, :, :, None], vc)
        return new_state, o_inner + o_cross

    state0 = jnp.zeros((b_, h_, dk, dv), jnp.float32)
    _, o = lax.scan(step, state0, (chunks(qf), chunks(kf), chunks(vf)))
    o = jnp.moveaxis(o, 0, 1).reshape(b_, s_, h_, dv)
    mu = jnp.mean(o, axis=-1, keepdims=True)
    var = jnp.mean(jnp.square(o - mu), axis=-1, keepdims=True)
    o = ((o - mu) * lax.rsqrt(var + GN_EPS)).reshape(b_, s_, h_ * dv) * gn_g.astype(jnp.float32)
    y = jax.nn.silu(g.astype(jnp.float32)) * o
    return y.astype(g.dtype)


def hier_moe(x, w_group, w_expert, w_gate, w_up, w_down):
    b_, s_, d = x.shape
    n = b_ * s_
    xf = x.reshape(n, d)
    group_logits = (xf @ w_group).astype(jnp.float32)
    group_probs = jax.nn.softmax(group_logits, axis=-1)
    g_sel = jnp.argmax(group_logits, axis=-1)
    g_weight = jnp.take_along_axis(group_probs, g_sel[:, None], axis=1)
    expert_logits = (xf @ w_expert).astype(jnp.float32).reshape(n, N_GROUPS, EXPERTS_PER_GROUP)
    within = jnp.take_along_axis(expert_logits, g_sel[:, None, None], axis=1)[:, 0]
    top_vals, top_idx = lax.top_k(within, TOP_K_EXPERTS)
    top_w = jax.nn.softmax(top_vals, axis=-1) * g_weight
    expert_ids = g_sel[:, None] * EXPERTS_PER_GROUP + top_idx
    combine = jnp.einsum("nke,nk->ne", jax.nn.one_hot(expert_ids, N_EXPERTS, dtype=jnp.float32), top_w)
    out = jnp.zeros((n, d), jnp.float32)
    for e in range(N_EXPERTS):
        hid = jax.nn.silu(xf @ w_gate[e]) * (xf @ w_up[e])
        out = out + combine[:, e:e + 1] * (hid @ w_down[e]).astype(jnp.float32)
    return out.reshape(b_, s_, d).astype(x.dtype)


def setup_inputs(seed: int = 0) -> dict:
    key = jax.random.key(seed)
    ks = iter(jax.random.split(key, 40))
    ne = (DEPTH + 1) // 2
    no = DEPTH // 2
    f32 = jnp.float32

    def nrm(shape, fan_in):
        return jax.random.normal(next(ks), shape, f32) * (fan_in ** -0.5)

    def gain(shape):
        return 1.0 + 0.05 * jax.random.normal(next(ks), shape, f32)

    def small(shape, s=0.02):
        return s * jax.random.normal(next(ks), shape, f32)

    u = jax.random.uniform(next(ks), (ne, LRU_WIDTH), f32, 0.9, 0.999)
    a0 = u ** (1.0 / LRU_C)
    lam = jnp.log(a0) - jnp.log1p(-a0)
    return {
        "x": jax.random.normal(next(ks), (BATCH, SEQ, D_MODEL), f32),
        "p": jax.random.normal(next(ks), (DEPTH, BATCH, SEQ, PLE_DIM), f32),
        "rel_bias": small((N_BUCKETS, ATT_HEADS), 0.2),
        "mix_norm_g": gain((DEPTH, D_MODEL)),
        "ffn_norm_g": gain((DEPTH, D_MODEL)),
        "ple_norm_g": gain((DEPTH, D_MODEL)),
        "ev_w_in": nrm((ne, D_MODEL, EV_IN), D_MODEL),
        "ev_conv_w": nrm((ne, CONV_WIDTH, LRU_WIDTH), CONV_WIDTH),
        "ev_conv_b": small((ne, LRU_WIDTH)),
        "ev_lru_wa": nrm((ne, LRU_BLOCKS, LRU_BLOCK, LRU_BLOCK), LRU_BLOCK),
        "ev_lru_ba": small((ne, LRU_WIDTH)),
        "ev_lru_wx": nrm((ne, LRU_BLOCKS, LRU_BLOCK, LRU_BLOCK), LRU_BLOCK),
        "ev_lru_bx": small((ne, LRU_WIDTH)),
        "ev_lru_lambda": lam,
        "ev_q_norm_g": gain((ne, ATT_HEAD_DIM)),
        "ev_k_norm_g": gain((ne, ATT_HEAD_DIM)),
        "ev_w_out": nrm((ne, EV_OUT, D_MODEL), EV_OUT),
        "od_w_in": nrm((no, D_MODEL, OD_IN), D_MODEL),
        "od_gn_g": gain((no, RET_V_WIDTH)),
        "od_w_out": nrm((no, RET_V_WIDTH, D_MODEL), RET_V_WIDTH),
        "moe_w_group": nrm((DEPTH, D_MODEL, N_GROUPS), D_MODEL),
        "moe_w_expert": nrm((DEPTH, D_MODEL, N_EXPERTS), D_MODEL),
        "moe_w_gate": nrm((DEPTH, N_EXPERTS, D_MODEL, EXPERT_FF), D_MODEL),
        "moe_w_up": nrm((DEPTH, N_EXPERTS, D_MODEL, EXPERT_FF), D_MODEL),
        "moe_w_down": nrm((DEPTH, N_EXPERTS, EXPERT_FF, D_MODEL), EXPERT_FF),
        "ple_w_up": nrm((DEPTH, PLE_DIM, D_MODEL), PLE_DIM),
        "ple_w_gate": nrm((DEPTH, D_MODEL, D_MODEL), D_MODEL),
    }


def reference(x, p, rel_bias, mix_norm_g, ffn_norm_g, ple_norm_g, ev_w_in, ev_conv_w, ev_conv_b, ev_lru_wa, ev_lru_ba, ev_lru_wx, ev_lru_bx, ev_lru_lambda, ev_q_norm_g, ev_k_norm_g, ev_w_out, od_w_in, od_gn_g, od_w_out, moe_w_group, moe_w_expert, moe_w_gate, moe_w_up, moe_w_down, ple_w_up, ple_w_gate):
    b_, s_, _ = x.shape
    h = x
    for i in range(DEPTH):
        j = i // 2
        hn = rms_norm(h, mix_norm_g[i])
        if i % 2 == 0:
            z = hn @ ev_w_in[j]
            xa, ga, q, k, v, iq, ik, iw = _split(z, EV_SPLIT)
            ya = rglru_mixer(xa, ga, ev_conv_w[j], ev_conv_b[j], ev_lru_wa[j], ev_lru_ba[j], ev_lru_wx[j], ev_lru_bx[j], ev_lru_lambda[j])
            yb = dsa_mixer(q.reshape(b_, s_, ATT_HEADS, ATT_HEAD_DIM), k.reshape(b_, s_, ATT_HEADS, ATT_HEAD_DIM), v.reshape(b_, s_, ATT_HEADS, ATT_HEAD_DIM), iq.reshape(b_, s_, IDX_HEADS, IDX_DIM), ik, iw, ev_q_norm_g[j], ev_k_norm_g[j], rel_bias)
            mix = jnp.concatenate([ya, yb], axis=-1) @ ev_w_out[j]
        else:
            z = hn @ od_w_in[j]
            q, k, v, g = _split(z, OD_SPLIT)
            yc = retention_mixer(q.reshape(b_, s_, RET_HEADS, RET_QK_DIM), k.reshape(b_, s_, RET_HEADS, RET_QK_DIM), v.reshape(b_, s_, RET_HEADS, RET_V_DIM), g, od_gn_g[j])
            mix = yc @ od_w_out[j]
        h = h + mix
        h = h + hier_moe(rms_norm(h, ffn_norm_g[i]), moe_w_group[i], moe_w_expert[i], moe_w_gate[i], moe_w_up[i], moe_w_down[i])
        e = rms_norm(p[i] @ ple_w_up[i], ple_norm_g[i])
        h = h + e * jax.nn.sigmoid(h @ ple_w_gate[i])
    return h
```

```python
import functools
import math

import jax
import jax.numpy as jnp
from jax import lax
from jax.experimental import pallas as pl
from jax.experimental.pallas import tpu as pltpu

F32 = jnp.float32
BF16 = jnp.bfloat16
I32 = jnp.int32

D_MODEL = 1024
CHUNK = 64
PLE_DIM = 256
RMS_EPS = 1e-6

LRU_WIDTH = 512
LRU_BLOCKS = 8
LRU_C = 8.0

ATT_HEADS = 8
ATT_HEAD_DIM = 64
ATT_WIDTH = 512
IDX_HEADS = 8
IDX_DIM = 64
MAX_TOP_K = 256
N_BUCKETS = 32
MAX_DISTANCE = 128

RET_HEADS = 8
RET_QK_DIM = 128
RET_V_DIM = 256
RET_QK_WIDTH = 1024
RET_V_WIDTH = 2048
ROPE_BASE = 10000.0
GN_EPS = 1e-5

N_GROUPS = 4
EXPERTS_PER_GROUP = 8
N_EXPERTS = 32
EXPERT_FF = 512

LANES = 128
INT_MIN = -(2 ** 31)
NEG_BIG = -1e30
VMEM_LIMIT = 56 * 1024 * 1024

Q_TILE = 128
K_TILE = 512
RET_TILE = 256


def _cparams(sem):
    return pltpu.CompilerParams(dimension_semantics=sem, vmem_limit_bytes=VMEM_LIMIT)


def _full(shape):
    nd = len(shape)
    return pl.BlockSpec(shape, lambda *_: (0,) * nd)


def _rms(xf, g):
    return xf * lax.rsqrt(jnp.mean(xf * xf, axis=-1, keepdims=True) + RMS_EPS) * g


def _dot(a, b):
    return jnp.dot(a, b, preferred_element_type=F32)


def _dot_nt(a, b):
    return lax.dot_general(a, b, (((1,), (1,)), ((), ())), preferred_element_type=F32)


def _neg_expm1(x):
    series = -x * (1.0 + x * (0.5 + x * (1.0 / 6.0 + x * (1.0 / 24.0 + x * (1.0 / 120.0)))))
    return jnp.where(x > -0.03, series, 1.0 - jnp.exp(x))


def _split_bf16(x):
    hi = x.astype(BF16)
    lo = (x - hi.astype(F32)).astype(BF16)
    return hi, lo


def _even_in_kernel(h_ref, g_ref, wrow_ref, wkt_ref, wikt_ref, bd_ref, qg_ref, kg_ref,
                    xg_ref, q_ref, v_ref, iq_ref, iw_ref, kt_ref, ikt_ref):
    tm = h_ref.shape[0]
    hn = _rms(h_ref[...], g_ref[...]).astype(BF16)

    def proj(a, b):
        return _dot(hn, wrow_ref[:, a:b])

    xg_ref[...] = proj(0, 1024)
    qf = proj(1024, 1536)
    hi, lo = _split_bf16(qf * qf)
    ss = _dot(hi, bd_ref[...]) + _dot(lo, bd_ref[...])
    q_ref[...] = (qf * lax.rsqrt(ss * (1.0 / ATT_HEAD_DIM) + RMS_EPS) * qg_ref[...]).astype(BF16)
    v_ref[...] = proj(1536, 2048).astype(BF16)
    iq_ref[...] = proj(2048, 2560).astype(BF16)
    iw_ref[...] = proj(2560, 2688)[:, :IDX_HEADS] * (IDX_HEADS ** -0.5 * IDX_DIM ** -0.5)

    kt = _dot_nt(wkt_ref[...], hn)
    k3 = kt.reshape(ATT_HEADS, ATT_HEAD_DIM, tm)
    ssk = jnp.sum(k3 * k3, axis=1, keepdims=True)
    kn = k3 * lax.rsqrt(ssk * (1.0 / ATT_HEAD_DIM) + RMS_EPS) * kg_ref[...][None]
    kt_ref[...] = kn.reshape(ATT_WIDTH, tm).astype(BF16)
    ikt_ref[...] = _dot_nt(wikt_ref[...], hn).astype(BF16)


def _even_in(h, g, wrow, wkt, wikt, bd, qg, kg, tm=256):
    n = h.shape[0]
    row = lambda w: pl.BlockSpec((tm, w), lambda i: (i, 0))
    col = lambda r: pl.BlockSpec((r, tm), lambda i: (0, i))
    return pl.pallas_call(
        _even_in_kernel,
        grid=(n // tm,),
        in_specs=[row(D_MODEL), _full(g.shape), _full(wrow.shape), _full(wkt.shape), _full(wikt.shape),
                  _full(bd.shape), _full(qg.shape), _full(kg.shape)],
        out_specs=[row(1024), row(512), row(512), row(512), row(IDX_HEADS), col(512), col(128)],
        out_shape=[jax.ShapeDtypeStruct((n, 1024), F32), jax.ShapeDtypeStruct((n, 512), BF16),
                   jax.ShapeDtypeStruct((n, 512), BF16), jax.ShapeDtypeStruct((n, 512), BF16),
                   jax.ShapeDtypeStruct((n, IDX_HEADS), F32), jax.ShapeDtypeStruct((512, n), BF16),
                   jax.ShapeDtypeStruct((128, n), BF16)],
        compiler_params=_cparams(("parallel",)),
        name="even_in_proj",
    )(h, g, wrow, wkt, wikt, bd, qg, kg)


def _rglru_kernel(xg_ref, cw_ref, cb_ref, wa_ref, ba_ref, wx_ref, bx_ref, lam_ref, ya_ref,
                  tail_ref, hst_ref):
    ts = xg_ref.shape[0]
    w = LRU_WIDTH

    @pl.when(pl.program_id(1) == 0)
    def _():
        tail_ref[...] = jnp.zeros_like(tail_ref)
        hst_ref[...] = jnp.zeros_like(hst_ref)

    xa = xg_ref[:, :w]
    ga = xg_ref[:, w:]
    row = lax.broadcasted_iota(I32, (ts, w), 0)
    tail = tail_ref[...]
    xc = xa * cw_ref[3:4, :] + cb_ref[...]
    for d in (1, 2, 3):
        cur = pltpu.roll(xa, d, 0)
        prev = jnp.concatenate([pltpu.roll(tail, d, 0), jnp.zeros((ts - 8, w), F32)], axis=0)
        xc = xc + jnp.where(row < d, prev, cur) * cw_ref[3 - d:4 - d, :]
    tail_ref[...] = xa[ts - 8:, :]

    xcb = xc.astype(BF16)
    r = jax.nn.sigmoid(_dot(xcb, wa_ref[...]) + ba_ref[...])
    gi = jax.nn.sigmoid(_dot(xcb, wx_ref[...]) + bx_ref[...])
    nl = -lam_ref[...]
    softplus = jnp.maximum(nl, 0.0) + jnp.log1p(jnp.exp(-jnp.abs(nl)))
    log_a = (-LRU_C) * r * softplus
    a = jnp.exp(log_a)
    u = jnp.sqrt(_neg_expm1(2.0 * log_a)) * (gi * xc)

    d = 1
    while d < ts:
        keep = row >= d
        a_sh = jnp.where(keep, pltpu.roll(a, d, 0), 1.0)
        u_sh = jnp.where(keep, pltpu.roll(u, d, 0), 0.0)
        u = a * u_sh + u
        a = a * a_sh
        d *= 2
    hseq = a * hst_ref[0:1, :] + u
    hst_ref[0:1, :] = hseq[ts - 1:ts, :]
    ya_ref[...] = (hseq * jax.nn.gelu(ga)).astype(BF16)


def _rglru(xg, cw, cb, wa, ba, wx, bx, lam, batch, seq, ts=256):
    n = xg.shape[0]
    nt = seq // ts
    return pl.pallas_call(
        _rglru_kernel,
        grid=(batch, nt),
        in_specs=[pl.BlockSpec((ts, 1024), lambda b, t: (b * nt + t, 0)),
                  _full(cw.shape), _full(cb.shape), _full(wa.shape), _full(ba.shape),
                  _full(wx.shape), _full(bx.shape), _full(lam.shape)],
        out_specs=pl.BlockSpec((ts, LRU_WIDTH), lambda b, t: (b * nt + t, 0)),
        out_shape=jax.ShapeDtypeStruct((n, LRU_WIDTH), BF16),
        scratch_shapes=[pltpu.VMEM((8, LRU_WIDTH), F32), pltpu.VMEM((8, LRU_WIDTH), F32)],
        compiler_params=_cparams(("parallel", "arbitrary")),
        name="rglru",
    )(xg, cw, cb, wa, ba, wx, bx, lam)


def _dsa_kernel(far_ref, q_ref, iq_ref, iw_ref, kt_ref, v_ref, ikt_ref, tab_ref, o_ref,
                keys_ref, sel_ref, cut_ref, *, seq, top_k):
    tq = Q_TILE
    j = pl.program_id(1)
    q0 = j * tq
    nkt = (j + K_TILE // tq) // (K_TILE // tq)
    lane_q = lax.broadcasted_iota(I32, (tq, LANES), 1)
    low_half = lane_q < ATT_HEAD_DIM
    qrow = q0 + lax.broadcasted_iota(I32, (tq, 1), 0)
    qlim = (qrow // CHUNK + 1) * CHUNK

    def head_pair(ref, h):
        pair = ref[:, (h // 2) * LANES:(h // 2 + 1) * LANES]
        mine = low_half if h % 2 == 0 else jnp.logical_not(low_half)
        return jnp.where(mine, pair, jnp.zeros_like(pair))

    def ktile(kt):
        return pl.multiple_of(kt * K_TILE, K_TILE)

    iq_m = [head_pair(iq_ref, h) for h in range(IDX_HEADS)]

    def score_body(kt, c):
        off = ktile(kt)
        ik = ikt_ref[:, pl.ds(off, K_TILE)]
        s = jnp.zeros((tq, K_TILE), F32)
        for h in range(IDX_HEADS):
            s = s + jnp.maximum(_dot(iq_m[h], ik), 0.0) * iw_ref[:, h:h + 1]
        s = jnp.where(s == 0.0, 0.0, s)
        bits = pltpu.bitcast(s, I32)
        key = bits ^ ((bits >> 31) & 0x7FFFFFFF)
        kpos = off + lax.broadcasted_iota(I32, (tq, K_TILE), 1)
        keys_ref[:, pl.ds(off, K_TILE)] = jnp.where(kpos < qlim, key, INT_MIN)
        return c

    lax.fori_loop(0, nkt, score_body, 0)

    def count(pred):
        def body(kt, acc):
            off = ktile(kt)
            kk = keys_ref[:, pl.ds(off, K_TILE)]
            kpos = off + lax.broadcasted_iota(I32, (tq, K_TILE), 1)
            ind = pred(kk, kpos)
            for c in range(K_TILE // LANES):
                acc = acc + ind[:, c * LANES:(c + 1) * LANES]
            return acc
        acc = lax.fori_loop(0, nkt, body, jnp.zeros((tq, LANES), F32))
        return jnp.sum(acc, axis=1, keepdims=True)

    kf = float(top_k)

    def search_body(i, ans):
        cand = ans + jnp.left_shift(jnp.int32(1), 31 - i)
        cnt = count(lambda kk, kpos: jnp.where(kk >= cand, 1.0, 0.0))
        return jnp.where(cnt >= kf, cand, ans)

    ans = lax.fori_loop(0, 32, search_body, jnp.full((tq, 1), INT_MIN, I32))

    cnt_ge = count(lambda kk, kpos: jnp.where(kk >= ans, 1.0, 0.0))
    excess = jnp.where(ans > INT_MIN, cnt_ge - kf, 0.0)
    cut_ref[...] = jnp.full(cut_ref.shape, seq, I32)

    @pl.when(jnp.max(excess) > 0.0)
    def _():
        need = kf - count(lambda kk, kpos: jnp.where(kk > ans, 1.0, 0.0))

        def idx_body(i, pos):
            cand = pos + jnp.left_shift(jnp.int32(1), (seq.bit_length() - 2) - i)
            c = count(lambda kk, kpos: jnp.where(kk == ans, jnp.where(kpos < cand, 1.0, 0.0), 0.0))
            return jnp.where(c < need, cand, pos)

        pos = lax.fori_loop(0, seq.bit_length() - 1, idx_body, jnp.zeros((tq, 1), I32))
        cut_ref[...] = jnp.broadcast_to(pos, cut_ref.shape)

    thr = jnp.maximum(ans, INT_MIN + 1)
    cut = cut_ref[:, 0:1]

    def sel_body(kt, c):
        off = ktile(kt)
        kk = keys_ref[:, pl.ds(off, K_TILE)]
        kpos = off + lax.broadcasted_iota(I32, (tq, K_TILE), 1)
        tie = jnp.where(kpos <= cut, 0.0, NEG_BIG)
        sel_ref[:, pl.ds(off, K_TILE)] = jnp.where(kk > thr, 0.0, jnp.where(kk == thr, tie, NEG_BIG))
        return c

    lax.fori_loop(0, nkt, sel_body, 0)

    n_far = jnp.maximum(j - 1, 0) // (K_TILE // tq)

    def step(qm, ktb, vb, add, carry):
        m, l, acc = carry
        s = _dot(qm, ktb) + add
        m_new = jnp.maximum(m, jnp.max(s, axis=1, keepdims=True))
        alpha = jnp.exp(m - m_new)
        p = jnp.exp(s - m_new)
        l = alpha * l + jnp.sum(p, axis=1, keepdims=True)
        acc = alpha * acc + _dot(p.astype(BF16), vb)
        return m_new, l, acc

    res = []
    for h in range(ATT_HEADS):
        qm = head_pair(q_ref, h)
        pr = slice((h // 2) * LANES, (h // 2 + 1) * LANES)

        def far_body(kt, carry, qm=qm, pr=pr, h=h):
            off = ktile(kt)
            add = sel_ref[:, pl.ds(off, K_TILE)] + far_ref[h]
            return step(qm, kt_ref[pr, pl.ds(off, K_TILE)], v_ref[pl.ds(off, K_TILE), pr], add, carry)

        def near_body(kb, carry, qm=qm, pr=pr, h=h):
            off = pl.multiple_of(kb * tq, tq)
            which = jnp.clip(kb - (j - 2), 0, 2)
            add = sel_ref[:, pl.ds(off, tq)] + tab_ref[which, h]
            return step(qm, kt_ref[pr, pl.ds(off, tq)], v_ref[pl.ds(off, tq), pr], add, carry)

        init = (jnp.full((tq, 1), NEG_BIG, F32), jnp.zeros((tq, 1), F32), jnp.zeros((tq, LANES), F32))
        carry = lax.fori_loop(0, n_far, far_body, init)
        m, l, acc = lax.fori_loop(n_far * (K_TILE // tq), j + 1, near_body, carry)
        res.append(acc / l)

    for p in range(ATT_HEADS // 2):
        o_ref[:, p * LANES:(p + 1) * LANES] = jnp.where(low_half, res[2 * p], res[2 * p + 1]).astype(BF16)


def _dsa(far, q, iq, iw, kt, v, ikt, tab, batch, seq):
    n = q.shape[0]
    nb = seq // Q_TILE
    top_k = min(MAX_TOP_K, seq // 4)
    qspec = lambda w: pl.BlockSpec((Q_TILE, w), lambda b, j: (b * nb + j, 0))
    return pl.pallas_call(
        functools.partial(_dsa_kernel, seq=seq, top_k=top_k),
        grid=(batch, nb),
        in_specs=[pl.BlockSpec(memory_space=pltpu.SMEM),
                  qspec(512), qspec(512), qspec(IDX_HEADS),
                  pl.BlockSpec((512, seq), lambda b, j: (0, b)),
                  pl.BlockSpec((seq, 512), lambda b, j: (b, 0)),
                  pl.BlockSpec((128, seq), lambda b, j: (0, b)),
                  _full(tab.shape)],
        out_specs=qspec(512),
        out_shape=jax.ShapeDtypeStruct((n, 512), BF16),
        scratch_shapes=[pltpu.VMEM((Q_TILE, seq), I32), pltpu.VMEM((Q_TILE, seq), F32),
                        pltpu.VMEM((Q_TILE, LANES), I32)],
        compiler_params=_cparams(("parallel", "arbitrary")),
        name="dsa",
    )(far, q, iq, iw, kt, v, ikt, tab)


def _mix_out_kernel(*refs, n_y):
    h_ref = refs[0]
    y_refs = refs[1:1 + n_y]
    w_refs = refs[1 + n_y:1 + 2 * n_y]
    g_ref, wr_hi_ref, wr_lo_ref = refs[1 + 2 * n_y:4 + 2 * n_y]
    h1_ref, xn_ref, comb_ref = refs[4 + 2 * n_y:]
    tm = h_ref.shape[0]

    mix = _dot(y_refs[0][...], w_refs[0][...])
    for y_ref, w_ref in zip(y_refs[1:], w_refs[1:]):
        mix = mix + _dot(y_ref[...], w_ref[...])
    h1 = h_ref[...] + mix
    h1_ref[...] = h1
    xn = _rms(h1, g_ref[...])
    hi, lo = _split_bf16(xn)
    xn_ref[...] = hi
    logits = _dot(hi, wr_hi_ref[...]) + _dot(lo, wr_hi_ref[...]) + _dot(hi, wr_lo_ref[...])

    lane = lax.broadcasted_iota(I32, (tm, LANES), 1)
    big = jnp.int32(LANES)
    ninf = -jnp.inf
    is_g = (lane >= N_EXPERTS) & (lane < N_EXPERTS + N_GROUPS)
    glog = jnp.where(is_g, logits, ninf)
    gmax = jnp.max(glog, axis=1, keepdims=True)
    gsel = jnp.min(jnp.where(glog == gmax, lane, big), axis=1, keepdims=True) - N_EXPERTS
    gprob = 1.0 / jnp.sum(jnp.exp(glog - gmax), axis=1, keepdims=True)
    lo_l = gsel * EXPERTS_PER_GROUP
    within = jnp.where((lane >= lo_l) & (lane < lo_l + EXPERTS_PER_GROUP), logits, ninf)
    v1 = jnp.max(within, axis=1, keepdims=True)
    i1 = jnp.min(jnp.where(within == v1, lane, big), axis=1, keepdims=True)
    rest = jnp.where(lane == i1, ninf, within)
    v2 = jnp.max(rest, axis=1, keepdims=True)
    i2 = jnp.min(jnp.where(rest == v2, lane, big), axis=1, keepdims=True)
    e2 = jnp.exp(v2 - v1)
    w1 = gprob / (1.0 + e2)
    w2 = gprob * e2 / (1.0 + e2)
    comb_ref[...] = jnp.where(lane == i1, w1, 0.0) + jnp.where(lane == i2, w2, 0.0)


def _mix_out(h, ys, ws, g, wr_hi, wr_lo, tm=512):
    n = h.shape[0]
    n_y = len(ys)
    row = lambda w: pl.BlockSpec((tm, w), lambda i: (i, 0))
    return pl.pallas_call(
        functools.partial(_mix_out_kernel, n_y=n_y),
        grid=(n // tm,),
        in_specs=[row(D_MODEL)] + [row(y.shape[1]) for y in ys] + [_full(w.shape) for w in ws]
                 + [_full(g.shape), _full(wr_hi.shape), _full(wr_lo.shape)],
        out_specs=[row(D_MODEL), row(D_MODEL), row(LANES)],
        out_shape=[jax.ShapeDtypeStruct((n, D_MODEL), F32), jax.ShapeDtypeStruct((n, D_MODEL), BF16),
                   jax.ShapeDtypeStruct((n, LANES), F32)],
        compiler_params=_cparams(("parallel",)),
        name="mix_out_router",
    )(h, *ys, *ws, g, wr_hi, wr_lo)


def _moe_kernel(xn_ref, comb_ref, wg_ref, wu_ref, wd_ref, o_ref):
    e = pl.program_id(1)
    tm = xn_ref.shape[0]

    @pl.when(e == 0)
    def _():
        o_ref[...] = jnp.zeros_like(o_ref)

    x = xn_ref[...]
    gate = _dot(x, wg_ref[0].astype(BF16))
    up = _dot(x, wu_ref[0].astype(BF16))
    hid = (jax.nn.silu(gate) * up).astype(BF16)
    y = _dot(hid, wd_ref[0].astype(BF16))
    lane = lax.broadcasted_iota(I32, (tm, LANES), 1)
    c = jnp.sum(jnp.where(lane == e, comb_ref[...], 0.0), axis=1, keepdims=True)
    o_ref[...] += c * y


def _moe(xn, comb, wg, wu, wd, tm=1024):
    n = xn.shape[0]
    return pl.pallas_call(
        _moe_kernel,
        grid=(n // tm, N_EXPERTS),
        in_specs=[pl.BlockSpec((tm, D_MODEL), lambda i, e: (i, 0)),
                  pl.BlockSpec((tm, LANES), lambda i, e: (i, 0)),
                  pl.BlockSpec((1, D_MODEL, EXPERT_FF), lambda i, e: (e, 0, 0)),
                  pl.BlockSpec((1, D_MODEL, EXPERT_FF), lambda i, e: (e, 0, 0)),
                  pl.BlockSpec((1, EXPERT_FF, D_MODEL), lambda i, e: (e, 0, 0))],
        out_specs=pl.BlockSpec((tm, D_MODEL), lambda i, e: (i, 0)),
        out_shape=jax.ShapeDtypeStruct((n, D_MODEL), F32),
        compiler_params=_cparams(("parallel", "arbitrary")),
        name="moe_experts",
    )(xn, comb, wg, wu, wd)


def _ple_kernel(h1_ref, moe_ref, p_ref, wup_ref, wgate_ref, g_ref, o_ref):
    h2 = h1_ref[...] + moe_ref[...]
    e = _rms(_dot(p_ref[...].astype(BF16), wup_ref[...]), g_ref[...])
    gate = jax.nn.sigmoid(_dot(h2.astype(BF16), wgate_ref[...]))
    o_ref[...] = h2 + e * gate


def _ple(h1, moe, p, wup, wgate, g, tm=512):
    n = h1.shape[0]
    row = lambda w: pl.BlockSpec((tm, w), lambda i: (i, 0))
    return pl.pallas_call(
        _ple_kernel,
        grid=(n // tm,),
        in_specs=[row(D_MODEL), row(D_MODEL), row(PLE_DIM), _full(wup.shape), _full(wgate.shape),
                  _full(g.shape)],
        out_specs=row(D_MODEL),
        out_shape=jax.ShapeDtypeStruct((n, D_MODEL), F32),
        compiler_params=_cparams(("parallel",)),
        name="ple",
    )(h1, moe, p, wup, wgate, g)


def _odd_in_kernel(h_ref, g_ref, wrow_ref, wkt_ref, cos_ref, sin_ref, cost_ref, sint_ref,
                   q_ref, v_ref, gate_ref, kt_ref):
    tm = h_ref.shape[0]
    hn = _rms(h_ref[...], g_ref[...]).astype(BF16)
    cos = cos_ref[...]
    sin = sin_ref[...]
    half = RET_QK_DIM // 2
    for hd in range(RET_HEADS):
        sl = slice(hd * RET_QK_DIM, (hd + 1) * RET_QK_DIM)
        qh = _dot(hn, wrow_ref[:, sl])
        q_ref[:, sl] = (qh * cos + pltpu.roll(qh, half, 1) * sin).astype(BF16)
    v_ref[...] = _dot(hn, wrow_ref[:, RET_QK_WIDTH:RET_QK_WIDTH + RET_V_WIDTH]).astype(BF16)
    gate_ref[...] = _dot(hn, wrow_ref[:, RET_QK_WIDTH + RET_V_WIDTH:]).astype(BF16)

    cost = cost_ref[...] * (RET_QK_DIM ** -0.5)
    sint = sint_ref[...] * (RET_QK_DIM ** -0.5)
    for hd in range(RET_HEADS):
        sl = slice(hd * RET_QK_DIM, (hd + 1) * RET_QK_DIM)
        kh = _dot_nt(wkt_ref[sl, :], hn)
        swapped = jnp.concatenate([kh[half:], kh[:half]], axis=0)
        kt_ref[sl, :] = (kh * cost + swapped * sint).astype(BF16)


def _odd_in(h, g, wrow, wkt, cos, sin, cost, sint, seq, tm=256):
    n = h.shape[0]
    nt = seq // tm
    row = lambda w: pl.BlockSpec((tm, w), lambda i: (i, 0))
    return pl.pallas_call(
        _odd_in_kernel,
        grid=(n // tm,),
        in_specs=[row(D_MODEL), _full(g.shape), _full(wrow.shape), _full(wkt.shape),
                  pl.BlockSpec((tm, RET_QK_DIM), lambda i: (i % nt, 0)),
                  pl.BlockSpec((tm, RET_QK_DIM), lambda i: (i % nt, 0)),
                  pl.BlockSpec((RET_QK_DIM, tm), lambda i: (0, i % nt)),
                  pl.BlockSpec((RET_QK_DIM, tm), lambda i: (0, i % nt))],
        out_specs=[row(RET_QK_WIDTH), row(RET_V_WIDTH), row(RET_V_WIDTH),
                   pl.BlockSpec((RET_QK_WIDTH, tm), lambda i: (0, i))],
        out_shape=[jax.ShapeDtypeStruct((n, RET_QK_WIDTH), BF16), jax.ShapeDtypeStruct((n, RET_V_WIDTH), BF16),
                   jax.ShapeDtypeStruct((n, RET_V_WIDTH), BF16), jax.ShapeDtypeStruct((RET_QK_WIDTH, n), BF16)],
        compiler_params=_cparams(("parallel",)),
        name="odd_in_proj",
    )(h, g, wrow, wkt, cos, sin, cost, sint)


def _ret_kernel(cdec_ref, q_ref, kt_ref, v_ref, gate_ref, dintra_ref, qdec_ref, kdec_ref, gn_ref,
                y_ref, state_ref):
    @pl.when(pl.program_id(1) == 0)
    def _():
        state_ref[...] = jnp.zeros_like(state_ref)

    for h in range(RET_HEADS):
        ks = slice(h * RET_QK_DIM, (h + 1) * RET_QK_DIM)
        vs = slice(h * RET_V_DIM, (h + 1) * RET_V_DIM)
        qh = q_ref[:, ks]
        kth = kt_ref[ks, :]
        vh = v_ref[:, vs]
        state = state_ref[h]
        inner = (_dot(qh, kth) * dintra_ref[h]).astype(BF16)
        o = _dot(inner, vh) + _dot(qh, state.astype(BF16)) * qdec_ref[h]
        kd = (kth.astype(F32) * kdec_ref[h]).astype(BF16)
        state_ref[h] = state * cdec_ref[h] + _dot(kd, vh)
        mu = jnp.mean(o, axis=-1, keepdims=True)
        oc = o - mu
        var = jnp.mean(oc * oc, axis=-1, keepdims=True)
        on = oc * lax.rsqrt(var + GN_EPS) * gn_ref[:, vs]
        y_ref[:, vs] = (jax.nn.silu(gate_ref[:, vs].astype(F32)) * on).astype(BF16)


def _retention(cdec, q, kt, v, gate, dintra, qdec, kdec, gn, batch, seq):
    n = q.shape[0]
    c = RET_TILE
    nc = seq // c
    row = lambda w: pl.BlockSpec((c, w), lambda b, t: (b * nc + t, 0))
    return pl.pallas_call(
        _ret_kernel,
        grid=(batch, nc),
        in_specs=[pl.BlockSpec(memory_space=pltpu.SMEM),
                  row(RET_QK_WIDTH),
                  pl.BlockSpec((RET_QK_WIDTH, c), lambda b, t: (0, b * nc + t)),
                  row(RET_V_WIDTH), row(RET_V_WIDTH),
                  _full(dintra.shape), _full(qdec.shape), _full(kdec.shape), _full(gn.shape)],
        out_specs=row(RET_V_WIDTH),
        out_shape=jax.ShapeDtypeStruct((n, RET_V_WIDTH), BF16),
        scratch_shapes=[pltpu.VMEM((RET_HEADS, RET_QK_DIM, RET_V_DIM), F32)],
        compiler_params=_cparams(("parallel", "arbitrary")),
        name="retention",
    )(cdec, q, kt, v, gate, dintra, qdec, kdec, gn)


def _t5_bucket(rel):
    half = N_BUCKETS // 2
    max_exact = half // 2
    ret = (rel > 0).astype(I32) * half
    n = jnp.abs(rel)
    nf = jnp.maximum(n, 1).astype(F32)
    large = max_exact + (jnp.log(nf / max_exact) / math.log(MAX_DISTANCE / max_exact)
                         * (half - max_exact)).astype(I32)
    large = jnp.minimum(large, half - 1)
    return ret + jnp.where(n < max_exact, n, large)


def _bias_tables(rel_bias):
    r = jnp.arange(Q_TILE, dtype=I32)[:, None]
    c = jnp.arange(Q_TILE, dtype=I32)[None, :]
    rels = jnp.stack([c - 2 * Q_TILE - r, c - Q_TILE - r, c - r])
    tab = jnp.transpose(rel_bias[_t5_bucket(rels)], (0, 3, 1, 2))
    far = rel_bias[_t5_bucket(jnp.int32(-2 * Q_TILE))]
    tab = tab.at[0].set(jnp.broadcast_to(far[:, None, None], tab.shape[1:]))
    return tab.astype(F32), far.astype(F32)


def _block_diag(w):
    nb, bs, _ = w.shape
    eye = jnp.eye(nb, dtype=w.dtype)
    return jnp.einsum("hij,hg->higj", w, eye).reshape(nb * bs, nb * bs)


def _router_weights(w_group, w_expert):
    wr = jnp.concatenate([w_expert, w_group, jnp.zeros((D_MODEL, LANES - N_EXPERTS - N_GROUPS), F32)], axis=1)
    hi = wr.astype(BF16)
    lo = (wr - hi.astype(F32)).astype(BF16)
    return hi, lo


def _rotary_tables(seq):
    half = RET_QK_DIM // 2
    inv = ROPE_BASE ** (-jnp.arange(half, dtype=F32) / half)
    ang = jnp.arange(seq, dtype=F32)[:, None] * inv[None, :]
    cos, sin = jnp.cos(ang), jnp.sin(ang)
    cos_row = jnp.concatenate([cos, cos], axis=1)
    sin_row = jnp.concatenate([-sin, sin], axis=1)
    cos_t = jnp.concatenate([cos.T, cos.T], axis=0)
    sin_t = jnp.concatenate([-sin.T, sin.T], axis=0)
    return cos_row, sin_row, cos_t, sin_t


def _retention_tables():
    c = RET_TILE
    log_g = jnp.log(1.0 - 2.0 ** (-5.0 - jnp.arange(RET_HEADS, dtype=F32)))
    pos = jnp.arange(c, dtype=F32)
    diff = pos[:, None] - pos[None, :]
    causal = diff >= 0
    dintra = jnp.where(causal[None], jnp.exp(jnp.where(causal, diff, 0.0)[None] * log_g[:, None, None]), 0.0)
    qdec = jnp.exp((pos + 1.0)[None, :, None] * log_g[:, None, None])
    kdec = jnp.exp((c - 1.0 - pos)[None, None, :] * log_g[:, None, None])
    cdec = jnp.exp(c * log_g)
    return dintra, qdec, kdec, cdec


def kernel(x, p, rel_bias, mix_norm_g, ffn_norm_g, ple_norm_g, ev_w_in, ev_conv_w, ev_conv_b, ev_lru_wa, ev_lru_ba, ev_lru_wx, ev_lru_bx, ev_lru_lambda, ev_q_norm_g, ev_k_norm_g, ev_w_out, od_w_in, od_gn_g, od_w_out, moe_w_group, moe_w_expert, moe_w_gate, moe_w_up, moe_w_down, ple_w_up, ple_w_gate):
    batch, seq, _ = x.shape
    n = batch * seq
    depth = p.shape[0]
    h = x.reshape(n, D_MODEL)
    row = lambda a: a.reshape(1, -1)

    tab, far = _bias_tables(rel_bias)
    cos_row, sin_row, cos_t, sin_t = _rotary_tables(seq)
    dintra, qdec, kdec, cdec = _retention_tables()
    head_ones = _block_diag(jnp.ones((ATT_HEADS, ATT_HEAD_DIM, ATT_HEAD_DIM), BF16))

    for i in range(depth):
        jdx = i // 2
        if i % 2 == 0:
            w = ev_w_in[jdx]
            o = [0, 512, 1024, 1536, 2048, 2560, 3072, 3136, 3144]
            xa_w, ga_w, q_w, k_w, v_w, iq_w, ik_w, iw_w = [w[:, o[t]:o[t + 1]] for t in range(8)]
            wrow = jnp.concatenate([xa_w, ga_w, q_w, v_w, iq_w, iw_w,
                                    jnp.zeros((D_MODEL, LANES - IDX_HEADS), F32)], axis=1).astype(BF16)
            wkt = k_w.T.astype(BF16)
            wikt = jnp.concatenate([ik_w.T, ik_w.T], axis=0).astype(BF16)
            qg = jnp.tile(ev_q_norm_g[jdx], ATT_HEADS).reshape(1, -1) * (ATT_HEAD_DIM ** -0.5)
            kg = ev_k_norm_g[jdx].reshape(-1, 1)
            xg, q, v, iq, iw, kt, ikt = _even_in(h, row(mix_norm_g[i]), wrow, wkt, wikt, head_ones, qg, kg)
            ya = _rglru(xg, ev_conv_w[jdx], row(ev_conv_b[jdx]),
                        _block_diag(ev_lru_wa[jdx]).astype(BF16), row(ev_lru_ba[jdx]),
                        _block_diag(ev_lru_wx[jdx]).astype(BF16), row(ev_lru_bx[jdx]),
                        row(ev_lru_lambda[jdx]), batch, seq)
            yb = _dsa(far, q, iq, iw, kt, v, ikt, tab, batch, seq)
            wo = ev_w_out[jdx].astype(BF16)
            ys, ws = [ya, yb], [wo[:LRU_WIDTH], wo[LRU_WIDTH:]]
        else:
            w = od_w_in[jdx]
            wrow = jnp.concatenate([w[:, :RET_QK_WIDTH], w[:, 2 * RET_QK_WIDTH:]], axis=1).astype(BF16)
            wkt = w[:, RET_QK_WIDTH:2 * RET_QK_WIDTH].T.astype(BF16)
            q, v, gate, kt = _odd_in(h, row(mix_norm_g[i]), wrow, wkt, cos_row, sin_row, cos_t, sin_t, seq)
            yc = _retention(cdec, q, kt, v, gate, dintra, qdec, kdec, row(od_gn_g[jdx]), batch, seq)
            ys, ws = [yc], [od_w_out[jdx].astype(BF16)]
        wr_hi, wr_lo = _router_weights(moe_w_group[i], moe_w_expert[i])
        h1, xn, comb = _mix_out(h, ys, ws, row(ffn_norm_g[i]), wr_hi, wr_lo)
        moe = _moe(xn, comb, moe_w_gate[i], moe_w_up[i], moe_w_down[i])
        h = _ple(h1, moe, p[i].reshape(n, PLE_DIM), ple_w_up[i].astype(BF16), ple_w_gate[i].astype(BF16),
                 row(ple_norm_g[i]))
    return h.reshape(batch, seq, D_MODEL)
```

```python
import functools
import math

import jax
import jax.numpy as jnp
from jax import lax
from jax.experimental import pallas as pl
from jax.experimental.pallas import tpu as pltpu

F32 = jnp.float32
BF16 = jnp.bfloat16
I32 = jnp.int32

D_MODEL = 1024
CHUNK = 64
PLE_DIM = 256
RMS_EPS = 1e-6

LRU_WIDTH = 512
LRU_BLOCKS = 8
LRU_C = 8.0

ATT_HEADS = 8
ATT_HEAD_DIM = 64
ATT_WIDTH = 512
IDX_HEADS = 8
IDX_DIM = 64
MAX_TOP_K = 256
N_BUCKETS = 32
MAX_DISTANCE = 128

RET_HEADS = 8
RET_QK_DIM = 128
RET_V_DIM = 256
RET_QK_WIDTH = 1024
RET_V_WIDTH = 2048
ROPE_BASE = 10000.0
GN_EPS = 1e-5

N_GROUPS = 4
EXPERTS_PER_GROUP = 8
N_EXPERTS = 32
EXPERT_FF = 512

LANES = 128
INT_MIN = -(2 ** 31)
NEG_BIG = -1e30
VMEM_LIMIT = 56 * 1024 * 1024

Q_TILE = 128
K_TILE = 512
RET_TILE = 256


def _cparams(sem):
    return pltpu.CompilerParams(dimension_semantics=sem, vmem_limit_bytes=VMEM_LIMIT)


def _full(shape):
    nd = len(shape)
    return pl.BlockSpec(shape, lambda *_: (0,) * nd)


def _rms(xf, g):
    return xf * lax.rsqrt(jnp.mean(xf * xf, axis=-1, keepdims=True) + RMS_EPS) * g


def _dot(a, b):
    return jnp.dot(a, b, preferred_element_type=F32)


def _dot_nt(a, b):
    return lax.dot_general(a, b, (((1,), (1,)), ((), ())), preferred_element_type=F32)


def _neg_expm1(x):
    series = -x * (1.0 + x * (0.5 + x * (1.0 / 6.0 + x * (1.0 / 24.0 + x * (1.0 / 120.0)))))
    return jnp.where(x > -0.03, series, 1.0 - jnp.exp(x))


def _split_bf16(x):
    hi = x.astype(BF16)
    lo = (x - hi.astype(F32)).astype(BF16)
    return hi, lo


def _even_in_kernel(h_ref, g_ref, wrow_ref, wkt_ref, wikt_ref, bd_ref, qg_ref, kg_ref,
                    xg_ref, q_ref, v_ref, iq_ref, iw_ref, kt_ref, ikt_ref):
    tm = h_ref.shape[0]
    hn = _rms(h_ref[...], g_ref[...]).astype(BF16)

    def proj(a, b):
        return _dot(hn, wrow_ref[:, a:b])

    xg_ref[...] = proj(0, 1024)
    qf = proj(1024, 1536)
    hi, lo = _split_bf16(qf * qf)
    ss = _dot(hi, bd_ref[...]) + _dot(lo, bd_ref[...])
    q_ref[...] = (qf * lax.rsqrt(ss * (1.0 / ATT_HEAD_DIM) + RMS_EPS) * qg_ref[...]).astype(BF16)
    v_ref[...] = proj(1536, 2048).astype(BF16)
    iq_ref[...] = proj(2048, 2560).astype(BF16)
    iw_ref[...] = proj(2560, 2688)[:, :IDX_HEADS] * (IDX_HEADS ** -0.5 * IDX_DIM ** -0.5)

    kt = _dot_nt(wkt_ref[...], hn)
    k3 = kt.reshape(ATT_HEADS, ATT_HEAD_DIM, tm)
    ssk = jnp.sum(k3 * k3, axis=1, keepdims=True)
    kn = k3 * lax.rsqrt(ssk * (1.0 / ATT_HEAD_DIM) + RMS_EPS) * kg_ref[...][None]
    kt_ref[...] = kn.reshape(ATT_WIDTH, tm).astype(BF16)
    ikt_ref[...] = _dot_nt(wikt_ref[...], hn).astype(BF16)


def _even_in(h, g, wrow, wkt, wikt, bd, qg, kg, tm=256):
    n = h.shape[0]
    row = lambda w: pl.BlockSpec((tm, w), lambda i: (i, 0))
    col = lambda r: pl.BlockSpec((r, tm), lambda i: (0, i))
    return pl.pallas_call(
        _even_in_kernel,
        grid=(n // tm,),
        in_specs=[row(D_MODEL), _full(g.shape), _full(wrow.shape), _full(wkt.shape), _full(wikt.shape),
                  _full(bd.shape), _full(qg.shape), _full(kg.shape)],
        out_specs=[row(1024), row(512), row(512), row(512), row(IDX_HEADS), col(512), col(128)],
        out_shape=[jax.ShapeDtypeStruct((n, 1024), F32), jax.ShapeDtypeStruct((n, 512), BF16),
                   jax.ShapeDtypeStruct((n, 512), BF16), jax.ShapeDtypeStruct((n, 512), BF16),
                   jax.ShapeDtypeStruct((n, IDX_HEADS), F32), jax.ShapeDtypeStruct((512, n), BF16),
                   jax.ShapeDtypeStruct((128, n), BF16)],
        compiler_params=_cparams(("parallel",)),
        name="even_in_proj",
    )(h, g, wrow, wkt, wikt, bd, qg, kg)


def _rglru_kernel(xg_ref, cw_ref, cb_ref, wa_ref, ba_ref, wx_ref, bx_ref, lam_ref, ya_ref,
                  tail_ref, hst_ref):
    ts = xg_ref.shape[0]
    w = LRU_WIDTH

    @pl.when(pl.program_id(1) == 0)
    def _():
        tail_ref[...] = jnp.zeros_like(tail_ref)
        hst_ref[...] = jnp.zeros_like(hst_ref)

    xa = xg_ref[:, :w]
    ga = xg_ref[:, w:]
    row = lax.broadcasted_iota(I32, (ts, w), 0)
    tail = tail_ref[...]
    xc = xa * cw_ref[3:4, :] + cb_ref[...]
    for d in (1, 2, 3):
        cur = pltpu.roll(xa, d, 0)
        prev = jnp.concatenate([pltpu.roll(tail, d, 0), jnp.zeros((ts - 8, w), F32)], axis=0)
        xc = xc + jnp.where(row < d, prev, cur) * cw_ref[3 - d:4 - d, :]
    tail_ref[...] = xa[ts - 8:, :]

    xcb = xc.astype(BF16)
    r = jax.nn.sigmoid(_dot(xcb, wa_ref[...]) + ba_ref[...])
    gi = jax.nn.sigmoid(_dot(xcb, wx_ref[...]) + bx_ref[...])
    nl = -lam_ref[...]
    softplus = jnp.maximum(nl, 0.0) + jnp.log1p(jnp.exp(-jnp.abs(nl)))
    log_a = (-LRU_C) * r * softplus
    a = jnp.exp(log_a)
    u = jnp.sqrt(_neg_expm1(2.0 * log_a)) * (gi * xc)

    d = 1
    while d < ts:
        keep = row >= d
        a_sh = jnp.where(keep, pltpu.roll(a, d, 0), 1.0)
        u_sh = jnp.where(keep, pltpu.roll(u, d, 0), 0.0)
        u = a * u_sh + u
        a = a * a_sh
        d *= 2
    hseq = a * hst_ref[0:1, :] + u
    hst_ref[0:1, :] = hseq[ts - 1:ts, :]
    ya_ref[...] = (hseq * jax.nn.gelu(ga)).astype(BF16)


def _rglru(xg, cw, cb, wa, ba, wx, bx, lam, batch, seq, ts=256):
    n = xg.shape[0]
    nt = seq // ts
    return pl.pallas_call(
        _rglru_kernel,
        grid=(batch, nt),
        in_specs=[pl.BlockSpec((ts, 1024), lambda b, t: (b * nt + t, 0)),
                  _full(cw.shape), _full(cb.shape), _full(wa.shape), _full(ba.shape),
                  _full(wx.shape), _full(bx.shape), _full(lam.shape)],
        out_specs=pl.BlockSpec((ts, LRU_WIDTH), lambda b, t: (b * nt + t, 0)),
        out_shape=jax.ShapeDtypeStruct((n, LRU_WIDTH), BF16),
        scratch_shapes=[pltpu.VMEM((8, LRU_WIDTH), F32), pltpu.VMEM((8, LRU_WIDTH), F32)],
        compiler_params=_cparams(("parallel", "arbitrary")),
        name="rglru",
    )(xg, cw, cb, wa, ba, wx, bx, lam)


def _dsa_kernel(far_ref, q_ref, iq_ref, iw_ref, kt_ref, v_ref, ikt_ref, tab_ref, o_ref,
                keys_ref, sel_ref, cut_ref, *, seq, top_k):
    tq = Q_TILE
    j = pl.program_id(1)
    q0 = j * tq
    nkt = (j + K_TILE // tq) // (K_TILE // tq)
    lane_q = lax.broadcasted_iota(I32, (tq, LANES), 1)
    low_half = lane_q < ATT_HEAD_DIM
    qrow = q0 + lax.broadcasted_iota(I32, (tq, 1), 0)
    qlim = (qrow // CHUNK + 1) * CHUNK

    def head_pair(ref, h):
        pair = ref[:, (h // 2) * LANES:(h // 2 + 1) * LANES]
        mine = low_half if h % 2 == 0 else jnp.logical_not(low_half)
        return jnp.where(mine, pair, jnp.zeros_like(pair))

    def ktile(kt):
        return pl.multiple_of(kt * K_TILE, K_TILE)

    iq_m = [head_pair(iq_ref, h) for h in range(IDX_HEADS)]

    def score_body(kt, c):
        off = ktile(kt)
        ik = ikt_ref[:, pl.ds(off, K_TILE)]
        s = jnp.zeros((tq, K_TILE), F32)
        for h in range(IDX_HEADS):
            s = s + jnp.maximum(_dot(iq_m[h], ik), 0.0) * iw_ref[:, h:h + 1]
        s = jnp.where(s == 0.0, 0.0, s)
        bits = pltpu.bitcast(s, I32)
        key = bits ^ ((bits >> 31) & 0x7FFFFFFF)
        kpos = off + lax.broadcasted_iota(I32, (tq, K_TILE), 1)
        keys_ref[:, pl.ds(off, K_TILE)] = jnp.where(kpos < qlim, key, INT_MIN)
        return c

    lax.fori_loop(0, nkt, score_body, 0)

    def count(pred):
        def body(kt, acc):
            off = ktile(kt)
            kk = keys_ref[:, pl.ds(off, K_TILE)]
            kpos = off + lax.broadcasted_iota(I32, (tq, K_TILE), 1)
            ind = pred(kk, kpos)
            for c in range(K_TILE // LANES):
                acc = acc + ind[:, c * LANES:(c + 1) * LANES]
            return acc
        acc = lax.fori_loop(0, nkt, body, jnp.zeros((tq, LANES), F32))
        return jnp.sum(acc, axis=1, keepdims=True)

    kf = float(top_k)

    def search_body(i, ans):
        cand = ans + jnp.left_shift(jnp.int32(1), 31 - i)
        cnt = count(lambda kk, kpos: jnp.where(kk >= cand, 1.0, 0.0))
        return jnp.where(cnt >= kf, cand, ans)

    ans = lax.fori_loop(0, 32, search_body, jnp.full((tq, 1), INT_MIN, I32))

    cnt_ge = count(lambda kk, kpos: jnp.where(kk >= ans, 1.0, 0.0))
    excess = jnp.where(ans > INT_MIN, cnt_ge - kf, 0.0)
    cut_ref[...] = jnp.full(cut_ref.shape, seq, I32)

    @pl.when(jnp.max(excess) > 0.0)
    def _():
        need = kf - count(lambda kk, kpos: jnp.where(kk > ans, 1.0, 0.0))

        def idx_body(i, pos):
            cand = pos + jnp.left_shift(jnp.int32(1), (seq.bit_length() - 2) - i)
            c = count(lambda kk, kpos: jnp.where(kk == ans, jnp.where(kpos < cand, 1.0, 0.0), 0.0))
            return jnp.where(c < need, cand, pos)

        pos = lax.fori_loop(0, seq.bit_length() - 1, idx_body, jnp.zeros((tq, 1), I32))
        cut_ref[...] = jnp.broadcast_to(pos, cut_ref.shape)

    thr = jnp.maximum(ans, INT_MIN + 1)
    cut = cut_ref[:, 0:1]

    near_w = 2 * tq
    near_start = pl.multiple_of(jnp.maximum(j - 1, 0) * tq, tq)

    def selection(kk, kpos):
        tie = jnp.where(kpos <= cut, 0.0, NEG_BIG)
        return jnp.where(kk > thr, 0.0, jnp.where(kk == thr, tie, NEG_BIG))

    def sel_body(kt, c):
        off = ktile(kt)
        kpos = off + lax.broadcasted_iota(I32, (tq, K_TILE), 1)
        s = selection(keys_ref[:, pl.ds(off, K_TILE)], kpos)
        sel_ref[:, pl.ds(off, K_TILE)] = jnp.where(kpos < near_start, s, NEG_BIG)
        return c

    n_far = (jnp.maximum(j - 1, 0) + K_TILE // tq - 1) // (K_TILE // tq)
    lax.fori_loop(0, n_far, sel_body, 0)
    sel_near = selection(keys_ref[:, pl.ds(near_start, near_w)],
                         near_start + lax.broadcasted_iota(I32, (tq, near_w), 1))

    def step(qm, ktb, vb, add, carry):
        m, l, acc = carry
        s = _dot(qm, ktb) + add
        m_new = jnp.maximum(m, jnp.max(s, axis=1, keepdims=True))
        alpha = jnp.exp(m - m_new)
        p = jnp.exp(s - m_new)
        l = alpha * l + jnp.sum(p, axis=1, keepdims=True)
        acc = alpha * acc + _dot(p.astype(BF16), vb)
        return m_new, l, acc

    qms = [head_pair(q_ref, h) for h in range(ATT_HEADS)]
    pairs = [slice((h // 2) * LANES, (h // 2 + 1) * LANES) for h in range(ATT_HEADS)]

    def far_body(kt, carries):
        off = ktile(kt)
        sel = sel_ref[:, pl.ds(off, K_TILE)]
        return tuple(
            step(qms[h], kt_ref[pairs[h], pl.ds(off, K_TILE)], v_ref[pl.ds(off, K_TILE), pairs[h]],
                 sel + far_ref[h], carries[h])
            for h in range(ATT_HEADS))

    init = (jnp.full((tq, 1), NEG_BIG, F32), jnp.zeros((tq, 1), F32), jnp.zeros((tq, LANES), F32))
    carries = lax.fori_loop(0, n_far, far_body, (init,) * ATT_HEADS)

    first = jnp.where(j == 0, 1, 0)
    res = []
    for h in range(ATT_HEADS):
        bias = jnp.concatenate([tab_ref[first, h], tab_ref[1, h]], axis=1)
        m, l, acc = step(qms[h], kt_ref[pairs[h], pl.ds(near_start, near_w)],
                         v_ref[pl.ds(near_start, near_w), pairs[h]], sel_near + bias, carries[h])
        res.append(acc / l)

    for p in range(ATT_HEADS // 2):
        o_ref[:, p * LANES:(p + 1) * LANES] = jnp.where(low_half, res[2 * p], res[2 * p + 1]).astype(BF16)


def _dsa(far, q, iq, iw, kt, v, ikt, tab, batch, seq):
    n = q.shape[0]
    nb = seq // Q_TILE
    top_k = min(MAX_TOP_K, seq // 4)
    qspec = lambda w: pl.BlockSpec((Q_TILE, w), lambda b, j: (b * nb + j, 0))
    return pl.pallas_call(
        functools.partial(_dsa_kernel, seq=seq, top_k=top_k),
        grid=(batch, nb),
        in_specs=[pl.BlockSpec(memory_space=pltpu.SMEM),
                  qspec(512), qspec(512), qspec(IDX_HEADS),
                  pl.BlockSpec((512, seq), lambda b, j: (0, b)),
                  pl.BlockSpec((seq, 512), lambda b, j: (b, 0)),
                  pl.BlockSpec((128, seq), lambda b, j: (0, b)),
                  _full(tab.shape)],
        out_specs=qspec(512),
        out_shape=jax.ShapeDtypeStruct((n, 512), BF16),
        scratch_shapes=[pltpu.VMEM((Q_TILE, seq), I32), pltpu.VMEM((Q_TILE, seq), F32),
                        pltpu.VMEM((Q_TILE, LANES), I32)],
        compiler_params=_cparams(("parallel", "arbitrary")),
        name="dsa",
    )(far, q, iq, iw, kt, v, ikt, tab)


def _mix_out_kernel(*refs, n_y):
    h_ref = refs[0]
    y_refs = refs[1:1 + n_y]
    w_refs = refs[1 + n_y:1 + 2 * n_y]
    g_ref, wr_hi_ref, wr_lo_ref = refs[1 + 2 * n_y:4 + 2 * n_y]
    h1_ref, xn_ref, comb_ref = refs[4 + 2 * n_y:]
    tm = h_ref.shape[0]

    mix = _dot(y_refs[0][...], w_refs[0][...])
    for y_ref, w_ref in zip(y_refs[1:], w_refs[1:]):
        mix = mix + _dot(y_ref[...], w_ref[...])
    h1 = h_ref[...] + mix
    h1_ref[...] = h1
    xn = _rms(h1, g_ref[...])
    hi, lo = _split_bf16(xn)
    xn_ref[...] = hi
    logits = _dot(hi, wr_hi_ref[...]) + _dot(lo, wr_hi_ref[...]) + _dot(hi, wr_lo_ref[...])

    lane = lax.broadcasted_iota(I32, (tm, LANES), 1)
    big = jnp.int32(LANES)
    ninf = -jnp.inf
    is_g = (lane >= N_EXPERTS) & (lane < N_EXPERTS + N_GROUPS)
    glog = jnp.where(is_g, logits, ninf)
    gmax = jnp.max(glog, axis=1, keepdims=True)
    gsel = jnp.min(jnp.where(glog == gmax, lane, big), axis=1, keepdims=True) - N_EXPERTS
    gprob = 1.0 / jnp.sum(jnp.exp(glog - gmax), axis=1, keepdims=True)
    lo_l = gsel * EXPERTS_PER_GROUP
    within = jnp.where((lane >= lo_l) & (lane < lo_l + EXPERTS_PER_GROUP), logits, ninf)
    v1 = jnp.max(within, axis=1, keepdims=True)
    i1 = jnp.min(jnp.where(within == v1, lane, big), axis=1, keepdims=True)
    rest = jnp.where(lane == i1, ninf, within)
    v2 = jnp.max(rest, axis=1, keepdims=True)
    i2 = jnp.min(jnp.where(rest == v2, lane, big), axis=1, keepdims=True)
    e2 = jnp.exp(v2 - v1)
    w1 = gprob / (1.0 + e2)
    w2 = gprob * e2 / (1.0 + e2)
    comb_ref[...] = jnp.where(lane == i1, w1, 0.0) + jnp.where(lane == i2, w2, 0.0)


def _mix_out(h, ys, ws, g, wr_hi, wr_lo, tm=512):
    n = h.shape[0]
    n_y = len(ys)
    row = lambda w: pl.BlockSpec((tm, w), lambda i: (i, 0))
    return pl.pallas_call(
        functools.partial(_mix_out_kernel, n_y=n_y),
        grid=(n // tm,),
        in_specs=[row(D_MODEL)] + [row(y.shape[1]) for y in ys] + [_full(w.shape) for w in ws]
                 + [_full(g.shape), _full(wr_hi.shape), _full(wr_lo.shape)],
        out_specs=[row(D_MODEL), row(D_MODEL), row(LANES)],
        out_shape=[jax.ShapeDtypeStruct((n, D_MODEL), F32), jax.ShapeDtypeStruct((n, D_MODEL), BF16),
                   jax.ShapeDtypeStruct((n, LANES), F32)],
        compiler_params=_cparams(("parallel",)),
        name="mix_out_router",
    )(h, *ys, *ws, g, wr_hi, wr_lo)


def _moe_kernel(xn_ref, comb_ref, wg_ref, wu_ref, wd_ref, o_ref):
    e = pl.program_id(1)
    tm = xn_ref.shape[0]

    @pl.when(e == 0)
    def _():
        o_ref[...] = jnp.zeros_like(o_ref)

    x = xn_ref[...]
    gate = _dot(x, wg_ref[0].astype(BF16))
    up = _dot(x, wu_ref[0].astype(BF16))
    hid = (jax.nn.silu(gate) * up).astype(BF16)
    y = _dot(hid, wd_ref[0].astype(BF16))
    lane = lax.broadcasted_iota(I32, (tm, LANES), 1)
    c = jnp.sum(jnp.where(lane == e, comb_ref[...], 0.0), axis=1, keepdims=True)
    o_ref[...] += c * y


def _moe(xn, comb, wg, wu, wd, tm=1024):
    n = xn.shape[0]
    return pl.pallas_call(
        _moe_kernel,
        grid=(n // tm, N_EXPERTS),
        in_specs=[pl.BlockSpec((tm, D_MODEL), lambda i, e: (i, 0)),
                  pl.BlockSpec((tm, LANES), lambda i, e: (i, 0)),
                  pl.BlockSpec((1, D_MODEL, EXPERT_FF), lambda i, e: (e, 0, 0)),
                  pl.BlockSpec((1, D_MODEL, EXPERT_FF), lambda i, e: (e, 0, 0)),
                  pl.BlockSpec((1, EXPERT_FF, D_MODEL), lambda i, e: (e, 0, 0))],
        out_specs=pl.BlockSpec((tm, D_MODEL), lambda i, e: (i, 0)),
        out_shape=jax.ShapeDtypeStruct((n, D_MODEL), F32),
        compiler_params=_cparams(("parallel", "arbitrary")),
        name="moe_experts",
    )(xn, comb, wg, wu, wd)


def _ple_kernel(h1_ref, moe_ref, p_ref, wup_ref, wgate_ref, g_ref, o_ref):
    h2 = h1_ref[...] + moe_ref[...]
    e = _rms(_dot(p_ref[...].astype(BF16), wup_ref[...]), g_ref[...])
    gate = jax.nn.sigmoid(_dot(h2.astype(BF16), wgate_ref[...]))
    o_ref[...] = h2 + e * gate


def _ple(h1, moe, p, wup, wgate, g, tm=512):
    n = h1.shape[0]
    row = lambda w: pl.BlockSpec((tm, w), lambda i: (i, 0))
    return pl.pallas_call(
        _ple_kernel,
        grid=(n // tm,),
        in_specs=[row(D_MODEL), row(D_MODEL), row(PLE_DIM), _full(wup.shape), _full(wgate.shape),
                  _full(g.shape)],
        out_specs=row(D_MODEL),
        out_shape=jax.ShapeDtypeStruct((n, D_MODEL), F32),
        compiler_params=_cparams(("parallel",)),
        name="ple",
    )(h1, moe, p, wup, wgate, g)


def _odd_in_kernel(h_ref, g_ref, wrow_ref, wkt_ref, cos_ref, sin_ref, cost_ref, sint_ref,
                   q_ref, v_ref, gate_ref, kt_ref):
    tm = h_ref.shape[0]
    hn = _rms(h_ref[...], g_ref[...]).astype(BF16)
    cos = cos_ref[...]
    sin = sin_ref[...]
    half = RET_QK_DIM // 2
    for hd in range(RET_HEADS):
        sl = slice(hd * RET_QK_DIM, (hd + 1) * RET_QK_DIM)
        qh = _dot(hn, wrow_ref[:, sl])
        q_ref[:, sl] = (qh * cos + pltpu.roll(qh, half, 1) * sin).astype(BF16)
    v_ref[...] = _dot(hn, wrow_ref[:, RET_QK_WIDTH:RET_QK_WIDTH + RET_V_WIDTH]).astype(BF16)
    gate_ref[...] = _dot(hn, wrow_ref[:, RET_QK_WIDTH + RET_V_WIDTH:]).astype(BF16)

    cost = cost_ref[...] * (RET_QK_DIM ** -0.5)
    sint = sint_ref[...] * (RET_QK_DIM ** -0.5)
    for hd in range(RET_HEADS):
        sl = slice(hd * RET_QK_DIM, (hd + 1) * RET_QK_DIM)
        kh = _dot_nt(wkt_ref[sl, :], hn)
        swapped = jnp.concatenate([kh[half:], kh[:half]], axis=0)
        kt_ref[sl, :] = (kh * cost + swapped * sint).astype(BF16)


def _odd_in(h, g, wrow, wkt, cos, sin, cost, sint, seq, tm=256):
    n = h.shape[0]
    nt = seq // tm
    row = lambda w: pl.BlockSpec((tm, w), lambda i: (i, 0))
    return pl.pallas_call(
        _odd_in_kernel,
        grid=(n // tm,),
        in_specs=[row(D_MODEL), _full(g.shape), _full(wrow.shape), _full(wkt.shape),
                  pl.BlockSpec((tm, RET_QK_DIM), lambda i: (i % nt, 0)),
                  pl.BlockSpec((tm, RET_QK_DIM), lambda i: (i % nt, 0)),
                  pl.BlockSpec((RET_QK_DIM, tm), lambda i: (0, i % nt)),
                  pl.BlockSpec((RET_QK_DIM, tm), lambda i: (0, i % nt))],
        out_specs=[row(RET_QK_WIDTH), row(RET_V_WIDTH), row(RET_V_WIDTH),
                   pl.BlockSpec((RET_QK_WIDTH, tm), lambda i: (0, i))],
        out_shape=[jax.ShapeDtypeStruct((n, RET_QK_WIDTH), BF16), jax.ShapeDtypeStruct((n, RET_V_WIDTH), BF16),
                   jax.ShapeDtypeStruct((n, RET_V_WIDTH), BF16), jax.ShapeDtypeStruct((RET_QK_WIDTH, n), BF16)],
        compiler_params=_cparams(("parallel",)),
        name="odd_in_proj",
    )(h, g, wrow, wkt, cos, sin, cost, sint)


def _ret_kernel(cdec_ref, q_ref, kt_ref, v_ref, gate_ref, dintra_ref, qdec_ref, kdec_ref, gn_ref,
                y_ref, state_ref):
    @pl.when(pl.program_id(1) == 0)
    def _():
        state_ref[...] = jnp.zeros_like(state_ref)

    for h in range(RET_HEADS):
        ks = slice(h * RET_QK_DIM, (h + 1) * RET_QK_DIM)
        vs = slice(h * RET_V_DIM, (h + 1) * RET_V_DIM)
        qh = q_ref[:, ks]
        kth = kt_ref[ks, :]
        vh = v_ref[:, vs]
        state = state_ref[h]
        inner = (_dot(qh, kth) * dintra_ref[h]).astype(BF16)
        o = _dot(inner, vh) + _dot(qh, state.astype(BF16)) * qdec_ref[h]
        kd = (kth.astype(F32) * kdec_ref[h]).astype(BF16)
        state_ref[h] = state * cdec_ref[h] + _dot(kd, vh)
        mu = jnp.mean(o, axis=-1, keepdims=True)
        oc = o - mu
        var = jnp.mean(oc * oc, axis=-1, keepdims=True)
        on = oc * lax.rsqrt(var + GN_EPS) * gn_ref[:, vs]
        y_ref[:, vs] = (jax.nn.silu(gate_ref[:, vs].astype(F32)) * on).astype(BF16)


def _retention(cdec, q, kt, v, gate, dintra, qdec, kdec, gn, batch, seq):
    n = q.shape[0]
    c = RET_TILE
    nc = seq // c
    row = lambda w: pl.BlockSpec((c, w), lambda b, t: (b * nc + t, 0))
    return pl.pallas_call(
        _ret_kernel,
        grid=(batch, nc),
        in_specs=[pl.BlockSpec(memory_space=pltpu.SMEM),
                  row(RET_QK_WIDTH),
                  pl.BlockSpec((RET_QK_WIDTH, c), lambda b, t: (0, b * nc + t)),
                  row(RET_V_WIDTH), row(RET_V_WIDTH),
                  _full(dintra.shape), _full(qdec.shape), _full(kdec.shape), _full(gn.shape)],
        out_specs=row(RET_V_WIDTH),
        out_shape=jax.ShapeDtypeStruct((n, RET_V_WIDTH), BF16),
        scratch_shapes=[pltpu.VMEM((RET_HEADS, RET_QK_DIM, RET_V_DIM), F32)],
        compiler_params=_cparams(("parallel", "arbitrary")),
        name="retention",
    )(cdec, q, kt, v, gate, dintra, qdec, kdec, gn)


def _t5_bucket(rel):
    half = N_BUCKETS // 2
    max_exact = half // 2
    ret = (rel > 0).astype(I32) * half
    n = jnp.abs(rel)
    nf = jnp.maximum(n, 1).astype(F32)
    large = max_exact + (jnp.log(nf / max_exact) / math.log(MAX_DISTANCE / max_exact)
                         * (half - max_exact)).astype(I32)
    large = jnp.minimum(large, half - 1)
    return ret + jnp.where(n < max_exact, n, large)


def _bias_tables(rel_bias):
    r = jnp.arange(Q_TILE, dtype=I32)[:, None]
    c = jnp.arange(Q_TILE, dtype=I32)[None, :]
    rels = jnp.stack([c - Q_TILE - r, c - r])
    onehot = jax.nn.one_hot(_t5_bucket(rels), N_BUCKETS, dtype=F32)
    tab = jnp.einsum("abcn,nh->ahbc", onehot, rel_bias, precision=lax.Precision.HIGHEST)
    far = rel_bias[_t5_bucket(jnp.int32(-2 * Q_TILE))]
    return tab.astype(F32), far.astype(F32)


def _block_diag(w):
    nb, bs, _ = w.shape
    eye = jnp.eye(nb, dtype=w.dtype)
    return jnp.einsum("hij,hg->higj", w, eye).reshape(nb * bs, nb * bs)


def _router_weights(w_group, w_expert):
    wr = jnp.concatenate([w_expert, w_group, jnp.zeros((D_MODEL, LANES - N_EXPERTS - N_GROUPS), F32)], axis=1)
    hi = wr.astype(BF16)
    lo = (wr - hi.astype(F32)).astype(BF16)
    return hi, lo


def _rotary_tables(seq):
    half = RET_QK_DIM // 2
    inv = ROPE_BASE ** (-jnp.arange(half, dtype=F32) / half)
    ang = jnp.arange(seq, dtype=F32)[:, None] * inv[None, :]
    cos, sin = jnp.cos(ang), jnp.sin(ang)
    cos_row = jnp.concatenate([cos, cos], axis=1)
    sin_row = jnp.concatenate([-sin, sin], axis=1)
    cos_t = jnp.concatenate([cos.T, cos.T], axis=0)
    sin_t = jnp.concatenate([-sin.T, sin.T], axis=0)
    return cos_row, sin_row, cos_t, sin_t


def _retention_tables():
    c = RET_TILE
    log_g = jnp.log(1.0 - 2.0 ** (-5.0 - jnp.arange(RET_HEADS, dtype=F32)))
    pos = jnp.arange(c, dtype=F32)
    diff = pos[:, None] - pos[None, :]
    causal = diff >= 0
    dintra = jnp.where(causal[None], jnp.exp(jnp.where(causal, diff, 0.0)[None] * log_g[:, None, None]), 0.0)
    qdec = jnp.exp((pos + 1.0)[None, :, None] * log_g[:, None, None])
    kdec = jnp.exp((c - 1.0 - pos)[None, None, :] * log_g[:, None, None])
    cdec = jnp.exp(c * log_g)
    return dintra, qdec, kdec, cdec


def kernel(x, p, rel_bias, mix_norm_g, ffn_norm_g, ple_norm_g, ev_w_in, ev_conv_w, ev_conv_b, ev_lru_wa, ev_lru_ba, ev_lru_wx, ev_lru_bx, ev_lru_lambda, ev_q_norm_g, ev_k_norm_g, ev_w_out, od_w_in, od_gn_g, od_w_out, moe_w_group, moe_w_expert, moe_w_gate, moe_w_up, moe_w_down, ple_w_up, ple_w_gate):
    batch, seq, _ = x.shape
    n = batch * seq
    depth = p.shape[0]
    h = x.reshape(n, D_MODEL)
    row = lambda a: a.reshape(1, -1)

    tab, far = _bias_tables(rel_bias)
    cos_row, sin_row, cos_t, sin_t = _rotary_tables(seq)
    dintra, qdec, kdec, cdec = _retention_tables()
    head_ones = _block_diag(jnp.ones((ATT_HEADS, ATT_HEAD_DIM, ATT_HEAD_DIM), BF16))

    for i in range(depth):
        jdx = i // 2
        if i % 2 == 0:
            w = ev_w_in[jdx]
            o = [0, 512, 1024, 1536, 2048, 2560, 3072, 3136, 3144]
            xa_w, ga_w, q_w, k_w, v_w, iq_w, ik_w, iw_w = [w[:, o[t]:o[t + 1]] for t in range(8)]
            wrow = jnp.concatenate([xa_w, ga_w, q_w, v_w, iq_w, iw_w,
                                    jnp.zeros((D_MODEL, LANES - IDX_HEADS), F32)], axis=1).astype(BF16)
            wkt = k_w.T.astype(BF16)
            wikt = jnp.concatenate([ik_w.T, ik_w.T], axis=0).astype(BF16)
            qg = jnp.tile(ev_q_norm_g[jdx], ATT_HEADS).reshape(1, -1) * (ATT_HEAD_DIM ** -0.5)
            kg = ev_k_norm_g[jdx].reshape(-1, 1)
            xg, q, v, iq, iw, kt, ikt = _even_in(h, row(mix_norm_g[i]), wrow, wkt, wikt, head_ones, qg, kg)
            ya = _rglru(xg, ev_conv_w[jdx], row(ev_conv_b[jdx]),
                        _block_diag(ev_lru_wa[jdx]).astype(BF16), row(ev_lru_ba[jdx]),
                        _block_diag(ev_lru_wx[jdx]).astype(BF16), row(ev_lru_bx[jdx]),
                        row(ev_lru_lambda[jdx]), batch, seq)
            yb = _dsa(far, q, iq, iw, kt, v, ikt, tab, batch, seq)
            wo = ev_w_out[jdx].astype(BF16)
            ys, ws = [ya, yb], [wo[:LRU_WIDTH], wo[LRU_WIDTH:]]
        else:
            w = od_w_in[jdx]
            wrow = jnp.concatenate([w[:, :RET_QK_WIDTH], w[:, 2 * RET_QK_WIDTH:]], axis=1).astype(BF16)
            wkt = w[:, RET_QK_WIDTH:2 * RET_QK_WIDTH].T.astype(BF16)
            q, v, gate, kt = _odd_in(h, row(mix_norm_g[i]), wrow, wkt, cos_row, sin_row, cos_t, sin_t, seq)
            yc = _retention(cdec, q, kt, v, gate, dintra, qdec, kdec, row(od_gn_g[jdx]), batch, seq)
            ys, ws = [yc], [od_w_out[jdx].astype(BF16)]
        wr_hi, wr_lo = _router_weights(moe_w_group[i], moe_w_expert[i])
        h1, xn, comb = _mix_out(h, ys, ws, row(ffn_norm_g[i]), wr_hi, wr_lo)
        moe = _moe(xn, comb, moe_w_gate[i], moe_w_up[i], moe_w_down[i])
        h = _ple(h1, moe, p[i].reshape(n, PLE_DIM), ple_w_up[i].astype(BF16), ple_w_gate[i].astype(BF16),
                 row(ple_norm_g[i]))
    return h.reshape(batch, seq, D_MODEL)
```

```python
import functools
import math

import jax
import jax.numpy as jnp
from jax import lax
from jax.experimental import pallas as pl
from jax.experimental.pallas import tpu as pltpu
from jax.experimental.pallas import tpu_sc as plsc

F32 = jnp.float32
BF16 = jnp.bfloat16
I32 = jnp.int32

D_MODEL = 1024
CHUNK = 64
PLE_DIM = 256
RMS_EPS = 1e-6

LRU_WIDTH = 512
LRU_BLOCKS = 8
LRU_C = 8.0

ATT_HEADS = 8
ATT_HEAD_DIM = 64
ATT_WIDTH = 512
IDX_HEADS = 8
IDX_DIM = 64
MAX_TOP_K = 256
N_BUCKETS = 32
MAX_DISTANCE = 128

RET_HEADS = 8
RET_QK_DIM = 128
RET_V_DIM = 256
RET_QK_WIDTH = 1024
RET_V_WIDTH = 2048
ROPE_BASE = 10000.0
GN_EPS = 1e-5

N_GROUPS = 4
EXPERTS_PER_GROUP = 8
N_EXPERTS = 32
EXPERT_FF = 512

LANES = 128
INT_MIN = -(2 ** 31)
NEG_BIG = -1e30
VMEM_LIMIT = 56 * 1024 * 1024

SC_CORES = 2
SC_SUBCORES = 16
SC_CHUNK = 64

Q_TILE = 128
K_TILE = 512
RET_TILE = 256


def _cparams(sem):
    return pltpu.CompilerParams(dimension_semantics=sem, vmem_limit_bytes=VMEM_LIMIT)


def _full(shape):
    nd = len(shape)
    return pl.BlockSpec(shape, lambda *_: (0,) * nd)


def _rms(xf, g):
    return xf * lax.rsqrt(jnp.mean(xf * xf, axis=-1, keepdims=True) + RMS_EPS) * g


def _dot(a, b):
    return jnp.dot(a, b, preferred_element_type=F32)


def _dot_nt(a, b):
    return lax.dot_general(a, b, (((1,), (1,)), ((), ())), preferred_element_type=F32)


def _neg_expm1(x):
    series = -x * (1.0 + x * (0.5 + x * (1.0 / 6.0 + x * (1.0 / 24.0 + x * (1.0 / 120.0)))))
    return jnp.where(x > -0.03, series, 1.0 - jnp.exp(x))


def _pack_halves(x):
    w = x.shape[1] // 2
    hi = pltpu.bitcast(x[:, :w].astype(BF16).astype(F32), I32)
    lo = pltpu.bitcast(x[:, w:].astype(BF16).astype(F32), I32)
    return hi | lax.shift_right_logical(lo, 16)


def _unpack_halves(p):
    hi = pltpu.bitcast(p & jnp.int32(-65536), F32)
    lo = pltpu.bitcast(lax.shift_left(p, 16), F32)
    return hi, lo


def _split_bf16(x):
    hi = x.astype(BF16)
    lo = (x - hi.astype(F32)).astype(BF16)
    return hi, lo


def _even_in_kernel(h_ref, g_ref, wrow_ref, wkt_ref, wikt_ref, bd_ref, qg_ref, kg_ref,
                    xg_ref, q_ref, v_ref, iq_ref, iw_ref, kt_ref, ikt_ref):
    tm = h_ref.shape[0]
    hn = _rms(h_ref[...], g_ref[...]).astype(BF16)

    def proj(a, b):
        return _dot(hn, wrow_ref[:, a:b])

    xg_ref[...] = proj(0, 1024)
    qf = proj(1024, 1536)
    hi, lo = _split_bf16(qf * qf)
    ss = _dot(hi, bd_ref[...]) + _dot(lo, bd_ref[...])
    q_ref[...] = (qf * lax.rsqrt(ss * (1.0 / ATT_HEAD_DIM) + RMS_EPS) * qg_ref[...]).astype(BF16)
    v_ref[...] = proj(1536, 2048).astype(BF16)
    iq_ref[...] = proj(2048, 2560).astype(BF16)
    iw_ref[...] = proj(2560, 2688)[:, :IDX_HEADS] * (IDX_HEADS ** -0.5 * IDX_DIM ** -0.5)

    kt = _dot_nt(wkt_ref[...], hn)
    k3 = kt.reshape(ATT_HEADS, ATT_HEAD_DIM, tm)
    ssk = jnp.sum(k3 * k3, axis=1, keepdims=True)
    kn = k3 * lax.rsqrt(ssk * (1.0 / ATT_HEAD_DIM) + RMS_EPS) * kg_ref[...][None]
    kt_ref[...] = kn.reshape(ATT_WIDTH, tm).astype(BF16)
    ikt_ref[...] = _dot_nt(wikt_ref[...], hn).astype(BF16)


def _even_in(h, g, wrow, wkt, wikt, bd, qg, kg, tm=256):
    n = h.shape[0]
    row = lambda w: pl.BlockSpec((tm, w), lambda i: (i, 0))
    col = lambda r: pl.BlockSpec((r, tm), lambda i: (0, i))
    return pl.pallas_call(
        _even_in_kernel,
        grid=(n // tm,),
        in_specs=[row(D_MODEL), _full(g.shape), _full(wrow.shape), _full(wkt.shape), _full(wikt.shape),
                  _full(bd.shape), _full(qg.shape), _full(kg.shape)],
        out_specs=[row(1024), row(512), row(512), row(512), row(IDX_HEADS), col(512), col(128)],
        out_shape=[jax.ShapeDtypeStruct((n, 1024), F32), jax.ShapeDtypeStruct((n, 512), BF16),
                   jax.ShapeDtypeStruct((n, 512), BF16), jax.ShapeDtypeStruct((n, 512), BF16),
                   jax.ShapeDtypeStruct((n, IDX_HEADS), F32), jax.ShapeDtypeStruct((512, n), BF16),
                   jax.ShapeDtypeStruct((128, n), BF16)],
        compiler_params=_cparams(("parallel",)),
        name="even_in_proj",
    )(h, g, wrow, wkt, wikt, bd, qg, kg)


def _rglru_kernel(xg_ref, cw_ref, cb_ref, wa_ref, ba_ref, wx_ref, bx_ref, lam_ref, ya_ref,
                  tail_ref, hst_ref):
    ts = xg_ref.shape[0]
    w = LRU_WIDTH

    @pl.when(pl.program_id(1) == 0)
    def _():
        tail_ref[...] = jnp.zeros_like(tail_ref)
        hst_ref[...] = jnp.zeros_like(hst_ref)

    xa = xg_ref[:, :w]
    ga = xg_ref[:, w:]
    row = lax.broadcasted_iota(I32, (ts, w), 0)
    tail = tail_ref[...]
    xc = xa * cw_ref[3:4, :] + cb_ref[...]
    for d in (1, 2, 3):
        cur = pltpu.roll(xa, d, 0)
        prev = jnp.concatenate([pltpu.roll(tail, d, 0), jnp.zeros((ts - 8, w), F32)], axis=0)
        xc = xc + jnp.where(row < d, prev, cur) * cw_ref[3 - d:4 - d, :]
    tail_ref[...] = xa[ts - 8:, :]

    xcb = xc.astype(BF16)
    r = jax.nn.sigmoid(_dot(xcb, wa_ref[...]) + ba_ref[...])
    gi = jax.nn.sigmoid(_dot(xcb, wx_ref[...]) + bx_ref[...])
    nl = -lam_ref[...]
    softplus = jnp.maximum(nl, 0.0) + jnp.log1p(jnp.exp(-jnp.abs(nl)))
    log_a = (-LRU_C) * r * softplus
    a = jnp.exp(log_a)
    u = jnp.sqrt(_neg_expm1(2.0 * log_a)) * (gi * xc)

    d = 1
    while d < ts:
        keep = row >= d
        a_sh = jnp.where(keep, pltpu.roll(a, d, 0), 1.0)
        u_sh = jnp.where(keep, pltpu.roll(u, d, 0), 0.0)
        u = a * u_sh + u
        a = a * a_sh
        d *= 2
    hseq = a * hst_ref[0:1, :] + u
    hst_ref[0:1, :] = hseq[ts - 1:ts, :]
    ya_ref[...] = (hseq * jax.nn.gelu(ga)).astype(BF16)


def _rglru(xg, cw, cb, wa, ba, wx, bx, lam, batch, seq, ts=256):
    n = xg.shape[0]
    nt = seq // ts
    return pl.pallas_call(
        _rglru_kernel,
        grid=(batch, nt),
        in_specs=[pl.BlockSpec((ts, 1024), lambda b, t: (b * nt + t, 0)),
                  _full(cw.shape), _full(cb.shape), _full(wa.shape), _full(ba.shape),
                  _full(wx.shape), _full(bx.shape), _full(lam.shape)],
        out_specs=pl.BlockSpec((ts, LRU_WIDTH), lambda b, t: (b * nt + t, 0)),
        out_shape=jax.ShapeDtypeStruct((n, LRU_WIDTH), BF16),
        scratch_shapes=[pltpu.VMEM((8, LRU_WIDTH), F32), pltpu.VMEM((8, LRU_WIDTH), F32)],
        compiler_params=_cparams(("parallel", "arbitrary")),
        name="rglru",
    )(xg, cw, cb, wa, ba, wx, bx, lam)


def _dsa_kernel(far_ref, q_ref, iq_ref, iw_ref, kt_ref, v_ref, ikt_ref, tab_ref, o_ref,
                keys_ref, sel_ref, cut_ref, *, seq, top_k):
    tq = Q_TILE
    j = pl.program_id(1)
    q0 = j * tq
    nkt = (j + K_TILE // tq) // (K_TILE // tq)
    lane_q = lax.broadcasted_iota(I32, (tq, LANES), 1)
    low_half = lane_q < ATT_HEAD_DIM
    qrow = q0 + lax.broadcasted_iota(I32, (tq, 1), 0)
    qlim = (qrow // CHUNK + 1) * CHUNK

    def head_pair(ref, h):
        pair = ref[:, (h // 2) * LANES:(h // 2 + 1) * LANES]
        mine = low_half if h % 2 == 0 else jnp.logical_not(low_half)
        return jnp.where(mine, pair, jnp.zeros_like(pair))

    def ktile(kt):
        return pl.multiple_of(kt * K_TILE, K_TILE)

    iq_m = [head_pair(iq_ref, h) for h in range(IDX_HEADS)]

    def score_body(kt, c):
        off = ktile(kt)
        ik = ikt_ref[:, pl.ds(off, K_TILE)]
        s = jnp.zeros((tq, K_TILE), F32)
        for h in range(IDX_HEADS):
            s = s + jnp.maximum(_dot(iq_m[h], ik), 0.0) * iw_ref[:, h:h + 1]
        s = jnp.where(s == 0.0, 0.0, s)
        bits = pltpu.bitcast(s, I32)
        key = bits ^ ((bits >> 31) & 0x7FFFFFFF)
        kpos = off + lax.broadcasted_iota(I32, (tq, K_TILE), 1)
        keys_ref[:, pl.ds(off, K_TILE)] = jnp.where(kpos < qlim, key, INT_MIN)
        return c

    lax.fori_loop(0, nkt, score_body, 0)

    def count(pred):
        def body(kt, acc):
            off = ktile(kt)
            kk = keys_ref[:, pl.ds(off, K_TILE)]
            kpos = off + lax.broadcasted_iota(I32, (tq, K_TILE), 1)
            ind = pred(kk, kpos)
            for c in range(K_TILE // LANES):
                acc = acc + ind[:, c * LANES:(c + 1) * LANES]
            return acc
        acc = lax.fori_loop(0, nkt, body, jnp.zeros((tq, LANES), F32))
        return jnp.sum(acc, axis=1, keepdims=True)

    kf = float(top_k)

    def search_body(i, ans):
        cand = ans + jnp.left_shift(jnp.int32(1), 31 - i)
        cnt = count(lambda kk, kpos: jnp.where(kk >= cand, 1.0, 0.0))
        return jnp.where(cnt >= kf, cand, ans)

    ans = lax.fori_loop(0, 32, search_body, jnp.full((tq, 1), INT_MIN, I32))

    cnt_ge = count(lambda kk, kpos: jnp.where(kk >= ans, 1.0, 0.0))
    excess = jnp.where(ans > INT_MIN, cnt_ge - kf, 0.0)
    cut_ref[...] = jnp.full(cut_ref.shape, seq, I32)

    @pl.when(jnp.max(excess) > 0.0)
    def _():
        need = kf - count(lambda kk, kpos: jnp.where(kk > ans, 1.0, 0.0))

        def idx_body(i, pos):
            cand = pos + jnp.left_shift(jnp.int32(1), (seq.bit_length() - 2) - i)
            c = count(lambda kk, kpos: jnp.where(kk == ans, jnp.where(kpos < cand, 1.0, 0.0), 0.0))
            return jnp.where(c < need, cand, pos)

        pos = lax.fori_loop(0, seq.bit_length() - 1, idx_body, jnp.zeros((tq, 1), I32))
        cut_ref[...] = jnp.broadcast_to(pos, cut_ref.shape)

    thr = jnp.maximum(ans, INT_MIN + 1)
    cut = cut_ref[:, 0:1]

    near_w = 2 * tq
    near_start = pl.multiple_of(jnp.maximum(j - 1, 0) * tq, tq)

    def selection(kk, kpos):
        tie = jnp.where(kpos <= cut, 0.0, NEG_BIG)
        return jnp.where(kk > thr, 0.0, jnp.where(kk == thr, tie, NEG_BIG))

    def sel_body(kt, c):
        off = ktile(kt)
        kpos = off + lax.broadcasted_iota(I32, (tq, K_TILE), 1)
        s = selection(keys_ref[:, pl.ds(off, K_TILE)], kpos)
        sel_ref[:, pl.ds(off, K_TILE)] = jnp.where(kpos < near_start, s, NEG_BIG)
        return c

    n_far = (jnp.maximum(j - 1, 0) + K_TILE // tq - 1) // (K_TILE // tq)
    lax.fori_loop(0, n_far, sel_body, 0)
    sel_near = selection(keys_ref[:, pl.ds(near_start, near_w)],
                         near_start + lax.broadcasted_iota(I32, (tq, near_w), 1))

    def step(qm, ktb, vb, add, carry):
        m, l, acc = carry
        s = _dot(qm, ktb) + add
        m_new = jnp.maximum(m, jnp.max(s, axis=1, keepdims=True))
        alpha = jnp.exp(m - m_new)
        p = jnp.exp(s - m_new)
        l = alpha * l + jnp.sum(p, axis=1, keepdims=True)
        acc = alpha * acc + _dot(p.astype(BF16), vb)
        return m_new, l, acc

    qms = [head_pair(q_ref, h) for h in range(ATT_HEADS)]
    pairs = [slice((h // 2) * LANES, (h // 2 + 1) * LANES) for h in range(ATT_HEADS)]

    def far_body(kt, carries):
        off = ktile(kt)
        sel = sel_ref[:, pl.ds(off, K_TILE)]
        return tuple(
            step(qms[h], kt_ref[pairs[h], pl.ds(off, K_TILE)], v_ref[pl.ds(off, K_TILE), pairs[h]],
                 sel + far_ref[h], carries[h])
            for h in range(ATT_HEADS))

    init = (jnp.full((tq, 1), NEG_BIG, F32), jnp.zeros((tq, 1), F32), jnp.zeros((tq, LANES), F32))
    carries = lax.fori_loop(0, n_far, far_body, (init,) * ATT_HEADS)

    first = jnp.where(j == 0, 1, 0)
    res = []
    for h in range(ATT_HEADS):
        bias = jnp.concatenate([tab_ref[first, h], tab_ref[1, h]], axis=1)
        m, l, acc = step(qms[h], kt_ref[pairs[h], pl.ds(near_start, near_w)],
                         v_ref[pl.ds(near_start, near_w), pairs[h]], sel_near + bias, carries[h])
        res.append(acc / l)

    for p in range(ATT_HEADS // 2):
        o_ref[:, p * LANES:(p + 1) * LANES] = jnp.where(low_half, res[2 * p], res[2 * p + 1]).astype(BF16)


def _dsa(far, q, iq, iw, kt, v, ikt, tab, batch, seq):
    n = q.shape[0]
    nb = seq // Q_TILE
    top_k = min(MAX_TOP_K, seq // 4)
    qspec = lambda w: pl.BlockSpec((Q_TILE, w), lambda b, j: (b * nb + j, 0))
    return pl.pallas_call(
        functools.partial(_dsa_kernel, seq=seq, top_k=top_k),
        grid=(batch, nb),
        in_specs=[pl.BlockSpec(memory_space=pltpu.SMEM),
                  qspec(512), qspec(512), qspec(IDX_HEADS),
                  pl.BlockSpec((512, seq), lambda b, j: (0, b)),
                  pl.BlockSpec((seq, 512), lambda b, j: (b, 0)),
                  pl.BlockSpec((128, seq), lambda b, j: (0, b)),
                  _full(tab.shape)],
        out_specs=qspec(512),
        out_shape=jax.ShapeDtypeStruct((n, 512), BF16),
        scratch_shapes=[pltpu.VMEM((Q_TILE, seq), I32), pltpu.VMEM((Q_TILE, seq), F32),
                        pltpu.VMEM((Q_TILE, LANES), I32)],
        compiler_params=_cparams(("parallel", "arbitrary")),
        name="dsa",
    )(far, q, iq, iw, kt, v, ikt, tab)


def _mix_out_kernel(*refs, n_y):
    h_ref = refs[0]
    y_refs = refs[1:1 + n_y]
    w_refs = refs[1 + n_y:1 + 2 * n_y]
    g_ref, wr_hi_ref, wr_lo_ref = refs[1 + 2 * n_y:4 + 2 * n_y]
    h1_ref, xn_ref, route_ref = refs[4 + 2 * n_y:]
    tm = h_ref.shape[0]

    mix = _dot(y_refs[0][...], w_refs[0][...])
    for y_ref, w_ref in zip(y_refs[1:], w_refs[1:]):
        mix = mix + _dot(y_ref[...], w_ref[...])
    h1 = h_ref[...] + mix
    h1_ref[...] = h1
    xn = _rms(h1, g_ref[...])
    hi, lo = _split_bf16(xn)
    xn_ref[...] = _pack_halves(xn)
    logits = _dot(hi, wr_hi_ref[...]) + _dot(lo, wr_hi_ref[...]) + _dot(hi, wr_lo_ref[...])

    lane = lax.broadcasted_iota(I32, (tm, LANES), 1)
    big = jnp.int32(LANES)
    ninf = -jnp.inf
    is_g = (lane >= N_EXPERTS) & (lane < N_EXPERTS + N_GROUPS)
    glog = jnp.where(is_g, logits, ninf)
    gmax = jnp.max(glog, axis=1, keepdims=True)
    gsel = jnp.min(jnp.where(glog == gmax, lane, big), axis=1, keepdims=True) - N_EXPERTS
    gprob = 1.0 / jnp.sum(jnp.exp(glog - gmax), axis=1, keepdims=True)
    lo_l = gsel * EXPERTS_PER_GROUP
    within = jnp.where((lane >= lo_l) & (lane < lo_l + EXPERTS_PER_GROUP), logits, ninf)
    v1 = jnp.max(within, axis=1, keepdims=True)
    i1 = jnp.min(jnp.where(within == v1, lane, big), axis=1, keepdims=True)
    rest = jnp.where(lane == i1, ninf, within)
    v2 = jnp.max(rest, axis=1, keepdims=True)
    i2 = jnp.min(jnp.where(rest == v2, lane, big), axis=1, keepdims=True)
    e2 = jnp.exp(v2 - v1)
    w1 = gprob / (1.0 + e2)
    w2 = gprob * e2 / (1.0 + e2)
    route_ref[...] = jnp.where(lane == 0, i1.astype(F32), jnp.where(lane == 1, i2.astype(F32),
                               jnp.where(lane == 2, w1, jnp.where(lane == 3, w2, 0.0))))


def _mix_out(h, ys, ws, g, wr_hi, wr_lo, tm=512):
    n = h.shape[0]
    n_y = len(ys)
    row = lambda w: pl.BlockSpec((tm, w), lambda i: (i, 0))
    return pl.pallas_call(
        functools.partial(_mix_out_kernel, n_y=n_y),
        grid=(n // tm,),
        in_specs=[row(D_MODEL)] + [row(y.shape[1]) for y in ys] + [_full(w.shape) for w in ws]
                 + [_full(g.shape), _full(wr_hi.shape), _full(wr_lo.shape)],
        out_specs=[row(D_MODEL), row(D_MODEL // 2), row(LANES)],
        out_shape=[jax.ShapeDtypeStruct((n, D_MODEL), F32), jax.ShapeDtypeStruct((n, D_MODEL // 2), I32),
                   jax.ShapeDtypeStruct((n, LANES), F32)],
        compiler_params=_cparams(("parallel",)),
        name="mix_out_router",
    )(h, *ys, *ws, g, wr_hi, wr_lo)


def _sc_rows(src, idx, scatter):
    r, d = idx.shape[0], src.shape[1]
    nw = SC_CORES * SC_SUBCORES
    per_w = r // nw
    n_chunks = per_w // SC_CHUNK
    idx3 = idx.reshape(nw, n_chunks, SC_CHUNK)
    mesh = plsc.VectorSubcoreMesh(core_axis_name="c", subcore_axis_name="s")

    @functools.partial(
        pl.kernel, mesh=mesh, out_type=jax.ShapeDtypeStruct((r, d), src.dtype),
        scratch_types=[pltpu.VMEM((n_chunks, SC_CHUNK), I32), pltpu.VMEM((SC_CHUNK, d), src.dtype),
                       pltpu.SemaphoreType.DMA])
    def permute(src_hbm, idx_hbm, out_hbm, idx_v, rows_v, sem):
        wid = lax.axis_index("s") * SC_CORES + lax.axis_index("c")
        pltpu.sync_copy(idx_hbm.at[wid], idx_v)

        @pl.loop(0, n_chunks)
        def _(c):
            lin = pl.ds(pl.multiple_of(wid * per_w + c * SC_CHUNK, SC_CHUNK), SC_CHUNK)
            if scatter:
                pltpu.sync_copy(src_hbm.at[lin], rows_v)
                pltpu.async_copy(rows_v, out_hbm.at[idx_v.at[c]], sem).wait()
            else:
                pltpu.async_copy(src_hbm.at[idx_v.at[c]], rows_v, sem).wait()
                pltpu.sync_copy(rows_v, out_hbm.at[lin])

    return permute(src, idx3)


def _moe_kernel(tile_ref, exp_ref, flag_ref, off_ref, xs_ref, wg_ref, wu_ref, wd_ref, ys_ref,
                wgb_ref, wub_ref, wdb_ref, acc_ref):
    w = pl.program_id(0)
    tm = xs_ref.shape[0]
    e = exp_ref[w]
    flags = flag_ref[w]

    @pl.when((flags & 4) != 0)
    def _():
        wgb_ref[...] = wg_ref[0].astype(BF16)
        wub_ref[...] = wu_ref[0].astype(BF16)
        wdb_ref[...] = wd_ref[0].astype(BF16)

    @pl.when((flags & 1) != 0)
    def _():
        xa, xb = _unpack_halves(xs_ref[...])
        xa = xa.astype(BF16)
        xb = xb.astype(BF16)
        half = D_MODEL // 2
        gate = _dot(xa, wgb_ref[:half, :]) + _dot(xb, wgb_ref[half:, :])
        up = _dot(xa, wub_ref[:half, :]) + _dot(xb, wub_ref[half:, :])
        hid = (jax.nn.silu(gate) * up).astype(BF16)
        y = _dot(hid, wdb_ref[...])
        rows = tile_ref[w] * tm + lax.broadcasted_iota(I32, (tm, 1), 0)
        mine = (rows >= off_ref[e]) & (rows < off_ref[e + 1])
        y = jnp.where(mine, y, 0.0)

        @pl.when((flags & 2) != 0)
        def _():
            acc_ref[...] = y

        @pl.when((flags & 2) == 0)
        def _():
            acc_ref[...] += y

        ys_ref[...] = _pack_halves(acc_ref[...])


def _moe_plan(sorted_e, tm):
    n_rows = sorted_e.shape[0]
    nt = n_rows // tm
    n_items = nt + N_EXPERTS - 1
    off = jnp.searchsorted(sorted_e, jnp.arange(N_EXPERTS + 1, dtype=I32), side="left").astype(I32)
    e_lo = sorted_e[::tm]
    e_hi = sorted_e[tm - 1::tm]
    span = e_hi - e_lo + 1
    start = jnp.cumsum(span) - span
    total = jnp.sum(span)
    w = jnp.arange(n_items, dtype=I32)
    valid = w < total
    tile = jnp.clip(jnp.searchsorted(start, w, side="right").astype(I32) - 1, 0, nt - 1)
    expert = jnp.where(valid, e_lo[tile] + (w - start[tile]), e_hi[nt - 1])
    first = valid & (w == start[tile])
    new_e = jnp.concatenate([jnp.ones((1,), bool), expert[1:] != expert[:-1]])
    flags = valid.astype(I32) + 2 * first.astype(I32) + 4 * new_e.astype(I32)
    return tile, expert, flags, off


def _moe(xs, tile, expert, flags, off, wg, wu, wd, tm):
    n_rows, half = xs.shape
    n_items = tile.shape[0]
    grid_spec = pltpu.PrefetchScalarGridSpec(
        num_scalar_prefetch=4,
        grid=(n_items,),
        in_specs=[pl.BlockSpec((tm, half), lambda w, t, e, f, o: (t[w], 0)),
                  pl.BlockSpec((1, D_MODEL, EXPERT_FF), lambda w, t, e, f, o: (e[w], 0, 0)),
                  pl.BlockSpec((1, D_MODEL, EXPERT_FF), lambda w, t, e, f, o: (e[w], 0, 0)),
                  pl.BlockSpec((1, EXPERT_FF, D_MODEL), lambda w, t, e, f, o: (e[w], 0, 0))],
        out_specs=pl.BlockSpec((tm, half), lambda w, t, e, f, o: (t[w], 0)),
        scratch_shapes=[pltpu.VMEM((D_MODEL, EXPERT_FF), BF16), pltpu.VMEM((D_MODEL, EXPERT_FF), BF16),
                        pltpu.VMEM((EXPERT_FF, D_MODEL), BF16), pltpu.VMEM((tm, D_MODEL), F32)])
    return pl.pallas_call(
        _moe_kernel,
        grid_spec=grid_spec,
        out_shape=jax.ShapeDtypeStruct((n_rows, half), I32),
        compiler_params=_cparams(("arbitrary",)),
        name="moe_experts",
    )(tile, expert, flags, off, xs, wg, wu, wd)


def _moe_sparse(xn_packed, route, wg, wu, wd, tm=256):
    n = xn_packed.shape[0]
    eid = jnp.concatenate([route[:, 0], route[:, 1]]).astype(I32)
    sorted_e, sorted_pair = lax.sort_key_val(eid, jnp.arange(2 * n, dtype=I32))
    token = jnp.where(sorted_pair >= n, sorted_pair - n, sorted_pair)
    tile, expert, flags, off = _moe_plan(sorted_e, tm)
    xs = _sc_rows(xn_packed, token, scatter=False)
    ys = _moe(xs, tile, expert, flags, off, wg, wu, wd, tm)
    return _sc_rows(ys, sorted_pair, scatter=True)


def _ple_kernel(h1_ref, y0_ref, y1_ref, route_ref, p_ref, wup_ref, wgate_ref, g_ref, o_ref):
    w1 = route_ref[:, 2:3]
    w2 = route_ref[:, 3:4]
    a0, b0 = _unpack_halves(y0_ref[...])
    a1, b1 = _unpack_halves(y1_ref[...])
    moe = jnp.concatenate([w1 * a0 + w2 * a1, w1 * b0 + w2 * b1], axis=1)
    h2 = h1_ref[...] + moe
    e = _rms(_dot(p_ref[...].astype(BF16), wup_ref[...]), g_ref[...])
    gate = jax.nn.sigmoid(_dot(h2.astype(BF16), wgate_ref[...]))
    o_ref[...] = h2 + e * gate


def _ple(h1, yp, route, p, wup, wgate, g, tm=512):
    n = h1.shape[0]
    nt = n // tm
    row = lambda w: pl.BlockSpec((tm, w), lambda i: (i, 0))
    return pl.pallas_call(
        _ple_kernel,
        grid=(nt,),
        in_specs=[row(D_MODEL), row(D_MODEL // 2),
                  pl.BlockSpec((tm, D_MODEL // 2), lambda i: (i + nt, 0)),
                  row(LANES), row(PLE_DIM), _full(wup.shape), _full(wgate.shape), _full(g.shape)],
        out_specs=row(D_MODEL),
        out_shape=jax.ShapeDtypeStruct((n, D_MODEL), F32),
        compiler_params=_cparams(("parallel",)),
        name="ple",
    )(h1, yp, yp, route, p, wup, wgate, g)


def _odd_in_kernel(h_ref, g_ref, wrow_ref, wkt_ref, cos_ref, sin_ref, cost_ref, sint_ref,
                   q_ref, v_ref, gate_ref, kt_ref):
    tm = h_ref.shape[0]
    hn = _rms(h_ref[...], g_ref[...]).astype(BF16)
    cos = cos_ref[...]
    sin = sin_ref[...]
    half = RET_QK_DIM // 2
    for hd in range(RET_HEADS):
        sl = slice(hd * RET_QK_DIM, (hd + 1) * RET_QK_DIM)
        qh = _dot(hn, wrow_ref[:, sl])
        q_ref[:, sl] = (qh * cos + pltpu.roll(qh, half, 1) * sin).astype(BF16)
    v_ref[...] = _dot(hn, wrow_ref[:, RET_QK_WIDTH:RET_QK_WIDTH + RET_V_WIDTH]).astype(BF16)
    gate_ref[...] = _dot(hn, wrow_ref[:, RET_QK_WIDTH + RET_V_WIDTH:]).astype(BF16)

    cost = cost_ref[...] * (RET_QK_DIM ** -0.5)
    sint = sint_ref[...] * (RET_QK_DIM ** -0.5)
    for hd in range(RET_HEADS):
        sl = slice(hd * RET_QK_DIM, (hd + 1) * RET_QK_DIM)
        kh = _dot_nt(wkt_ref[sl, :], hn)
        swapped = jnp.concatenate([kh[half:], kh[:half]], axis=0)
        kt_ref[sl, :] = (kh * cost + swapped * sint).astype(BF16)


def _odd_in(h, g, wrow, wkt, cos, sin, cost, sint, seq, tm=256):
    n = h.shape[0]
    nt = seq // tm
    row = lambda w: pl.BlockSpec((tm, w), lambda i: (i, 0))
    return pl.pallas_call(
        _odd_in_kernel,
        grid=(n // tm,),
        in_specs=[row(D_MODEL), _full(g.shape), _full(wrow.shape), _full(wkt.shape),
                  pl.BlockSpec((tm, RET_QK_DIM), lambda i: (i % nt, 0)),
                  pl.BlockSpec((tm, RET_QK_DIM), lambda i: (i % nt, 0)),
                  pl.BlockSpec((RET_QK_DIM, tm), lambda i: (0, i % nt)),
                  pl.BlockSpec((RET_QK_DIM, tm), lambda i: (0, i % nt))],
        out_specs=[row(RET_QK_WIDTH), row(RET_V_WIDTH), row(RET_V_WIDTH),
                   pl.BlockSpec((RET_QK_WIDTH, tm), lambda i: (0, i))],
        out_shape=[jax.ShapeDtypeStruct((n, RET_QK_WIDTH), BF16), jax.ShapeDtypeStruct((n, RET_V_WIDTH), BF16),
                   jax.ShapeDtypeStruct((n, RET_V_WIDTH), BF16), jax.ShapeDtypeStruct((RET_QK_WIDTH, n), BF16)],
        compiler_params=_cparams(("parallel",)),
        name="odd_in_proj",
    )(h, g, wrow, wkt, cos, sin, cost, sint)


def _ret_kernel(cdec_ref, q_ref, kt_ref, v_ref, gate_ref, dintra_ref, qdec_ref, kdec_ref, gn_ref,
                y_ref, state_ref):
    @pl.when(pl.program_id(1) == 0)
    def _():
        state_ref[...] = jnp.zeros_like(state_ref)

    for h in range(RET_HEADS):
        ks = slice(h * RET_QK_DIM, (h + 1) * RET_QK_DIM)
        vs = slice(h * RET_V_DIM, (h + 1) * RET_V_DIM)
        qh = q_ref[:, ks]
        kth = kt_ref[ks, :]
        vh = v_ref[:, vs]
        state = state_ref[h]
        inner = (_dot(qh, kth) * dintra_ref[h]).astype(BF16)
        o = _dot(inner, vh) + _dot(qh, state.astype(BF16)) * qdec_ref[h]
        kd = (kth.astype(F32) * kdec_ref[h]).astype(BF16)
        state_ref[h] = state * cdec_ref[h] + _dot(kd, vh)
        mu = jnp.mean(o, axis=-1, keepdims=True)
        oc = o - mu
        var = jnp.mean(oc * oc, axis=-1, keepdims=True)
        on = oc * lax.rsqrt(var + GN_EPS) * gn_ref[:, vs]
        y_ref[:, vs] = (jax.nn.silu(gate_ref[:, vs].astype(F32)) * on).astype(BF16)


def _retention(cdec, q, kt, v, gate, dintra, qdec, kdec, gn, batch, seq):
    n = q.shape[0]
    c = RET_TILE
    nc = seq // c
    row = lambda w: pl.BlockSpec((c, w), lambda b, t: (b * nc + t, 0))
    return pl.pallas_call(
        _ret_kernel,
        grid=(batch, nc),
        in_specs=[pl.BlockSpec(memory_space=pltpu.SMEM),
                  row(RET_QK_WIDTH),
                  pl.BlockSpec((RET_QK_WIDTH, c), lambda b, t: (0, b * nc + t)),
                  row(RET_V_WIDTH), row(RET_V_WIDTH),
                  _full(dintra.shape), _full(qdec.shape), _full(kdec.shape), _full(gn.shape)],
        out_specs=row(RET_V_WIDTH),
        out_shape=jax.ShapeDtypeStruct((n, RET_V_WIDTH), BF16),
        scratch_shapes=[pltpu.VMEM((RET_HEADS, RET_QK_DIM, RET_V_DIM), F32)],
        compiler_params=_cparams(("parallel", "arbitrary")),
        name="retention",
    )(cdec, q, kt, v, gate, dintra, qdec, kdec, gn)


def _t5_bucket(rel):
    half = N_BUCKETS // 2
    max_exact = half // 2
    ret = (rel > 0).astype(I32) * half
    n = jnp.abs(rel)
    nf = jnp.maximum(n, 1).astype(F32)
    large = max_exact + (jnp.log(nf / max_exact) / math.log(MAX_DISTANCE / max_exact)
                         * (half - max_exact)).astype(I32)
    large = jnp.minimum(large, half - 1)
    return ret + jnp.where(n < max_exact, n, large)


def _bias_tables(rel_bias):
    r = jnp.arange(Q_TILE, dtype=I32)[:, None]
    c = jnp.arange(Q_TILE, dtype=I32)[None, :]
    rels = jnp.stack([c - Q_TILE - r, c - r])
    onehot = jax.nn.one_hot(_t5_bucket(rels), N_BUCKETS, dtype=F32)
    tab = jnp.einsum("abcn,nh->ahbc", onehot, rel_bias, precision=lax.Precision.HIGHEST)
    far = rel_bias[_t5_bucket(jnp.int32(-2 * Q_TILE))]
    return tab.astype(F32), far.astype(F32)


def _block_diag(w):
    nb, bs, _ = w.shape
    eye = jnp.eye(nb, dtype=w.dtype)
    return jnp.einsum("hij,hg->higj", w, eye).reshape(nb * bs, nb * bs)


def _router_weights(w_group, w_expert):
    wr = jnp.concatenate([w_expert, w_group, jnp.zeros((D_MODEL, LANES - N_EXPERTS - N_GROUPS), F32)], axis=1)
    hi = wr.astype(BF16)
    lo = (wr - hi.astype(F32)).astype(BF16)
    return hi, lo


def _rotary_tables(seq):
    half = RET_QK_DIM // 2
    inv = ROPE_BASE ** (-jnp.arange(half, dtype=F32) / half)
    ang = jnp.arange(seq, dtype=F32)[:, None] * inv[None, :]
    cos, sin = jnp.cos(ang), jnp.sin(ang)
    cos_row = jnp.concatenate([cos, cos], axis=1)
    sin_row = jnp.concatenate([-sin, sin], axis=1)
    cos_t = jnp.concatenate([cos.T, cos.T], axis=0)
    sin_t = jnp.concatenate([-sin.T, sin.T], axis=0)
    return cos_row, sin_row, cos_t, sin_t


def _retention_tables():
    c = RET_TILE
    log_g = jnp.log(1.0 - 2.0 ** (-5.0 - jnp.arange(RET_HEADS, dtype=F32)))
    pos = jnp.arange(c, dtype=F32)
    diff = pos[:, None] - pos[None, :]
    causal = diff >= 0
    dintra = jnp.where(causal[None], jnp.exp(jnp.where(causal, diff, 0.0)[None] * log_g[:, None, None]), 0.0)
    qdec = jnp.exp((pos + 1.0)[None, :, None] * log_g[:, None, None])
    kdec = jnp.exp((c - 1.0 - pos)[None, None, :] * log_g[:, None, None])
    cdec = jnp.exp(c * log_g)
    return dintra, qdec, kdec, cdec


def kernel(x, p, rel_bias, mix_norm_g, ffn_norm_g, ple_norm_g, ev_w_in, ev_conv_w, ev_conv_b, ev_lru_wa, ev_lru_ba, ev_lru_wx, ev_lru_bx, ev_lru_lambda, ev_q_norm_g, ev_k_norm_g, ev_w_out, od_w_in, od_gn_g, od_w_out, moe_w_group, moe_w_expert, moe_w_gate, moe_w_up, moe_w_down, ple_w_up, ple_w_gate):
    batch, seq, _ = x.shape
    n = batch * seq
    depth = p.shape[0]
    h = x.reshape(n, D_MODEL)
    row = lambda a: a.reshape(1, -1)

    tab, far = _bias_tables(rel_bias)
    cos_row, sin_row, cos_t, sin_t = _rotary_tables(seq)
    dintra, qdec, kdec, cdec = _retention_tables()
    head_ones = _block_diag(jnp.ones((ATT_HEADS, ATT_HEAD_DIM, ATT_HEAD_DIM), BF16))

    for i in range(depth):
        jdx = i // 2
        if i % 2 == 0:
            w = ev_w_in[jdx]
            o = [0, 512, 1024, 1536, 2048, 2560, 3072, 3136, 3144]
            xa_w, ga_w, q_w, k_w, v_w, iq_w, ik_w, iw_w = [w[:, o[t]:o[t + 1]] for t in range(8)]
            wrow = jnp.concatenate([xa_w, ga_w, q_w, v_w, iq_w, iw_w,
                                    jnp.zeros((D_MODEL, LANES - IDX_HEADS), F32)], axis=1).astype(BF16)
            wkt = k_w.T.astype(BF16)
            wikt = jnp.concatenate([ik_w.T, ik_w.T], axis=0).astype(BF16)
            qg = jnp.tile(ev_q_norm_g[jdx], ATT_HEADS).reshape(1, -1) * (ATT_HEAD_DIM ** -0.5)
            kg = ev_k_norm_g[jdx].reshape(-1, 1)
            xg, q, v, iq, iw, kt, ikt = _even_in(h, row(mix_norm_g[i]), wrow, wkt, wikt, head_ones, qg, kg)
            ya = _rglru(xg, ev_conv_w[jdx], row(ev_conv_b[jdx]),
                        _block_diag(ev_lru_wa[jdx]).astype(BF16), row(ev_lru_ba[jdx]),
                        _block_diag(ev_lru_wx[jdx]).astype(BF16), row(ev_lru_bx[jdx]),
                        row(ev_lru_lambda[jdx]), batch, seq)
            yb = _dsa(far, q, iq, iw, kt, v, ikt, tab, batch, seq)
            wo = ev_w_out[jdx].astype(BF16)
            ys, ws = [ya, yb], [wo[:LRU_WIDTH], wo[LRU_WIDTH:]]
        else:
            w = od_w_in[jdx]
            wrow = jnp.concatenate([w[:, :RET_QK_WIDTH], w[:, 2 * RET_QK_WIDTH:]], axis=1).astype(BF16)
            wkt = w[:, RET_QK_WIDTH:2 * RET_QK_WIDTH].T.astype(BF16)
            q, v, gate, kt = _odd_in(h, row(mix_norm_g[i]), wrow, wkt, cos_row, sin_row, cos_t, sin_t, seq)
            yc = _retention(cdec, q, kt, v, gate, dintra, qdec, kdec, row(od_gn_g[jdx]), batch, seq)
            ys, ws = [yc], [od_w_out[jdx].astype(BF16)]
        wr_hi, wr_lo = _router_weights(moe_w_group[i], moe_w_expert[i])
        h1, xn, route = _mix_out(h, ys, ws, row(ffn_norm_g[i]), wr_hi, wr_lo)
        yp = _moe_sparse(xn, route, moe_w_gate[i], moe_w_up[i], moe_w_down[i])
        h = _ple(h1, yp, route, p[i].reshape(n, PLE_DIM), ple_w_up[i].astype(BF16),
                 ple_w_gate[i].astype(BF16), row(ple_norm_g[i]))
    return h.reshape(batch, seq, D_MODEL)
```

```python
import functools
import math

import jax
import jax.numpy as jnp
from jax import lax
from jax.experimental import pallas as pl
from jax.experimental.pallas import tpu as pltpu
from jax.experimental.pallas import tpu_sc as plsc

F32 = jnp.float32
BF16 = jnp.bfloat16
I32 = jnp.int32

D_MODEL = 1024
CHUNK = 64
PLE_DIM = 256
RMS_EPS = 1e-6

LRU_WIDTH = 512
LRU_BLOCKS = 8
LRU_C = 8.0

ATT_HEADS = 8
ATT_HEAD_DIM = 64
ATT_WIDTH = 512
IDX_HEADS = 8
IDX_DIM = 64
MAX_TOP_K = 256
N_BUCKETS = 32
MAX_DISTANCE = 128

RET_HEADS = 8
RET_QK_DIM = 128
RET_V_DIM = 256
RET_QK_WIDTH = 1024
RET_V_WIDTH = 2048
ROPE_BASE = 10000.0
GN_EPS = 1e-5

N_GROUPS = 4
EXPERTS_PER_GROUP = 8
N_EXPERTS = 32
EXPERT_FF = 512

LANES = 128
INT_MIN = -(2 ** 31)
NEG_BIG = -1e30
LOG2E = 1.4426950408889634
VMEM_LIMIT = 56 * 1024 * 1024

SC_CORES = 2
SC_SUBCORES = 16
SC_CHUNK = 64

Q_TILE = 128
K_TILE = 512
RET_TILE = 256


def _cparams(sem):
    return pltpu.CompilerParams(dimension_semantics=sem, vmem_limit_bytes=VMEM_LIMIT)


def _full(shape):
    nd = len(shape)
    return pl.BlockSpec(shape, lambda *_: (0,) * nd)


def _rms(xf, g):
    return xf * lax.rsqrt(jnp.mean(xf * xf, axis=-1, keepdims=True) + RMS_EPS) * g


def _dot(a, b):
    return jnp.dot(a, b, preferred_element_type=F32)


def _dot_nt(a, b):
    return lax.dot_general(a, b, (((1,), (1,)), ((), ())), preferred_element_type=F32)


def _neg_expm1(x):
    series = -x * (1.0 + x * (0.5 + x * (1.0 / 6.0 + x * (1.0 / 24.0 + x * (1.0 / 120.0)))))
    return jnp.where(x > -0.03, series, 1.0 - jnp.exp(x))


def _pack_halves(x):
    w = x.shape[1] // 2
    hi = pltpu.bitcast(x[:, :w].astype(BF16).astype(F32), I32)
    lo = pltpu.bitcast(x[:, w:].astype(BF16).astype(F32), I32)
    return hi | lax.shift_right_logical(lo, 16)


def _unpack_halves(p):
    hi = pltpu.bitcast(p & jnp.int32(-65536), F32)
    lo = pltpu.bitcast(lax.shift_left(p, 16), F32)
    return hi, lo


def _split_bf16(x):
    hi = x.astype(BF16)
    lo = (x - hi.astype(F32)).astype(BF16)
    return hi, lo


def _even_in_kernel(h_ref, g_ref, wrow_ref, wt_ref, bd_ref, qg_ref, kg_ref,
                    xg_ref, k_ref, ik_ref, qt_ref, iqt_ref, vt_ref, iwt_ref):
    tm = h_ref.shape[0]
    w = ATT_WIDTH
    hn = _rms(h_ref[...], g_ref[...]).astype(BF16)

    xg_ref[...] = _dot(hn, wrow_ref[:, :2 * LRU_WIDTH])
    kf = _dot(hn, wrow_ref[:, 2 * LRU_WIDTH:2 * LRU_WIDTH + w])
    hi, lo = _split_bf16(kf * kf)
    ss = _dot(hi, bd_ref[...]) + _dot(lo, bd_ref[...])
    k_ref[...] = (kf * lax.rsqrt(ss * (1.0 / ATT_HEAD_DIM) + RMS_EPS) * kg_ref[...]).astype(BF16)
    ik_ref[...] = _dot(hn, wrow_ref[:, 2 * LRU_WIDTH + w:]).astype(BF16)

    q3 = _dot_nt(wt_ref[0:w, :], hn).reshape(ATT_HEADS, ATT_HEAD_DIM, tm)
    ssq = jnp.sum(q3 * q3, axis=1, keepdims=True)
    qn = q3 * lax.rsqrt(ssq * (1.0 / ATT_HEAD_DIM) + RMS_EPS) * qg_ref[...][None]
    qt_ref[...] = qn.reshape(w, tm).astype(BF16)
    iqt_ref[...] = _dot_nt(wt_ref[w:2 * w, :], hn).astype(BF16)
    vt_ref[...] = _dot_nt(wt_ref[2 * w:3 * w, :], hn).astype(BF16)
    iwt = _dot_nt(wt_ref[3 * w:, :], hn)
    iwt_ref[...] = iwt[:IDX_HEADS, :] * (IDX_HEADS ** -0.5 * IDX_DIM ** -0.5)


def _even_in(h, g, wrow, wt, bd, qg, kg, tm=256):
    n = h.shape[0]
    row = lambda w: pl.BlockSpec((tm, w), lambda i: (i, 0))
    col = lambda r: pl.BlockSpec((r, tm), lambda i: (0, i))
    return pl.pallas_call(
        _even_in_kernel,
        grid=(n // tm,),
        in_specs=[row(D_MODEL), _full(g.shape), _full(wrow.shape), _full(wt.shape),
                  _full(bd.shape), _full(qg.shape), _full(kg.shape)],
        out_specs=[row(1024), row(512), row(LANES), col(512), col(512), col(512), col(IDX_HEADS)],
        out_shape=[jax.ShapeDtypeStruct((n, 1024), F32), jax.ShapeDtypeStruct((n, 512), BF16),
                   jax.ShapeDtypeStruct((n, LANES), BF16), jax.ShapeDtypeStruct((512, n), BF16),
                   jax.ShapeDtypeStruct((512, n), BF16), jax.ShapeDtypeStruct((512, n), BF16),
                   jax.ShapeDtypeStruct((IDX_HEADS, n), F32)],
        compiler_params=_cparams(("parallel",)),
        name="even_in_proj",
    )(h, g, wrow, wt, bd, qg, kg)


def _rglru_kernel(xg_ref, cw_ref, cb_ref, wa_ref, ba_ref, wx_ref, bx_ref, lam_ref, ya_ref,
                  tail_ref, hst_ref):
    ts = xg_ref.shape[0]
    w = LRU_WIDTH

    @pl.when(pl.program_id(1) == 0)
    def _():
        tail_ref[...] = jnp.zeros_like(tail_ref)
        hst_ref[...] = jnp.zeros_like(hst_ref)

    xa = xg_ref[:, :w]
    ga = xg_ref[:, w:]
    row = lax.broadcasted_iota(I32, (ts, w), 0)
    tail = tail_ref[...]
    xc = xa * cw_ref[3:4, :] + cb_ref[...]
    for d in (1, 2, 3):
        cur = pltpu.roll(xa, d, 0)
        prev = jnp.concatenate([pltpu.roll(tail, d, 0), jnp.zeros((ts - 8, w), F32)], axis=0)
        xc = xc + jnp.where(row < d, prev, cur) * cw_ref[3 - d:4 - d, :]
    tail_ref[...] = xa[ts - 8:, :]

    xcb = xc.astype(BF16)
    r = jax.nn.sigmoid(_dot(xcb, wa_ref[...]) + ba_ref[...])
    gi = jax.nn.sigmoid(_dot(xcb, wx_ref[...]) + bx_ref[...])
    nl = -lam_ref[...]
    softplus = jnp.maximum(nl, 0.0) + jnp.log1p(jnp.exp(-jnp.abs(nl)))
    log_a = (-LRU_C) * r * softplus
    a = jnp.exp(log_a)
    u = jnp.sqrt(_neg_expm1(2.0 * log_a)) * (gi * xc)

    d = 1
    while d < ts:
        keep = row >= d
        a_sh = jnp.where(keep, pltpu.roll(a, d, 0), 1.0)
        u_sh = jnp.where(keep, pltpu.roll(u, d, 0), 0.0)
        u = a * u_sh + u
        a = a * a_sh
        d *= 2
    hseq = a * hst_ref[0:1, :] + u
    hst_ref[0:1, :] = hseq[ts - 1:ts, :]
    ya_ref[...] = (hseq * jax.nn.gelu(ga)).astype(BF16)


def _rglru(xg, cw, cb, wa, ba, wx, bx, lam, batch, seq, ts=256):
    n = xg.shape[0]
    nt = seq // ts
    return pl.pallas_call(
        _rglru_kernel,
        grid=(batch, nt),
        in_specs=[pl.BlockSpec((ts, 1024), lambda b, t: (b * nt + t, 0)),
                  _full(cw.shape), _full(cb.shape), _full(wa.shape), _full(ba.shape),
                  _full(wx.shape), _full(bx.shape), _full(lam.shape)],
        out_specs=pl.BlockSpec((ts, LRU_WIDTH), lambda b, t: (b * nt + t, 0)),
        out_shape=jax.ShapeDtypeStruct((n, LRU_WIDTH), BF16),
        scratch_shapes=[pltpu.VMEM((8, LRU_WIDTH), F32), pltpu.VMEM((8, LRU_WIDTH), F32)],
        compiler_params=_cparams(("parallel", "arbitrary")),
        name="rglru",
    )(xg, cw, cb, wa, ba, wx, bx, lam)


def _dsa_kernel(far_ref, qt_ref, iqt_ref, iwt_ref, k_ref, ik_ref, vt_ref, tab_ref, o_ref,
                keys_ref, sel_ref, cut_ref, *, seq, top_k):
    tq = Q_TILE
    hd = ATT_HEAD_DIM
    j = pl.program_id(1)
    q0 = j * tq
    nkt = (j + K_TILE // tq) // (K_TILE // tq)
    qcol = q0 + lax.broadcasted_iota(I32, (1, tq), 1)
    qlim = (qcol // CHUNK + 1) * CHUNK
    zero_rows = jnp.zeros((hd, tq), BF16)

    def head_rows(ref, h):
        blk = ref[h * hd:(h + 1) * hd, :]
        return jnp.concatenate([blk, zero_rows] if h % 2 == 0 else [zero_rows, blk], axis=0)

    def ktile(kt):
        return pl.multiple_of(kt * K_TILE, K_TILE)

    def key_pos(off, rows):
        return off + lax.broadcasted_iota(I32, (rows, tq), 0)

    iq_m = [head_rows(iqt_ref, h) for h in range(IDX_HEADS)]

    def score_body(kt, c):
        off = ktile(kt)
        ik = ik_ref[pl.ds(off, K_TILE), :]
        s = jnp.zeros((K_TILE, tq), F32)
        for h in range(IDX_HEADS):
            s = s + jnp.maximum(_dot(ik, iq_m[h]), 0.0) * iwt_ref[h:h + 1, :]
        s = jnp.where(s == 0.0, 0.0, s)
        bits = pltpu.bitcast(s, I32)
        key = bits ^ ((bits >> 31) & 0x7FFFFFFF)
        keys_ref[pl.ds(off, K_TILE), :] = jnp.where(key_pos(off, K_TILE) < qlim, key, INT_MIN)
        return c

    lax.fori_loop(0, nkt, score_body, 0)

    def count(pred):
        def body(kt, acc):
            off = ktile(kt)
            ind = pred(keys_ref[pl.ds(off, K_TILE), :], key_pos(off, K_TILE)).reshape(K_TILE // 8, 8, tq)
            while ind.shape[0] > 1:
                half = ind.shape[0] // 2
                ind = ind[:half] + ind[half:]
            return acc + ind[0]
        acc = lax.fori_loop(0, nkt, body, jnp.zeros((8, tq), F32))
        return jnp.sum(acc, axis=0, keepdims=True)

    kf = float(top_k)

    def search_body(i, ans):
        cand = ans + jnp.left_shift(jnp.int32(1), 31 - i)
        cnt = count(lambda kk, kpos: jnp.where(kk >= cand, 1.0, 0.0))
        return jnp.where(cnt >= kf, cand, ans)

    ans = lax.fori_loop(0, 32, search_body, jnp.full((1, tq), INT_MIN, I32))

    cnt_ge = count(lambda kk, kpos: jnp.where(kk >= ans, 1.0, 0.0))
    excess = jnp.where(ans > INT_MIN, cnt_ge - kf, 0.0)
    cut_ref[...] = jnp.full(cut_ref.shape, seq, I32)

    @pl.when(jnp.max(excess) > 0.0)
    def _():
        need = kf - count(lambda kk, kpos: jnp.where(kk > ans, 1.0, 0.0))

        def idx_body(i, pos):
            cand = pos + jnp.left_shift(jnp.int32(1), (seq.bit_length() - 2) - i)
            c = count(lambda kk, kpos: jnp.where(kk == ans, jnp.where(kpos < cand, 1.0, 0.0), 0.0))
            return jnp.where(c < need, cand, pos)

        pos = lax.fori_loop(0, seq.bit_length() - 1, idx_body, jnp.zeros((1, tq), I32))
        cut_ref[...] = jnp.broadcast_to(pos, cut_ref.shape)

    thr = jnp.maximum(ans, INT_MIN + 1)
    cut = cut_ref[0:1, :]

    near_w = 2 * tq
    near_start = pl.multiple_of(jnp.maximum(j - 1, 0) * tq, tq)

    def selection(kk, kpos):
        tie = jnp.where(kpos <= cut, 0.0, NEG_BIG)
        return jnp.where(kk > thr, 0.0, jnp.where(kk == thr, tie, NEG_BIG))

    def sel_body(kt, c):
        off = ktile(kt)
        kpos = key_pos(off, K_TILE)
        s = selection(keys_ref[pl.ds(off, K_TILE), :], kpos)
        sel_ref[pl.ds(off, K_TILE), :] = jnp.where(kpos < near_start, s, NEG_BIG)
        return c

    n_far = (jnp.maximum(j - 1, 0) + K_TILE // tq - 1) // (K_TILE // tq)
    lax.fori_loop(0, n_far, sel_body, 0)
    sel_near = selection(keys_ref[pl.ds(near_start, near_w), :], key_pos(near_start, near_w))

    def step(qm, kb, vte, add, carry):
        m, acc = carry
        s = _dot(kb, qm) + add
        m_new = jnp.maximum(m, jnp.max(s, axis=0, keepdims=True))
        p = jnp.exp2(s - m_new).astype(BF16)
        acc = jnp.exp2(m - m_new) * acc + _dot(vte, p)
        return m_new, acc

    qms = [head_rows(qt_ref, h) for h in range(ATT_HEADS)]
    pairs = [slice((h // 2) * LANES, (h // 2 + 1) * LANES) for h in range(ATT_HEADS)]

    def values(h, off, width):
        return jnp.concatenate([vt_ref[h * hd:(h + 1) * hd, pl.ds(off, width)],
                                jnp.ones((hd, width), BF16)], axis=0)

    def far_body(kt, carries):
        off = ktile(kt)
        sel = sel_ref[pl.ds(off, K_TILE), :]
        return tuple(step(qms[h], k_ref[pl.ds(off, K_TILE), pairs[h]], values(h, off, K_TILE), sel, carries[h])
                     for h in range(ATT_HEADS))

    init = (jnp.full((1, tq), NEG_BIG, F32), jnp.zeros((2 * hd, tq), F32))
    carries = lax.fori_loop(0, n_far, far_body, (init,) * ATT_HEADS)

    first = jnp.where(j == 0, 1, 0)
    res = []
    for h in range(ATT_HEADS):
        m, acc = carries[h]
        bias = jnp.concatenate([tab_ref[first, h], tab_ref[1, h]], axis=0)
        m, acc = step(qms[h], k_ref[pl.ds(near_start, near_w), pairs[h]], values(h, near_start, near_w),
                      sel_near + bias, (m + far_ref[h], acc))
        res.append(acc[:hd, :] / acc[hd:hd + 1, :])

    for p in range(ATT_HEADS // 2):
        pair_t = jnp.concatenate([res[2 * p], res[2 * p + 1]], axis=0)
        o_ref[:, p * LANES:(p + 1) * LANES] = pair_t.T.astype(BF16)


def _dsa(far, qt, iqt, iwt, k, ik, vt, tab, batch, seq):
    n = k.shape[0]
    nb = seq // Q_TILE
    top_k = min(MAX_TOP_K, seq // 4)
    qspec = lambda r: pl.BlockSpec((r, Q_TILE), lambda b, j: (0, b * nb + j))
    return pl.pallas_call(
        functools.partial(_dsa_kernel, seq=seq, top_k=top_k),
        grid=(batch, nb),
        in_specs=[pl.BlockSpec(memory_space=pltpu.SMEM),
                  qspec(512), qspec(512), qspec(IDX_HEADS),
                  pl.BlockSpec((seq, 512), lambda b, j: (b, 0)),
                  pl.BlockSpec((seq, LANES), lambda b, j: (b, 0)),
                  pl.BlockSpec((512, seq), lambda b, j: (0, b)),
                  _full(tab.shape)],
        out_specs=pl.BlockSpec((Q_TILE, 512), lambda b, j: (b * nb + j, 0)),
        out_shape=jax.ShapeDtypeStruct((n, 512), BF16),
        scratch_shapes=[pltpu.VMEM((seq, Q_TILE), I32), pltpu.VMEM((seq, Q_TILE), F32),
                        pltpu.VMEM((8, Q_TILE), I32)],
        compiler_params=_cparams(("parallel", "arbitrary")),
        name="dsa",
    )(far, qt, iqt, iwt, k, ik, vt, tab)


def _mix_out_kernel(*refs, n_y):
    h_ref = refs[0]
    y_refs = refs[1:1 + n_y]
    w_refs = refs[1 + n_y:1 + 2 * n_y]
    g_ref, wr_hi_ref, wr_lo_ref = refs[1 + 2 * n_y:4 + 2 * n_y]
    h1_ref, xn_ref, route_ref = refs[4 + 2 * n_y:]
    tm = h_ref.shape[0]

    mix = _dot(y_refs[0][...], w_refs[0][...])
    for y_ref, w_ref in zip(y_refs[1:], w_refs[1:]):
        mix = mix + _dot(y_ref[...], w_ref[...])
    h1 = h_ref[...] + mix
    h1_ref[...] = h1
    xn = _rms(h1, g_ref[...])
    hi, lo = _split_bf16(xn)
    xn_ref[...] = _pack_halves(xn)
    logits = _dot(hi, wr_hi_ref[...]) + _dot(lo, wr_hi_ref[...]) + _dot(hi, wr_lo_ref[...])

    lane = lax.broadcasted_iota(I32, (tm, LANES), 1)
    big = jnp.int32(LANES)
    ninf = -jnp.inf
    is_g = (lane >= N_EXPERTS) & (lane < N_EXPERTS + N_GROUPS)
    glog = jnp.where(is_g, logits, ninf)
    gmax = jnp.max(glog, axis=1, keepdims=True)
    gsel = jnp.min(jnp.where(glog == gmax, lane, big), axis=1, keepdims=True) - N_EXPERTS
    gprob = 1.0 / jnp.sum(jnp.exp(glog - gmax), axis=1, keepdims=True)
    lo_l = gsel * EXPERTS_PER_GROUP
    within = jnp.where((lane >= lo_l) & (lane < lo_l + EXPERTS_PER_GROUP), logits, ninf)
    v1 = jnp.max(within, axis=1, keepdims=True)
    i1 = jnp.min(jnp.where(within == v1, lane, big), axis=1, keepdims=True)
    rest = jnp.where(lane == i1, ninf, within)
    v2 = jnp.max(rest, axis=1, keepdims=True)
    i2 = jnp.min(jnp.where(rest == v2, lane, big), axis=1, keepdims=True)
    e2 = jnp.exp(v2 - v1)
    w1 = gprob / (1.0 + e2)
    w2 = gprob * e2 / (1.0 + e2)
    route_ref[...] = jnp.where(lane == 0, i1.astype(F32), jnp.where(lane == 1, i2.astype(F32),
                               jnp.where(lane == 2, w1, jnp.where(lane == 3, w2, 0.0))))


def _mix_out(h, ys, ws, g, wr_hi, wr_lo, tm=512):
    n = h.shape[0]
    n_y = len(ys)
    row = lambda w: pl.BlockSpec((tm, w), lambda i: (i, 0))
    return pl.pallas_call(
        functools.partial(_mix_out_kernel, n_y=n_y),
        grid=(n // tm,),
        in_specs=[row(D_MODEL)] + [row(y.shape[1]) for y in ys] + [_full(w.shape) for w in ws]
                 + [_full(g.shape), _full(wr_hi.shape), _full(wr_lo.shape)],
        out_specs=[row(D_MODEL), row(D_MODEL // 2), row(LANES)],
        out_shape=[jax.ShapeDtypeStruct((n, D_MODEL), F32), jax.ShapeDtypeStruct((n, D_MODEL // 2), I32),
                   jax.ShapeDtypeStruct((n, LANES), F32)],
        compiler_params=_cparams(("parallel",)),
        name="mix_out_router",
    )(h, *ys, *ws, g, wr_hi, wr_lo)


def _sc_rows(src, idx, scatter):
    r, d = idx.shape[0], src.shape[1]
    nw = SC_CORES * SC_SUBCORES
    per_w = r // nw
    n_chunks = per_w // SC_CHUNK
    idx3 = idx.reshape(nw, n_chunks, SC_CHUNK)
    mesh = plsc.VectorSubcoreMesh(core_axis_name="c", subcore_axis_name="s")

    @functools.partial(
        pl.kernel, mesh=mesh, out_type=jax.ShapeDtypeStruct((r, d), src.dtype),
        scratch_types=[pltpu.VMEM((n_chunks, SC_CHUNK), I32), pltpu.VMEM((SC_CHUNK, d), src.dtype),
                       pltpu.SemaphoreType.DMA])
    def permute(src_hbm, idx_hbm, out_hbm, idx_v, rows_v, sem):
        wid = lax.axis_index("s") * SC_CORES + lax.axis_index("c")
        pltpu.sync_copy(idx_hbm.at[wid], idx_v)

        @pl.loop(0, n_chunks)
        def _(c):
            lin = pl.ds(pl.multiple_of(wid * per_w + c * SC_CHUNK, SC_CHUNK), SC_CHUNK)
            if scatter:
                pltpu.sync_copy(src_hbm.at[lin], rows_v)
                pltpu.async_copy(rows_v, out_hbm.at[idx_v.at[c]], sem).wait()
            else:
                pltpu.async_copy(src_hbm.at[idx_v.at[c]], rows_v, sem).wait()
                pltpu.sync_copy(rows_v, out_hbm.at[lin])

    return permute(src, idx3)


def _moe_kernel(tile_ref, exp_ref, flag_ref, off_ref, xs_ref, wg_ref, wu_ref, wd_ref, ys_ref,
                wgb_ref, wub_ref, wdb_ref, acc_ref):
    w = pl.program_id(0)
    tm = xs_ref.shape[0]
    e = exp_ref[w]
    flags = flag_ref[w]

    @pl.when((flags & 4) != 0)
    def _():
        wgb_ref[...] = wg_ref[0].astype(BF16)
        wub_ref[...] = wu_ref[0].astype(BF16)
        wdb_ref[...] = wd_ref[0].astype(BF16)

    @pl.when((flags & 1) != 0)
    def _():
        xa, xb = _unpack_halves(xs_ref[...])
        xa = xa.astype(BF16)
        xb = xb.astype(BF16)
        half = D_MODEL // 2
        gate = _dot(xa, wgb_ref[:half, :]) + _dot(xb, wgb_ref[half:, :])
        up = _dot(xa, wub_ref[:half, :]) + _dot(xb, wub_ref[half:, :])
        hid = (jax.nn.silu(gate) * up).astype(BF16)
        y = _dot(hid, wdb_ref[...])
        rows = tile_ref[w] * tm + lax.broadcasted_iota(I32, (tm, 1), 0)
        mine = (rows >= off_ref[e]) & (rows < off_ref[e + 1])
        y = jnp.where(mine, y, 0.0)

        @pl.when((flags & 2) != 0)
        def _():
            acc_ref[...] = y

        @pl.when((flags & 2) == 0)
        def _():
            acc_ref[...] += y

        ys_ref[...] = _pack_halves(acc_ref[...])


def _moe_plan(sorted_e, tm):
    n_rows = sorted_e.shape[0]
    nt = n_rows // tm
    n_items = nt + N_EXPERTS - 1
    off = jnp.searchsorted(sorted_e, jnp.arange(N_EXPERTS + 1, dtype=I32), side="left").astype(I32)
    e_lo = sorted_e[::tm]
    e_hi = sorted_e[tm - 1::tm]
    span = e_hi - e_lo + 1
    start = jnp.cumsum(span) - span
    total = jnp.sum(span)
    w = jnp.arange(n_items, dtype=I32)
    valid = w < total
    tile = jnp.clip(jnp.searchsorted(start, w, side="right").astype(I32) - 1, 0, nt - 1)
    expert = jnp.where(valid, e_lo[tile] + (w - start[tile]), e_hi[nt - 1])
    first = valid & (w == start[tile])
    new_e = jnp.concatenate([jnp.ones((1,), bool), expert[1:] != expert[:-1]])
    flags = valid.astype(I32) + 2 * first.astype(I32) + 4 * new_e.astype(I32)
    return tile, expert, flags, off


def _moe(xs, tile, expert, flags, off, wg, wu, wd, tm):
    n_rows, half = xs.shape
    n_items = tile.shape[0]
    grid_spec = pltpu.PrefetchScalarGridSpec(
        num_scalar_prefetch=4,
        grid=(n_items,),
        in_specs=[pl.BlockSpec((tm, half), lambda w, t, e, f, o: (t[w], 0)),
                  pl.BlockSpec((1, D_MODEL, EXPERT_FF), lambda w, t, e, f, o: (e[w], 0, 0)),
                  pl.BlockSpec((1, D_MODEL, EXPERT_FF), lambda w, t, e, f, o: (e[w], 0, 0)),
                  pl.BlockSpec((1, EXPERT_FF, D_MODEL), lambda w, t, e, f, o: (e[w], 0, 0))],
        out_specs=pl.BlockSpec((tm, half), lambda w, t, e, f, o: (t[w], 0)),
        scratch_shapes=[pltpu.VMEM((D_MODEL, EXPERT_FF), BF16), pltpu.VMEM((D_MODEL, EXPERT_FF), BF16),
                        pltpu.VMEM((EXPERT_FF, D_MODEL), BF16), pltpu.VMEM((tm, D_MODEL), F32)])
    return pl.pallas_call(
        _moe_kernel,
        grid_spec=grid_spec,
        out_shape=jax.ShapeDtypeStruct((n_rows, half), I32),
        compiler_params=_cparams(("arbitrary",)),
        name="moe_experts",
    )(tile, expert, flags, off, xs, wg, wu, wd)


def _moe_sparse(xn_packed, route, wg, wu, wd, tm=256):
    n = xn_packed.shape[0]
    eid = jnp.concatenate([route[:, 0], route[:, 1]]).astype(I32)
    sorted_e, sorted_pair = lax.sort_key_val(eid, jnp.arange(2 * n, dtype=I32))
    token = jnp.where(sorted_pair >= n, sorted_pair - n, sorted_pair)
    tile, expert, flags, off = _moe_plan(sorted_e, tm)
    xs = _sc_rows(xn_packed, token, scatter=False)
    ys = _moe(xs, tile, expert, flags, off, wg, wu, wd, tm)
    return _sc_rows(ys, sorted_pair, scatter=True)


def _ple_kernel(h1_ref, y0_ref, y1_ref, route_ref, p_ref, wup_ref, wgate_ref, g_ref, o_ref):
    w1 = route_ref[:, 2:3]
    w2 = route_ref[:, 3:4]
    a0, b0 = _unpack_halves(y0_ref[...])
    a1, b1 = _unpack_halves(y1_ref[...])
    moe = jnp.concatenate([w1 * a0 + w2 * a1, w1 * b0 + w2 * b1], axis=1)
    h2 = h1_ref[...] + moe
    e = _rms(_dot(p_ref[...].astype(BF16), wup_ref[...]), g_ref[...])
    gate = jax.nn.sigmoid(_dot(h2.astype(BF16), wgate_ref[...]))
    o_ref[...] = h2 + e * gate


def _ple(h1, yp, route, p, wup, wgate, g, tm=512):
    n = h1.shape[0]
    nt = n // tm
    row = lambda w: pl.BlockSpec((tm, w), lambda i: (i, 0))
    return pl.pallas_call(
        _ple_kernel,
        grid=(nt,),
        in_specs=[row(D_MODEL), row(D_MODEL // 2),
                  pl.BlockSpec((tm, D_MODEL // 2), lambda i: (i + nt, 0)),
                  row(LANES), row(PLE_DIM), _full(wup.shape), _full(wgate.shape), _full(g.shape)],
        out_specs=row(D_MODEL),
        out_shape=jax.ShapeDtypeStruct((n, D_MODEL), F32),
        compiler_params=_cparams(("parallel",)),
        name="ple",
    )(h1, yp, yp, route, p, wup, wgate, g)


def _odd_in_kernel(h_ref, g_ref, wrow_ref, wkt_ref, cos_ref, sin_ref, cost_ref, sint_ref,
                   q_ref, v_ref, gate_ref, kt_ref):
    tm = h_ref.shape[0]
    hn = _rms(h_ref[...], g_ref[...]).astype(BF16)
    cos = cos_ref[...]
    sin = sin_ref[...]
    half = RET_QK_DIM // 2
    for hd in range(RET_HEADS):
        sl = slice(hd * RET_QK_DIM, (hd + 1) * RET_QK_DIM)
        qh = _dot(hn, wrow_ref[:, sl])
        q_ref[:, sl] = (qh * cos + pltpu.roll(qh, half, 1) * sin).astype(BF16)
    v_ref[...] = _dot(hn, wrow_ref[:, RET_QK_WIDTH:RET_QK_WIDTH + RET_V_WIDTH]).astype(BF16)
    gate_ref[...] = _dot(hn, wrow_ref[:, RET_QK_WIDTH + RET_V_WIDTH:]).astype(BF16)

    cost = cost_ref[...] * (RET_QK_DIM ** -0.5)
    sint = sint_ref[...] * (RET_QK_DIM ** -0.5)
    for hd in range(RET_HEADS):
        sl = slice(hd * RET_QK_DIM, (hd + 1) * RET_QK_DIM)
        kh = _dot_nt(wkt_ref[sl, :], hn)
        swapped = jnp.concatenate([kh[half:], kh[:half]], axis=0)
        kt_ref[sl, :] = (kh * cost + swapped * sint).astype(BF16)


def _odd_in(h, g, wrow, wkt, cos, sin, cost, sint, seq, tm=256):
    n = h.shape[0]
    nt = seq // tm
    row = lambda w: pl.BlockSpec((tm, w), lambda i: (i, 0))
    return pl.pallas_call(
        _odd_in_kernel,
        grid=(n // tm,),
        in_specs=[row(D_MODEL), _full(g.shape), _full(wrow.shape), _full(wkt.shape),
                  pl.BlockSpec((tm, RET_QK_DIM), lambda i: (i % nt, 0)),
                  pl.BlockSpec((tm, RET_QK_DIM), lambda i: (i % nt, 0)),
                  pl.BlockSpec((RET_QK_DIM, tm), lambda i: (0, i % nt)),
                  pl.BlockSpec((RET_QK_DIM, tm), lambda i: (0, i % nt))],
        out_specs=[row(RET_QK_WIDTH), row(RET_V_WIDTH), row(RET_V_WIDTH),
                   pl.BlockSpec((RET_QK_WIDTH, tm), lambda i: (0, i))],
        out_shape=[jax.ShapeDtypeStruct((n, RET_QK_WIDTH), BF16), jax.ShapeDtypeStruct((n, RET_V_WIDTH), BF16),
                   jax.ShapeDtypeStruct((n, RET_V_WIDTH), BF16), jax.ShapeDtypeStruct((RET_QK_WIDTH, n), BF16)],
        compiler_params=_cparams(("parallel",)),
        name="odd_in_proj",
    )(h, g, wrow, wkt, cos, sin, cost, sint)


def _ret_kernel(cdec_ref, q_ref, kt_ref, v_ref, gate_ref, dintra_ref, qdec_ref, kdec_ref, gn_ref,
                y_ref, state_ref):
    @pl.when(pl.program_id(1) == 0)
    def _():
        state_ref[...] = jnp.zeros_like(state_ref)

    for h in range(RET_HEADS):
        ks = slice(h * RET_QK_DIM, (h + 1) * RET_QK_DIM)
        vs = slice(h * RET_V_DIM, (h + 1) * RET_V_DIM)
        qh = q_ref[:, ks]
        kth = kt_ref[ks, :]
        vh = v_ref[:, vs]
        state = state_ref[h]
        inner = (_dot(qh, kth) * dintra_ref[h]).astype(BF16)
        o = _dot(inner, vh) + _dot(qh, state.astype(BF16)) * qdec_ref[h]
        kd = (kth.astype(F32) * kdec_ref[h]).astype(BF16)
        state_ref[h] = state * cdec_ref[h] + _dot(kd, vh)
        mu = jnp.mean(o, axis=-1, keepdims=True)
        oc = o - mu
        var = jnp.mean(oc * oc, axis=-1, keepdims=True)
        on = oc * lax.rsqrt(var + GN_EPS) * gn_ref[:, vs]
        y_ref[:, vs] = (jax.nn.silu(gate_ref[:, vs].astype(F32)) * on).astype(BF16)


def _retention(cdec, q, kt, v, gate, dintra, qdec, kdec, gn, batch, seq):
    n = q.shape[0]
    c = RET_TILE
    nc = seq // c
    row = lambda w: pl.BlockSpec((c, w), lambda b, t: (b * nc + t, 0))
    return pl.pallas_call(
        _ret_kernel,
        grid=(batch, nc),
        in_specs=[pl.BlockSpec(memory_space=pltpu.SMEM),
                  row(RET_QK_WIDTH),
                  pl.BlockSpec((RET_QK_WIDTH, c), lambda b, t: (0, b * nc + t)),
                  row(RET_V_WIDTH), row(RET_V_WIDTH),
                  _full(dintra.shape), _full(qdec.shape), _full(kdec.shape), _full(gn.shape)],
        out_specs=row(RET_V_WIDTH),
        out_shape=jax.ShapeDtypeStruct((n, RET_V_WIDTH), BF16),
        scratch_shapes=[pltpu.VMEM((RET_HEADS, RET_QK_DIM, RET_V_DIM), F32)],
        compiler_params=_cparams(("parallel", "arbitrary")),
        name="retention",
    )(cdec, q, kt, v, gate, dintra, qdec, kdec, gn)


def _t5_bucket(rel):
    half = N_BUCKETS // 2
    max_exact = half // 2
    ret = (rel > 0).astype(I32) * half
    n = jnp.abs(rel)
    nf = jnp.maximum(n, 1).astype(F32)
    large = max_exact + (jnp.log(nf / max_exact) / math.log(MAX_DISTANCE / max_exact)
                         * (half - max_exact)).astype(I32)
    large = jnp.minimum(large, half - 1)
    return ret + jnp.where(n < max_exact, n, large)


def _bias_tables(rel_bias):
    c = jnp.arange(Q_TILE, dtype=I32)[:, None]
    r = jnp.arange(Q_TILE, dtype=I32)[None, :]
    rels = jnp.stack([c - Q_TILE - r, c - r])
    onehot = jax.nn.one_hot(_t5_bucket(rels), N_BUCKETS, dtype=F32)
    tab = jnp.einsum("abcn,nh->ahbc", onehot, rel_bias, precision=lax.Precision.HIGHEST)
    far = rel_bias[_t5_bucket(jnp.int32(-2 * Q_TILE))]
    return (tab * LOG2E).astype(F32), (far * LOG2E).astype(F32)


def _block_diag(w):
    nb, bs, _ = w.shape
    eye = jnp.eye(nb, dtype=w.dtype)
    return jnp.einsum("hij,hg->higj", w, eye).reshape(nb * bs, nb * bs)


def _router_weights(w_group, w_expert):
    wr = jnp.concatenate([w_expert, w_group, jnp.zeros((D_MODEL, LANES - N_EXPERTS - N_GROUPS), F32)], axis=1)
    hi = wr.astype(BF16)
    lo = (wr - hi.astype(F32)).astype(BF16)
    return hi, lo


def _rotary_tables(seq):
    half = RET_QK_DIM // 2
    inv = ROPE_BASE ** (-jnp.arange(half, dtype=F32) / half)
    ang = jnp.arange(seq, dtype=F32)[:, None] * inv[None, :]
    cos, sin = jnp.cos(ang), jnp.sin(ang)
    cos_row = jnp.concatenate([cos, cos], axis=1)
    sin_row = jnp.concatenate([-sin, sin], axis=1)
    cos_t = jnp.concatenate([cos.T, cos.T], axis=0)
    sin_t = jnp.concatenate([-sin.T, sin.T], axis=0)
    return cos_row, sin_row, cos_t, sin_t


def _retention_tables():
    c = RET_TILE
    log_g = jnp.log(1.0 - 2.0 ** (-5.0 - jnp.arange(RET_HEADS, dtype=F32)))
    pos = jnp.arange(c, dtype=F32)
    diff = pos[:, None] - pos[None, :]
    causal = diff >= 0
    dintra = jnp.where(causal[None], jnp.exp(jnp.where(causal, diff, 0.0)[None] * log_g[:, None, None]), 0.0)
    qdec = jnp.exp((pos + 1.0)[None, :, None] * log_g[:, None, None])
    kdec = jnp.exp((c - 1.0 - pos)[None, None, :] * log_g[:, None, None])
    cdec = jnp.exp(c * log_g)
    return dintra, qdec, kdec, cdec


def kernel(x, p, rel_bias, mix_norm_g, ffn_norm_g, ple_norm_g, ev_w_in, ev_conv_w, ev_conv_b, ev_lru_wa, ev_lru_ba, ev_lru_wx, ev_lru_bx, ev_lru_lambda, ev_q_norm_g, ev_k_norm_g, ev_w_out, od_w_in, od_gn_g, od_w_out, moe_w_group, moe_w_expert, moe_w_gate, moe_w_up, moe_w_down, ple_w_up, ple_w_gate):
    batch, seq, _ = x.shape
    n = batch * seq
    depth = p.shape[0]
    h = x.reshape(n, D_MODEL)
    row = lambda a: a.reshape(1, -1)

    tab, far = _bias_tables(rel_bias)
    cos_row, sin_row, cos_t, sin_t = _rotary_tables(seq)
    dintra, qdec, kdec, cdec = _retention_tables()
    head_ones = _block_diag(jnp.ones((ATT_HEADS, ATT_HEAD_DIM, ATT_HEAD_DIM), BF16))

    for i in range(depth):
        jdx = i // 2
        if i % 2 == 0:
            w = ev_w_in[jdx]
            o = [0, 512, 1024, 1536, 2048, 2560, 3072, 3136, 3144]
            xa_w, ga_w, q_w, k_w, v_w, iq_w, ik_w, iw_w = [w[:, o[t]:o[t + 1]] for t in range(8)]
            wrow = jnp.concatenate([xa_w, ga_w, k_w, ik_w, ik_w], axis=1).astype(BF16)
            wt = jnp.concatenate([q_w, iq_w, v_w, iw_w, jnp.zeros((D_MODEL, 16 - IDX_HEADS), F32)],
                                 axis=1).T.astype(BF16)
            qg = ev_q_norm_g[jdx].reshape(-1, 1) * (ATT_HEAD_DIM ** -0.5 * LOG2E)
            kg = jnp.tile(ev_k_norm_g[jdx], ATT_HEADS).reshape(1, -1)
            xg, k, ik, qt, iqt, vt, iwt = _even_in(h, row(mix_norm_g[i]), wrow, wt, head_ones, qg, kg)
            ya = _rglru(xg, ev_conv_w[jdx], row(ev_conv_b[jdx]),
                        _block_diag(ev_lru_wa[jdx]).astype(BF16), row(ev_lru_ba[jdx]),
                        _block_diag(ev_lru_wx[jdx]).astype(BF16), row(ev_lru_bx[jdx]),
                        row(ev_lru_lambda[jdx]), batch, seq)
            yb = _dsa(far, qt, iqt, iwt, k, ik, vt, tab, batch, seq)
            wo = ev_w_out[jdx].astype(BF16)
            ys, ws = [ya, yb], [wo[:LRU_WIDTH], wo[LRU_WIDTH:]]
        else:
            w = od_w_in[jdx]
            wrow = jnp.concatenate([w[:, :RET_QK_WIDTH], w[:, 2 * RET_QK_WIDTH:]], axis=1).astype(BF16)
            wkt = w[:, RET_QK_WIDTH:2 * RET_QK_WIDTH].T.astype(BF16)
            q, v, gate, kt = _odd_in(h, row(mix_norm_g[i]), wrow, wkt, cos_row, sin_row, cos_t, sin_t, seq)
            yc = _retention(cdec, q, kt, v, gate, dintra, qdec, kdec, row(od_gn_g[jdx]), batch, seq)
            ys, ws = [yc], [od_w_out[jdx].astype(BF16)]
        wr_hi, wr_lo = _router_weights(moe_w_group[i], moe_w_expert[i])
        h1, xn, route = _mix_out(h, ys, ws, row(ffn_norm_g[i]), wr_hi, wr_lo)
        yp = _moe_sparse(xn, route, moe_w_gate[i], moe_w_up[i], moe_w_down[i])
        h = _ple(h1, yp, route, p[i].reshape(n, PLE_DIM), ple_w_up[i].astype(BF16),
                 ple_w_gate[i].astype(BF16), row(ple_norm_g[i]))
    return h.reshape(batch, seq, D_MODEL)
```

```python
import functools
import math

import jax
import jax.numpy as jnp
from jax import lax
from jax.experimental import pallas as pl
from jax.experimental.pallas import tpu as pltpu
from jax.experimental.pallas import tpu_sc as plsc

F32 = jnp.float32
BF16 = jnp.bfloat16
I32 = jnp.int32

D_MODEL = 1024
CHUNK = 64
PLE_DIM = 256
RMS_EPS = 1e-6

LRU_WIDTH = 512
LRU_BLOCKS = 8
LRU_C = 8.0

ATT_HEADS = 8
ATT_HEAD_DIM = 64
ATT_WIDTH = 512
IDX_HEADS = 8
IDX_DIM = 64
MAX_TOP_K = 256
N_BUCKETS = 32
MAX_DISTANCE = 128

RET_HEADS = 8
RET_QK_DIM = 128
RET_V_DIM = 256
RET_QK_WIDTH = 1024
RET_V_WIDTH = 2048
ROPE_BASE = 10000.0
GN_EPS = 1e-5

N_GROUPS = 4
EXPERTS_PER_GROUP = 8
N_EXPERTS = 32
EXPERT_FF = 512

LANES = 128
INT_MIN = -(2 ** 31)
NEG_BIG = -1e30
LOG2E = 1.4426950408889634
VMEM_LIMIT = 56 * 1024 * 1024

SC_CORES = 2
SC_SUBCORES = 16
SC_CHUNK = 64

MIX_SUB = 256
Q_TILE = 128
K_TILE = 512
RET_TILE = 256


def _cparams(sem):
    return pltpu.CompilerParams(dimension_semantics=sem, vmem_limit_bytes=VMEM_LIMIT)


def _full(shape):
    nd = len(shape)
    return pl.BlockSpec(shape, lambda *_: (0,) * nd)


def _rms(xf, g):
    return xf * lax.rsqrt(jnp.mean(xf * xf, axis=-1, keepdims=True) + RMS_EPS) * g


def _dot(a, b):
    return jnp.dot(a, b, preferred_element_type=F32)


def _dot_nt(a, b):
    return lax.dot_general(a, b, (((1,), (1,)), ((), ())), preferred_element_type=F32)


def _neg_expm1(x):
    series = -x * (1.0 + x * (0.5 + x * (1.0 / 6.0 + x * (1.0 / 24.0 + x * (1.0 / 120.0)))))
    return jnp.where(x > -0.03, series, 1.0 - jnp.exp(x))


def _pack_halves(x):
    w = x.shape[1] // 2
    hi = pltpu.bitcast(x[:, :w].astype(BF16).astype(F32), I32)
    lo = pltpu.bitcast(x[:, w:].astype(BF16).astype(F32), I32)
    return hi | lax.shift_right_logical(lo, 16)


def _unpack_halves(p):
    hi = pltpu.bitcast(p & jnp.int32(-65536), F32)
    lo = pltpu.bitcast(lax.shift_left(p, 16), F32)
    return hi, lo


def _split_bf16(x):
    hi = x.astype(BF16)
    lo = (x - hi.astype(F32)).astype(BF16)
    return hi, lo


def _even_in_kernel(h_ref, g_ref, wrow_ref, wt_ref, bd_ref, qg_ref, kg_ref,
                    xg_ref, k_ref, ik_ref, qt_ref, iqt_ref, vt_ref, iwt_ref):
    tm = h_ref.shape[0]
    w = ATT_WIDTH
    hn = _rms(h_ref[...], g_ref[...]).astype(BF16)

    xg_ref[...] = _dot(hn, wrow_ref[:, :2 * LRU_WIDTH])
    kf = _dot(hn, wrow_ref[:, 2 * LRU_WIDTH:2 * LRU_WIDTH + w])
    hi, lo = _split_bf16(kf * kf)
    ss = _dot(hi, bd_ref[...]) + _dot(lo, bd_ref[...])
    k_ref[...] = (kf * lax.rsqrt(ss * (1.0 / ATT_HEAD_DIM) + RMS_EPS) * kg_ref[...]).astype(BF16)
    ik_ref[...] = _dot(hn, wrow_ref[:, 2 * LRU_WIDTH + w:]).astype(BF16)

    q3 = _dot_nt(wt_ref[0:w, :], hn).reshape(ATT_HEADS, ATT_HEAD_DIM, tm)
    ssq = jnp.sum(q3 * q3, axis=1, keepdims=True)
    qn = q3 * lax.rsqrt(ssq * (1.0 / ATT_HEAD_DIM) + RMS_EPS) * qg_ref[...][None]
    qt_ref[...] = qn.reshape(w, tm).astype(BF16)
    iqt_ref[...] = _dot_nt(wt_ref[w:2 * w, :], hn).astype(BF16)
    vt_ref[...] = _dot_nt(wt_ref[2 * w:3 * w, :], hn).astype(BF16)
    iwt = _dot_nt(wt_ref[3 * w:, :], hn)
    iwt_ref[...] = iwt[:IDX_HEADS, :] * (IDX_HEADS ** -0.5 * IDX_DIM ** -0.5)


def _even_in(h, g, wrow, wt, bd, qg, kg, tm=256):
    n = h.shape[0]
    row = lambda w: pl.BlockSpec((tm, w), lambda i: (i, 0))
    col = lambda r: pl.BlockSpec((r, tm), lambda i: (0, i))
    return pl.pallas_call(
        _even_in_kernel,
        grid=(n // tm,),
        in_specs=[row(D_MODEL), _full(g.shape), _full(wrow.shape), _full(wt.shape),
                  _full(bd.shape), _full(qg.shape), _full(kg.shape)],
        out_specs=[row(1024), row(512), row(LANES), col(512), col(512), col(512), col(IDX_HEADS)],
        out_shape=[jax.ShapeDtypeStruct((n, 1024), F32), jax.ShapeDtypeStruct((n, 512), BF16),
                   jax.ShapeDtypeStruct((n, LANES), BF16), jax.ShapeDtypeStruct((512, n), BF16),
                   jax.ShapeDtypeStruct((512, n), BF16), jax.ShapeDtypeStruct((512, n), BF16),
                   jax.ShapeDtypeStruct((IDX_HEADS, n), F32)],
        compiler_params=_cparams(("parallel",)),
        name="even_in_proj",
    )(h, g, wrow, wt, bd, qg, kg)


def _rglru_kernel(xg_ref, cw_ref, cb_ref, wa_ref, ba_ref, wx_ref, bx_ref, lam_ref, ya_ref,
                  tail_ref, hst_ref):
    ts = xg_ref.shape[0]
    w = LRU_WIDTH

    @pl.when(pl.program_id(1) == 0)
    def _():
        tail_ref[...] = jnp.zeros_like(tail_ref)
        hst_ref[...] = jnp.zeros_like(hst_ref)

    xa = xg_ref[:, :w]
    ga = xg_ref[:, w:]
    row = lax.broadcasted_iota(I32, (ts, w), 0)
    tail = tail_ref[...]
    xc = xa * cw_ref[3:4, :] + cb_ref[...]
    for d in (1, 2, 3):
        cur = pltpu.roll(xa, d, 0)
        prev = jnp.concatenate([pltpu.roll(tail, d, 0), jnp.zeros((ts - 8, w), F32)], axis=0)
        xc = xc + jnp.where(row < d, prev, cur) * cw_ref[3 - d:4 - d, :]
    tail_ref[...] = xa[ts - 8:, :]

    xcb = xc.astype(BF16)
    r = jax.nn.sigmoid(_dot(xcb, wa_ref[...]) + ba_ref[...])
    gi = jax.nn.sigmoid(_dot(xcb, wx_ref[...]) + bx_ref[...])
    nl = -lam_ref[...]
    softplus = jnp.maximum(nl, 0.0) + jnp.log1p(jnp.exp(-jnp.abs(nl)))
    log_a = (-LRU_C) * r * softplus
    a = jnp.exp(log_a)
    u = jnp.sqrt(_neg_expm1(2.0 * log_a)) * (gi * xc)

    d = 1
    while d < ts:
        keep = row >= d
        a_sh = jnp.where(keep, pltpu.roll(a, d, 0), 1.0)
        u_sh = jnp.where(keep, pltpu.roll(u, d, 0), 0.0)
        u = a * u_sh + u
        a = a * a_sh
        d *= 2
    hseq = a * hst_ref[0:1, :] + u
    hst_ref[0:1, :] = hseq[ts - 1:ts, :]
    ya_ref[...] = (hseq * jax.nn.gelu(ga)).astype(BF16)


def _rglru(xg, cw, cb, wa, ba, wx, bx, lam, batch, seq, ts=256):
    n = xg.shape[0]
    nt = seq // ts
    return pl.pallas_call(
        _rglru_kernel,
        grid=(batch, nt),
        in_specs=[pl.BlockSpec((ts, 1024), lambda b, t: (b * nt + t, 0)),
                  _full(cw.shape), _full(cb.shape), _full(wa.shape), _full(ba.shape),
                  _full(wx.shape), _full(bx.shape), _full(lam.shape)],
        out_specs=pl.BlockSpec((ts, LRU_WIDTH), lambda b, t: (b * nt + t, 0)),
        out_shape=jax.ShapeDtypeStruct((n, LRU_WIDTH), BF16),
        scratch_shapes=[pltpu.VMEM((8, LRU_WIDTH), F32), pltpu.VMEM((8, LRU_WIDTH), F32)],
        compiler_params=_cparams(("parallel", "arbitrary")),
        name="rglru",
    )(xg, cw, cb, wa, ba, wx, bx, lam)


def _dsa_kernel(far_ref, qt_ref, iqt_ref, iwt_ref, k_ref, ik_ref, vt_ref, tab_ref, o_ref,
                keys_ref, sel_ref, cut_ref, *, seq, top_k):
    tq = Q_TILE
    hd = ATT_HEAD_DIM
    j = pl.program_id(1)
    q0 = j * tq
    nkt = (j + K_TILE // tq) // (K_TILE // tq)
    qcol = q0 + lax.broadcasted_iota(I32, (1, tq), 1)
    qlim = (qcol // CHUNK + 1) * CHUNK
    zero_rows = jnp.zeros((hd, tq), BF16)

    def head_rows(ref, h):
        blk = ref[h * hd:(h + 1) * hd, :]
        return jnp.concatenate([blk, zero_rows] if h % 2 == 0 else [zero_rows, blk], axis=0)

    def ktile(kt):
        return pl.multiple_of(kt * K_TILE, K_TILE)

    def key_pos(off, rows):
        return off + lax.broadcasted_iota(I32, (rows, tq), 0)

    iq_m = [head_rows(iqt_ref, h) for h in range(IDX_HEADS)]

    def score_body(kt, c):
        off = ktile(kt)
        ik = ik_ref[pl.ds(off, K_TILE), :]
        s = jnp.zeros((K_TILE, tq), F32)
        for h in range(IDX_HEADS):
            s = s + jnp.maximum(_dot(ik, iq_m[h]), 0.0) * iwt_ref[h:h + 1, :]
        s = jnp.where(s == 0.0, 0.0, s)
        bits = pltpu.bitcast(s, I32)
        key = bits ^ ((bits >> 31) & 0x7FFFFFFF)
        keys_ref[pl.ds(off, K_TILE), :] = jnp.where(key_pos(off, K_TILE) < qlim, key, INT_MIN)
        return c

    lax.fori_loop(0, nkt, score_body, 0)

    def count(pred):
        def body(kt, acc):
            off = ktile(kt)
            ind = pred(keys_ref[pl.ds(off, K_TILE), :], key_pos(off, K_TILE)).reshape(K_TILE // 8, 8, tq)
            while ind.shape[0] > 1:
                half = ind.shape[0] // 2
                ind = ind[:half] + ind[half:]
            return acc + ind[0]
        acc = lax.fori_loop(0, nkt, body, jnp.zeros((8, tq), F32))
        return jnp.sum(acc, axis=0, keepdims=True)

    kf = float(top_k)

    def search_body(i, ans):
        cand = ans + jnp.left_shift(jnp.int32(1), 31 - i)
        cnt = count(lambda kk, kpos: jnp.where(kk >= cand, 1.0, 0.0))
        return jnp.where(cnt >= kf, cand, ans)

    ans = lax.fori_loop(0, 32, search_body, jnp.full((1, tq), INT_MIN, I32))

    cnt_ge = count(lambda kk, kpos: jnp.where(kk >= ans, 1.0, 0.0))
    excess = jnp.where(ans > INT_MIN, cnt_ge - kf, 0.0)
    cut_ref[...] = jnp.full(cut_ref.shape, seq, I32)

    @pl.when(jnp.max(excess) > 0.0)
    def _():
        need = kf - count(lambda kk, kpos: jnp.where(kk > ans, 1.0, 0.0))

        def idx_body(i, pos):
            cand = pos + jnp.left_shift(jnp.int32(1), (seq.bit_length() - 2) - i)
            c = count(lambda kk, kpos: jnp.where(kk == ans, jnp.where(kpos < cand, 1.0, 0.0), 0.0))
            return jnp.where(c < need, cand, pos)

        pos = lax.fori_loop(0, seq.bit_length() - 1, idx_body, jnp.zeros((1, tq), I32))
        cut_ref[...] = jnp.broadcast_to(pos, cut_ref.shape)

    thr = jnp.maximum(ans, INT_MIN + 1)
    cut = cut_ref[0:1, :]

    near_w = 2 * tq
    near_start = pl.multiple_of(jnp.maximum(j - 1, 0) * tq, tq)

    def selection(kk, kpos):
        tie = jnp.where(kpos <= cut, 0.0, NEG_BIG)
        return jnp.where(kk > thr, 0.0, jnp.where(kk == thr, tie, NEG_BIG))

    def sel_body(kt, c):
        off = ktile(kt)
        kpos = key_pos(off, K_TILE)
        s = selection(keys_ref[pl.ds(off, K_TILE), :], kpos)
        sel_ref[pl.ds(off, K_TILE), :] = jnp.where(kpos < near_start, s, NEG_BIG)
        return c

    n_far = (jnp.maximum(j - 1, 0) + K_TILE // tq - 1) // (K_TILE // tq)
    lax.fori_loop(0, n_far, sel_body, 0)
    sel_near = selection(keys_ref[pl.ds(near_start, near_w), :], key_pos(near_start, near_w))

    def step(qm, kb, vte, add, carry):
        m, acc = carry
        s = _dot(kb, qm) + add
        m_new = jnp.maximum(m, jnp.max(s, axis=0, keepdims=True))
        p = jnp.exp2(s - m_new).astype(BF16)
        acc = jnp.exp2(m - m_new) * acc + _dot(vte, p)
        return m_new, acc

    qms = [head_rows(qt_ref, h) for h in range(ATT_HEADS)]
    pairs = [slice((h // 2) * LANES, (h // 2 + 1) * LANES) for h in range(ATT_HEADS)]

    def values(h, off, width):
        return jnp.concatenate([vt_ref[h * hd:(h + 1) * hd, pl.ds(off, width)],
                                jnp.ones((hd, width), BF16)], axis=0)

    def far_body(kt, carries):
        off = ktile(kt)
        sel = sel_ref[pl.ds(off, K_TILE), :]
        return tuple(step(qms[h], k_ref[pl.ds(off, K_TILE), pairs[h]], values(h, off, K_TILE), sel, carries[h])
                     for h in range(ATT_HEADS))

    init = (jnp.full((1, tq), NEG_BIG, F32), jnp.zeros((2 * hd, tq), F32))
    carries = lax.fori_loop(0, n_far, far_body, (init,) * ATT_HEADS)

    first = jnp.where(j == 0, 1, 0)
    res = []
    for h in range(ATT_HEADS):
        m, acc = carries[h]
        bias = jnp.concatenate([tab_ref[first, h], tab_ref[1, h]], axis=0)
        m, acc = step(qms[h], k_ref[pl.ds(near_start, near_w), pairs[h]], values(h, near_start, near_w),
                      sel_near + bias, (m + far_ref[h], acc))
        res.append(acc[:hd, :] / acc[hd:hd + 1, :])

    for p in range(ATT_HEADS // 2):
        pair_t = jnp.concatenate([res[2 * p], res[2 * p + 1]], axis=0)
        o_ref[:, p * LANES:(p + 1) * LANES] = pair_t.T.astype(BF16)


def _dsa(far, qt, iqt, iwt, k, ik, vt, tab, batch, seq):
    n = k.shape[0]
    nb = seq // Q_TILE
    top_k = min(MAX_TOP_K, seq // 4)
    qspec = lambda r: pl.BlockSpec((r, Q_TILE), lambda b, j: (0, b * nb + j))
    return pl.pallas_call(
        functools.partial(_dsa_kernel, seq=seq, top_k=top_k),
        grid=(batch, nb),
        in_specs=[pl.BlockSpec(memory_space=pltpu.SMEM),
                  qspec(512), qspec(512), qspec(IDX_HEADS),
                  pl.BlockSpec((seq, 512), lambda b, j: (b, 0)),
                  pl.BlockSpec((seq, LANES), lambda b, j: (b, 0)),
                  pl.BlockSpec((512, seq), lambda b, j: (0, b)),
                  _full(tab.shape)],
        out_specs=pl.BlockSpec((Q_TILE, 512), lambda b, j: (b * nb + j, 0)),
        out_shape=jax.ShapeDtypeStruct((n, 512), BF16),
        scratch_shapes=[pltpu.VMEM((seq, Q_TILE), I32), pltpu.VMEM((seq, Q_TILE), F32),
                        pltpu.VMEM((8, Q_TILE), I32)],
        compiler_params=_cparams(("parallel", "arbitrary")),
        name="dsa",
    )(far, qt, iqt, iwt, k, ik, vt, tab)


def _mix_out_kernel(*refs, n_y):
    h_ref = refs[0]
    y_refs = refs[1:1 + n_y]
    w_refs = refs[1 + n_y:1 + 2 * n_y]
    g_ref, wr_hi_ref, wr_lo_ref = refs[1 + 2 * n_y:4 + 2 * n_y]
    h1_ref, xn_ref, route_ref = refs[4 + 2 * n_y:]
    for sub in range(h_ref.shape[0] // MIX_SUB):
        rs = slice(sub * MIX_SUB, (sub + 1) * MIX_SUB)
        _mix_out_rows(rs, h_ref, y_refs, w_refs, g_ref, wr_hi_ref, wr_lo_ref, h1_ref, xn_ref, route_ref)


def _mix_out_rows(rs, h_ref, y_refs, w_refs, g_ref, wr_hi_ref, wr_lo_ref, h1_ref, xn_ref, route_ref):
    tm = rs.stop - rs.start
    mix = _dot(y_refs[0][rs, :], w_refs[0][...])
    for y_ref, w_ref in zip(y_refs[1:], w_refs[1:]):
        mix = mix + _dot(y_ref[rs, :], w_ref[...])
    h1 = h_ref[rs, :] + mix
    h1_ref[rs, :] = h1
    xn = _rms(h1, g_ref[...])
    hi, lo = _split_bf16(xn)
    xn_ref[rs, :] = _pack_halves(xn)
    logits = _dot(hi, wr_hi_ref[...]) + _dot(lo, wr_hi_ref[...]) + _dot(hi, wr_lo_ref[...])

    lane = lax.broadcasted_iota(I32, (tm, LANES), 1)
    big = jnp.int32(LANES)
    ninf = -jnp.inf
    is_g = (lane >= N_EXPERTS) & (lane < N_EXPERTS + N_GROUPS)
    glog = jnp.where(is_g, logits, ninf)
    gmax = jnp.max(glog, axis=1, keepdims=True)
    gsel = jnp.min(jnp.where(glog == gmax, lane, big), axis=1, keepdims=True) - N_EXPERTS
    gprob = 1.0 / jnp.sum(jnp.exp(glog - gmax), axis=1, keepdims=True)
    lo_l = gsel * EXPERTS_PER_GROUP
    within = jnp.where((lane >= lo_l) & (lane < lo_l + EXPERTS_PER_GROUP), logits, ninf)
    v1 = jnp.max(within, axis=1, keepdims=True)
    i1 = jnp.min(jnp.where(within == v1, lane, big), axis=1, keepdims=True)
    rest = jnp.where(lane == i1, ninf, within)
    v2 = jnp.max(rest, axis=1, keepdims=True)
    i2 = jnp.min(jnp.where(rest == v2, lane, big), axis=1, keepdims=True)
    e2 = jnp.exp(v2 - v1)
    w1 = gprob / (1.0 + e2)
    w2 = gprob * e2 / (1.0 + e2)
    route_ref[rs, :] = jnp.where(lane == 0, i1.astype(F32), jnp.where(lane == 1, i2.astype(F32),
                                 jnp.where(lane == 2, w1, jnp.where(lane == 3, w2, 0.0))))


def _mix_out(h, ys, ws, g, wr_hi, wr_lo, tm=512):
    n = h.shape[0]
    n_y = len(ys)
    row = lambda w: pl.BlockSpec((tm, w), lambda i: (i, 0))
    return pl.pallas_call(
        functools.partial(_mix_out_kernel, n_y=n_y),
        grid=(n // tm,),
        in_specs=[row(D_MODEL)] + [row(y.shape[1]) for y in ys] + [_full(w.shape) for w in ws]
                 + [_full(g.shape), _full(wr_hi.shape), _full(wr_lo.shape)],
        out_specs=[row(D_MODEL), row(D_MODEL // 2), row(LANES)],
        out_shape=[jax.ShapeDtypeStruct((n, D_MODEL), F32), jax.ShapeDtypeStruct((n, D_MODEL // 2), I32),
                   jax.ShapeDtypeStruct((n, LANES), F32)],
        compiler_params=_cparams(("parallel",)),
        name="mix_out_router",
    )(h, *ys, *ws, g, wr_hi, wr_lo)


def _sc_rows(src, idx, scatter):
    r, d = idx.shape[0], src.shape[1]
    nw = SC_CORES * SC_SUBCORES
    per_w = r // nw
    n_chunks = per_w // SC_CHUNK
    idx3 = idx.reshape(nw, n_chunks, SC_CHUNK)
    mesh = plsc.VectorSubcoreMesh(core_axis_name="c", subcore_axis_name="s")

    @functools.partial(
        pl.kernel, mesh=mesh, out_type=jax.ShapeDtypeStruct((r, d), src.dtype),
        scratch_types=[pltpu.VMEM((n_chunks, SC_CHUNK), I32), pltpu.VMEM((SC_CHUNK, d), src.dtype),
                       pltpu.SemaphoreType.DMA])
    def permute(src_hbm, idx_hbm, out_hbm, idx_v, rows_v, sem):
        wid = lax.axis_index("s") * SC_CORES + lax.axis_index("c")
        pltpu.sync_copy(idx_hbm.at[wid], idx_v)

        @pl.loop(0, n_chunks)
        def _(c):
            lin = pl.ds(pl.multiple_of(wid * per_w + c * SC_CHUNK, SC_CHUNK), SC_CHUNK)
            if scatter:
                pltpu.sync_copy(src_hbm.at[lin], rows_v)
                pltpu.async_copy(rows_v, out_hbm.at[idx_v.at[c]], sem).wait()
            else:
                pltpu.async_copy(src_hbm.at[idx_v.at[c]], rows_v, sem).wait()
                pltpu.sync_copy(rows_v, out_hbm.at[lin])

    return permute(src, idx3)


def _moe_kernel(tile_ref, exp_ref, flag_ref, off_ref, xs_ref, wg_ref, wu_ref, wd_ref, ys_ref,
                wgb_ref, wub_ref, wdb_ref, acc_ref):
    w = pl.program_id(0)
    tm = xs_ref.shape[0]
    e = exp_ref[w]
    flags = flag_ref[w]

    @pl.when((flags & 4) != 0)
    def _():
        wgb_ref[...] = wg_ref[0, 0].astype(BF16)
        wub_ref[...] = wu_ref[0, 0].astype(BF16)
        wdb_ref[...] = wd_ref[0, 0].astype(BF16)

    @pl.when((flags & 1) != 0)
    def _():
        xa, xb = _unpack_halves(xs_ref[...])
        xa = xa.astype(BF16)
        xb = xb.astype(BF16)
        half = D_MODEL // 2
        gate = _dot(xa, wgb_ref[:half, :]) + _dot(xb, wgb_ref[half:, :])
        up = _dot(xa, wub_ref[:half, :]) + _dot(xb, wub_ref[half:, :])
        hid = (jax.nn.silu(gate) * up).astype(BF16)
        y = _dot(hid, wdb_ref[...])
        rows = tile_ref[w] * tm + lax.broadcasted_iota(I32, (tm, 1), 0)
        mine = (rows >= off_ref[e]) & (rows < off_ref[e + 1])
        y = jnp.where(mine, y, 0.0)

        @pl.when((flags & 2) != 0)
        def _():
            acc_ref[...] = y

        @pl.when((flags & 2) == 0)
        def _():
            acc_ref[...] += y

        ys_ref[...] = _pack_halves(acc_ref[...])


def _moe_plan(sorted_e, tm):
    n_rows = sorted_e.shape[0]
    nt = n_rows // tm
    n_items = nt + N_EXPERTS - 1
    off = jnp.searchsorted(sorted_e, jnp.arange(N_EXPERTS + 1, dtype=I32), side="left").astype(I32)
    e_lo = sorted_e[::tm]
    e_hi = sorted_e[tm - 1::tm]
    span = e_hi - e_lo + 1
    start = jnp.cumsum(span) - span
    total = jnp.sum(span)
    w = jnp.arange(n_items, dtype=I32)
    valid = w < total
    tile = jnp.clip(jnp.searchsorted(start, w, side="right").astype(I32) - 1, 0, nt - 1)
    expert = jnp.where(valid, e_lo[tile] + (w - start[tile]), e_hi[nt - 1])
    first = valid & (w == start[tile])
    new_e = jnp.concatenate([jnp.ones((1,), bool), expert[1:] != expert[:-1]])
    flags = valid.astype(I32) + 2 * first.astype(I32) + 4 * new_e.astype(I32)
    return tile, expert, flags, off


def _moe(xs, tile, expert, flags, off, wg, wu, wd, layer, tm):
    n_rows, half = xs.shape
    n_items = tile.shape[0]
    grid_spec = pltpu.PrefetchScalarGridSpec(
        num_scalar_prefetch=4,
        grid=(n_items,),
        in_specs=[pl.BlockSpec((tm, half), lambda w, t, e, f, o: (t[w], 0)),
                  pl.BlockSpec((1, 1, D_MODEL, EXPERT_FF), lambda w, t, e, f, o: (layer, e[w], 0, 0)),
                  pl.BlockSpec((1, 1, D_MODEL, EXPERT_FF), lambda w, t, e, f, o: (layer, e[w], 0, 0)),
                  pl.BlockSpec((1, 1, EXPERT_FF, D_MODEL), lambda w, t, e, f, o: (layer, e[w], 0, 0))],
        out_specs=pl.BlockSpec((tm, half), lambda w, t, e, f, o: (t[w], 0)),
        scratch_shapes=[pltpu.VMEM((D_MODEL, EXPERT_FF), BF16), pltpu.VMEM((D_MODEL, EXPERT_FF), BF16),
                        pltpu.VMEM((EXPERT_FF, D_MODEL), BF16), pltpu.VMEM((tm, D_MODEL), F32)])
    return pl.pallas_call(
        _moe_kernel,
        grid_spec=grid_spec,
        out_shape=jax.ShapeDtypeStruct((n_rows, half), I32),
        compiler_params=_cparams(("arbitrary",)),
        name="moe_experts",
    )(tile, expert, flags, off, xs, wg, wu, wd)


def _moe_sparse(xn_packed, route, wg, wu, wd, layer, tm=256):
    n = xn_packed.shape[0]
    eid = jnp.concatenate([route[:, 0], route[:, 1]]).astype(I32)
    sorted_e, sorted_pair = lax.sort_key_val(eid, jnp.arange(2 * n, dtype=I32))
    token = jnp.where(sorted_pair >= n, sorted_pair - n, sorted_pair)
    tile, expert, flags, off = _moe_plan(sorted_e, tm)
    xs = _sc_rows(xn_packed, token, scatter=False)
    ys = _moe(xs, tile, expert, flags, off, wg, wu, wd, layer, tm)
    return _sc_rows(ys, sorted_pair, scatter=True)


def _ple_kernel(h1_ref, y0_ref, y1_ref, route_ref, p_ref, wup_ref, wgate_ref, g_ref, o_ref):
    w1 = route_ref[:, 2:3]
    w2 = route_ref[:, 3:4]
    a0, b0 = _unpack_halves(y0_ref[...])
    a1, b1 = _unpack_halves(y1_ref[...])
    moe = jnp.concatenate([w1 * a0 + w2 * a1, w1 * b0 + w2 * b1], axis=1)
    h2 = h1_ref[...] + moe
    e = _rms(_dot(p_ref[...].astype(BF16), wup_ref[...]), g_ref[...])
    gate = jax.nn.sigmoid(_dot(h2.astype(BF16), wgate_ref[...]))
    o_ref[...] = h2 + e * gate


def _ple(h1, yp, route, p_all, layer, wup, wgate, g, tm=512):
    n = h1.shape[0]
    nt = n // tm
    row = lambda w: pl.BlockSpec((tm, w), lambda i: (i, 0))
    return pl.pallas_call(
        _ple_kernel,
        grid=(nt,),
        in_specs=[row(D_MODEL), row(D_MODEL // 2),
                  pl.BlockSpec((tm, D_MODEL // 2), lambda i: (i + nt, 0)),
                  row(LANES), pl.BlockSpec((tm, PLE_DIM), lambda i: (i + layer * nt, 0)),
                  _full(wup.shape), _full(wgate.shape), _full(g.shape)],
        out_specs=row(D_MODEL),
        out_shape=jax.ShapeDtypeStruct((n, D_MODEL), F32),
        compiler_params=_cparams(("parallel",)),
        name="ple",
    )(h1, yp, yp, route, p_all, wup, wgate, g)


def _odd_in_kernel(h_ref, g_ref, wrow_ref, wkt_ref, cos_ref, sin_ref, cost_ref, sint_ref,
                   q_ref, v_ref, gate_ref, kt_ref):
    tm = h_ref.shape[0]
    hn = _rms(h_ref[...], g_ref[...]).astype(BF16)
    cos = cos_ref[...]
    sin = sin_ref[...]
    half = RET_QK_DIM // 2
    for hd in range(RET_HEADS):
        sl = slice(hd * RET_QK_DIM, (hd + 1) * RET_QK_DIM)
        qh = _dot(hn, wrow_ref[:, sl])
        q_ref[:, sl] = (qh * cos + pltpu.roll(qh, half, 1) * sin).astype(BF16)
    v_ref[...] = _dot(hn, wrow_ref[:, RET_QK_WIDTH:RET_QK_WIDTH + RET_V_WIDTH]).astype(BF16)
    gate_ref[...] = _dot(hn, wrow_ref[:, RET_QK_WIDTH + RET_V_WIDTH:]).astype(BF16)

    cost = cost_ref[...] * (RET_QK_DIM ** -0.5)
    sint = sint_ref[...] * (RET_QK_DIM ** -0.5)
    for hd in range(RET_HEADS):
        sl = slice(hd * RET_QK_DIM, (hd + 1) * RET_QK_DIM)
        kh = _dot_nt(wkt_ref[sl, :], hn)
        swapped = jnp.concatenate([kh[half:], kh[:half]], axis=0)
        kt_ref[sl, :] = (kh * cost + swapped * sint).astype(BF16)


def _odd_in(h, g, wrow, wkt, cos, sin, cost, sint, seq, tm=256):
    n = h.shape[0]
    nt = seq // tm
    row = lambda w: pl.BlockSpec((tm, w), lambda i: (i, 0))
    return pl.pallas_call(
        _odd_in_kernel,
        grid=(n // tm,),
        in_specs=[row(D_MODEL), _full(g.shape), _full(wrow.shape), _full(wkt.shape),
                  pl.BlockSpec((tm, RET_QK_DIM), lambda i: (i % nt, 0)),
                  pl.BlockSpec((tm, RET_QK_DIM), lambda i: (i % nt, 0)),
                  pl.BlockSpec((RET_QK_DIM, tm), lambda i: (0, i % nt)),
                  pl.BlockSpec((RET_QK_DIM, tm), lambda i: (0, i % nt))],
        out_specs=[row(RET_QK_WIDTH), row(RET_V_WIDTH), row(RET_V_WIDTH),
                   pl.BlockSpec((RET_QK_WIDTH, tm), lambda i: (0, i))],
        out_shape=[jax.ShapeDtypeStruct((n, RET_QK_WIDTH), BF16), jax.ShapeDtypeStruct((n, RET_V_WIDTH), BF16),
                   jax.ShapeDtypeStruct((n, RET_V_WIDTH), BF16), jax.ShapeDtypeStruct((RET_QK_WIDTH, n), BF16)],
        compiler_params=_cparams(("parallel",)),
        name="odd_in_proj",
    )(h, g, wrow, wkt, cos, sin, cost, sint)


def _ret_kernel(cdec_ref, q_ref, kt_ref, v_ref, gate_ref, dintra_ref, qdec_ref, kdec_ref, gn_ref,
                y_ref, state_ref):
    @pl.when(pl.program_id(1) == 0)
    def _():
        state_ref[...] = jnp.zeros_like(state_ref)

    for h in range(RET_HEADS):
        ks = slice(h * RET_QK_DIM, (h + 1) * RET_QK_DIM)
        vs = slice(h * RET_V_DIM, (h + 1) * RET_V_DIM)
        qh = q_ref[:, ks]
        kth = kt_ref[ks, :]
        vh = v_ref[:, vs]
        state = state_ref[h]
        inner = (_dot(qh, kth) * dintra_ref[h]).astype(BF16)
        o = _dot(inner, vh) + _dot(qh, state.astype(BF16)) * qdec_ref[h]
        kd = (kth.astype(F32) * kdec_ref[h]).astype(BF16)
        state_ref[h] = state * cdec_ref[h] + _dot(kd, vh)
        mu = jnp.mean(o, axis=-1, keepdims=True)
        oc = o - mu
        var = jnp.mean(oc * oc, axis=-1, keepdims=True)
        on = oc * lax.rsqrt(var + GN_EPS) * gn_ref[:, vs]
        y_ref[:, vs] = (jax.nn.silu(gate_ref[:, vs].astype(F32)) * on).astype(BF16)


def _retention(cdec, q, kt, v, gate, dintra, qdec, kdec, gn, batch, seq):
    n = q.shape[0]
    c = RET_TILE
    nc = seq // c
    row = lambda w: pl.BlockSpec((c, w), lambda b, t: (b * nc + t, 0))
    return pl.pallas_call(
        _ret_kernel,
        grid=(batch, nc),
        in_specs=[pl.BlockSpec(memory_space=pltpu.SMEM),
                  row(RET_QK_WIDTH),
                  pl.BlockSpec((RET_QK_WIDTH, c), lambda b, t: (0, b * nc + t)),
                  row(RET_V_WIDTH), row(RET_V_WIDTH),
                  _full(dintra.shape), _full(qdec.shape), _full(kdec.shape), _full(gn.shape)],
        out_specs=row(RET_V_WIDTH),
        out_shape=jax.ShapeDtypeStruct((n, RET_V_WIDTH), BF16),
        scratch_shapes=[pltpu.VMEM((RET_HEADS, RET_QK_DIM, RET_V_DIM), F32)],
        compiler_params=_cparams(("parallel", "arbitrary")),
        name="retention",
    )(cdec, q, kt, v, gate, dintra, qdec, kdec, gn)


def _t5_bucket(rel):
    half = N_BUCKETS // 2
    max_exact = half // 2
    ret = (rel > 0).astype(I32) * half
    n = jnp.abs(rel)
    nf = jnp.maximum(n, 1).astype(F32)
    large = max_exact + (jnp.log(nf / max_exact) / math.log(MAX_DISTANCE / max_exact)
                         * (half - max_exact)).astype(I32)
    large = jnp.minimum(large, half - 1)
    return ret + jnp.where(n < max_exact, n, large)


def _bias_tables(rel_bias):
    c = jnp.arange(Q_TILE, dtype=I32)[:, None]
    r = jnp.arange(Q_TILE, dtype=I32)[None, :]
    rels = jnp.stack([c - Q_TILE - r, c - r])
    onehot = jax.nn.one_hot(_t5_bucket(rels), N_BUCKETS, dtype=F32)
    tab = jnp.einsum("abcn,nh->ahbc", onehot, rel_bias, precision=lax.Precision.HIGHEST)
    far = rel_bias[_t5_bucket(jnp.int32(-2 * Q_TILE))]
    return (tab * LOG2E).astype(F32), (far * LOG2E).astype(F32)


def _block_diag(w):
    nb, bs, _ = w.shape
    eye = jnp.eye(nb, dtype=w.dtype)
    return jnp.einsum("hij,hg->higj", w, eye).reshape(nb * bs, nb * bs)


def _router_weights(w_group, w_expert):
    wr = jnp.concatenate([w_expert, w_group, jnp.zeros((D_MODEL, LANES - N_EXPERTS - N_GROUPS), F32)], axis=1)
    hi = wr.astype(BF16)
    lo = (wr - hi.astype(F32)).astype(BF16)
    return hi, lo


def _rotary_tables(seq):
    half = RET_QK_DIM // 2
    inv = ROPE_BASE ** (-jnp.arange(half, dtype=F32) / half)
    ang = jnp.arange(seq, dtype=F32)[:, None] * inv[None, :]
    cos, sin = jnp.cos(ang), jnp.sin(ang)
    cos_row = jnp.concatenate([cos, cos], axis=1)
    sin_row = jnp.concatenate([-sin, sin], axis=1)
    cos_t = jnp.concatenate([cos.T, cos.T], axis=0)
    sin_t = jnp.concatenate([-sin.T, sin.T], axis=0)
    return cos_row, sin_row, cos_t, sin_t


def _retention_tables():
    c = RET_TILE
    log_g = jnp.log(1.0 - 2.0 ** (-5.0 - jnp.arange(RET_HEADS, dtype=F32)))
    pos = jnp.arange(c, dtype=F32)
    diff = pos[:, None] - pos[None, :]
    causal = diff >= 0
    dintra = jnp.where(causal[None], jnp.exp(jnp.where(causal, diff, 0.0)[None] * log_g[:, None, None]), 0.0)
    qdec = jnp.exp((pos + 1.0)[None, :, None] * log_g[:, None, None])
    kdec = jnp.exp((c - 1.0 - pos)[None, None, :] * log_g[:, None, None])
    cdec = jnp.exp(c * log_g)
    return dintra, qdec, kdec, cdec


def kernel(x, p, rel_bias, mix_norm_g, ffn_norm_g, ple_norm_g, ev_w_in, ev_conv_w, ev_conv_b, ev_lru_wa, ev_lru_ba, ev_lru_wx, ev_lru_bx, ev_lru_lambda, ev_q_norm_g, ev_k_norm_g, ev_w_out, od_w_in, od_gn_g, od_w_out, moe_w_group, moe_w_expert, moe_w_gate, moe_w_up, moe_w_down, ple_w_up, ple_w_gate):
    batch, seq, _ = x.shape
    n = batch * seq
    depth = p.shape[0]
    h = x.reshape(n, D_MODEL)
    row = lambda a: a.reshape(1, -1)

    tab, far = _bias_tables(rel_bias)
    cos_row, sin_row, cos_t, sin_t = _rotary_tables(seq)
    dintra, qdec, kdec, cdec = _retention_tables()
    head_ones = _block_diag(jnp.ones((ATT_HEADS, ATT_HEAD_DIM, ATT_HEAD_DIM), BF16))

    for i in range(depth):
        jdx = i // 2
        if i % 2 == 0:
            w = ev_w_in[jdx]
            o = [0, 512, 1024, 1536, 2048, 2560, 3072, 3136, 3144]
            xa_w, ga_w, q_w, k_w, v_w, iq_w, ik_w, iw_w = [w[:, o[t]:o[t + 1]] for t in range(8)]
            wrow = jnp.concatenate([xa_w, ga_w, k_w, ik_w, ik_w], axis=1).astype(BF16)
            wt = jnp.concatenate([q_w, iq_w, v_w, iw_w, jnp.zeros((D_MODEL, 16 - IDX_HEADS), F32)],
                                 axis=1).T.astype(BF16)
            qg = ev_q_norm_g[jdx].reshape(-1, 1) * (ATT_HEAD_DIM ** -0.5 * LOG2E)
            kg = jnp.tile(ev_k_norm_g[jdx], ATT_HEADS).reshape(1, -1)
            xg, k, ik, qt, iqt, vt, iwt = _even_in(h, row(mix_norm_g[i]), wrow, wt, head_ones, qg, kg)
            ya = _rglru(xg, ev_conv_w[jdx], row(ev_conv_b[jdx]),
                        _block_diag(ev_lru_wa[jdx]).astype(BF16), row(ev_lru_ba[jdx]),
                        _block_diag(ev_lru_wx[jdx]).astype(BF16), row(ev_lru_bx[jdx]),
                        row(ev_lru_lambda[jdx]), batch, seq)
            yb = _dsa(far, qt, iqt, iwt, k, ik, vt, tab, batch, seq)
            wo = ev_w_out[jdx].astype(BF16)
            ys, ws = [ya, yb], [wo[:LRU_WIDTH], wo[LRU_WIDTH:]]
        else:
            w = od_w_in[jdx]
            wrow = jnp.concatenate([w[:, :RET_QK_WIDTH], w[:, 2 * RET_QK_WIDTH:]], axis=1).astype(BF16)
            wkt = w[:, RET_QK_WIDTH:2 * RET_QK_WIDTH].T.astype(BF16)
            q, v, gate, kt = _odd_in(h, row(mix_norm_g[i]), wrow, wkt, cos_row, sin_row, cos_t, sin_t, seq)
            yc = _retention(cdec, q, kt, v, gate, dintra, qdec, kdec, row(od_gn_g[jdx]), batch, seq)
            ys, ws = [yc], [od_w_out[jdx].astype(BF16)]
        wr_hi, wr_lo = _router_weights(moe_w_group[i], moe_w_expert[i])
        h1, xn, route = _mix_out(h, ys, ws, row(ffn_norm_g[i]), wr_hi, wr_lo)
        yp = _moe_sparse(xn, route, moe_w_gate, moe_w_up, moe_w_down, i)
        h = _ple(h1, yp, route, p.reshape(depth * n, PLE_DIM), i, ple_w_up[i].astype(BF16),
                 ple_w_gate[i].astype(BF16), row(ple_norm_g[i]))
    return h.reshape(batch, seq, D_MODEL)
```

```python
import functools
import math

import jax
import jax.numpy as jnp
from jax import lax
from jax.experimental import pallas as pl
from jax.experimental.pallas import tpu as pltpu
from jax.experimental.pallas import tpu_sc as plsc

F32 = jnp.float32
BF16 = jnp.bfloat16
I32 = jnp.int32

D_MODEL = 1024
CHUNK = 64
PLE_DIM = 256
RMS_EPS = 1e-6

LRU_WIDTH = 512
LRU_BLOCKS = 8
LRU_C = 8.0

ATT_HEADS = 8
ATT_HEAD_DIM = 64
ATT_WIDTH = 512
IDX_HEADS = 8
IDX_DIM = 64
MAX_TOP_K = 256
N_BUCKETS = 32
MAX_DISTANCE = 128

RET_HEADS = 8
RET_QK_DIM = 128
RET_V_DIM = 256
RET_QK_WIDTH = 1024
RET_V_WIDTH = 2048
ROPE_BASE = 10000.0
GN_EPS = 1e-5

N_GROUPS = 4
EXPERTS_PER_GROUP = 8
N_EXPERTS = 32
EXPERT_FF = 512

LANES = 128
INT_MIN = -(2 ** 31)
NEG_BIG = -1e30
LOG2E = 1.4426950408889634
VMEM_LIMIT = 56 * 1024 * 1024

SC_CORES = 2
SC_SUBCORES = 16
SC_CHUNK = 64

MIX_SUB = 256
Q_TILE = 128
BIAS_BLOCK = 128
K_TILE = 512
RET_TILE = 256


def _cparams(sem):
    return pltpu.CompilerParams(dimension_semantics=sem, vmem_limit_bytes=VMEM_LIMIT)


def _full(shape):
    nd = len(shape)
    return pl.BlockSpec(shape, lambda *_: (0,) * nd)


def _rms(xf, g):
    return xf * lax.rsqrt(jnp.mean(xf * xf, axis=-1, keepdims=True) + RMS_EPS) * g


def _dot(a, b):
    return jnp.dot(a, b, preferred_element_type=F32)


def _dot_nt(a, b):
    return lax.dot_general(a, b, (((1,), (1,)), ((), ())), preferred_element_type=F32)


def _neg_expm1(x):
    series = -x * (1.0 + x * (0.5 + x * (1.0 / 6.0 + x * (1.0 / 24.0 + x * (1.0 / 120.0)))))
    return jnp.where(x > -0.03, series, 1.0 - jnp.exp(x))


def _pack_halves(x):
    w = x.shape[1] // 2
    hi = pltpu.bitcast(x[:, :w].astype(BF16).astype(F32), I32)
    lo = pltpu.bitcast(x[:, w:].astype(BF16).astype(F32), I32)
    return hi | lax.shift_right_logical(lo, 16)


def _unpack_halves(p):
    hi = pltpu.bitcast(p & jnp.int32(-65536), F32)
    lo = pltpu.bitcast(lax.shift_left(p, 16), F32)
    return hi, lo


def _split_bf16(x):
    hi = x.astype(BF16)
    lo = (x - hi.astype(F32)).astype(BF16)
    return hi, lo


def _even_in_kernel(h_ref, g_ref, wrow_ref, wt_ref, bd_ref, qg_ref, kg_ref,
                    xg_ref, k_ref, ik_ref, qt_ref, iqt_ref, vt_ref, iwt_ref):
    tm = h_ref.shape[0]
    w = ATT_WIDTH
    hn = _rms(h_ref[...], g_ref[...]).astype(BF16)

    xg_ref[...] = _dot(hn, wrow_ref[:, :2 * LRU_WIDTH])
    kf = _dot(hn, wrow_ref[:, 2 * LRU_WIDTH:2 * LRU_WIDTH + w])
    hi, lo = _split_bf16(kf * kf)
    ss = _dot(hi, bd_ref[...]) + _dot(lo, bd_ref[...])
    k_ref[...] = (kf * lax.rsqrt(ss * (1.0 / ATT_HEAD_DIM) + RMS_EPS) * kg_ref[...]).astype(BF16)
    ik_ref[...] = _dot(hn, wrow_ref[:, 2 * LRU_WIDTH + w:]).astype(BF16)

    q3 = _dot_nt(wt_ref[0:w, :], hn).reshape(ATT_HEADS, ATT_HEAD_DIM, tm)
    ssq = jnp.sum(q3 * q3, axis=1, keepdims=True)
    qn = q3 * lax.rsqrt(ssq * (1.0 / ATT_HEAD_DIM) + RMS_EPS) * qg_ref[...][None]
    qt_ref[...] = qn.reshape(w, tm).astype(BF16)
    iqt_ref[...] = _dot_nt(wt_ref[w:2 * w, :], hn).astype(BF16)
    vt_ref[...] = _dot_nt(wt_ref[2 * w:3 * w, :], hn).astype(BF16)
    iwt = _dot_nt(wt_ref[3 * w:, :], hn)
    iwt_ref[...] = iwt[:IDX_HEADS, :] * (IDX_HEADS ** -0.5 * IDX_DIM ** -0.5)


def _even_in(h, g, wrow, wt, bd, qg, kg, tm=512):
    n = h.shape[0]
    row = lambda w: pl.BlockSpec((tm, w), lambda i: (i, 0))
    col = lambda r: pl.BlockSpec((r, tm), lambda i: (0, i))
    return pl.pallas_call(
        _even_in_kernel,
        grid=(n // tm,),
        in_specs=[row(D_MODEL), _full(g.shape), _full(wrow.shape), _full(wt.shape),
                  _full(bd.shape), _full(qg.shape), _full(kg.shape)],
        out_specs=[row(1024), row(512), row(LANES), col(512), col(512), col(512), col(IDX_HEADS)],
        out_shape=[jax.ShapeDtypeStruct((n, 1024), F32), jax.ShapeDtypeStruct((n, 512), BF16),
                   jax.ShapeDtypeStruct((n, LANES), BF16), jax.ShapeDtypeStruct((512, n), BF16),
                   jax.ShapeDtypeStruct((512, n), BF16), jax.ShapeDtypeStruct((512, n), BF16),
                   jax.ShapeDtypeStruct((IDX_HEADS, n), F32)],
        compiler_params=_cparams(("parallel",)),
        name="even_in_proj",
    )(h, g, wrow, wt, bd, qg, kg)


def _rglru_kernel(xg_ref, cw_ref, cb_ref, wa_ref, ba_ref, wx_ref, bx_ref, lam_ref, ya_ref,
                  tail_ref, hst_ref):
    ts = xg_ref.shape[0]
    w = LRU_WIDTH

    @pl.when(pl.program_id(1) == 0)
    def _():
        tail_ref[...] = jnp.zeros_like(tail_ref)
        hst_ref[...] = jnp.zeros_like(hst_ref)

    xa = xg_ref[:, :w]
    ga = xg_ref[:, w:]
    row = lax.broadcasted_iota(I32, (ts, w), 0)
    tail = tail_ref[...]
    xc = xa * cw_ref[3:4, :] + cb_ref[...]
    for d in (1, 2, 3):
        cur = pltpu.roll(xa, d, 0)
        prev = jnp.concatenate([pltpu.roll(tail, d, 0), jnp.zeros((ts - 8, w), F32)], axis=0)
        xc = xc + jnp.where(row < d, prev, cur) * cw_ref[3 - d:4 - d, :]
    tail_ref[...] = xa[ts - 8:, :]

    xcb = xc.astype(BF16)
    r = jax.nn.sigmoid(_dot(xcb, wa_ref[...]) + ba_ref[...])
    gi = jax.nn.sigmoid(_dot(xcb, wx_ref[...]) + bx_ref[...])
    nl = -lam_ref[...]
    softplus = jnp.maximum(nl, 0.0) + jnp.log1p(jnp.exp(-jnp.abs(nl)))
    log_a = (-LRU_C) * r * softplus
    a = jnp.exp(log_a)
    u = jnp.sqrt(_neg_expm1(2.0 * log_a)) * (gi * xc)

    d = 1
    while d < ts:
        keep = row >= d
        a_sh = jnp.where(keep, pltpu.roll(a, d, 0), 1.0)
        u_sh = jnp.where(keep, pltpu.roll(u, d, 0), 0.0)
        u = a * u_sh + u
        a = a * a_sh
        d *= 2
    hseq = a * hst_ref[0:1, :] + u
    hst_ref[0:1, :] = hseq[ts - 1:ts, :]
    ya_ref[...] = (hseq * jax.nn.gelu(ga)).astype(BF16)


def _rglru(xg, cw, cb, wa, ba, wx, bx, lam, batch, seq, ts=256):
    n = xg.shape[0]
    nt = seq // ts
    return pl.pallas_call(
        _rglru_kernel,
        grid=(batch, nt),
        in_specs=[pl.BlockSpec((ts, 1024), lambda b, t: (b * nt + t, 0)),
                  _full(cw.shape), _full(cb.shape), _full(wa.shape), _full(ba.shape),
                  _full(wx.shape), _full(bx.shape), _full(lam.shape)],
        out_specs=pl.BlockSpec((ts, LRU_WIDTH), lambda b, t: (b * nt + t, 0)),
        out_shape=jax.ShapeDtypeStruct((n, LRU_WIDTH), BF16),
        scratch_shapes=[pltpu.VMEM((8, LRU_WIDTH), F32), pltpu.VMEM((8, LRU_WIDTH), F32)],
        compiler_params=_cparams(("parallel", "arbitrary")),
        name="rglru",
    )(xg, cw, cb, wa, ba, wx, bx, lam)


def _dsa_kernel(far_ref, qt_ref, iqt_ref, iwt_ref, k_ref, ik_ref, vt_ref, tab_ref, o_ref,
                keys_ref, sel_ref, cut_ref, *, seq, top_k):
    tq = Q_TILE
    hd = ATT_HEAD_DIM
    j = pl.program_id(1)
    q0 = j * tq
    nkt = (q0 + tq + K_TILE - 1) // K_TILE
    qcol = q0 + lax.broadcasted_iota(I32, (1, tq), 1)
    qlim = (qcol // CHUNK + 1) * CHUNK
    zero_rows = jnp.zeros((hd, tq), BF16)

    def head_rows(ref, h):
        blk = ref[h * hd:(h + 1) * hd, :]
        return jnp.concatenate([blk, zero_rows] if h % 2 == 0 else [zero_rows, blk], axis=0)

    def ktile(kt):
        return pl.multiple_of(kt * K_TILE, K_TILE)

    def key_pos(off, rows):
        return off + lax.broadcasted_iota(I32, (rows, tq), 0)

    iq_m = [head_rows(iqt_ref, h) for h in range(IDX_HEADS)]

    def score_body(kt, c):
        off = ktile(kt)
        ik = ik_ref[pl.ds(off, K_TILE), :]
        s = jnp.zeros((K_TILE, tq), F32)
        for h in range(IDX_HEADS):
            s = s + jnp.maximum(_dot(ik, iq_m[h]), 0.0) * iwt_ref[h:h + 1, :]
        s = jnp.where(s == 0.0, 0.0, s)
        bits = pltpu.bitcast(s, I32)
        key = bits ^ ((bits >> 31) & 0x7FFFFFFF)
        keys_ref[pl.ds(off, K_TILE), :] = jnp.where(key_pos(off, K_TILE) < qlim, key, INT_MIN)
        return c

    lax.fori_loop(0, nkt, score_body, 0)

    def count(pred):
        def body(kt, acc):
            off = ktile(kt)
            ind = pred(keys_ref[pl.ds(off, K_TILE), :], key_pos(off, K_TILE)).reshape(K_TILE // 8, 8, tq)
            while ind.shape[0] > 1:
                half = ind.shape[0] // 2
                ind = ind[:half] + ind[half:]
            return acc + ind[0]
        acc = lax.fori_loop(0, nkt, body, jnp.zeros((8, tq), F32))
        return jnp.sum(acc, axis=0, keepdims=True)

    kf = float(top_k)

    def search_body(i, ans):
        cand = ans + jnp.left_shift(jnp.int32(1), 31 - i)
        cnt = count(lambda kk, kpos: jnp.where(kk >= cand, 1.0, 0.0))
        return jnp.where(cnt >= kf, cand, ans)

    ans = lax.fori_loop(0, 32, search_body, jnp.full((1, tq), INT_MIN, I32))

    cnt_ge = count(lambda kk, kpos: jnp.where(kk >= ans, 1.0, 0.0))
    excess = jnp.where(ans > INT_MIN, cnt_ge - kf, 0.0)
    cut_ref[...] = jnp.full(cut_ref.shape, seq, I32)

    @pl.when(jnp.max(excess) > 0.0)
    def _():
        need = kf - count(lambda kk, kpos: jnp.where(kk > ans, 1.0, 0.0))

        def idx_body(i, pos):
            cand = pos + jnp.left_shift(jnp.int32(1), (seq.bit_length() - 2) - i)
            c = count(lambda kk, kpos: jnp.where(kk == ans, jnp.where(kpos < cand, 1.0, 0.0), 0.0))
            return jnp.where(c < need, cand, pos)

        pos = lax.fori_loop(0, seq.bit_length() - 1, idx_body, jnp.zeros((1, tq), I32))
        cut_ref[...] = jnp.broadcast_to(pos, cut_ref.shape)

    thr = jnp.maximum(ans, INT_MIN + 1)
    cut = cut_ref[0:1, :]

    near_w = tq + BIAS_BLOCK
    near_start = pl.multiple_of(jnp.maximum(q0 - BIAS_BLOCK, 0), BIAS_BLOCK)

    def selection(kk, kpos):
        tie = jnp.where(kpos <= cut, 0.0, NEG_BIG)
        return jnp.where(kk > thr, 0.0, jnp.where(kk == thr, tie, NEG_BIG))

    def sel_body(kt, c):
        off = ktile(kt)
        kpos = key_pos(off, K_TILE)
        s = selection(keys_ref[pl.ds(off, K_TILE), :], kpos)
        sel_ref[pl.ds(off, K_TILE), :] = jnp.where(kpos < near_start, s, NEG_BIG)
        return c

    n_far = (near_start + K_TILE - 1) // K_TILE
    lax.fori_loop(0, n_far, sel_body, 0)
    sel_near = selection(keys_ref[pl.ds(near_start, near_w), :], key_pos(near_start, near_w))

    def step(qm, kb, vte, add, carry):
        m, acc = carry
        s = _dot(kb, qm) + add
        m_new = jnp.maximum(m, jnp.max(s, axis=0, keepdims=True))
        p = jnp.exp2(s - m_new).astype(BF16)
        acc = jnp.exp2(m - m_new) * acc + _dot(vte, p)
        return m_new, acc

    qms = [head_rows(qt_ref, h) for h in range(ATT_HEADS)]
    pairs = [slice((h // 2) * LANES, (h // 2 + 1) * LANES) for h in range(ATT_HEADS)]

    def values(h, off, width):
        return jnp.concatenate([vt_ref[h * hd:(h + 1) * hd, pl.ds(off, width)],
                                jnp.ones((hd, width), BF16)], axis=0)

    def far_body(kt, carries):
        off = ktile(kt)
        sel = sel_ref[pl.ds(off, K_TILE), :]
        return tuple(step(qms[h], k_ref[pl.ds(off, K_TILE), pairs[h]], values(h, off, K_TILE), sel, carries[h])
                     for h in range(ATT_HEADS))

    init = (jnp.full((1, tq), NEG_BIG, F32), jnp.zeros((2 * hd, tq), F32))
    carries = lax.fori_loop(0, n_far, far_body, (init,) * ATT_HEADS)

    n_blocks = near_w // BIAS_BLOCK
    first = jnp.where(j == 0, 1, 0)
    res = []
    for h in range(ATT_HEADS):
        m, acc = carries[h]
        bias = jnp.concatenate([tab_ref[jnp.minimum(first + b, n_blocks - 1), h] for b in range(n_blocks)],
                               axis=0)
        m, acc = step(qms[h], k_ref[pl.ds(near_start, near_w), pairs[h]], values(h, near_start, near_w),
                      sel_near + bias, (m + far_ref[h], acc))
        res.append(acc[:hd, :] / acc[hd:hd + 1, :])

    for p in range(ATT_HEADS // 2):
        pair_t = jnp.concatenate([res[2 * p], res[2 * p + 1]], axis=0)
        o_ref[:, p * LANES:(p + 1) * LANES] = pair_t.T.astype(BF16)


def _dsa(far, qt, iqt, iwt, k, ik, vt, tab, batch, seq):
    n = k.shape[0]
    nb = seq // Q_TILE
    top_k = min(MAX_TOP_K, seq // 4)
    qspec = lambda r: pl.BlockSpec((r, Q_TILE), lambda b, j: (0, b * nb + j))
    return pl.pallas_call(
        functools.partial(_dsa_kernel, seq=seq, top_k=top_k),
        grid=(batch, nb),
        in_specs=[pl.BlockSpec(memory_space=pltpu.SMEM),
                  qspec(512), qspec(512), qspec(IDX_HEADS),
                  pl.BlockSpec((seq, 512), lambda b, j: (b, 0)),
                  pl.BlockSpec((seq, LANES), lambda b, j: (b, 0)),
                  pl.BlockSpec((512, seq), lambda b, j: (0, b)),
                  _full(tab.shape)],
        out_specs=pl.BlockSpec((Q_TILE, 512), lambda b, j: (b * nb + j, 0)),
        out_shape=jax.ShapeDtypeStruct((n, 512), BF16),
        scratch_shapes=[pltpu.VMEM((seq, Q_TILE), I32), pltpu.VMEM((seq, Q_TILE), F32),
                        pltpu.VMEM((8, Q_TILE), I32)],
        compiler_params=_cparams(("parallel", "arbitrary")),
        name="dsa",
    )(far, qt, iqt, iwt, k, ik, vt, tab)


def _mix_out_kernel(*refs, n_y):
    h_ref = refs[0]
    y_refs = refs[1:1 + n_y]
    w_refs = refs[1 + n_y:1 + 2 * n_y]
    g_ref, wr_hi_ref, wr_lo_ref, tri_ref = refs[1 + 2 * n_y:5 + 2 * n_y]
    h1_ref, xn_ref, route_ref, cnt_out_ref, cnt_ref = refs[5 + 2 * n_y:]

    @pl.when(pl.program_id(0) == 0)
    def _():
        cnt_ref[...] = jnp.zeros_like(cnt_ref)

    for sub in range(h_ref.shape[0] // MIX_SUB):
        rs = slice(sub * MIX_SUB, (sub + 1) * MIX_SUB)
        _mix_out_rows(rs, h_ref, y_refs, w_refs, g_ref, wr_hi_ref, wr_lo_ref, tri_ref, h1_ref, xn_ref,
                      route_ref, cnt_ref)
    cnt_out_ref[...] = cnt_ref[...]


def _mix_out_rows(rs, h_ref, y_refs, w_refs, g_ref, wr_hi_ref, wr_lo_ref, tri_ref, h1_ref, xn_ref,
                  route_ref, cnt_ref):
    tm = rs.stop - rs.start
    mix = _dot(y_refs[0][rs, :], w_refs[0][...])
    for y_ref, w_ref in zip(y_refs[1:], w_refs[1:]):
        mix = mix + _dot(y_ref[rs, :], w_ref[...])
    h1 = h_ref[rs, :] + mix
    h1_ref[rs, :] = h1
    xn = _rms(h1, g_ref[...])
    hi, lo = _split_bf16(xn)
    xn_ref[rs, :] = _pack_halves(xn)
    logits = _dot(hi, wr_hi_ref[...]) + _dot(lo, wr_hi_ref[...]) + _dot(hi, wr_lo_ref[...])

    lane = lax.broadcasted_iota(I32, (tm, LANES), 1)
    big = jnp.int32(LANES)
    ninf = -jnp.inf
    is_g = (lane >= N_EXPERTS) & (lane < N_EXPERTS + N_GROUPS)
    glog = jnp.where(is_g, logits, ninf)
    gmax = jnp.max(glog, axis=1, keepdims=True)
    gsel = jnp.min(jnp.where(glog == gmax, lane, big), axis=1, keepdims=True) - N_EXPERTS
    gprob = 1.0 / jnp.sum(jnp.exp(glog - gmax), axis=1, keepdims=True)
    lo_l = gsel * EXPERTS_PER_GROUP
    within = jnp.where((lane >= lo_l) & (lane < lo_l + EXPERTS_PER_GROUP), logits, ninf)
    v1 = jnp.max(within, axis=1, keepdims=True)
    i1 = jnp.min(jnp.where(within == v1, lane, big), axis=1, keepdims=True)
    rest = jnp.where(lane == i1, ninf, within)
    v2 = jnp.max(rest, axis=1, keepdims=True)
    i2 = jnp.min(jnp.where(rest == v2, lane, big), axis=1, keepdims=True)
    e2 = jnp.exp(v2 - v1)
    w1 = gprob / (1.0 + e2)
    w2 = gprob * e2 / (1.0 + e2)

    ranks = []
    for slot, idx in enumerate((i1, i2)):
        onehot = jnp.where(lane == idx, 1.0, 0.0)
        before = _dot(tri_ref[...], onehot.astype(BF16)) + cnt_ref[slot:slot + 1, :]
        ranks.append(jnp.sum(onehot * before, axis=1, keepdims=True))
        cnt_ref[slot:slot + 1, :] += jnp.sum(onehot, axis=0, keepdims=True)

    cols = (i1.astype(F32), i2.astype(F32), w1, w2, ranks[0], ranks[1])
    route = jnp.zeros((tm, LANES), F32)
    for c, col in enumerate(cols):
        route = jnp.where(lane == c, col, route)
    route_ref[rs, :] = route


def _mix_out(h, ys, ws, g, wr_hi, wr_lo, tm=512):
    n = h.shape[0]
    n_y = len(ys)
    row = lambda w: pl.BlockSpec((tm, w), lambda i: (i, 0))
    r = jnp.arange(MIX_SUB, dtype=I32)
    tri = (r[None, :] < r[:, None]).astype(BF16)
    return pl.pallas_call(
        functools.partial(_mix_out_kernel, n_y=n_y),
        grid=(n // tm,),
        in_specs=[row(D_MODEL)] + [row(y.shape[1]) for y in ys] + [_full(w.shape) for w in ws]
                 + [_full(g.shape), _full(wr_hi.shape), _full(wr_lo.shape), _full(tri.shape)],
        out_specs=[row(D_MODEL), row(D_MODEL // 2), row(LANES), _full((8, LANES))],
        out_shape=[jax.ShapeDtypeStruct((n, D_MODEL), F32), jax.ShapeDtypeStruct((n, D_MODEL // 2), I32),
                   jax.ShapeDtypeStruct((n, LANES), F32), jax.ShapeDtypeStruct((8, LANES), F32)],
        scratch_shapes=[pltpu.VMEM((8, LANES), F32)],
        compiler_params=_cparams(("arbitrary",)),
        name="mix_out_router",
    )(h, *ys, *ws, g, wr_hi, wr_lo, tri)


def _sc_rows(src, idx, scatter):
    r, d = idx.shape[0], src.shape[1]
    n_src = src.shape[0]
    nw = SC_CORES * SC_SUBCORES
    per_w = r // nw
    n_chunks = per_w // SC_CHUNK
    assert r % nw == 0 and per_w % SC_CHUNK == 0 and (not scatter or n_src % per_w == 0)
    idx3 = idx.reshape(nw, n_chunks, SC_CHUNK)
    mesh = plsc.VectorSubcoreMesh(core_axis_name="c", subcore_axis_name="s")

    @functools.partial(
        pl.kernel, mesh=mesh, out_type=jax.ShapeDtypeStruct((r, d), src.dtype),
        scratch_types=[pltpu.VMEM((n_chunks, SC_CHUNK), I32), pltpu.VMEM((SC_CHUNK, d), src.dtype),
                       pltpu.SemaphoreType.DMA])
    def permute(src_hbm, idx_hbm, out_hbm, idx_v, rows_v, sem):
        wid = lax.axis_index("s") * SC_CORES + lax.axis_index("c")
        pltpu.sync_copy(idx_hbm.at[wid], idx_v)

        @pl.loop(0, n_chunks)
        def _(c):
            base = wid * per_w + c * SC_CHUNK
            lin = pl.ds(pl.multiple_of(base, SC_CHUNK), SC_CHUNK)
            if scatter:
                src_lin = pl.ds(pl.multiple_of(lax.rem(base, n_src), SC_CHUNK), SC_CHUNK)
                pltpu.sync_copy(src_hbm.at[src_lin], rows_v)
                pltpu.async_copy(rows_v, out_hbm.at[idx_v.at[c]], sem).wait()
            else:
                pltpu.async_copy(src_hbm.at[idx_v.at[c]], rows_v, sem).wait()
                pltpu.sync_copy(rows_v, out_hbm.at[lin])

    return permute(src, idx3)


def _moe_kernel(tile_ref, exp_ref, flag_ref, off_ref, xs_ref, wg_ref, wu_ref, wd_ref, ys_ref,
                wgb_ref, wub_ref, wdb_ref, acc_ref):
    w = pl.program_id(0)
    tm = xs_ref.shape[0]
    e = exp_ref[w]
    flags = flag_ref[w]

    @pl.when((flags & 4) != 0)
    def _():
        wgb_ref[...] = wg_ref[0, 0].astype(BF16)
        wub_ref[...] = wu_ref[0, 0].astype(BF16)
        wdb_ref[...] = wd_ref[0, 0].astype(BF16)

    @pl.when((flags & 1) != 0)
    def _():
        xa, xb = _unpack_halves(xs_ref[...])
        xa = xa.astype(BF16)
        xb = xb.astype(BF16)
        half = D_MODEL // 2
        gate = _dot(xa, wgb_ref[:half, :]) + _dot(xb, wgb_ref[half:, :])
        up = _dot(xa, wub_ref[:half, :]) + _dot(xb, wub_ref[half:, :])
        hid = (jax.nn.silu(gate) * up).astype(BF16)
        y = _dot(hid, wdb_ref[...])
        rows = tile_ref[w] * tm + lax.broadcasted_iota(I32, (tm, 1), 0)
        mine = (rows >= off_ref[e]) & (rows < off_ref[e + 1])
        y = jnp.where(mine, y, 0.0)

        @pl.when((flags & 2) != 0)
        def _():
            acc_ref[...] = y

        @pl.when((flags & 2) == 0)
        def _():
            acc_ref[...] += y

        ys_ref[...] = _pack_halves(acc_ref[...])


def _moe_plan(off, n_rows, tm):
    nt = n_rows // tm
    n_items = nt + N_EXPERTS - 1
    row_lo = jnp.arange(nt, dtype=I32) * tm
    ends = off[1:N_EXPERTS][None, :]
    e_lo = jnp.sum((ends <= row_lo[:, None]).astype(I32), axis=1)
    e_hi = jnp.sum((ends <= (row_lo + tm - 1)[:, None]).astype(I32), axis=1)
    span = e_hi - e_lo + 1
    start = jnp.cumsum(span) - span
    total = jnp.sum(span)
    w = jnp.arange(n_items, dtype=I32)
    valid = w < total
    tile = jnp.clip(jnp.searchsorted(start, w, side="right").astype(I32) - 1, 0, nt - 1)
    expert = jnp.where(valid, e_lo[tile] + (w - start[tile]), e_hi[nt - 1])
    first = valid & (w == start[tile])
    new_e = jnp.concatenate([jnp.ones((1,), bool), expert[1:] != expert[:-1]])
    flags = valid.astype(I32) + 2 * first.astype(I32) + 4 * new_e.astype(I32)
    return tile, expert, flags


def _moe(xs, tile, expert, flags, off, wg, wu, wd, layer, tm):
    n_rows, half = xs.shape
    n_items = tile.shape[0]
    grid_spec = pltpu.PrefetchScalarGridSpec(
        num_scalar_prefetch=4,
        grid=(n_items,),
        in_specs=[pl.BlockSpec((tm, half), lambda w, t, e, f, o: (t[w], 0)),
                  pl.BlockSpec((1, 1, D_MODEL, EXPERT_FF), lambda w, t, e, f, o: (layer, e[w], 0, 0)),
                  pl.BlockSpec((1, 1, D_MODEL, EXPERT_FF), lambda w, t, e, f, o: (layer, e[w], 0, 0)),
                  pl.BlockSpec((1, 1, EXPERT_FF, D_MODEL), lambda w, t, e, f, o: (layer, e[w], 0, 0))],
        out_specs=pl.BlockSpec((tm, half), lambda w, t, e, f, o: (t[w], 0)),
        scratch_shapes=[pltpu.VMEM((D_MODEL, EXPERT_FF), BF16), pltpu.VMEM((D_MODEL, EXPERT_FF), BF16),
                        pltpu.VMEM((EXPERT_FF, D_MODEL), BF16), pltpu.VMEM((tm, D_MODEL), F32)])
    return pl.pallas_call(
        _moe_kernel,
        grid_spec=grid_spec,
        out_shape=jax.ShapeDtypeStruct((n_rows, half), I32),
        compiler_params=_cparams(("arbitrary",)),
        name="moe_experts",
    )(tile, expert, flags, off, xs, wg, wu, wd)


def _moe_sparse(xn_packed, route, counts, wg, wu, wd, layer, tm=256):
    n = xn_packed.shape[0]
    experts = jnp.arange(N_EXPERTS, dtype=F32)
    c0, c1 = counts[0, :N_EXPERTS], counts[1, :N_EXPERTS]
    seg = jnp.cumsum(c0 + c1)
    start = seg - (c0 + c1)
    pos0 = route[:, 4] + jnp.sum(jnp.where(route[:, 0:1] == experts[None, :], start[None, :], 0.0), axis=1)
    pos1 = route[:, 5] + jnp.sum(jnp.where(route[:, 1:2] == experts[None, :], (start + c0)[None, :], 0.0), axis=1)
    pos = jnp.concatenate([pos0, pos1]).astype(I32)
    off = jnp.concatenate([start, seg[-1:]]).astype(I32)
    tile, expert, flags = _moe_plan(off, 2 * n, tm)
    xs = _sc_rows(xn_packed, pos, scatter=True)
    ys = _moe(xs, tile, expert, flags, off, wg, wu, wd, layer, tm)
    return _sc_rows(ys, pos, scatter=False)


def _ple_kernel(h1_ref, y0_ref, y1_ref, route_ref, p_ref, wup_ref, wgate_ref, g_ref, o_ref):
    w1 = route_ref[:, 2:3]
    w2 = route_ref[:, 3:4]
    a0, b0 = _unpack_halves(y0_ref[...])
    a1, b1 = _unpack_halves(y1_ref[...])
    moe = jnp.concatenate([w1 * a0 + w2 * a1, w1 * b0 + w2 * b1], axis=1)
    h2 = h1_ref[...] + moe
    e = _rms(_dot(p_ref[...].astype(BF16), wup_ref[...]), g_ref[...])
    gate = jax.nn.sigmoid(_dot(h2.astype(BF16), wgate_ref[...]))
    o_ref[...] = h2 + e * gate


def _ple(h1, yp, route, p_all, layer, wup, wgate, g, tm=512):
    n = h1.shape[0]
    nt = n // tm
    row = lambda w: pl.BlockSpec((tm, w), lambda i: (i, 0))
    return pl.pallas_call(
        _ple_kernel,
        grid=(nt,),
        in_specs=[row(D_MODEL), row(D_MODEL // 2),
                  pl.BlockSpec((tm, D_MODEL // 2), lambda i: (i + nt, 0)),
                  row(LANES), pl.BlockSpec((tm, PLE_DIM), lambda i: (i + layer * nt, 0)),
                  _full(wup.shape), _full(wgate.shape), _full(g.shape)],
        out_specs=row(D_MODEL),
        out_shape=jax.ShapeDtypeStruct((n, D_MODEL), F32),
        compiler_params=_cparams(("parallel",)),
        name="ple",
    )(h1, yp, yp, route, p_all, wup, wgate, g)


def _odd_in_kernel(h_ref, g_ref, wrow_ref, wkt_ref, cos_ref, sin_ref, cost_ref, sint_ref,
                   q_ref, v_ref, gate_ref, kt_ref):
    tm = h_ref.shape[0]
    hn = _rms(h_ref[...], g_ref[...]).astype(BF16)
    cos = cos_ref[...]
    sin = sin_ref[...]
    half = RET_QK_DIM // 2
    for hd in range(RET_HEADS):
        sl = slice(hd * RET_QK_DIM, (hd + 1) * RET_QK_DIM)
        qh = _dot(hn, wrow_ref[:, sl])
        q_ref[:, sl] = (qh * cos + pltpu.roll(qh, half, 1) * sin).astype(BF16)
    v_ref[...] = _dot(hn, wrow_ref[:, RET_QK_WIDTH:RET_QK_WIDTH + RET_V_WIDTH]).astype(BF16)
    gate_ref[...] = _dot(hn, wrow_ref[:, RET_QK_WIDTH + RET_V_WIDTH:]).astype(BF16)

    cost = cost_ref[...] * (RET_QK_DIM ** -0.5)
    sint = sint_ref[...] * (RET_QK_DIM ** -0.5)
    for hd in range(RET_HEADS):
        sl = slice(hd * RET_QK_DIM, (hd + 1) * RET_QK_DIM)
        kh = _dot_nt(wkt_ref[sl, :], hn)
        swapped = jnp.concatenate([kh[half:], kh[:half]], axis=0)
        kt_ref[sl, :] = (kh * cost + swapped * sint).astype(BF16)


def _odd_in(h, g, wrow, wkt, cos, sin, cost, sint, seq, tm=512):
    n = h.shape[0]
    nt = seq // tm
    row = lambda w: pl.BlockSpec((tm, w), lambda i: (i, 0))
    return pl.pallas_call(
        _odd_in_kernel,
        grid=(n // tm,),
        in_specs=[row(D_MODEL), _full(g.shape), _full(wrow.shape), _full(wkt.shape),
                  pl.BlockSpec((tm, RET_QK_DIM), lambda i: (i % nt, 0)),
                  pl.BlockSpec((tm, RET_QK_DIM), lambda i: (i % nt, 0)),
                  pl.BlockSpec((RET_QK_DIM, tm), lambda i: (0, i % nt)),
                  pl.BlockSpec((RET_QK_DIM, tm), lambda i: (0, i % nt))],
        out_specs=[row(RET_QK_WIDTH), row(RET_V_WIDTH), row(RET_V_WIDTH),
                   pl.BlockSpec((RET_QK_WIDTH, tm), lambda i: (0, i))],
        out_shape=[jax.ShapeDtypeStruct((n, RET_QK_WIDTH), BF16), jax.ShapeDtypeStruct((n, RET_V_WIDTH), BF16),
                   jax.ShapeDtypeStruct((n, RET_V_WIDTH), BF16), jax.ShapeDtypeStruct((RET_QK_WIDTH, n), BF16)],
        compiler_params=_cparams(("parallel",)),
        name="odd_in_proj",
    )(h, g, wrow, wkt, cos, sin, cost, sint)


def _ret_kernel(cdec_ref, q_ref, kt_ref, v_ref, gate_ref, dintra_ref, qdec_ref, kdec_ref, gn_ref,
                y_ref, state_ref):
    @pl.when(pl.program_id(1) == 0)
    def _():
        state_ref[...] = jnp.zeros_like(state_ref)

    for h in range(RET_HEADS):
        ks = slice(h * RET_QK_DIM, (h + 1) * RET_QK_DIM)
        vs = slice(h * RET_V_DIM, (h + 1) * RET_V_DIM)
        qh = q_ref[:, ks]
        kth = kt_ref[ks, :]
        vh = v_ref[:, vs]
        state = state_ref[h]
        inner = (_dot(qh, kth) * dintra_ref[h]).astype(BF16)
        o = _dot(inner, vh) + _dot(qh, state.astype(BF16)) * qdec_ref[h]
        kd = (kth.astype(F32) * kdec_ref[h]).astype(BF16)
        state_ref[h] = state * cdec_ref[h] + _dot(kd, vh)
        mu = jnp.mean(o, axis=-1, keepdims=True)
        oc = o - mu
        var = jnp.mean(oc * oc, axis=-1, keepdims=True)
        on = oc * lax.rsqrt(var + GN_EPS) * gn_ref[:, vs]
        y_ref[:, vs] = (jax.nn.silu(gate_ref[:, vs].astype(F32)) * on).astype(BF16)


def _retention(cdec, q, kt, v, gate, dintra, qdec, kdec, gn, batch, seq):
    n = q.shape[0]
    c = RET_TILE
    nc = seq // c
    row = lambda w: pl.BlockSpec((c, w), lambda b, t: (b * nc + t, 0))
    return pl.pallas_call(
        _ret_kernel,
        grid=(batch, nc),
        in_specs=[pl.BlockSpec(memory_space=pltpu.SMEM),
                  row(RET_QK_WIDTH),
                  pl.BlockSpec((RET_QK_WIDTH, c), lambda b, t: (0, b * nc + t)),
                  row(RET_V_WIDTH), row(RET_V_WIDTH),
                  _full(dintra.shape), _full(qdec.shape), _full(kdec.shape), _full(gn.shape)],
        out_specs=row(RET_V_WIDTH),
        out_shape=jax.ShapeDtypeStruct((n, RET_V_WIDTH), BF16),
        scratch_shapes=[pltpu.VMEM((RET_HEADS, RET_QK_DIM, RET_V_DIM), F32)],
        compiler_params=_cparams(("parallel", "arbitrary")),
        name="retention",
    )(cdec, q, kt, v, gate, dintra, qdec, kdec, gn)


def _t5_bucket(rel):
    half = N_BUCKETS // 2
    max_exact = half // 2
    ret = (rel > 0).astype(I32) * half
    n = jnp.abs(rel)
    nf = jnp.maximum(n, 1).astype(F32)
    large = max_exact + (jnp.log(nf / max_exact) / math.log(MAX_DISTANCE / max_exact)
                         * (half - max_exact)).astype(I32)
    large = jnp.minimum(large, half - 1)
    return ret + jnp.where(n < max_exact, n, large)


def _bias_tables(rel_bias):
    c = jnp.arange(BIAS_BLOCK, dtype=I32)[:, None]
    r = jnp.arange(Q_TILE, dtype=I32)[None, :]
    rels = jnp.stack([c + (b - 1) * BIAS_BLOCK - r for b in range(Q_TILE // BIAS_BLOCK + 1)])
    onehot = jax.nn.one_hot(_t5_bucket(rels), N_BUCKETS, dtype=F32)
    tab = jnp.einsum("abcn,nh->ahbc", onehot, rel_bias, precision=lax.Precision.HIGHEST)
    far = rel_bias[_t5_bucket(jnp.int32(-2 * MAX_DISTANCE))]
    return (tab * LOG2E).astype(F32), (far * LOG2E).astype(F32)


def _block_diag(w):
    nb, bs, _ = w.shape
    eye = jnp.eye(nb, dtype=w.dtype)
    return jnp.einsum("hij,hg->higj", w, eye).reshape(nb * bs, nb * bs)


def _router_weights(w_group, w_expert):
    wr = jnp.concatenate([w_expert, w_group, jnp.zeros((D_MODEL, LANES - N_EXPERTS - N_GROUPS), F32)], axis=1)
    hi = wr.astype(BF16)
    lo = (wr - hi.astype(F32)).astype(BF16)
    return hi, lo


def _rotary_tables(seq):
    half = RET_QK_DIM // 2
    inv = ROPE_BASE ** (-jnp.arange(half, dtype=F32) / half)
    ang = jnp.arange(seq, dtype=F32)[:, None] * inv[None, :]
    cos, sin = jnp.cos(ang), jnp.sin(ang)
    cos_row = jnp.concatenate([cos, cos], axis=1)
    sin_row = jnp.concatenate([-sin, sin], axis=1)
    cos_t = jnp.concatenate([cos.T, cos.T], axis=0)
    sin_t = jnp.concatenate([-sin.T, sin.T], axis=0)
    return cos_row, sin_row, cos_t, sin_t


def _retention_tables():
    c = RET_TILE
    log_g = jnp.log(1.0 - 2.0 ** (-5.0 - jnp.arange(RET_HEADS, dtype=F32)))
    pos = jnp.arange(c, dtype=F32)
    diff = pos[:, None] - pos[None, :]
    causal = diff >= 0
    dintra = jnp.where(causal[None], jnp.exp(jnp.where(causal, diff, 0.0)[None] * log_g[:, None, None]), 0.0)
    qdec = jnp.exp((pos + 1.0)[None, :, None] * log_g[:, None, None])
    kdec = jnp.exp((c - 1.0 - pos)[None, None, :] * log_g[:, None, None])
    cdec = jnp.exp(c * log_g)
    return dintra, qdec, kdec, cdec


def kernel(x, p, rel_bias, mix_norm_g, ffn_norm_g, ple_norm_g, ev_w_in, ev_conv_w, ev_conv_b, ev_lru_wa, ev_lru_ba, ev_lru_wx, ev_lru_bx, ev_lru_lambda, ev_q_norm_g, ev_k_norm_g, ev_w_out, od_w_in, od_gn_g, od_w_out, moe_w_group, moe_w_expert, moe_w_gate, moe_w_up, moe_w_down, ple_w_up, ple_w_gate):
    batch, seq, _ = x.shape
    n = batch * seq
    depth = p.shape[0]
    h = x.reshape(n, D_MODEL)
    row = lambda a: a.reshape(1, -1)

    tab, far = _bias_tables(rel_bias)
    cos_row, sin_row, cos_t, sin_t = _rotary_tables(seq)
    dintra, qdec, kdec, cdec = _retention_tables()
    head_ones = _block_diag(jnp.ones((ATT_HEADS, ATT_HEAD_DIM, ATT_HEAD_DIM), BF16))

    for i in range(depth):
        jdx = i // 2
        if i % 2 == 0:
            w = ev_w_in[jdx]
            o = [0, 512, 1024, 1536, 2048, 2560, 3072, 3136, 3144]
            xa_w, ga_w, q_w, k_w, v_w, iq_w, ik_w, iw_w = [w[:, o[t]:o[t + 1]] for t in range(8)]
            wrow = jnp.concatenate([xa_w, ga_w, k_w, ik_w, ik_w], axis=1).astype(BF16)
            wt = jnp.concatenate([q_w, iq_w, v_w, iw_w, jnp.zeros((D_MODEL, 16 - IDX_HEADS), F32)],
                                 axis=1).T.astype(BF16)
            qg = ev_q_norm_g[jdx].reshape(-1, 1) * (ATT_HEAD_DIM ** -0.5 * LOG2E)
            kg = jnp.tile(ev_k_norm_g[jdx], ATT_HEADS).reshape(1, -1)
            xg, k, ik, qt, iqt, vt, iwt = _even_in(h, row(mix_norm_g[i]), wrow, wt, head_ones, qg, kg)
            ya = _rglru(xg, ev_conv_w[jdx], row(ev_conv_b[jdx]),
                        _block_diag(ev_lru_wa[jdx]).astype(BF16), row(ev_lru_ba[jdx]),
                        _block_diag(ev_lru_wx[jdx]).astype(BF16), row(ev_lru_bx[jdx]),
                        row(ev_lru_lambda[jdx]), batch, seq)
            yb = _dsa(far, qt, iqt, iwt, k, ik, vt, tab, batch, seq)
            wo = ev_w_out[jdx].astype(BF16)
            ys, ws = [ya, yb], [wo[:LRU_WIDTH], wo[LRU_WIDTH:]]
        else:
            w = od_w_in[jdx]
            wrow = jnp.concatenate([w[:, :RET_QK_WIDTH], w[:, 2 * RET_QK_WIDTH:]], axis=1).astype(BF16)
            wkt = w[:, RET_QK_WIDTH:2 * RET_QK_WIDTH].T.astype(BF16)
            q, v, gate, kt = _odd_in(h, row(mix_norm_g[i]), wrow, wkt, cos_row, sin_row, cos_t, sin_t, seq)
            yc = _retention(cdec, q, kt, v, gate, dintra, qdec, kdec, row(od_gn_g[jdx]), batch, seq)
            ys, ws = [yc], [od_w_out[jdx].astype(BF16)]
        wr_hi, wr_lo = _router_weights(moe_w_group[i], moe_w_expert[i])
        h1, xn, route, counts = _mix_out(h, ys, ws, row(ffn_norm_g[i]), wr_hi, wr_lo)
        yp = _moe_sparse(xn, route, counts, moe_w_gate, moe_w_up, moe_w_down, i)
        h = _ple(h1, yp, route, p.reshape(depth * n, PLE_DIM), i, ple_w_up[i].astype(BF16),
                 ple_w_gate[i].astype(BF16), row(ple_norm_g[i]))
    return h.reshape(batch, seq, D_MODEL)
```

```python
import functools
import math

import jax
import jax.numpy as jnp
from jax import lax
from jax.experimental import pallas as pl
from jax.experimental.pallas import tpu as pltpu
from jax.experimental.pallas import tpu_sc as plsc

F32 = jnp.float32
BF16 = jnp.bfloat16
I32 = jnp.int32

D_MODEL = 1024
CHUNK = 64
PLE_DIM = 256
RMS_EPS = 1e-6

LRU_WIDTH = 512
LRU_BLOCKS = 8
LRU_C = 8.0

ATT_HEADS = 8
ATT_HEAD_DIM = 64
ATT_WIDTH = 512
IDX_HEADS = 8
IDX_DIM = 64
MAX_TOP_K = 256
N_BUCKETS = 32
MAX_DISTANCE = 128

RET_HEADS = 8
RET_QK_DIM = 128
RET_V_DIM = 256
RET_QK_WIDTH = 1024
RET_V_WIDTH = 2048
ROPE_BASE = 10000.0
GN_EPS = 1e-5

N_GROUPS = 4
EXPERTS_PER_GROUP = 8
N_EXPERTS = 32
EXPERT_FF = 512

LANES = 128
INT_MIN = -(2 ** 31)
NEG_BIG = -1e30
LOG2E = 1.4426950408889634
VMEM_LIMIT = 56 * 1024 * 1024

SC_CORES = 2
SC_SUBCORES = 16
SC_CHUNK = 64

MIX_SUB = 256
Q_TILE = 128
BIAS_BLOCK = 128
K_TILE = 512
RET_TILE = 256


def _cparams(sem):
    return pltpu.CompilerParams(dimension_semantics=sem, vmem_limit_bytes=VMEM_LIMIT)


def _full(shape):
    nd = len(shape)
    return pl.BlockSpec(shape, lambda *_: (0,) * nd)


def _rms(xf, g):
    return xf * lax.rsqrt(jnp.mean(xf * xf, axis=-1, keepdims=True) + RMS_EPS) * g


def _dot(a, b):
    return jnp.dot(a, b, preferred_element_type=F32)


def _dot_nt(a, b):
    return lax.dot_general(a, b, (((1,), (1,)), ((), ())), preferred_element_type=F32)


def _neg_expm1(x):
    series = -x * (1.0 + x * (0.5 + x * (1.0 / 6.0 + x * (1.0 / 24.0 + x * (1.0 / 120.0)))))
    return jnp.where(x > -0.03, series, 1.0 - jnp.exp(x))


def _pack_halves(x):
    w = x.shape[1] // 2
    hi = pltpu.bitcast(x[:, :w].astype(BF16).astype(F32), I32)
    lo = pltpu.bitcast(x[:, w:].astype(BF16).astype(F32), I32)
    return hi | lax.shift_right_logical(lo, 16)


def _unpack_halves(p):
    hi = pltpu.bitcast(p & jnp.int32(-65536), F32)
    lo = pltpu.bitcast(lax.shift_left(p, 16), F32)
    return hi, lo


def _split_bf16(x):
    hi = x.astype(BF16)
    lo = (x - hi.astype(F32)).astype(BF16)
    return hi, lo


def _even_in_kernel(h_ref, g_ref, wrow_ref, wt_ref, bd_ref, qg_ref, kg_ref,
                    xg_ref, k_ref, ik_ref, qt_ref, iqt_ref, vt_ref, iwt_ref):
    tm = h_ref.shape[0]
    w = ATT_WIDTH
    hn = _rms(h_ref[...], g_ref[...]).astype(BF16)

    xg_ref[...] = _dot(hn, wrow_ref[:, :2 * LRU_WIDTH])
    kf = _dot(hn, wrow_ref[:, 2 * LRU_WIDTH:2 * LRU_WIDTH + w])
    hi, lo = _split_bf16(kf * kf)
    ss = _dot(hi, bd_ref[...]) + _dot(lo, bd_ref[...])
    k_ref[...] = (kf * lax.rsqrt(ss * (1.0 / ATT_HEAD_DIM) + RMS_EPS) * kg_ref[...]).astype(BF16)
    ik_ref[...] = _dot(hn, wrow_ref[:, 2 * LRU_WIDTH + w:]).astype(BF16)

    q3 = _dot_nt(wt_ref[0:w, :], hn).reshape(ATT_HEADS, ATT_HEAD_DIM, tm)
    ssq = jnp.sum(q3 * q3, axis=1, keepdims=True)
    qn = q3 * lax.rsqrt(ssq * (1.0 / ATT_HEAD_DIM) + RMS_EPS) * qg_ref[...][None]
    qt_ref[...] = qn.reshape(w, tm).astype(BF16)
    iqt_ref[...] = _dot_nt(wt_ref[w:2 * w, :], hn).astype(BF16)
    vt_ref[...] = _dot_nt(wt_ref[2 * w:3 * w, :], hn).astype(BF16)
    iwt = _dot_nt(wt_ref[3 * w:, :], hn)
    iwt_ref[...] = iwt[:IDX_HEADS, :] * (IDX_HEADS ** -0.5 * IDX_DIM ** -0.5)


def _even_in(h, g, wrow, wt, bd, qg, kg, tm=512):
    n = h.shape[0]
    row = lambda w: pl.BlockSpec((tm, w), lambda i: (i, 0))
    col = lambda r: pl.BlockSpec((r, tm), lambda i: (0, i))
    return pl.pallas_call(
        _even_in_kernel,
        grid=(n // tm,),
        in_specs=[row(D_MODEL), _full(g.shape), _full(wrow.shape), _full(wt.shape),
                  _full(bd.shape), _full(qg.shape), _full(kg.shape)],
        out_specs=[row(1024), row(512), row(LANES), col(512), col(512), col(512), col(IDX_HEADS)],
        out_shape=[jax.ShapeDtypeStruct((n, 1024), F32), jax.ShapeDtypeStruct((n, 512), BF16),
                   jax.ShapeDtypeStruct((n, LANES), BF16), jax.ShapeDtypeStruct((512, n), BF16),
                   jax.ShapeDtypeStruct((512, n), BF16), jax.ShapeDtypeStruct((512, n), BF16),
                   jax.ShapeDtypeStruct((IDX_HEADS, n), F32)],
        compiler_params=_cparams(("parallel",)),
        name="even_in_proj",
    )(h, g, wrow, wt, bd, qg, kg)


def _rglru_kernel(xg_ref, cw_ref, cb_ref, wa_ref, ba_ref, wx_ref, bx_ref, lam_ref, ya_ref,
                  tail_ref, hst_ref):
    ts = xg_ref.shape[0]
    w = LRU_WIDTH

    @pl.when(pl.program_id(1) == 0)
    def _():
        tail_ref[...] = jnp.zeros_like(tail_ref)
        hst_ref[...] = jnp.zeros_like(hst_ref)

    xa = xg_ref[:, :w]
    ga = xg_ref[:, w:]
    row = lax.broadcasted_iota(I32, (ts, w), 0)
    tail = tail_ref[...]
    xc = xa * cw_ref[3:4, :] + cb_ref[...]
    for d in (1, 2, 3):
        cur = pltpu.roll(xa, d, 0)
        prev = jnp.concatenate([pltpu.roll(tail, d, 0), jnp.zeros((ts - 8, w), F32)], axis=0)
        xc = xc + jnp.where(row < d, prev, cur) * cw_ref[3 - d:4 - d, :]
    tail_ref[...] = xa[ts - 8:, :]

    xcb = xc.astype(BF16)
    r = jax.nn.sigmoid(_dot(xcb, wa_ref[...]) + ba_ref[...])
    gi = jax.nn.sigmoid(_dot(xcb, wx_ref[...]) + bx_ref[...])
    nl = -lam_ref[...]
    softplus = jnp.maximum(nl, 0.0) + jnp.log1p(jnp.exp(-jnp.abs(nl)))
    log_a = (-LRU_C) * r * softplus
    a = jnp.exp(log_a)
    u = jnp.sqrt(_neg_expm1(2.0 * log_a)) * (gi * xc)

    d = 1
    while d < ts:
        keep = row >= d
        a_sh = jnp.where(keep, pltpu.roll(a, d, 0), 1.0)
        u_sh = jnp.where(keep, pltpu.roll(u, d, 0), 0.0)
        u = a * u_sh + u
        a = a * a_sh
        d *= 2
    hseq = a * hst_ref[0:1, :] + u
    hst_ref[0:1, :] = hseq[ts - 1:ts, :]
    ya_ref[...] = (hseq * jax.nn.gelu(ga)).astype(BF16)


def _rglru(xg, cw, cb, wa, ba, wx, bx, lam, batch, seq, ts=256):
    n = xg.shape[0]
    nt = seq // ts
    return pl.pallas_call(
        _rglru_kernel,
        grid=(batch, nt),
        in_specs=[pl.BlockSpec((ts, 1024), lambda b, t: (b * nt + t, 0)),
                  _full(cw.shape), _full(cb.shape), _full(wa.shape), _full(ba.shape),
                  _full(wx.shape), _full(bx.shape), _full(lam.shape)],
        out_specs=pl.BlockSpec((ts, LRU_WIDTH), lambda b, t: (b * nt + t, 0)),
        out_shape=jax.ShapeDtypeStruct((n, LRU_WIDTH), BF16),
        scratch_shapes=[pltpu.VMEM((8, LRU_WIDTH), F32), pltpu.VMEM((8, LRU_WIDTH), F32)],
        compiler_params=_cparams(("parallel", "arbitrary")),
        name="rglru",
    )(xg, cw, cb, wa, ba, wx, bx, lam)


def _dsa_kernel(far_ref, qt_ref, iqt_ref, iwt_ref, k_ref, ik_ref, vt_ref, tab_ref, o_ref,
                keys_ref, sel_ref, cut_ref, *, seq, top_k):
    tq = Q_TILE
    hd = ATT_HEAD_DIM
    j = pl.program_id(1)
    q0 = j * tq
    nkt = (q0 + tq + K_TILE - 1) // K_TILE
    qcol = q0 + lax.broadcasted_iota(I32, (1, tq), 1)
    qlim = (qcol // CHUNK + 1) * CHUNK
    zero_rows = jnp.zeros((hd, tq), BF16)

    def head_rows(ref, h):
        blk = ref[h * hd:(h + 1) * hd, :]
        return jnp.concatenate([blk, zero_rows] if h % 2 == 0 else [zero_rows, blk], axis=0)

    def ktile(kt):
        return pl.multiple_of(kt * K_TILE, K_TILE)

    def key_pos(off, rows):
        return off + lax.broadcasted_iota(I32, (rows, tq), 0)

    iq_m = [head_rows(iqt_ref, h) for h in range(IDX_HEADS)]

    def score_body(kt, c):
        off = ktile(kt)
        ik = ik_ref[pl.ds(off, K_TILE), :]
        s = jnp.zeros((K_TILE, tq), F32)
        for h in range(IDX_HEADS):
            s = s + jnp.maximum(_dot(ik, iq_m[h]), 0.0) * iwt_ref[h:h + 1, :]
        s = jnp.where(s == 0.0, 0.0, s)
        bits = pltpu.bitcast(s, I32)
        key = bits ^ ((bits >> 31) & 0x7FFFFFFF)
        keys_ref[pl.ds(off, K_TILE), :] = jnp.where(key_pos(off, K_TILE) < qlim, key, INT_MIN)
        return c

    lax.fori_loop(0, nkt, score_body, 0)

    def count(pred):
        def body(kt, acc):
            off = ktile(kt)
            ind = pred(keys_ref[pl.ds(off, K_TILE), :], key_pos(off, K_TILE)).reshape(K_TILE // 8, 8, tq)
            while ind.shape[0] > 1:
                half = ind.shape[0] // 2
                ind = ind[:half] + ind[half:]
            return acc + ind[0]
        acc = lax.fori_loop(0, nkt, body, jnp.zeros((8, tq), F32))
        return jnp.sum(acc, axis=0, keepdims=True)

    kf = float(top_k)

    def search_body(i, ans):
        cand = ans + jnp.left_shift(jnp.int32(1), 31 - i)
        cnt = count(lambda kk, kpos: jnp.where(kk >= cand, 1.0, 0.0))
        return jnp.where(cnt >= kf, cand, ans)

    ans = lax.fori_loop(0, 32, search_body, jnp.full((1, tq), INT_MIN, I32))

    cnt_ge = count(lambda kk, kpos: jnp.where(kk >= ans, 1.0, 0.0))
    excess = jnp.where(ans > INT_MIN, cnt_ge - kf, 0.0)
    cut_ref[...] = jnp.full(cut_ref.shape, seq, I32)

    @pl.when(jnp.max(excess) > 0.0)
    def _():
        need = kf - count(lambda kk, kpos: jnp.where(kk > ans, 1.0, 0.0))

        def idx_body(i, pos):
            cand = pos + jnp.left_shift(jnp.int32(1), (seq.bit_length() - 2) - i)
            c = count(lambda kk, kpos: jnp.where(kk == ans, jnp.where(kpos < cand, 1.0, 0.0), 0.0))
            return jnp.where(c < need, cand, pos)

        pos = lax.fori_loop(0, seq.bit_length() - 1, idx_body, jnp.zeros((1, tq), I32))
        cut_ref[...] = jnp.broadcast_to(pos, cut_ref.shape)

    thr = jnp.maximum(ans, INT_MIN + 1)
    cut = cut_ref[0:1, :]

    near_w = tq + BIAS_BLOCK
    near_start = pl.multiple_of(jnp.maximum(q0 - BIAS_BLOCK, 0), BIAS_BLOCK)

    def selection(kk, kpos):
        tie = jnp.where(kpos <= cut, 0.0, NEG_BIG)
        return jnp.where(kk > thr, 0.0, jnp.where(kk == thr, tie, NEG_BIG))

    def sel_body(kt, c):
        off = ktile(kt)
        kpos = key_pos(off, K_TILE)
        s = selection(keys_ref[pl.ds(off, K_TILE), :], kpos)
        sel_ref[pl.ds(off, K_TILE), :] = jnp.where(kpos < near_start, s, NEG_BIG)
        return c

    n_far = (near_start + K_TILE - 1) // K_TILE
    lax.fori_loop(0, n_far, sel_body, 0)
    sel_near = selection(keys_ref[pl.ds(near_start, near_w), :], key_pos(near_start, near_w))

    def step(qm, kb, vte, add, carry):
        m, acc = carry
        s = _dot(kb, qm) + add
        m_new = jnp.maximum(m, jnp.max(s, axis=0, keepdims=True))
        p = jnp.exp2(s - m_new).astype(BF16)
        acc = jnp.exp2(m - m_new) * acc + _dot(vte, p)
        return m_new, acc

    qms = [head_rows(qt_ref, h) for h in range(ATT_HEADS)]
    pairs = [slice((h // 2) * LANES, (h // 2 + 1) * LANES) for h in range(ATT_HEADS)]

    def values(h, off, width):
        return jnp.concatenate([vt_ref[h * hd:(h + 1) * hd, pl.ds(off, width)],
                                jnp.ones((hd, width), BF16)], axis=0)

    def far_body(kt, carries):
        off = ktile(kt)
        sel = sel_ref[pl.ds(off, K_TILE), :]
        return tuple(step(qms[h], k_ref[pl.ds(off, K_TILE), pairs[h]], values(h, off, K_TILE), sel, carries[h])
                     for h in range(ATT_HEADS))

    init = (jnp.full((1, tq), NEG_BIG, F32), jnp.zeros((2 * hd, tq), F32))
    carries = lax.fori_loop(0, n_far, far_body, (init,) * ATT_HEADS)

    n_blocks = near_w // BIAS_BLOCK
    first = jnp.where(j == 0, 1, 0)
    res = []
    for h in range(ATT_HEADS):
        m, acc = carries[h]
        bias = jnp.concatenate([tab_ref[jnp.minimum(first + b, n_blocks - 1), h] for b in range(n_blocks)],
                               axis=0)
        m, acc = step(qms[h], k_ref[pl.ds(near_start, near_w), pairs[h]], values(h, near_start, near_w),
                      sel_near + bias, (m + far_ref[h], acc))
        res.append(acc[:hd, :] / acc[hd:hd + 1, :])

    for p in range(ATT_HEADS // 2):
        pair_t = jnp.concatenate([res[2 * p], res[2 * p + 1]], axis=0)
        o_ref[:, p * LANES:(p + 1) * LANES] = pair_t.T.astype(BF16)


def _dsa(far, qt, iqt, iwt, k, ik, vt, tab, batch, seq):
    n = k.shape[0]
    nb = seq // Q_TILE
    top_k = min(MAX_TOP_K, seq // 4)
    qspec = lambda r: pl.BlockSpec((r, Q_TILE), lambda b, j: (0, b * nb + j))
    return pl.pallas_call(
        functools.partial(_dsa_kernel, seq=seq, top_k=top_k),
        grid=(batch, nb),
        in_specs=[pl.BlockSpec(memory_space=pltpu.SMEM),
                  qspec(512), qspec(512), qspec(IDX_HEADS),
                  pl.BlockSpec((seq, 512), lambda b, j: (b, 0)),
                  pl.BlockSpec((seq, LANES), lambda b, j: (b, 0)),
                  pl.BlockSpec((512, seq), lambda b, j: (0, b)),
                  _full(tab.shape)],
        out_specs=pl.BlockSpec((Q_TILE, 512), lambda b, j: (b * nb + j, 0)),
        out_shape=jax.ShapeDtypeStruct((n, 512), BF16),
        scratch_shapes=[pltpu.VMEM((seq, Q_TILE), I32), pltpu.VMEM((seq, Q_TILE), F32),
                        pltpu.VMEM((8, Q_TILE), I32)],
        compiler_params=_cparams(("parallel", "arbitrary")),
        name="dsa",
    )(far, qt, iqt, iwt, k, ik, vt, tab)


def _mix_out_kernel(*refs, n_y):
    h_ref = refs[0]
    y_refs = refs[1:1 + n_y]
    w_refs = refs[1 + n_y:1 + 2 * n_y]
    g_ref, wr_hi_ref, wr_lo_ref = refs[1 + 2 * n_y:4 + 2 * n_y]
    h1_ref, xn_ref, route_ref = refs[4 + 2 * n_y:]
    for sub in range(h_ref.shape[0] // MIX_SUB):
        rs = slice(sub * MIX_SUB, (sub + 1) * MIX_SUB)
        _mix_out_rows(rs, h_ref, y_refs, w_refs, g_ref, wr_hi_ref, wr_lo_ref, h1_ref, xn_ref, route_ref)


def _mix_out_rows(rs, h_ref, y_refs, w_refs, g_ref, wr_hi_ref, wr_lo_ref, h1_ref, xn_ref, route_ref):
    tm = rs.stop - rs.start
    mix = _dot(y_refs[0][rs, :], w_refs[0][...])
    for y_ref, w_ref in zip(y_refs[1:], w_refs[1:]):
        mix = mix + _dot(y_ref[rs, :], w_ref[...])
    h1 = h_ref[rs, :] + mix
    h1_ref[rs, :] = h1
    xn = _rms(h1, g_ref[...])
    hi, lo = _split_bf16(xn)
    xn_ref[rs, :] = _pack_halves(xn)
    logits = _dot(hi, wr_hi_ref[...]) + _dot(lo, wr_hi_ref[...]) + _dot(hi, wr_lo_ref[...])

    lane = lax.broadcasted_iota(I32, (tm, LANES), 1)
    big = jnp.int32(LANES)
    ninf = -jnp.inf
    is_g = (lane >= N_EXPERTS) & (lane < N_EXPERTS + N_GROUPS)
    glog = jnp.where(is_g, logits, ninf)
    gmax = jnp.max(glog, axis=1, keepdims=True)
    gsel = jnp.min(jnp.where(glog == gmax, lane, big), axis=1, keepdims=True) - N_EXPERTS
    gprob = 1.0 / jnp.sum(jnp.exp(glog - gmax), axis=1, keepdims=True)
    lo_l = gsel * EXPERTS_PER_GROUP
    within = jnp.where((lane >= lo_l) & (lane < lo_l + EXPERTS_PER_GROUP), logits, ninf)
    v1 = jnp.max(within, axis=1, keepdims=True)
    i1 = jnp.min(jnp.where(within == v1, lane, big), axis=1, keepdims=True)
    rest = jnp.where(lane == i1, ninf, within)
    v2 = jnp.max(rest, axis=1, keepdims=True)
    i2 = jnp.min(jnp.where(rest == v2, lane, big), axis=1, keepdims=True)
    e2 = jnp.exp(v2 - v1)
    w1 = gprob / (1.0 + e2)
    w2 = gprob * e2 / (1.0 + e2)
    route_ref[rs, :] = jnp.where(lane == 0, i1.astype(F32), jnp.where(lane == 1, i2.astype(F32),
                                 jnp.where(lane == 2, w1, jnp.where(lane == 3, w2, 0.0))))


def _mix_out(h, ys, ws, g, wr_hi, wr_lo, tm=512):
    n = h.shape[0]
    n_y = len(ys)
    row = lambda w: pl.BlockSpec((tm, w), lambda i: (i, 0))
    return pl.pallas_call(
        functools.partial(_mix_out_kernel, n_y=n_y),
        grid=(n // tm,),
        in_specs=[row(D_MODEL)] + [row(y.shape[1]) for y in ys] + [_full(w.shape) for w in ws]
                 + [_full(g.shape), _full(wr_hi.shape), _full(wr_lo.shape)],
        out_specs=[row(D_MODEL), row(D_MODEL // 2), row(LANES)],
        out_shape=[jax.ShapeDtypeStruct((n, D_MODEL), F32), jax.ShapeDtypeStruct((n, D_MODEL // 2), I32),
                   jax.ShapeDtypeStruct((n, LANES), F32)],
        compiler_params=_cparams(("parallel",)),
        name="mix_out_router",
    )(h, *ys, *ws, g, wr_hi, wr_lo)


def _sc_rows(src, idx, scatter):
    r, d = idx.shape[0], src.shape[1]
    n_src = src.shape[0]
    nw = SC_CORES * SC_SUBCORES
    per_w = r // nw
    n_chunks = per_w // SC_CHUNK
    assert r % nw == 0 and per_w % SC_CHUNK == 0 and (not scatter or n_src % per_w == 0)
    idx3 = idx.reshape(nw, n_chunks, SC_CHUNK)
    mesh = plsc.VectorSubcoreMesh(core_axis_name="c", subcore_axis_name="s")

    row_buf = pltpu.VMEM((SC_CHUNK, d), src.dtype)

    @functools.partial(
        pl.kernel, mesh=mesh, out_type=jax.ShapeDtypeStruct((r, d), src.dtype),
        scratch_types=[pltpu.VMEM((n_chunks, SC_CHUNK), I32), row_buf, row_buf] + [pltpu.SemaphoreType.DMA] * 4)
    def permute(src_hbm, idx_hbm, out_hbm, idx_v, rows0, rows1, in0, in1, out0, out1):
        wid = lax.axis_index("s") * SC_CORES + lax.axis_index("c")
        pltpu.sync_copy(idx_hbm.at[wid], idx_v)
        bufs, in_sems, out_sems = (rows0, rows1), (in0, in1), (out0, out1)

        def linear(c, n_rows):
            start = lax.rem(wid * per_w + c * SC_CHUNK, n_rows)
            return pl.ds(pl.multiple_of(start, SC_CHUNK), SC_CHUNK)

        def load(c):
            ref = src_hbm.at[linear(c, n_src)] if scatter else src_hbm.at[idx_v.at[c]]
            return pltpu.async_copy(ref, bufs[c % 2], in_sems[c % 2])

        def store(c):
            ref = out_hbm.at[idx_v.at[c]] if scatter else out_hbm.at[linear(c, r)]
            return pltpu.async_copy(bufs[c % 2], ref, out_sems[c % 2])

        loads, stores = {0: load(0)}, {}
        for c in range(n_chunks):
            if c + 1 < n_chunks:
                if c >= 1:
                    stores[c - 1].wait()
                loads[c + 1] = load(c + 1)
            loads[c].wait()
            stores[c] = store(c)
        for c in range(max(n_chunks - 2, 0), n_chunks):
            stores[c].wait()

    return permute(src, idx3)


def _moe_kernel(tile_ref, exp_ref, flag_ref, off_ref, xs_ref, wg_ref, wu_ref, wd_ref, ys_ref,
                wgb_ref, wub_ref, wdb_ref, acc_ref):
    w = pl.program_id(0)
    tm = xs_ref.shape[0]
    e = exp_ref[w]
    flags = flag_ref[w]

    @pl.when((flags & 4) != 0)
    def _():
        wgb_ref[...] = wg_ref[0, 0].astype(BF16)
        wub_ref[...] = wu_ref[0, 0].astype(BF16)
        wdb_ref[...] = wd_ref[0, 0].astype(BF16)

    @pl.when((flags & 1) != 0)
    def _():
        xa, xb = _unpack_halves(xs_ref[...])
        xa = xa.astype(BF16)
        xb = xb.astype(BF16)
        half = D_MODEL // 2
        gate = _dot(xa, wgb_ref[:half, :]) + _dot(xb, wgb_ref[half:, :])
        up = _dot(xa, wub_ref[:half, :]) + _dot(xb, wub_ref[half:, :])
        hid = (jax.nn.silu(gate) * up).astype(BF16)
        y = _dot(hid, wdb_ref[...])
        rows = tile_ref[w] * tm + lax.broadcasted_iota(I32, (tm, 1), 0)
        mine = (rows >= off_ref[e]) & (rows < off_ref[e + 1])
        y = jnp.where(mine, y, 0.0)

        @pl.when((flags & 2) != 0)
        def _():
            acc_ref[...] = y

        @pl.when((flags & 2) == 0)
        def _():
            acc_ref[...] += y

        ys_ref[...] = _pack_halves(acc_ref[...])


def _moe_plan(off, n_rows, tm):
    nt = n_rows // tm
    n_items = nt + N_EXPERTS - 1
    row_lo = jnp.arange(nt, dtype=I32) * tm
    ends = off[1:N_EXPERTS][None, :]
    e_lo = jnp.sum((ends <= row_lo[:, None]).astype(I32), axis=1)
    e_hi = jnp.sum((ends <= (row_lo + tm - 1)[:, None]).astype(I32), axis=1)
    span = e_hi - e_lo + 1
    start = jnp.cumsum(span) - span
    total = jnp.sum(span)
    w = jnp.arange(n_items, dtype=I32)
    valid = w < total
    tile = jnp.clip(jnp.sum((start[None, :] <= w[:, None]).astype(I32), axis=1) - 1, 0, nt - 1)
    expert = jnp.where(valid, e_lo[tile] + (w - start[tile]), e_hi[nt - 1])
    first = valid & (w == start[tile])
    new_e = jnp.concatenate([jnp.ones((1,), bool), expert[1:] != expert[:-1]])
    flags = valid.astype(I32) + 2 * first.astype(I32) + 4 * new_e.astype(I32)
    return tile, expert, flags


def _moe(xs, tile, expert, flags, off, wg, wu, wd, layer, tm):
    n_rows, half = xs.shape
    n_items = tile.shape[0]
    grid_spec = pltpu.PrefetchScalarGridSpec(
        num_scalar_prefetch=4,
        grid=(n_items,),
        in_specs=[pl.BlockSpec((tm, half), lambda w, t, e, f, o: (t[w], 0)),
                  pl.BlockSpec((1, 1, D_MODEL, EXPERT_FF), lambda w, t, e, f, o: (layer, e[w], 0, 0)),
                  pl.BlockSpec((1, 1, D_MODEL, EXPERT_FF), lambda w, t, e, f, o: (layer, e[w], 0, 0)),
                  pl.BlockSpec((1, 1, EXPERT_FF, D_MODEL), lambda w, t, e, f, o: (layer, e[w], 0, 0))],
        out_specs=pl.BlockSpec((tm, half), lambda w, t, e, f, o: (t[w], 0)),
        scratch_shapes=[pltpu.VMEM((D_MODEL, EXPERT_FF), BF16), pltpu.VMEM((D_MODEL, EXPERT_FF), BF16),
                        pltpu.VMEM((EXPERT_FF, D_MODEL), BF16), pltpu.VMEM((tm, D_MODEL), F32)])
    return pl.pallas_call(
        _moe_kernel,
        grid_spec=grid_spec,
        out_shape=jax.ShapeDtypeStruct((n_rows, half), I32),
        compiler_params=_cparams(("arbitrary",)),
        name="moe_experts",
    )(tile, expert, flags, off, xs, wg, wu, wd)


def _moe_sparse(xn_packed, route, wg, wu, wd, layer, tm=256):
    n = xn_packed.shape[0]
    eid = jnp.concatenate([route[:, 0], route[:, 1]]).astype(I32)
    sorted_e, sorted_pair = lax.sort_key_val(eid, jnp.arange(2 * n, dtype=I32))
    token = jnp.where(sorted_pair >= n, sorted_pair - n, sorted_pair)
    bounds = jnp.arange(N_EXPERTS + 1, dtype=I32)
    off = jnp.sum((eid[None, :] < bounds[:, None]).astype(I32), axis=1)
    tile, expert, flags = _moe_plan(off, 2 * n, tm)
    xs = _sc_rows(xn_packed, token, scatter=False)
    ys = _moe(xs, tile, expert, flags, off, wg, wu, wd, layer, tm)
    return _sc_rows(ys, sorted_pair, scatter=True)


def _ple_kernel(h1_ref, y0_ref, y1_ref, route_ref, p_ref, wup_ref, wgate_ref, g_ref, o_ref):
    w1 = route_ref[:, 2:3]
    w2 = route_ref[:, 3:4]
    a0, b0 = _unpack_halves(y0_ref[...])
    a1, b1 = _unpack_halves(y1_ref[...])
    moe = jnp.concatenate([w1 * a0 + w2 * a1, w1 * b0 + w2 * b1], axis=1)
    h2 = h1_ref[...] + moe
    e = _rms(_dot(p_ref[...].astype(BF16), wup_ref[...]), g_ref[...])
    gate = jax.nn.sigmoid(_dot(h2.astype(BF16), wgate_ref[...]))
    o_ref[...] = h2 + e * gate


def _ple(h1, yp, route, p_all, layer, wup, wgate, g, tm=512):
    n = h1.shape[0]
    nt = n // tm
    row = lambda w: pl.BlockSpec((tm, w), lambda i: (i, 0))
    return pl.pallas_call(
        _ple_kernel,
        grid=(nt,),
        in_specs=[row(D_MODEL), row(D_MODEL // 2),
                  pl.BlockSpec((tm, D_MODEL // 2), lambda i: (i + nt, 0)),
                  row(LANES), pl.BlockSpec((tm, PLE_DIM), lambda i: (i + layer * nt, 0)),
                  _full(wup.shape), _full(wgate.shape), _full(g.shape)],
        out_specs=row(D_MODEL),
        out_shape=jax.ShapeDtypeStruct((n, D_MODEL), F32),
        compiler_params=_cparams(("parallel",)),
        name="ple",
    )(h1, yp, yp, route, p_all, wup, wgate, g)


def _odd_in_kernel(h_ref, g_ref, wrow_ref, wkt_ref, cos_ref, sin_ref, cost_ref, sint_ref,
                   q_ref, v_ref, gate_ref, kt_ref):
    tm = h_ref.shape[0]
    hn = _rms(h_ref[...], g_ref[...]).astype(BF16)
    cos = cos_ref[...]
    sin = sin_ref[...]
    half = RET_QK_DIM // 2
    for hd in range(RET_HEADS):
        sl = slice(hd * RET_QK_DIM, (hd + 1) * RET_QK_DIM)
        qh = _dot(hn, wrow_ref[:, sl])
        q_ref[:, sl] = (qh * cos + pltpu.roll(qh, half, 1) * sin).astype(BF16)
    v_ref[...] = _dot(hn, wrow_ref[:, RET_QK_WIDTH:RET_QK_WIDTH + RET_V_WIDTH]).astype(BF16)
    gate_ref[...] = _dot(hn, wrow_ref[:, RET_QK_WIDTH + RET_V_WIDTH:]).astype(BF16)

    cost = cost_ref[...] * (RET_QK_DIM ** -0.5)
    sint = sint_ref[...] * (RET_QK_DIM ** -0.5)
    for hd in range(RET_HEADS):
        sl = slice(hd * RET_QK_DIM, (hd + 1) * RET_QK_DIM)
        kh = _dot_nt(wkt_ref[sl, :], hn)
        swapped = jnp.concatenate([kh[half:], kh[:half]], axis=0)
        kt_ref[sl, :] = (kh * cost + swapped * sint).astype(BF16)


def _odd_in(h, g, wrow, wkt, cos, sin, cost, sint, seq, tm=512):
    n = h.shape[0]
    nt = seq // tm
    row = lambda w: pl.BlockSpec((tm, w), lambda i: (i, 0))
    return pl.pallas_call(
        _odd_in_kernel,
        grid=(n // tm,),
        in_specs=[row(D_MODEL), _full(g.shape), _full(wrow.shape), _full(wkt.shape),
                  pl.BlockSpec((tm, RET_QK_DIM), lambda i: (i % nt, 0)),
                  pl.BlockSpec((tm, RET_QK_DIM), lambda i: (i % nt, 0)),
                  pl.BlockSpec((RET_QK_DIM, tm), lambda i: (0, i % nt)),
                  pl.BlockSpec((RET_QK_DIM, tm), lambda i: (0, i % nt))],
        out_specs=[row(RET_QK_WIDTH), row(RET_V_WIDTH), row(RET_V_WIDTH),
                   pl.BlockSpec((RET_QK_WIDTH, tm), lambda i: (0, i))],
        out_shape=[jax.ShapeDtypeStruct((n, RET_QK_WIDTH), BF16), jax.ShapeDtypeStruct((n, RET_V_WIDTH), BF16),
                   jax.ShapeDtypeStruct((n, RET_V_WIDTH), BF16), jax.ShapeDtypeStruct((RET_QK_WIDTH, n), BF16)],
        compiler_params=_cparams(("parallel",)),
        name="odd_in_proj",
    )(h, g, wrow, wkt, cos, sin, cost, sint)


def _ret_kernel(cdec_ref, q_ref, kt_ref, v_ref, gate_ref, dintra_ref, qdec_ref, kdec_ref, gn_ref,
                y_ref, state_ref):
    @pl.when(pl.program_id(1) == 0)
    def _():
        state_ref[...] = jnp.zeros_like(state_ref)

    for h in range(RET_HEADS):
        ks = slice(h * RET_QK_DIM, (h + 1) * RET_QK_DIM)
        vs = slice(h * RET_V_DIM, (h + 1) * RET_V_DIM)
        qh = q_ref[:, ks]
        kth = kt_ref[ks, :]
        vh = v_ref[:, vs]
        state = state_ref[h]
        inner = (_dot(qh, kth) * dintra_ref[h]).astype(BF16)
        o = _dot(inner, vh) + _dot(qh, state.astype(BF16)) * qdec_ref[h]
        kd = (kth.astype(F32) * kdec_ref[h]).astype(BF16)
        state_ref[h] = state * cdec_ref[h] + _dot(kd, vh)
        mu = jnp.mean(o, axis=-1, keepdims=True)
        oc = o - mu
        var = jnp.mean(oc * oc, axis=-1, keepdims=True)
        on = oc * lax.rsqrt(var + GN_EPS) * gn_ref[:, vs]
        y_ref[:, vs] = (jax.nn.silu(gate_ref[:, vs].astype(F32)) * on).astype(BF16)


def _retention(cdec, q, kt, v, gate, dintra, qdec, kdec, gn, batch, seq):
    n = q.shape[0]
    c = RET_TILE
    nc = seq // c
    row = lambda w: pl.BlockSpec((c, w), lambda b, t: (b * nc + t, 0))
    return pl.pallas_call(
        _ret_kernel,
        grid=(batch, nc),
        in_specs=[pl.BlockSpec(memory_space=pltpu.SMEM),
                  row(RET_QK_WIDTH),
                  pl.BlockSpec((RET_QK_WIDTH, c), lambda b, t: (0, b * nc + t)),
                  row(RET_V_WIDTH), row(RET_V_WIDTH),
                  _full(dintra.shape), _full(qdec.shape), _full(kdec.shape), _full(gn.shape)],
        out_specs=row(RET_V_WIDTH),
        out_shape=jax.ShapeDtypeStruct((n, RET_V_WIDTH), BF16),
        scratch_shapes=[pltpu.VMEM((RET_HEADS, RET_QK_DIM, RET_V_DIM), F32)],
        compiler_params=_cparams(("parallel", "arbitrary")),
        name="retention",
    )(cdec, q, kt, v, gate, dintra, qdec, kdec, gn)


def _t5_bucket(rel):
    half = N_BUCKETS // 2
    max_exact = half // 2
    ret = (rel > 0).astype(I32) * half
    n = jnp.abs(rel)
    nf = jnp.maximum(n, 1).astype(F32)
    large = max_exact + (jnp.log(nf / max_exact) / math.log(MAX_DISTANCE / max_exact)
                         * (half - max_exact)).astype(I32)
    large = jnp.minimum(large, half - 1)
    return ret + jnp.where(n < max_exact, n, large)


def _bias_tables(rel_bias):
    c = jnp.arange(BIAS_BLOCK, dtype=I32)[:, None]
    r = jnp.arange(Q_TILE, dtype=I32)[None, :]
    rels = jnp.stack([c + (b - 1) * BIAS_BLOCK - r for b in range(Q_TILE // BIAS_BLOCK + 1)])
    onehot = jax.nn.one_hot(_t5_bucket(rels), N_BUCKETS, dtype=F32)
    tab = jnp.einsum("abcn,nh->ahbc", onehot, rel_bias, precision=lax.Precision.HIGHEST)
    far = rel_bias[_t5_bucket(jnp.int32(-2 * MAX_DISTANCE))]
    return (tab * LOG2E).astype(F32), (far * LOG2E).astype(F32)


def _block_diag(w):
    nb, bs, _ = w.shape
    eye = jnp.eye(nb, dtype=w.dtype)
    return jnp.einsum("hij,hg->higj", w, eye).reshape(nb * bs, nb * bs)


def _router_weights(w_group, w_expert):
    wr = jnp.concatenate([w_expert, w_group, jnp.zeros((D_MODEL, LANES - N_EXPERTS - N_GROUPS), F32)], axis=1)
    hi = wr.astype(BF16)
    lo = (wr - hi.astype(F32)).astype(BF16)
    return hi, lo


def _rotary_tables(seq):
    half = RET_QK_DIM // 2
    inv = ROPE_BASE ** (-jnp.arange(half, dtype=F32) / half)
    ang = jnp.arange(seq, dtype=F32)[:, None] * inv[None, :]
    cos, sin = jnp.cos(ang), jnp.sin(ang)
    cos_row = jnp.concatenate([cos, cos], axis=1)
    sin_row = jnp.concatenate([-sin, sin], axis=1)
    cos_t = jnp.concatenate([cos.T, cos.T], axis=0)
    sin_t = jnp.concatenate([-sin.T, sin.T], axis=0)
    return cos_row, sin_row, cos_t, sin_t


def _retention_tables():
    c = RET_TILE
    log_g = jnp.log(1.0 - 2.0 ** (-5.0 - jnp.arange(RET_HEADS, dtype=F32)))
    pos = jnp.arange(c, dtype=F32)
    diff = pos[:, None] - pos[None, :]
    causal = diff >= 0
    dintra = jnp.where(causal[None], jnp.exp(jnp.where(causal, diff, 0.0)[None] * log_g[:, None, None]), 0.0)
    qdec = jnp.exp((pos + 1.0)[None, :, None] * log_g[:, None, None])
    kdec = jnp.exp((c - 1.0 - pos)[None, None, :] * log_g[:, None, None])
    cdec = jnp.exp(c * log_g)
    return dintra, qdec, kdec, cdec


def kernel(x, p, rel_bias, mix_norm_g, ffn_norm_g, ple_norm_g, ev_w_in, ev_conv_w, ev_conv_b, ev_lru_wa, ev_lru_ba, ev_lru_wx, ev_lru_bx, ev_lru_lambda, ev_q_norm_g, ev_k_norm_g, ev_w_out, od_w_in, od_gn_g, od_w_out, moe_w_group, moe_w_expert, moe_w_gate, moe_w_up, moe_w_down, ple_w_up, ple_w_gate):
    batch, seq, _ = x.shape
    n = batch * seq
    depth = p.shape[0]
    h = x.reshape(n, D_MODEL)
    row = lambda a: a.reshape(1, -1)

    tab, far = _bias_tables(rel_bias)
    cos_row, sin_row, cos_t, sin_t = _rotary_tables(seq)
    dintra, qdec, kdec, cdec = _retention_tables()
    head_ones = _block_diag(jnp.ones((ATT_HEADS, ATT_HEAD_DIM, ATT_HEAD_DIM), BF16))

    for i in range(depth):
        jdx = i // 2
        if i % 2 == 0:
            w = ev_w_in[jdx]
            o = [0, 512, 1024, 1536, 2048, 2560, 3072, 3136, 3144]
            xa_w, ga_w, q_w, k_w, v_w, iq_w, ik_w, iw_w = [w[:, o[t]:o[t + 1]] for t in range(8)]
            wrow = jnp.concatenate([xa_w, ga_w, k_w, ik_w, ik_w], axis=1).astype(BF16)
            wt = jnp.concatenate([q_w, iq_w, v_w, iw_w, jnp.zeros((D_MODEL, 16 - IDX_HEADS), F32)],
                                 axis=1).T.astype(BF16)
            qg = ev_q_norm_g[jdx].reshape(-1, 1) * (ATT_HEAD_DIM ** -0.5 * LOG2E)
            kg = jnp.tile(ev_k_norm_g[jdx], ATT_HEADS).reshape(1, -1)
            xg, k, ik, qt, iqt, vt, iwt = _even_in(h, row(mix_norm_g[i]), wrow, wt, head_ones, qg, kg)
            ya = _rglru(xg, ev_conv_w[jdx], row(ev_conv_b[jdx]),
                        _block_diag(ev_lru_wa[jdx]).astype(BF16), row(ev_lru_ba[jdx]),
                        _block_diag(ev_lru_wx[jdx]).astype(BF16), row(ev_lru_bx[jdx]),
                        row(ev_lru_lambda[jdx]), batch, seq)
            yb = _dsa(far, qt, iqt, iwt, k, ik, vt, tab, batch, seq)
            wo = ev_w_out[jdx].astype(BF16)
            ys, ws = [ya, yb], [wo[:LRU_WIDTH], wo[LRU_WIDTH:]]
        else:
            w = od_w_in[jdx]
            wrow = jnp.concatenate([w[:, :RET_QK_WIDTH], w[:, 2 * RET_QK_WIDTH:]], axis=1).astype(BF16)
            wkt = w[:, RET_QK_WIDTH:2 * RET_QK_WIDTH].T.astype(BF16)
            q, v, gate, kt = _odd_in(h, row(mix_norm_g[i]), wrow, wkt, cos_row, sin_row, cos_t, sin_t, seq)
            yc = _retention(cdec, q, kt, v, gate, dintra, qdec, kdec, row(od_gn_g[jdx]), batch, seq)
            ys, ws = [yc], [od_w_out[jdx].astype(BF16)]
        wr_hi, wr_lo = _router_weights(moe_w_group[i], moe_w_expert[i])
        h1, xn, route = _mix_out(h, ys, ws, row(ffn_norm_g[i]), wr_hi, wr_lo)
        yp = _moe_sparse(xn, route, moe_w_gate, moe_w_up, moe_w_down, i)
        h = _ple(h1, yp, route, p.reshape(depth * n, PLE_DIM), i, ple_w_up[i].astype(BF16),
                 ple_w_gate[i].astype(BF16), row(ple_norm_g[i]))
    return h.reshape(batch, seq, D_MODEL)
```

```python
import functools
import math

import jax
import jax.numpy as jnp
from jax import lax
from jax.experimental import pallas as pl
from jax.experimental.pallas import tpu as pltpu
from jax.experimental.pallas import tpu_sc as plsc

F32 = jnp.float32
BF16 = jnp.bfloat16
I32 = jnp.int32

D_MODEL = 1024
CHUNK = 64
PLE_DIM = 256
RMS_EPS = 1e-6

LRU_WIDTH = 512
LRU_BLOCKS = 8
LRU_C = 8.0

ATT_HEADS = 8
ATT_HEAD_DIM = 64
ATT_WIDTH = 512
IDX_HEADS = 8
IDX_DIM = 64
MAX_TOP_K = 256
N_BUCKETS = 32
MAX_DISTANCE = 128

RET_HEADS = 8
RET_QK_DIM = 128
RET_V_DIM = 256
RET_QK_WIDTH = 1024
RET_V_WIDTH = 2048
ROPE_BASE = 10000.0
GN_EPS = 1e-5

N_GROUPS = 4
EXPERTS_PER_GROUP = 8
N_EXPERTS = 32
EXPERT_FF = 512

LANES = 128
INT_MIN = -(2 ** 31)
NEG_BIG = -1e30
LOG2E = 1.4426950408889634
VMEM_LIMIT = 56 * 1024 * 1024

SC_CORES = 2
SC_SUBCORES = 16
SC_CHUNK = 64

MIX_SUB = 256
Q_TILE = 128
BIAS_BLOCK = 128
K_TILE = 512
RET_TILE = 256


def _cparams(sem):
    return pltpu.CompilerParams(dimension_semantics=sem, vmem_limit_bytes=VMEM_LIMIT)


def _full(shape):
    nd = len(shape)
    return pl.BlockSpec(shape, lambda *_: (0,) * nd)


def _rms(xf, g):
    return xf * lax.rsqrt(jnp.mean(xf * xf, axis=-1, keepdims=True) + RMS_EPS) * g


def _dot(a, b):
    return jnp.dot(a, b, preferred_element_type=F32)


def _dot_nt(a, b):
    return lax.dot_general(a, b, (((1,), (1,)), ((), ())), preferred_element_type=F32)


def _neg_expm1(x):
    series = -x * (1.0 + x * (0.5 + x * (1.0 / 6.0 + x * (1.0 / 24.0 + x * (1.0 / 120.0)))))
    return jnp.where(x > -0.03, series, 1.0 - jnp.exp(x))


def _pack_halves(x):
    w = x.shape[1] // 2
    hi = pltpu.bitcast(x[:, :w].astype(BF16).astype(F32), I32)
    lo = pltpu.bitcast(x[:, w:].astype(BF16).astype(F32), I32)
    return hi | lax.shift_right_logical(lo, 16)


def _unpack_halves(p):
    hi = pltpu.bitcast(p & jnp.int32(-65536), F32)
    lo = pltpu.bitcast(lax.shift_left(p, 16), F32)
    return hi, lo


def _split_bf16(x):
    hi = x.astype(BF16)
    lo = (x - hi.astype(F32)).astype(BF16)
    return hi, lo


def _even_in_kernel(h_ref, g_ref, wrow_ref, wt_ref, bd_ref, qg_ref, kg_ref,
                    xg_ref, k_ref, ik_ref, qt_ref, iqt_ref, vt_ref, iwt_ref):
    tm = h_ref.shape[0]
    w = ATT_WIDTH
    hn = _rms(h_ref[...], g_ref[...]).astype(BF16)

    xg_ref[...] = _dot(hn, wrow_ref[:, :2 * LRU_WIDTH])
    kf = _dot(hn, wrow_ref[:, 2 * LRU_WIDTH:2 * LRU_WIDTH + w])
    hi, lo = _split_bf16(kf * kf)
    ss = _dot(hi, bd_ref[...]) + _dot(lo, bd_ref[...])
    k_ref[...] = (kf * lax.rsqrt(ss * (1.0 / ATT_HEAD_DIM) + RMS_EPS) * kg_ref[...]).astype(BF16)
    ik_ref[...] = _dot(hn, wrow_ref[:, 2 * LRU_WIDTH + w:]).astype(BF16)

    feat = _dot_nt(wt_ref[...], hn)
    q3 = feat[0:w].reshape(ATT_HEADS, ATT_HEAD_DIM, tm)
    ssq = jnp.sum(q3 * q3, axis=1, keepdims=True)
    qn = q3 * lax.rsqrt(ssq * (1.0 / ATT_HEAD_DIM) + RMS_EPS) * qg_ref[...][None]
    qt_ref[...] = qn.reshape(w, tm).astype(BF16)
    iqt_ref[...] = feat[w:2 * w].astype(BF16)
    vt_ref[...] = feat[2 * w:3 * w].astype(BF16)
    iwt_ref[...] = feat[3 * w:3 * w + IDX_HEADS] * (IDX_HEADS ** -0.5 * IDX_DIM ** -0.5)


def _even_in(h, g, wrow, wt, bd, qg, kg, tm=512):
    n = h.shape[0]
    row = lambda w: pl.BlockSpec((tm, w), lambda i: (i, 0))
    col = lambda r: pl.BlockSpec((r, tm), lambda i: (0, i))
    return pl.pallas_call(
        _even_in_kernel,
        grid=(n // tm,),
        in_specs=[row(D_MODEL), _full(g.shape), _full(wrow.shape), _full(wt.shape),
                  _full(bd.shape), _full(qg.shape), _full(kg.shape)],
        out_specs=[row(1024), row(512), row(LANES), col(512), col(512), col(512), col(IDX_HEADS)],
        out_shape=[jax.ShapeDtypeStruct((n, 1024), F32), jax.ShapeDtypeStruct((n, 512), BF16),
                   jax.ShapeDtypeStruct((n, LANES), BF16), jax.ShapeDtypeStruct((512, n), BF16),
                   jax.ShapeDtypeStruct((512, n), BF16), jax.ShapeDtypeStruct((512, n), BF16),
                   jax.ShapeDtypeStruct((IDX_HEADS, n), F32)],
        compiler_params=_cparams(("parallel",)),
        name="even_in_proj",
    )(h, g, wrow, wt, bd, qg, kg)


def _rglru_kernel(xg_ref, cw_ref, cb_ref, wa_ref, ba_ref, wx_ref, bx_ref, lam_ref, ya_ref,
                  tail_ref, hst_ref):
    ts = xg_ref.shape[0]
    w = LRU_WIDTH

    @pl.when(pl.program_id(1) == 0)
    def _():
        tail_ref[...] = jnp.zeros_like(tail_ref)
        hst_ref[...] = jnp.zeros_like(hst_ref)

    xa = xg_ref[:, :w]
    ga = xg_ref[:, w:]
    row = lax.broadcasted_iota(I32, (ts, w), 0)
    tail = tail_ref[...]
    xc = xa * cw_ref[3:4, :] + cb_ref[...]
    for d in (1, 2, 3):
        cur = pltpu.roll(xa, d, 0)
        prev = jnp.concatenate([pltpu.roll(tail, d, 0), jnp.zeros((ts - 8, w), F32)], axis=0)
        xc = xc + jnp.where(row < d, prev, cur) * cw_ref[3 - d:4 - d, :]
    tail_ref[...] = xa[ts - 8:, :]

    xcb = xc.astype(BF16)
    r = jax.nn.sigmoid(_dot(xcb, wa_ref[...]) + ba_ref[...])
    gi = jax.nn.sigmoid(_dot(xcb, wx_ref[...]) + bx_ref[...])
    nl = -lam_ref[...]
    softplus = jnp.maximum(nl, 0.0) + jnp.log1p(jnp.exp(-jnp.abs(nl)))
    log_a = (-LRU_C) * r * softplus
    a = jnp.exp(log_a)
    u = jnp.sqrt(_neg_expm1(2.0 * log_a)) * (gi * xc)

    d = 1
    while d < ts:
        keep = row >= d
        a_sh = jnp.where(keep, pltpu.roll(a, d, 0), 1.0)
        u_sh = jnp.where(keep, pltpu.roll(u, d, 0), 0.0)
        u = a * u_sh + u
        a = a * a_sh
        d *= 2
    hseq = a * hst_ref[0:1, :] + u
    hst_ref[0:1, :] = hseq[ts - 1:ts, :]
    ya_ref[...] = (hseq * jax.nn.gelu(ga)).astype(BF16)


def _rglru(xg, cw, cb, wa, ba, wx, bx, lam, batch, seq, ts=256):
    n = xg.shape[0]
    nt = seq // ts
    return pl.pallas_call(
        _rglru_kernel,
        grid=(batch, nt),
        in_specs=[pl.BlockSpec((ts, 1024), lambda b, t: (b * nt + t, 0)),
                  _full(cw.shape), _full(cb.shape), _full(wa.shape), _full(ba.shape),
                  _full(wx.shape), _full(bx.shape), _full(lam.shape)],
        out_specs=pl.BlockSpec((ts, LRU_WIDTH), lambda b, t: (b * nt + t, 0)),
        out_shape=jax.ShapeDtypeStruct((n, LRU_WIDTH), BF16),
        scratch_shapes=[pltpu.VMEM((8, LRU_WIDTH), F32), pltpu.VMEM((8, LRU_WIDTH), F32)],
        compiler_params=_cparams(("parallel", "arbitrary")),
        name="rglru",
    )(xg, cw, cb, wa, ba, wx, bx, lam)


def _dsa_kernel(far_ref, qt_ref, iqt_ref, iwt_ref, k_ref, ik_ref, vt_ref, tab_ref, o_ref,
                keys_ref, sel_ref, cut_ref, *, seq, top_k):
    tq = Q_TILE
    hd = ATT_HEAD_DIM
    j = pl.program_id(1)
    q0 = j * tq
    nkt = (q0 + tq + K_TILE - 1) // K_TILE
    qcol = q0 + lax.broadcasted_iota(I32, (1, tq), 1)
    qlim = (qcol // CHUNK + 1) * CHUNK
    zero_rows = jnp.zeros((hd, tq), BF16)

    def head_rows(ref, h):
        blk = ref[h * hd:(h + 1) * hd, :]
        return jnp.concatenate([blk, zero_rows] if h % 2 == 0 else [zero_rows, blk], axis=0)

    def ktile(kt):
        return pl.multiple_of(kt * K_TILE, K_TILE)

    def key_pos(off, rows):
        return off + lax.broadcasted_iota(I32, (rows, tq), 0)

    iq_m = [head_rows(iqt_ref, h) for h in range(IDX_HEADS)]

    def score_body(kt, c):
        off = ktile(kt)
        ik = ik_ref[pl.ds(off, K_TILE), :]
        s = jnp.zeros((K_TILE, tq), F32)
        for h in range(IDX_HEADS):
            s = s + jnp.maximum(_dot(ik, iq_m[h]), 0.0) * iwt_ref[h:h + 1, :]
        s = jnp.where(s == 0.0, 0.0, s)
        bits = pltpu.bitcast(s, I32)
        key = bits ^ ((bits >> 31) & 0x7FFFFFFF)
        keys_ref[pl.ds(off, K_TILE), :] = jnp.where(key_pos(off, K_TILE) < qlim, key, INT_MIN)
        return c

    lax.fori_loop(0, nkt, score_body, 0)

    def count(pred):
        def body(kt, acc):
            off = ktile(kt)
            ind = pred(keys_ref[pl.ds(off, K_TILE), :], key_pos(off, K_TILE)).reshape(K_TILE // 8, 8, tq)
            while ind.shape[0] > 1:
                half = ind.shape[0] // 2
                ind = ind[:half] + ind[half:]
            return acc + ind[0]
        acc = lax.fori_loop(0, nkt, body, jnp.zeros((8, tq), F32))
        return jnp.sum(acc, axis=0, keepdims=True)

    kf = float(top_k)

    def search_body(i, ans):
        cand = ans + jnp.left_shift(jnp.int32(1), 31 - i)
        cnt = count(lambda kk, kpos: jnp.where(kk >= cand, 1.0, 0.0))
        return jnp.where(cnt >= kf, cand, ans)

    ans = lax.fori_loop(0, 32, search_body, jnp.full((1, tq), INT_MIN, I32))

    cnt_ge = count(lambda kk, kpos: jnp.where(kk >= ans, 1.0, 0.0))
    excess = jnp.where(ans > INT_MIN, cnt_ge - kf, 0.0)
    cut_ref[...] = jnp.full(cut_ref.shape, seq, I32)

    @pl.when(jnp.max(excess) > 0.0)
    def _():
        need = kf - count(lambda kk, kpos: jnp.where(kk > ans, 1.0, 0.0))

        def idx_body(i, pos):
            cand = pos + jnp.left_shift(jnp.int32(1), (seq.bit_length() - 2) - i)
            c = count(lambda kk, kpos: jnp.where(kk == ans, jnp.where(kpos < cand, 1.0, 0.0), 0.0))
            return jnp.where(c < need, cand, pos)

        pos = lax.fori_loop(0, seq.bit_length() - 1, idx_body, jnp.zeros((1, tq), I32))
        cut_ref[...] = jnp.broadcast_to(pos, cut_ref.shape)

    thr = jnp.maximum(ans, INT_MIN + 1)
    cut = cut_ref[0:1, :]

    near_w = tq + BIAS_BLOCK
    near_start = pl.multiple_of(jnp.maximum(q0 - BIAS_BLOCK, 0), BIAS_BLOCK)

    def selection(kk, kpos):
        tie = jnp.where(kpos <= cut, 0.0, NEG_BIG)
        return jnp.where(kk > thr, 0.0, jnp.where(kk == thr, tie, NEG_BIG))

    def sel_body(kt, c):
        off = ktile(kt)
        kpos = key_pos(off, K_TILE)
        s = selection(keys_ref[pl.ds(off, K_TILE), :], kpos)
        sel_ref[pl.ds(off, K_TILE), :] = jnp.where(kpos < near_start, s, NEG_BIG)
        return c

    n_far = (near_start + K_TILE - 1) // K_TILE
    lax.fori_loop(0, n_far, sel_body, 0)
    sel_near = selection(keys_ref[pl.ds(near_start, near_w), :], key_pos(near_start, near_w))

    def step(qm, kb, vte, add, carry):
        m, acc = carry
        s = _dot(kb, qm) + add
        m_new = jnp.maximum(m, jnp.max(s, axis=0, keepdims=True))
        p = jnp.exp2(s - m_new).astype(BF16)
        acc = jnp.exp2(m - m_new) * acc + _dot(vte, p)
        return m_new, acc

    qms = [head_rows(qt_ref, h) for h in range(ATT_HEADS)]
    pairs = [slice((h // 2) * LANES, (h // 2 + 1) * LANES) for h in range(ATT_HEADS)]

    def values(h, off, width):
        return jnp.concatenate([vt_ref[h * hd:(h + 1) * hd, pl.ds(off, width)],
                                jnp.ones((hd, width), BF16)], axis=0)

    def far_body(kt, carries):
        off = ktile(kt)
        sel = sel_ref[pl.ds(off, K_TILE), :]
        return tuple(step(qms[h], k_ref[pl.ds(off, K_TILE), pairs[h]], values(h, off, K_TILE), sel, carries[h])
                     for h in range(ATT_HEADS))

    init = (jnp.full((1, tq), NEG_BIG, F32), jnp.zeros((2 * hd, tq), F32))
    carries = lax.fori_loop(0, n_far, far_body, (init,) * ATT_HEADS)

    n_blocks = near_w // BIAS_BLOCK
    first = jnp.where(j == 0, 1, 0)
    res = []
    for h in range(ATT_HEADS):
        m, acc = carries[h]
        bias = jnp.concatenate([tab_ref[jnp.minimum(first + b, n_blocks - 1), h] for b in range(n_blocks)],
                               axis=0)
        m, acc = step(qms[h], k_ref[pl.ds(near_start, near_w), pairs[h]], values(h, near_start, near_w),
                      sel_near + bias, (m + far_ref[h], acc))
        res.append(acc[:hd, :] / acc[hd:hd + 1, :])

    for p in range(ATT_HEADS // 2):
        pair_t = jnp.concatenate([res[2 * p], res[2 * p + 1]], axis=0)
        o_ref[:, p * LANES:(p + 1) * LANES] = pair_t.T.astype(BF16)


def _dsa(far, qt, iqt, iwt, k, ik, vt, tab, batch, seq):
    n = k.shape[0]
    nb = seq // Q_TILE
    top_k = min(MAX_TOP_K, seq // 4)
    qspec = lambda r: pl.BlockSpec((r, Q_TILE), lambda b, j: (0, b * nb + j))
    return pl.pallas_call(
        functools.partial(_dsa_kernel, seq=seq, top_k=top_k),
        grid=(batch, nb),
        in_specs=[pl.BlockSpec(memory_space=pltpu.SMEM),
                  qspec(512), qspec(512), qspec(IDX_HEADS),
                  pl.BlockSpec((seq, 512), lambda b, j: (b, 0)),
                  pl.BlockSpec((seq, LANES), lambda b, j: (b, 0)),
                  pl.BlockSpec((512, seq), lambda b, j: (0, b)),
                  _full(tab.shape)],
        out_specs=pl.BlockSpec((Q_TILE, 512), lambda b, j: (b * nb + j, 0)),
        out_shape=jax.ShapeDtypeStruct((n, 512), BF16),
        scratch_shapes=[pltpu.VMEM((seq, Q_TILE), I32), pltpu.VMEM((seq, Q_TILE), F32),
                        pltpu.VMEM((8, Q_TILE), I32)],
        compiler_params=_cparams(("parallel", "arbitrary")),
        name="dsa",
    )(far, qt, iqt, iwt, k, ik, vt, tab)


def _mix_out_kernel(*refs, n_y):
    h_ref = refs[0]
    y_refs = refs[1:1 + n_y]
    w_refs = refs[1 + n_y:1 + 2 * n_y]
    g_ref, wr_hi_ref, wr_lo_ref = refs[1 + 2 * n_y:4 + 2 * n_y]
    h1_ref, xn_ref, route_ref = refs[4 + 2 * n_y:]
    for sub in range(h_ref.shape[0] // MIX_SUB):
        rs = slice(sub * MIX_SUB, (sub + 1) * MIX_SUB)
        _mix_out_rows(rs, h_ref, y_refs, w_refs, g_ref, wr_hi_ref, wr_lo_ref, h1_ref, xn_ref, route_ref)


def _mix_out_rows(rs, h_ref, y_refs, w_refs, g_ref, wr_hi_ref, wr_lo_ref, h1_ref, xn_ref, route_ref):
    tm = rs.stop - rs.start
    mix = _dot(y_refs[0][rs, :], w_refs[0][...])
    for y_ref, w_ref in zip(y_refs[1:], w_refs[1:]):
        mix = mix + _dot(y_ref[rs, :], w_ref[...])
    h1 = h_ref[rs, :] + mix
    h1_ref[rs, :] = h1
    xn = _rms(h1, g_ref[...])
    hi, lo = _split_bf16(xn)
    xn_ref[rs, :] = _pack_halves(xn)
    logits = _dot(hi, wr_hi_ref[...]) + _dot(lo, wr_hi_ref[...]) + _dot(hi, wr_lo_ref[...])

    lane = lax.broadcasted_iota(I32, (tm, LANES), 1)
    big = jnp.int32(LANES)
    ninf = -jnp.inf
    is_g = (lane >= N_EXPERTS) & (lane < N_EXPERTS + N_GROUPS)
    glog = jnp.where(is_g, logits, ninf)
    gmax = jnp.max(glog, axis=1, keepdims=True)
    gsel = jnp.min(jnp.where(glog == gmax, lane, big), axis=1, keepdims=True) - N_EXPERTS
    gprob = 1.0 / jnp.sum(jnp.exp(glog - gmax), axis=1, keepdims=True)
    lo_l = gsel * EXPERTS_PER_GROUP
    within = jnp.where((lane >= lo_l) & (lane < lo_l + EXPERTS_PER_GROUP), logits, ninf)
    v1 = jnp.max(within, axis=1, keepdims=True)
    i1 = jnp.min(jnp.where(within == v1, lane, big), axis=1, keepdims=True)
    rest = jnp.where(lane == i1, ninf, within)
    v2 = jnp.max(rest, axis=1, keepdims=True)
    i2 = jnp.min(jnp.where(rest == v2, lane, big), axis=1, keepdims=True)
    e2 = jnp.exp(v2 - v1)
    w1 = gprob / (1.0 + e2)
    w2 = gprob * e2 / (1.0 + e2)
    route_ref[rs, :] = jnp.where(lane == 0, i1.astype(F32), jnp.where(lane == 1, i2.astype(F32),
                                 jnp.where(lane == 2, w1, jnp.where(lane == 3, w2, 0.0))))


def _mix_out(h, ys, ws, g, wr_hi, wr_lo, tm=512):
    n = h.shape[0]
    n_y = len(ys)
    row = lambda w: pl.BlockSpec((tm, w), lambda i: (i, 0))
    return pl.pallas_call(
        functools.partial(_mix_out_kernel, n_y=n_y),
        grid=(n // tm,),
        in_specs=[row(D_MODEL)] + [row(y.shape[1]) for y in ys] + [_full(w.shape) for w in ws]
                 + [_full(g.shape), _full(wr_hi.shape), _full(wr_lo.shape)],
        out_specs=[row(D_MODEL), row(D_MODEL // 2), row(LANES)],
        out_shape=[jax.ShapeDtypeStruct((n, D_MODEL), F32), jax.ShapeDtypeStruct((n, D_MODEL // 2), I32),
                   jax.ShapeDtypeStruct((n, LANES), F32)],
        compiler_params=_cparams(("parallel",)),
        name="mix_out_router",
    )(h, *ys, *ws, g, wr_hi, wr_lo)


def _sc_rows(src, idx, scatter):
    r, d = idx.shape[0], src.shape[1]
    n_src = src.shape[0]
    nw = SC_CORES * SC_SUBCORES
    per_w = r // nw
    n_chunks = per_w // SC_CHUNK
    assert r % nw == 0 and per_w % SC_CHUNK == 0 and (not scatter or n_src % per_w == 0)
    idx3 = idx.reshape(nw, n_chunks, SC_CHUNK)
    mesh = plsc.VectorSubcoreMesh(core_axis_name="c", subcore_axis_name="s")

    row_buf = pltpu.VMEM((SC_CHUNK, d), src.dtype)

    @functools.partial(
        pl.kernel, mesh=mesh, out_type=jax.ShapeDtypeStruct((r, d), src.dtype),
        scratch_types=[pltpu.VMEM((n_chunks, SC_CHUNK), I32), row_buf, row_buf] + [pltpu.SemaphoreType.DMA] * 4)
    def permute(src_hbm, idx_hbm, out_hbm, idx_v, rows0, rows1, in0, in1, out0, out1):
        wid = lax.axis_index("s") * SC_CORES + lax.axis_index("c")
        pltpu.sync_copy(idx_hbm.at[wid], idx_v)
        bufs, in_sems, out_sems = (rows0, rows1), (in0, in1), (out0, out1)

        def linear(c, n_rows):
            start = lax.rem(wid * per_w + c * SC_CHUNK, n_rows)
            return pl.ds(pl.multiple_of(start, SC_CHUNK), SC_CHUNK)

        def load(c):
            ref = src_hbm.at[linear(c, n_src)] if scatter else src_hbm.at[idx_v.at[c]]
            return pltpu.async_copy(ref, bufs[c % 2], in_sems[c % 2])

        def store(c):
            ref = out_hbm.at[idx_v.at[c]] if scatter else out_hbm.at[linear(c, r)]
            return pltpu.async_copy(bufs[c % 2], ref, out_sems[c % 2])

        loads, stores = {0: load(0)}, {}
        for c in range(n_chunks):
            if c + 1 < n_chunks:
                if c >= 1:
                    stores[c - 1].wait()
                loads[c + 1] = load(c + 1)
            loads[c].wait()
            stores[c] = store(c)
        for c in range(max(n_chunks - 2, 0), n_chunks):
            stores[c].wait()

    return permute(src, idx3)


def _moe_kernel(tile_ref, exp_ref, flag_ref, off_ref, xs_ref, wg_ref, wu_ref, wd_ref, ys_ref,
                wgu_ref, wdb_ref, acc_ref):
    w = pl.program_id(0)
    tm = xs_ref.shape[0]
    e = exp_ref[w]
    flags = flag_ref[w]

    @pl.when((flags & 4) != 0)
    def _():
        wgu_ref[:, :EXPERT_FF] = wg_ref[0, 0].astype(BF16)
        wgu_ref[:, EXPERT_FF:] = wu_ref[0, 0].astype(BF16)
        wdb_ref[...] = wd_ref[0, 0].astype(BF16)

    @pl.when((flags & 1) != 0)
    def _():
        xa, xb = _unpack_halves(xs_ref[...])
        xa = xa.astype(BF16)
        xb = xb.astype(BF16)
        half = D_MODEL // 2
        gate_up = _dot(xa, wgu_ref[:half, :]) + _dot(xb, wgu_ref[half:, :])
        hid = (jax.nn.silu(gate_up[:, :EXPERT_FF]) * gate_up[:, EXPERT_FF:]).astype(BF16)
        y = _dot(hid, wdb_ref[...])
        rows = tile_ref[w] * tm + lax.broadcasted_iota(I32, (tm, 1), 0)
        mine = (rows >= off_ref[e]) & (rows < off_ref[e + 1])
        y = jnp.where(mine, y, 0.0)

        @pl.when((flags & 2) != 0)
        def _():
            acc_ref[...] = y

        @pl.when((flags & 2) == 0)
        def _():
            acc_ref[...] += y

        ys_ref[...] = _pack_halves(acc_ref[...])


def _moe_plan(off, n_rows, tm):
    nt = n_rows // tm
    n_items = nt + N_EXPERTS - 1
    row_lo = jnp.arange(nt, dtype=I32) * tm
    ends = off[1:N_EXPERTS][None, :]
    e_lo = jnp.sum((ends <= row_lo[:, None]).astype(I32), axis=1)
    e_hi = jnp.sum((ends <= (row_lo + tm - 1)[:, None]).astype(I32), axis=1)
    span = e_hi - e_lo + 1
    start = jnp.cumsum(span) - span
    total = jnp.sum(span)
    w = jnp.arange(n_items, dtype=I32)
    valid = w < total
    tile = jnp.clip(jnp.sum((start[None, :] <= w[:, None]).astype(I32), axis=1) - 1, 0, nt - 1)
    expert = jnp.where(valid, e_lo[tile] + (w - start[tile]), e_hi[nt - 1])
    first = valid & (w == start[tile])
    new_e = jnp.concatenate([jnp.ones((1,), bool), expert[1:] != expert[:-1]])
    flags = valid.astype(I32) + 2 * first.astype(I32) + 4 * new_e.astype(I32)
    return tile, expert, flags


def _moe(xs, tile, expert, flags, off, wg, wu, wd, layer, tm):
    n_rows, half = xs.shape
    n_items = tile.shape[0]
    grid_spec = pltpu.PrefetchScalarGridSpec(
        num_scalar_prefetch=4,
        grid=(n_items,),
        in_specs=[pl.BlockSpec((tm, half), lambda w, t, e, f, o: (t[w], 0)),
                  pl.BlockSpec((1, 1, D_MODEL, EXPERT_FF), lambda w, t, e, f, o: (layer, e[w], 0, 0)),
                  pl.BlockSpec((1, 1, D_MODEL, EXPERT_FF), lambda w, t, e, f, o: (layer, e[w], 0, 0)),
                  pl.BlockSpec((1, 1, EXPERT_FF, D_MODEL), lambda w, t, e, f, o: (layer, e[w], 0, 0))],
        out_specs=pl.BlockSpec((tm, half), lambda w, t, e, f, o: (t[w], 0)),
        scratch_shapes=[pltpu.VMEM((D_MODEL, 2 * EXPERT_FF), BF16),
                        pltpu.VMEM((EXPERT_FF, D_MODEL), BF16), pltpu.VMEM((tm, D_MODEL), F32)])
    return pl.pallas_call(
        _moe_kernel,
        grid_spec=grid_spec,
        out_shape=jax.ShapeDtypeStruct((n_rows, half), I32),
        compiler_params=_cparams(("arbitrary",)),
        name="moe_experts",
    )(tile, expert, flags, off, xs, wg, wu, wd)


def _moe_sparse(xn_packed, route, wg, wu, wd, layer, tm=256):
    n = xn_packed.shape[0]
    eid = jnp.concatenate([route[:, 0], route[:, 1]]).astype(I32)
    sorted_e, sorted_pair = lax.sort_key_val(eid, jnp.arange(2 * n, dtype=I32))
    token = jnp.where(sorted_pair >= n, sorted_pair - n, sorted_pair)
    bounds = jnp.arange(N_EXPERTS + 1, dtype=I32)
    off = jnp.sum((eid[None, :] < bounds[:, None]).astype(I32), axis=1)
    tile, expert, flags = _moe_plan(off, 2 * n, tm)
    xs = _sc_rows(xn_packed, token, scatter=False)
    ys = _moe(xs, tile, expert, flags, off, wg, wu, wd, layer, tm)
    return _sc_rows(ys, sorted_pair, scatter=True)


def _ple_kernel(h1_ref, y0_ref, y1_ref, route_ref, p_ref, wup_ref, wgate_ref, g_ref, o_ref):
    w1 = route_ref[:, 2:3]
    w2 = route_ref[:, 3:4]
    a0, b0 = _unpack_halves(y0_ref[...])
    a1, b1 = _unpack_halves(y1_ref[...])
    moe = jnp.concatenate([w1 * a0 + w2 * a1, w1 * b0 + w2 * b1], axis=1)
    h2 = h1_ref[...] + moe
    e = _rms(_dot(p_ref[...].astype(BF16), wup_ref[...]), g_ref[...])
    gate = jax.nn.sigmoid(_dot(h2.astype(BF16), wgate_ref[...]))
    o_ref[...] = h2 + e * gate


def _ple(h1, yp, route, p_all, layer, wup, wgate, g, tm=512):
    n = h1.shape[0]
    nt = n // tm
    row = lambda w: pl.BlockSpec((tm, w), lambda i: (i, 0))
    return pl.pallas_call(
        _ple_kernel,
        grid=(nt,),
        in_specs=[row(D_MODEL), row(D_MODEL // 2),
                  pl.BlockSpec((tm, D_MODEL // 2), lambda i: (i + nt, 0)),
                  row(LANES), pl.BlockSpec((tm, PLE_DIM), lambda i: (i + layer * nt, 0)),
                  _full(wup.shape), _full(wgate.shape), _full(g.shape)],
        out_specs=row(D_MODEL),
        out_shape=jax.ShapeDtypeStruct((n, D_MODEL), F32),
        compiler_params=_cparams(("parallel",)),
        name="ple",
    )(h1, yp, yp, route, p_all, wup, wgate, g)


def _odd_in_kernel(h_ref, g_ref, wrow_ref, wkt_ref, cos_ref, sin_ref, cost_ref, sint_ref,
                   q_ref, v_ref, gate_ref, kt_ref):
    tm = h_ref.shape[0]
    hn = _rms(h_ref[...], g_ref[...]).astype(BF16)
    cos = cos_ref[...]
    sin = sin_ref[...]
    half = RET_QK_DIM // 2
    q = _dot(hn, wrow_ref[:, :RET_QK_WIDTH])
    for hd in range(RET_HEADS):
        sl = slice(hd * RET_QK_DIM, (hd + 1) * RET_QK_DIM)
        qh = q[:, sl]
        q_ref[:, sl] = (qh * cos + pltpu.roll(qh, half, 1) * sin).astype(BF16)
    v_ref[...] = _dot(hn, wrow_ref[:, RET_QK_WIDTH:RET_QK_WIDTH + RET_V_WIDTH]).astype(BF16)
    gate_ref[...] = _dot(hn, wrow_ref[:, RET_QK_WIDTH + RET_V_WIDTH:]).astype(BF16)

    cost = cost_ref[...] * (RET_QK_DIM ** -0.5)
    sint = sint_ref[...] * (RET_QK_DIM ** -0.5)
    kt = _dot_nt(wkt_ref[...], hn)
    for hd in range(RET_HEADS):
        sl = slice(hd * RET_QK_DIM, (hd + 1) * RET_QK_DIM)
        kh = kt[sl, :]
        swapped = jnp.concatenate([kh[half:], kh[:half]], axis=0)
        kt_ref[sl, :] = (kh * cost + swapped * sint).astype(BF16)


def _odd_in(h, g, wrow, wkt, cos, sin, cost, sint, seq, tm=512):
    n = h.shape[0]
    nt = seq // tm
    row = lambda w: pl.BlockSpec((tm, w), lambda i: (i, 0))
    return pl.pallas_call(
        _odd_in_kernel,
        grid=(n // tm,),
        in_specs=[row(D_MODEL), _full(g.shape), _full(wrow.shape), _full(wkt.shape),
                  pl.BlockSpec((tm, RET_QK_DIM), lambda i: (i % nt, 0)),
                  pl.BlockSpec((tm, RET_QK_DIM), lambda i: (i % nt, 0)),
                  pl.BlockSpec((RET_QK_DIM, tm), lambda i: (0, i % nt)),
                  pl.BlockSpec((RET_QK_DIM, tm), lambda i: (0, i % nt))],
        out_specs=[row(RET_QK_WIDTH), row(RET_V_WIDTH), row(RET_V_WIDTH),
                   pl.BlockSpec((RET_QK_WIDTH, tm), lambda i: (0, i))],
        out_shape=[jax.ShapeDtypeStruct((n, RET_QK_WIDTH), BF16), jax.ShapeDtypeStruct((n, RET_V_WIDTH), BF16),
                   jax.ShapeDtypeStruct((n, RET_V_WIDTH), BF16), jax.ShapeDtypeStruct((RET_QK_WIDTH, n), BF16)],
        compiler_params=_cparams(("parallel",)),
        name="odd_in_proj",
    )(h, g, wrow, wkt, cos, sin, cost, sint)


def _ret_kernel(cdec_ref, q_ref, kt_ref, v_ref, gate_ref, dintra_ref, qdec_ref, kdec_ref, gn_ref,
                y_ref, state_ref):
    @pl.when(pl.program_id(1) == 0)
    def _():
        state_ref[...] = jnp.zeros_like(state_ref)

    for h in range(RET_HEADS):
        ks = slice(h * RET_QK_DIM, (h + 1) * RET_QK_DIM)
        vs = slice(h * RET_V_DIM, (h + 1) * RET_V_DIM)
        qh = q_ref[:, ks]
        kth = kt_ref[ks, :]
        vh = v_ref[:, vs]
        state = state_ref[h]
        inner = (_dot(qh, kth) * dintra_ref[h]).astype(BF16)
        o = _dot(inner, vh) + _dot(qh, state.astype(BF16)) * qdec_ref[h]
        kd = (kth.astype(F32) * kdec_ref[h]).astype(BF16)
        state_ref[h] = state * cdec_ref[h] + _dot(kd, vh)
        mu = jnp.mean(o, axis=-1, keepdims=True)
        oc = o - mu
        var = jnp.mean(oc * oc, axis=-1, keepdims=True)
        on = oc * lax.rsqrt(var + GN_EPS) * gn_ref[:, vs]
        y_ref[:, vs] = (jax.nn.silu(gate_ref[:, vs].astype(F32)) * on).astype(BF16)


def _retention(cdec, q, kt, v, gate, dintra, qdec, kdec, gn, batch, seq):
    n = q.shape[0]
    c = RET_TILE
    nc = seq // c
    row = lambda w: pl.BlockSpec((c, w), lambda b, t: (b * nc + t, 0))
    return pl.pallas_call(
        _ret_kernel,
        grid=(batch, nc),
        in_specs=[pl.BlockSpec(memory_space=pltpu.SMEM),
                  row(RET_QK_WIDTH),
                  pl.BlockSpec((RET_QK_WIDTH, c), lambda b, t: (0, b * nc + t)),
                  row(RET_V_WIDTH), row(RET_V_WIDTH),
                  _full(dintra.shape), _full(qdec.shape), _full(kdec.shape), _full(gn.shape)],
        out_specs=row(RET_V_WIDTH),
        out_shape=jax.ShapeDtypeStruct((n, RET_V_WIDTH), BF16),
        scratch_shapes=[pltpu.VMEM((RET_HEADS, RET_QK_DIM, RET_V_DIM), F32)],
        compiler_params=_cparams(("parallel", "arbitrary")),
        name="retention",
    )(cdec, q, kt, v, gate, dintra, qdec, kdec, gn)


def _t5_bucket(rel):
    half = N_BUCKETS // 2
    max_exact = half // 2
    ret = (rel > 0).astype(I32) * half
    n = jnp.abs(rel)
    nf = jnp.maximum(n, 1).astype(F32)
    large = max_exact + (jnp.log(nf / max_exact) / math.log(MAX_DISTANCE / max_exact)
                         * (half - max_exact)).astype(I32)
    large = jnp.minimum(large, half - 1)
    return ret + jnp.where(n < max_exact, n, large)


def _bias_tables(rel_bias):
    c = jnp.arange(BIAS_BLOCK, dtype=I32)[:, None]
    r = jnp.arange(Q_TILE, dtype=I32)[None, :]
    rels = jnp.stack([c + (b - 1) * BIAS_BLOCK - r for b in range(Q_TILE // BIAS_BLOCK + 1)])
    onehot = jax.nn.one_hot(_t5_bucket(rels), N_BUCKETS, dtype=F32)
    tab = jnp.einsum("abcn,nh->ahbc", onehot, rel_bias, precision=lax.Precision.HIGHEST)
    far = rel_bias[_t5_bucket(jnp.int32(-2 * MAX_DISTANCE))]
    return (tab * LOG2E).astype(F32), (far * LOG2E).astype(F32)


def _block_diag(w):
    nb, bs, _ = w.shape
    eye = jnp.eye(nb, dtype=w.dtype)
    return jnp.einsum("hij,hg->higj", w, eye).reshape(nb * bs, nb * bs)


def _router_weights(w_group, w_expert):
    wr = jnp.concatenate([w_expert, w_group, jnp.zeros((D_MODEL, LANES - N_EXPERTS - N_GROUPS), F32)], axis=1)
    hi = wr.astype(BF16)
    lo = (wr - hi.astype(F32)).astype(BF16)
    return hi, lo


def _rotary_tables(seq):
    half = RET_QK_DIM // 2
    inv = ROPE_BASE ** (-jnp.arange(half, dtype=F32) / half)
    ang = jnp.arange(seq, dtype=F32)[:, None] * inv[None, :]
    cos, sin = jnp.cos(ang), jnp.sin(ang)
    cos_row = jnp.concatenate([cos, cos], axis=1)
    sin_row = jnp.concatenate([-sin, sin], axis=1)
    cos_t = jnp.concatenate([cos.T, cos.T], axis=0)
    sin_t = jnp.concatenate([-sin.T, sin.T], axis=0)
    return cos_row, sin_row, cos_t, sin_t


def _retention_tables():
    c = RET_TILE
    log_g = jnp.log(1.0 - 2.0 ** (-5.0 - jnp.arange(RET_HEADS, dtype=F32)))
    pos = jnp.arange(c, dtype=F32)
    diff = pos[:, None] - pos[None, :]
    causal = diff >= 0
    dintra = jnp.where(causal[None], jnp.exp(jnp.where(causal, diff, 0.0)[None] * log_g[:, None, None]), 0.0)
    qdec = jnp.exp((pos + 1.0)[None, :, None] * log_g[:, None, None])
    kdec = jnp.exp((c - 1.0 - pos)[None, None, :] * log_g[:, None, None])
    cdec = jnp.exp(c * log_g)
    return dintra, qdec, kdec, cdec


def kernel(x, p, rel_bias, mix_norm_g, ffn_norm_g, ple_norm_g, ev_w_in, ev_conv_w, ev_conv_b, ev_lru_wa, ev_lru_ba, ev_lru_wx, ev_lru_bx, ev_lru_lambda, ev_q_norm_g, ev_k_norm_g, ev_w_out, od_w_in, od_gn_g, od_w_out, moe_w_group, moe_w_expert, moe_w_gate, moe_w_up, moe_w_down, ple_w_up, ple_w_gate):
    batch, seq, _ = x.shape
    n = batch * seq
    depth = p.shape[0]
    h = x.reshape(n, D_MODEL)
    row = lambda a: a.reshape(1, -1)

    tab, far = _bias_tables(rel_bias)
    cos_row, sin_row, cos_t, sin_t = _rotary_tables(seq)
    dintra, qdec, kdec, cdec = _retention_tables()
    head_ones = _block_diag(jnp.ones((ATT_HEADS, ATT_HEAD_DIM, ATT_HEAD_DIM), BF16))

    for i in range(depth):
        jdx = i // 2
        if i % 2 == 0:
            w = ev_w_in[jdx]
            o = [0, 512, 1024, 1536, 2048, 2560, 3072, 3136, 3144]
            xa_w, ga_w, q_w, k_w, v_w, iq_w, ik_w, iw_w = [w[:, o[t]:o[t + 1]] for t in range(8)]
            wrow = jnp.concatenate([xa_w, ga_w, k_w, ik_w, ik_w], axis=1).astype(BF16)
            wt = jnp.concatenate([q_w, iq_w, v_w, iw_w, jnp.zeros((D_MODEL, 16 - IDX_HEADS), F32)],
                                 axis=1).T.astype(BF16)
            qg = ev_q_norm_g[jdx].reshape(-1, 1) * (ATT_HEAD_DIM ** -0.5 * LOG2E)
            kg = jnp.tile(ev_k_norm_g[jdx], ATT_HEADS).reshape(1, -1)
            xg, k, ik, qt, iqt, vt, iwt = _even_in(h, row(mix_norm_g[i]), wrow, wt, head_ones, qg, kg)
            ya = _rglru(xg, ev_conv_w[jdx], row(ev_conv_b[jdx]),
                        _block_diag(ev_lru_wa[jdx]).astype(BF16), row(ev_lru_ba[jdx]),
                        _block_diag(ev_lru_wx[jdx]).astype(BF16), row(ev_lru_bx[jdx]),
                        row(ev_lru_lambda[jdx]), batch, seq)
            yb = _dsa(far, qt, iqt, iwt, k, ik, vt, tab, batch, seq)
            wo = ev_w_out[jdx].astype(BF16)
            ys, ws = [ya, yb], [wo[:LRU_WIDTH], wo[LRU_WIDTH:]]
        else:
            w = od_w_in[jdx]
            wrow = jnp.concatenate([w[:, :RET_QK_WIDTH], w[:, 2 * RET_QK_WIDTH:]], axis=1).astype(BF16)
            wkt = w[:, RET_QK_WIDTH:2 * RET_QK_WIDTH].T.astype(BF16)
            q, v, gate, kt = _odd_in(h, row(mix_norm_g[i]), wrow, wkt, cos_row, sin_row, cos_t, sin_t, seq)
            yc = _retention(cdec, q, kt, v, gate, dintra, qdec, kdec, row(od_gn_g[jdx]), batch, seq)
            ys, ws = [yc], [od_w_out[jdx].astype(BF16)]
        wr_hi, wr_lo = _router_weights(moe_w_group[i], moe_w_expert[i])
        h1, xn, route = _mix_out(h, ys, ws, row(ffn_norm_g[i]), wr_hi, wr_lo)
        yp = _moe_sparse(xn, route, moe_w_gate, moe_w_up, moe_w_down, i)
        h = _ple(h1, yp, route, p.reshape(depth * n, PLE_DIM), i, ple_w_up[i].astype(BF16),
                 ple_w_gate[i].astype(BF16), row(ple_norm_g[i]))
    return h.reshape(batch, seq, D_MODEL)
```

```python
import functools
import math

import jax
import jax.numpy as jnp
from jax import lax
from jax.experimental import pallas as pl
from jax.experimental.pallas import tpu as pltpu
from jax.experimental.pallas import tpu_sc as plsc

F32 = jnp.float32
BF16 = jnp.bfloat16
I32 = jnp.int32

D_MODEL = 1024
CHUNK = 64
PLE_DIM = 256
RMS_EPS = 1e-6

LRU_WIDTH = 512
LRU_BLOCKS = 8
LRU_C = 8.0

ATT_HEADS = 8
ATT_HEAD_DIM = 64
ATT_WIDTH = 512
IDX_HEADS = 8
IDX_DIM = 64
MAX_TOP_K = 256
N_BUCKETS = 32
MAX_DISTANCE = 128

RET_HEADS = 8
RET_QK_DIM = 128
RET_V_DIM = 256
RET_QK_WIDTH = 1024
RET_V_WIDTH = 2048
ROPE_BASE = 10000.0
GN_EPS = 1e-5

N_GROUPS = 4
EXPERTS_PER_GROUP = 8
N_EXPERTS = 32
EXPERT_FF = 512

LANES = 128
INT_MIN = -(2 ** 31)
NEG_BIG = -1e30
LOG2E = 1.4426950408889634
VMEM_LIMIT = 56 * 1024 * 1024

SC_CORES = 2
SC_SUBCORES = 16
SC_CHUNK = 64

MIX_SUB = 256
Q_TILE = 128
BIAS_BLOCK = 128
K_TILE = 512
RET_TILE = 256


def _cparams(sem):
    return pltpu.CompilerParams(dimension_semantics=sem, vmem_limit_bytes=VMEM_LIMIT)


def _full(shape):
    nd = len(shape)
    return pl.BlockSpec(shape, lambda *_: (0,) * nd)


def _rms(xf, g):
    return xf * lax.rsqrt(jnp.mean(xf * xf, axis=-1, keepdims=True) + RMS_EPS) * g


def _dot(a, b):
    return jnp.dot(a, b, preferred_element_type=F32)


def _dot_nt(a, b):
    return lax.dot_general(a, b, (((1,), (1,)), ((), ())), preferred_element_type=F32)


def _neg_expm1(x):
    series = -x * (1.0 + x * (0.5 + x * (1.0 / 6.0 + x * (1.0 / 24.0))))
    return jnp.where(x > -0.03, series, 1.0 - jnp.exp(x))


def _pack_halves(x):
    w = x.shape[1] // 2
    hi = pltpu.bitcast(x[:, :w].astype(BF16).astype(F32), I32)
    lo = pltpu.bitcast(x[:, w:].astype(BF16).astype(F32), I32)
    return hi | lax.shift_right_logical(lo, 16)


def _unpack_halves(p):
    hi = pltpu.bitcast(p & jnp.int32(-65536), F32)
    lo = pltpu.bitcast(lax.shift_left(p, 16), F32)
    return hi, lo


def _split_bf16(x):
    hi = x.astype(BF16)
    lo = (x - hi.astype(F32)).astype(BF16)
    return hi, lo


def _even_in_kernel(h_ref, g_ref, wrow_ref, wt_ref, bd_ref, qg_ref, kg_ref,
                    xg_ref, k_ref, ik_ref, qt_ref, iqt_ref, vt_ref, iwt_ref):
    tm = h_ref.shape[0]
    w = ATT_WIDTH
    hn = _rms(h_ref[...], g_ref[...]).astype(BF16)

    xg_ref[...] = _dot(hn, wrow_ref[:, :2 * LRU_WIDTH])
    kf = _dot(hn, wrow_ref[:, 2 * LRU_WIDTH:2 * LRU_WIDTH + w])
    hi, lo = _split_bf16(kf * kf)
    ss = _dot(hi, bd_ref[...]) + _dot(lo, bd_ref[...])
    k_ref[...] = (kf * lax.rsqrt(ss * (1.0 / ATT_HEAD_DIM) + RMS_EPS) * kg_ref[...]).astype(BF16)
    ik_ref[...] = _dot(hn, wrow_ref[:, 2 * LRU_WIDTH + w:]).astype(BF16)

    feat = _dot_nt(wt_ref[...], hn)
    q3 = feat[0:w].reshape(ATT_HEADS, ATT_HEAD_DIM, tm)
    ssq = jnp.sum(q3 * q3, axis=1, keepdims=True)
    qn = q3 * lax.rsqrt(ssq * (1.0 / ATT_HEAD_DIM) + RMS_EPS) * qg_ref[...][None]
    qt_ref[...] = qn.reshape(w, tm).astype(BF16)
    iqt_ref[...] = feat[w:2 * w].astype(BF16)
    vt_ref[...] = feat[2 * w:3 * w].astype(BF16)
    iwt_ref[...] = feat[3 * w:3 * w + IDX_HEADS] * (IDX_HEADS ** -0.5 * IDX_DIM ** -0.5)


def _even_in(h, g, wrow, wt, bd, qg, kg, tm=512):
    n = h.shape[0]
    row = lambda w: pl.BlockSpec((tm, w), lambda i: (i, 0))
    col = lambda r: pl.BlockSpec((r, tm), lambda i: (0, i))
    return pl.pallas_call(
        _even_in_kernel,
        grid=(n // tm,),
        in_specs=[row(D_MODEL), _full(g.shape), _full(wrow.shape), _full(wt.shape),
                  _full(bd.shape), _full(qg.shape), _full(kg.shape)],
        out_specs=[row(1024), row(512), row(LANES), col(512), col(512), col(512), col(IDX_HEADS)],
        out_shape=[jax.ShapeDtypeStruct((n, 1024), F32), jax.ShapeDtypeStruct((n, 512), BF16),
                   jax.ShapeDtypeStruct((n, LANES), BF16), jax.ShapeDtypeStruct((512, n), BF16),
                   jax.ShapeDtypeStruct((512, n), BF16), jax.ShapeDtypeStruct((512, n), BF16),
                   jax.ShapeDtypeStruct((IDX_HEADS, n), F32)],
        compiler_params=_cparams(("parallel",)),
        name="even_in_proj",
    )(h, g, wrow, wt, bd, qg, kg)


def _rglru_kernel(xg_ref, cw_ref, cb_ref, wa_ref, ba_ref, wx_ref, bx_ref, lam_ref, ya_ref,
                  ext_ref, hst_ref):
    ts = xg_ref.shape[0]
    w = LRU_WIDTH

    @pl.when(pl.program_id(1) == 0)
    def _():
        ext_ref[0:8, :] = jnp.zeros((8, w), F32)
        hst_ref[...] = jnp.zeros_like(hst_ref)

    xa = xg_ref[:, :w]
    ga = xg_ref[:, w:]
    row = lax.broadcasted_iota(I32, (ts, w), 0)
    ext_ref[8:, :] = xa
    xc = xa * cw_ref[3:4, :] + cb_ref[...]
    for d in (1, 2, 3):
        xc = xc + ext_ref[8 - d:8 - d + ts, :] * cw_ref[3 - d:4 - d, :]
    ext_ref[0:8, :] = xa[ts - 8:, :]

    xcb = xc.astype(BF16)
    r = jax.nn.sigmoid(_dot(xcb, wa_ref[...]) + ba_ref[...])
    gi = jax.nn.sigmoid(_dot(xcb, wx_ref[...]) + bx_ref[...])
    nl = -lam_ref[...]
    softplus = jnp.maximum(nl, 0.0) + jnp.log1p(jnp.exp(-jnp.abs(nl)))
    log_a = (-LRU_C) * r * softplus
    a = jnp.exp(log_a)
    u = jnp.sqrt(_neg_expm1(2.0 * log_a)) * (gi * xc)

    groups = ts // 8
    a3 = a.reshape(groups, 8, w)
    u3 = u.reshape(groups, 8, w)
    pos = lax.broadcasted_iota(I32, (groups, 8, w), 1)
    for d in (1, 2, 4):
        keep = pos >= d
        a_sh = jnp.where(keep, pltpu.roll(a3, d, 1), 1.0)
        u_sh = jnp.where(keep, pltpu.roll(u3, d, 1), 0.0)
        u3 = a3 * u_sh + u3
        a3 = a3 * a_sh
    carry = hst_ref[0:1, :]
    hs = []
    for g in range(groups):
        hg = a3[g] * carry + u3[g]
        carry = hg[7:8, :]
        hs.append(hg)
    hst_ref[0:1, :] = carry
    hseq = jnp.concatenate(hs, axis=0)
    ya_ref[...] = (hseq * jax.nn.gelu(ga)).astype(BF16)


def _rglru(xg, cw, cb, wa, ba, wx, bx, lam, batch, seq, ts=256):
    n = xg.shape[0]
    nt = seq // ts
    return pl.pallas_call(
        _rglru_kernel,
        grid=(batch, nt),
        in_specs=[pl.BlockSpec((ts, 1024), lambda b, t: (b * nt + t, 0)),
                  _full(cw.shape), _full(cb.shape), _full(wa.shape), _full(ba.shape),
                  _full(wx.shape), _full(bx.shape), _full(lam.shape)],
        out_specs=pl.BlockSpec((ts, LRU_WIDTH), lambda b, t: (b * nt + t, 0)),
        out_shape=jax.ShapeDtypeStruct((n, LRU_WIDTH), BF16),
        scratch_shapes=[pltpu.VMEM((ts + 8, LRU_WIDTH), F32), pltpu.VMEM((8, LRU_WIDTH), F32)],
        compiler_params=_cparams(("parallel", "arbitrary")),
        name="rglru",
    )(xg, cw, cb, wa, ba, wx, bx, lam)


def _dsa_kernel(far_ref, qt_ref, iqt_ref, iwt_ref, k_ref, ik_ref, vt_ref, tab_ref, o_ref,
                keys_ref, sel_ref, cut_ref, *, seq, top_k):
    tq = Q_TILE
    hd = ATT_HEAD_DIM
    j = pl.program_id(1)
    q0 = j * tq
    nkt = (q0 + tq + K_TILE - 1) // K_TILE
    qcol = q0 + lax.broadcasted_iota(I32, (1, tq), 1)
    qlim = (qcol // CHUNK + 1) * CHUNK
    zero_rows = jnp.zeros((hd, tq), BF16)

    def head_rows(ref, h):
        blk = ref[h * hd:(h + 1) * hd, :]
        return jnp.concatenate([blk, zero_rows] if h % 2 == 0 else [zero_rows, blk], axis=0)

    def ktile(kt):
        return pl.multiple_of(kt * K_TILE, K_TILE)

    def key_pos(off, rows):
        return off + lax.broadcasted_iota(I32, (rows, tq), 0)

    iq_m = [head_rows(iqt_ref, h) for h in range(IDX_HEADS)]

    def score_body(kt, c):
        off = ktile(kt)
        ik = ik_ref[pl.ds(off, K_TILE), :]
        s = jnp.zeros((K_TILE, tq), F32)
        for h in range(IDX_HEADS):
            s = s + jnp.maximum(_dot(ik, iq_m[h]), 0.0) * iwt_ref[h:h + 1, :]
        s = jnp.where(s == 0.0, 0.0, s)
        bits = pltpu.bitcast(s, I32)
        key = bits ^ ((bits >> 31) & 0x7FFFFFFF)
        keys_ref[pl.ds(off, K_TILE), :] = jnp.where(key_pos(off, K_TILE) < qlim, key, INT_MIN)
        return c

    lax.fori_loop(0, nkt, score_body, 0)

    def count(pred):
        def body(kt, acc):
            off = ktile(kt)
            ind = pred(keys_ref[pl.ds(off, K_TILE), :], key_pos(off, K_TILE)).reshape(K_TILE // 8, 8, tq)
            while ind.shape[0] > 1:
                half = ind.shape[0] // 2
                ind = ind[:half] + ind[half:]
            return acc + ind[0]
        acc = lax.fori_loop(0, nkt, body, jnp.zeros((8, tq), F32))
        return jnp.sum(acc, axis=0, keepdims=True)

    kf = float(top_k)

    def search_body(i, ans):
        cand = ans + jnp.left_shift(jnp.int32(1), 31 - i)
        cnt = count(lambda kk, kpos: jnp.where(kk >= cand, 1.0, 0.0))
        return jnp.where(cnt >= kf, cand, ans)

    ans = lax.fori_loop(0, 32, search_body, jnp.full((1, tq), INT_MIN, I32))

    cnt_ge = count(lambda kk, kpos: jnp.where(kk >= ans, 1.0, 0.0))
    excess = jnp.where(ans > INT_MIN, cnt_ge - kf, 0.0)
    cut_ref[...] = jnp.full(cut_ref.shape, seq, I32)

    @pl.when(jnp.max(excess) > 0.0)
    def _():
        need = kf - count(lambda kk, kpos: jnp.where(kk > ans, 1.0, 0.0))

        def idx_body(i, pos):
            cand = pos + jnp.left_shift(jnp.int32(1), (seq.bit_length() - 2) - i)
            c = count(lambda kk, kpos: jnp.where(kk == ans, jnp.where(kpos < cand, 1.0, 0.0), 0.0))
            return jnp.where(c < need, cand, pos)

        pos = lax.fori_loop(0, seq.bit_length() - 1, idx_body, jnp.zeros((1, tq), I32))
        cut_ref[...] = jnp.broadcast_to(pos, cut_ref.shape)

    thr = jnp.maximum(ans, INT_MIN + 1)
    cut = cut_ref[0:1, :]

    near_w = tq + BIAS_BLOCK
    near_start = pl.multiple_of(jnp.maximum(q0 - BIAS_BLOCK, 0), BIAS_BLOCK)

    def selection(kk, kpos):
        tie = jnp.where(kpos <= cut, 0.0, NEG_BIG)
        return jnp.where(kk > thr, 0.0, jnp.where(kk == thr, tie, NEG_BIG))

    def sel_body(kt, c):
        off = ktile(kt)
        kpos = key_pos(off, K_TILE)
        s = selection(keys_ref[pl.ds(off, K_TILE), :], kpos)
        sel_ref[pl.ds(off, K_TILE), :] = jnp.where(kpos < near_start, s, NEG_BIG)
        return c

    n_far = (near_start + K_TILE - 1) // K_TILE
    lax.fori_loop(0, n_far, sel_body, 0)
    sel_near = selection(keys_ref[pl.ds(near_start, near_w), :], key_pos(near_start, near_w))

    def step(qm, kb, vte, add, carry):
        m, acc = carry
        s = _dot(kb, qm) + add
        m_new = jnp.maximum(m, jnp.max(s, axis=0, keepdims=True))
        p = jnp.exp2(s - m_new).astype(BF16)
        acc = jnp.exp2(m - m_new) * acc + _dot(vte, p)
        return m_new, acc

    qms = [head_rows(qt_ref, h) for h in range(ATT_HEADS)]
    pairs = [slice((h // 2) * LANES, (h // 2 + 1) * LANES) for h in range(ATT_HEADS)]

    def values(h, off, width):
        return jnp.concatenate([vt_ref[h * hd:(h + 1) * hd, pl.ds(off, width)],
                                jnp.ones((hd, width), BF16)], axis=0)

    def far_body(kt, carries):
        off = ktile(kt)
        sel = sel_ref[pl.ds(off, K_TILE), :]
        return tuple(step(qms[h], k_ref[pl.ds(off, K_TILE), pairs[h]], values(h, off, K_TILE), sel, carries[h])
                     for h in range(ATT_HEADS))

    init = (jnp.full((1, tq), NEG_BIG, F32), jnp.zeros((2 * hd, tq), F32))
    carries = lax.fori_loop(0, n_far, far_body, (init,) * ATT_HEADS)

    n_blocks = near_w // BIAS_BLOCK
    first = jnp.where(j == 0, 1, 0)
    res = []
    for h in range(ATT_HEADS):
        m, acc = carries[h]
        bias = jnp.concatenate([tab_ref[jnp.minimum(first + b, n_blocks - 1), h] for b in range(n_blocks)],
                               axis=0)
        m, acc = step(qms[h], k_ref[pl.ds(near_start, near_w), pairs[h]], values(h, near_start, near_w),
                      sel_near + bias, (m + far_ref[h], acc))
        res.append(acc[:hd, :] / acc[hd:hd + 1, :])

    for p in range(ATT_HEADS // 2):
        pair_t = jnp.concatenate([res[2 * p], res[2 * p + 1]], axis=0)
        o_ref[:, p * LANES:(p + 1) * LANES] = pair_t.T.astype(BF16)


def _dsa(far, qt, iqt, iwt, k, ik, vt, tab, batch, seq):
    n = k.shape[0]
    nb = seq // Q_TILE
    top_k = min(MAX_TOP_K, seq // 4)
    qspec = lambda r: pl.BlockSpec((r, Q_TILE), lambda b, j: (0, b * nb + j))
    return pl.pallas_call(
        functools.partial(_dsa_kernel, seq=seq, top_k=top_k),
        grid=(batch, nb),
        in_specs=[pl.BlockSpec(memory_space=pltpu.SMEM),
                  qspec(512), qspec(512), qspec(IDX_HEADS),
                  pl.BlockSpec((seq, 512), lambda b, j: (b, 0)),
                  pl.BlockSpec((seq, LANES), lambda b, j: (b, 0)),
                  pl.BlockSpec((512, seq), lambda b, j: (0, b)),
                  _full(tab.shape)],
        out_specs=pl.BlockSpec((Q_TILE, 512), lambda b, j: (b * nb + j, 0)),
        out_shape=jax.ShapeDtypeStruct((n, 512), BF16),
        scratch_shapes=[pltpu.VMEM((seq, Q_TILE), I32), pltpu.VMEM((seq, Q_TILE), F32),
                        pltpu.VMEM((8, Q_TILE), I32)],
        compiler_params=_cparams(("parallel", "arbitrary")),
        name="dsa",
    )(far, qt, iqt, iwt, k, ik, vt, tab)


def _mix_out_kernel(*refs, n_y):
    h_ref = refs[0]
    y_refs = refs[1:1 + n_y]
    w_refs = refs[1 + n_y:1 + 2 * n_y]
    g_ref, wr_hi_ref, wr_lo_ref = refs[1 + 2 * n_y:4 + 2 * n_y]
    h1_ref, xn_ref, route_ref = refs[4 + 2 * n_y:]
    for sub in range(h_ref.shape[0] // MIX_SUB):
        rs = slice(sub * MIX_SUB, (sub + 1) * MIX_SUB)
        _mix_out_rows(rs, h_ref, y_refs, w_refs, g_ref, wr_hi_ref, wr_lo_ref, h1_ref, xn_ref, route_ref)


def _mix_out_rows(rs, h_ref, y_refs, w_refs, g_ref, wr_hi_ref, wr_lo_ref, h1_ref, xn_ref, route_ref):
    tm = rs.stop - rs.start
    mix = _dot(y_refs[0][rs, :], w_refs[0][...])
    for y_ref, w_ref in zip(y_refs[1:], w_refs[1:]):
        mix = mix + _dot(y_ref[rs, :], w_ref[...])
    h1 = h_ref[rs, :] + mix
    h1_ref[rs, :] = h1
    xn = _rms(h1, g_ref[...])
    hi, lo = _split_bf16(xn)
    xn_ref[rs, :] = _pack_halves(xn)
    logits = _dot(hi, wr_hi_ref[...]) + _dot(lo, wr_hi_ref[...]) + _dot(hi, wr_lo_ref[...])

    lane = lax.broadcasted_iota(I32, (tm, LANES), 1).astype(F32)
    big = float(LANES)
    ninf = -jnp.inf
    is_g = (lane >= N_EXPERTS) & (lane < N_EXPERTS + N_GROUPS)
    glog = jnp.where(is_g, logits, ninf)
    gmax = jnp.max(glog, axis=1, keepdims=True)
    gsel = jnp.min(jnp.where(glog == gmax, lane, big), axis=1, keepdims=True) - N_EXPERTS
    gprob = 1.0 / jnp.sum(jnp.exp(glog - gmax), axis=1, keepdims=True)
    lo_l = gsel * EXPERTS_PER_GROUP
    within = jnp.where((lane >= lo_l) & (lane < lo_l + EXPERTS_PER_GROUP), logits, ninf)
    v1 = jnp.max(within, axis=1, keepdims=True)
    i1 = jnp.min(jnp.where(within == v1, lane, big), axis=1, keepdims=True)
    rest = jnp.where(lane == i1, ninf, within)
    v2 = jnp.max(rest, axis=1, keepdims=True)
    i2 = jnp.min(jnp.where(rest == v2, lane, big), axis=1, keepdims=True)
    e2 = jnp.exp(v2 - v1)
    w1 = gprob / (1.0 + e2)
    w2 = gprob * e2 / (1.0 + e2)
    route_ref[rs, :] = jnp.where(lane == 0, i1, jnp.where(lane == 1, i2,
                                 jnp.where(lane == 2, w1, jnp.where(lane == 3, w2, 0.0))))


def _mix_out(h, ys, ws, g, wr_hi, wr_lo, tm=512):
    n = h.shape[0]
    n_y = len(ys)
    row = lambda w: pl.BlockSpec((tm, w), lambda i: (i, 0))
    return pl.pallas_call(
        functools.partial(_mix_out_kernel, n_y=n_y),
        grid=(n // tm,),
        in_specs=[row(D_MODEL)] + [row(y.shape[1]) for y in ys] + [_full(w.shape) for w in ws]
                 + [_full(g.shape), _full(wr_hi.shape), _full(wr_lo.shape)],
        out_specs=[row(D_MODEL), row(D_MODEL // 2), row(LANES)],
        out_shape=[jax.ShapeDtypeStruct((n, D_MODEL), F32), jax.ShapeDtypeStruct((n, D_MODEL // 2), I32),
                   jax.ShapeDtypeStruct((n, LANES), F32)],
        compiler_params=_cparams(("parallel",)),
        name="mix_out_router",
    )(h, *ys, *ws, g, wr_hi, wr_lo)


def _sc_rows(src, idx, scatter):
    r, d = idx.shape[0], src.shape[1]
    n_src = src.shape[0]
    nw = SC_CORES * SC_SUBCORES
    per_w = r // nw
    n_chunks = per_w // SC_CHUNK
    assert r % nw == 0 and per_w % SC_CHUNK == 0 and (not scatter or n_src % per_w == 0)
    idx3 = idx.reshape(nw, n_chunks, SC_CHUNK)
    mesh = plsc.VectorSubcoreMesh(core_axis_name="c", subcore_axis_name="s")

    row_buf = pltpu.VMEM((SC_CHUNK, d), src.dtype)

    @functools.partial(
        pl.kernel, mesh=mesh, out_type=jax.ShapeDtypeStruct((r, d), src.dtype),
        scratch_types=[pltpu.VMEM((n_chunks, SC_CHUNK), I32), row_buf, row_buf] + [pltpu.SemaphoreType.DMA] * 4)
    def permute(src_hbm, idx_hbm, out_hbm, idx_v, rows0, rows1, in0, in1, out0, out1):
        wid = lax.axis_index("s") * SC_CORES + lax.axis_index("c")
        pltpu.sync_copy(idx_hbm.at[wid], idx_v)
        bufs, in_sems, out_sems = (rows0, rows1), (in0, in1), (out0, out1)

        def linear(c, n_rows):
            start = lax.rem(wid * per_w + c * SC_CHUNK, n_rows)
            return pl.ds(pl.multiple_of(start, SC_CHUNK), SC_CHUNK)

        def load(c):
            ref = src_hbm.at[linear(c, n_src)] if scatter else src_hbm.at[idx_v.at[c]]
            return pltpu.async_copy(ref, bufs[c % 2], in_sems[c % 2])

        def store(c):
            ref = out_hbm.at[idx_v.at[c]] if scatter else out_hbm.at[linear(c, r)]
            return pltpu.async_copy(bufs[c % 2], ref, out_sems[c % 2])

        loads, stores = {0: load(0)}, {}
        for c in range(n_chunks):
            if c + 1 < n_chunks:
                if c >= 1:
                    stores[c - 1].wait()
                loads[c + 1] = load(c + 1)
            loads[c].wait()
            stores[c] = store(c)
        for c in range(max(n_chunks - 2, 0), n_chunks):
            stores[c].wait()

    return permute(src, idx3)


def _moe_kernel(tile_ref, exp_ref, flag_ref, off_ref, xs_ref, wg_ref, wu_ref, wd_ref, ys_ref,
                wgu_ref, wdb_ref, acc_ref):
    w = pl.program_id(0)
    tm = xs_ref.shape[0]
    e = exp_ref[w]
    flags = flag_ref[w]

    @pl.when((flags & 4) != 0)
    def _():
        wgu_ref[:, :EXPERT_FF] = wg_ref[0, 0].astype(BF16)
        wgu_ref[:, EXPERT_FF:] = wu_ref[0, 0].astype(BF16)
        wdb_ref[...] = wd_ref[0, 0].astype(BF16)

    @pl.when((flags & 1) != 0)
    def _():
        xa, xb = _unpack_halves(xs_ref[...])
        xa = xa.astype(BF16)
        xb = xb.astype(BF16)
        half = D_MODEL // 2
        gate_up = _dot(xa, wgu_ref[:half, :]) + _dot(xb, wgu_ref[half:, :])
        hid = (jax.nn.silu(gate_up[:, :EXPERT_FF]) * gate_up[:, EXPERT_FF:]).astype(BF16)
        y = _dot(hid, wdb_ref[...])
        rows = tile_ref[w] * tm + lax.broadcasted_iota(I32, (tm, 1), 0)
        mine = (rows >= off_ref[e]) & (rows < off_ref[e + 1])
        y = jnp.where(mine, y, 0.0)

        @pl.when((flags & 2) != 0)
        def _():
            acc_ref[...] = y

        @pl.when((flags & 2) == 0)
        def _():
            acc_ref[...] += y

        ys_ref[...] = _pack_halves(acc_ref[...])


def _moe_plan(off, n_rows, tm):
    nt = n_rows // tm
    n_items = nt + N_EXPERTS - 1
    row_lo = jnp.arange(nt, dtype=I32) * tm
    ends = off[1:N_EXPERTS][None, :]
    e_lo = jnp.sum((ends <= row_lo[:, None]).astype(I32), axis=1)
    e_hi = jnp.sum((ends <= (row_lo + tm - 1)[:, None]).astype(I32), axis=1)
    span = e_hi - e_lo + 1
    start = jnp.cumsum(span) - span
    total = jnp.sum(span)
    w = jnp.arange(n_items, dtype=I32)
    valid = w < total
    tile = jnp.clip(jnp.sum((start[None, :] <= w[:, None]).astype(I32), axis=1) - 1, 0, nt - 1)
    expert = jnp.where(valid, e_lo[tile] + (w - start[tile]), e_hi[nt - 1])
    first = valid & (w == start[tile])
    new_e = jnp.concatenate([jnp.ones((1,), bool), expert[1:] != expert[:-1]])
    flags = valid.astype(I32) + 2 * first.astype(I32) + 4 * new_e.astype(I32)
    return tile, expert, flags


def _moe(xs, tile, expert, flags, off, wg, wu, wd, layer, tm):
    n_rows, half = xs.shape
    n_items = tile.shape[0]
    grid_spec = pltpu.PrefetchScalarGridSpec(
        num_scalar_prefetch=4,
        grid=(n_items,),
        in_specs=[pl.BlockSpec((tm, half), lambda w, t, e, f, o: (t[w], 0)),
                  pl.BlockSpec((1, 1, D_MODEL, EXPERT_FF), lambda w, t, e, f, o: (layer, e[w], 0, 0)),
                  pl.BlockSpec((1, 1, D_MODEL, EXPERT_FF), lambda w, t, e, f, o: (layer, e[w], 0, 0)),
                  pl.BlockSpec((1, 1, EXPERT_FF, D_MODEL), lambda w, t, e, f, o: (layer, e[w], 0, 0))],
        out_specs=pl.BlockSpec((tm, half), lambda w, t, e, f, o: (t[w], 0)),
        scratch_shapes=[pltpu.VMEM((D_MODEL, 2 * EXPERT_FF), BF16),
                        pltpu.VMEM((EXPERT_FF, D_MODEL), BF16), pltpu.VMEM((tm, D_MODEL), F32)])
    return pl.pallas_call(
        _moe_kernel,
        grid_spec=grid_spec,
        out_shape=jax.ShapeDtypeStruct((n_rows, half), I32),
        compiler_params=_cparams(("arbitrary",)),
        name="moe_experts",
    )(tile, expert, flags, off, xs, wg, wu, wd)


def _moe_sparse(xn_packed, route, wg, wu, wd, layer, tm=256):
    n = xn_packed.shape[0]
    eid = jnp.concatenate([route[:, 0], route[:, 1]]).astype(I32)
    sorted_e, sorted_pair = lax.sort_key_val(eid, jnp.arange(2 * n, dtype=I32))
    token = jnp.where(sorted_pair >= n, sorted_pair - n, sorted_pair)
    bounds = jnp.arange(N_EXPERTS + 1, dtype=I32)
    off = jnp.sum((eid[None, :] < bounds[:, None]).astype(I32), axis=1)
    tile, expert, flags = _moe_plan(off, 2 * n, tm)
    xs = _sc_rows(xn_packed, token, scatter=False)
    ys = _moe(xs, tile, expert, flags, off, wg, wu, wd, layer, tm)
    return _sc_rows(ys, sorted_pair, scatter=True)


def _ple_kernel(h1_ref, y0_ref, y1_ref, route_ref, p_ref, wup_ref, wgate_ref, g_ref, o_ref):
    w1 = route_ref[:, 2:3]
    w2 = route_ref[:, 3:4]
    a0, b0 = _unpack_halves(y0_ref[...])
    a1, b1 = _unpack_halves(y1_ref[...])
    moe = jnp.concatenate([w1 * a0 + w2 * a1, w1 * b0 + w2 * b1], axis=1)
    h2 = h1_ref[...] + moe
    e = _rms(_dot(p_ref[...].astype(BF16), wup_ref[...]), g_ref[...])
    gate = jax.nn.sigmoid(_dot(h2.astype(BF16), wgate_ref[...]))
    o_ref[...] = h2 + e * gate


def _ple(h1, yp, route, p_all, layer, wup, wgate, g, tm=512):
    n = h1.shape[0]
    nt = n // tm
    row = lambda w: pl.BlockSpec((tm, w), lambda i: (i, 0))
    return pl.pallas_call(
        _ple_kernel,
        grid=(nt,),
        in_specs=[row(D_MODEL), row(D_MODEL // 2),
                  pl.BlockSpec((tm, D_MODEL // 2), lambda i: (i + nt, 0)),
                  row(LANES), pl.BlockSpec((tm, PLE_DIM), lambda i: (i + layer * nt, 0)),
                  _full(wup.shape), _full(wgate.shape), _full(g.shape)],
        out_specs=row(D_MODEL),
        out_shape=jax.ShapeDtypeStruct((n, D_MODEL), F32),
        compiler_params=_cparams(("parallel",)),
        name="ple",
    )(h1, yp, yp, route, p_all, wup, wgate, g)


def _odd_in_kernel(h_ref, g_ref, wrow_ref, wkt_ref, cos_ref, sin_ref, cost_ref, sint_ref,
                   q_ref, v_ref, gate_ref, kt_ref):
    tm = h_ref.shape[0]
    hn = _rms(h_ref[...], g_ref[...]).astype(BF16)
    cos = cos_ref[...]
    sin = sin_ref[...]
    half = RET_QK_DIM // 2
    q = _dot(hn, wrow_ref[:, :RET_QK_WIDTH])
    for hd in range(RET_HEADS):
        sl = slice(hd * RET_QK_DIM, (hd + 1) * RET_QK_DIM)
        qh = q[:, sl]
        q_ref[:, sl] = (qh * cos + pltpu.roll(qh, half, 1) * sin).astype(BF16)
    v_ref[...] = _dot(hn, wrow_ref[:, RET_QK_WIDTH:RET_QK_WIDTH + RET_V_WIDTH]).astype(BF16)
    gate_ref[...] = _dot(hn, wrow_ref[:, RET_QK_WIDTH + RET_V_WIDTH:]).astype(BF16)

    cost = cost_ref[...] * (RET_QK_DIM ** -0.5)
    sint = sint_ref[...] * (RET_QK_DIM ** -0.5)
    kt = _dot_nt(wkt_ref[...], hn)
    for hd in range(RET_HEADS):
        sl = slice(hd * RET_QK_DIM, (hd + 1) * RET_QK_DIM)
        kh = kt[sl, :]
        swapped = jnp.concatenate([kh[half:], kh[:half]], axis=0)
        kt_ref[sl, :] = (kh * cost + swapped * sint).astype(BF16)


def _odd_in(h, g, wrow, wkt, cos, sin, cost, sint, seq, tm=512):
    n = h.shape[0]
    nt = seq // tm
    row = lambda w: pl.BlockSpec((tm, w), lambda i: (i, 0))
    return pl.pallas_call(
        _odd_in_kernel,
        grid=(n // tm,),
        in_specs=[row(D_MODEL), _full(g.shape), _full(wrow.shape), _full(wkt.shape),
                  pl.BlockSpec((tm, RET_QK_DIM), lambda i: (i % nt, 0)),
                  pl.BlockSpec((tm, RET_QK_DIM), lambda i: (i % nt, 0)),
                  pl.BlockSpec((RET_QK_DIM, tm), lambda i: (0, i % nt)),
                  pl.BlockSpec((RET_QK_DIM, tm), lambda i: (0, i % nt))],
        out_specs=[row(RET_QK_WIDTH), row(RET_V_WIDTH), row(RET_V_WIDTH),
                   pl.BlockSpec((RET_QK_WIDTH, tm), lambda i: (0, i))],
        out_shape=[jax.ShapeDtypeStruct((n, RET_QK_WIDTH), BF16), jax.ShapeDtypeStruct((n, RET_V_WIDTH), BF16),
                   jax.ShapeDtypeStruct((n, RET_V_WIDTH), BF16), jax.ShapeDtypeStruct((RET_QK_WIDTH, n), BF16)],
        compiler_params=_cparams(("parallel",)),
        name="odd_in_proj",
    )(h, g, wrow, wkt, cos, sin, cost, sint)


def _ret_kernel(cdec_ref, q_ref, kt_ref, v_ref, gate_ref, dintra_ref, qdec_ref, kdec_ref, gn_ref,
                y_ref, state_ref):
    @pl.when(pl.program_id(1) == 0)
    def _():
        state_ref[...] = jnp.zeros_like(state_ref)

    for h in range(RET_HEADS):
        ks = slice(h * RET_QK_DIM, (h + 1) * RET_QK_DIM)
        vs = slice(h * RET_V_DIM, (h + 1) * RET_V_DIM)
        qh = q_ref[:, ks]
        kth = kt_ref[ks, :]
        vh = v_ref[:, vs]
        state = state_ref[h]
        inner = (_dot(qh, kth) * dintra_ref[h]).astype(BF16)
        o = _dot(inner, vh) + _dot(qh, state.astype(BF16)) * qdec_ref[h]
        kd = (kth.astype(F32) * kdec_ref[h]).astype(BF16)
        state_ref[h] = state * cdec_ref[h] + _dot(kd, vh)
        mu = jnp.mean(o, axis=-1, keepdims=True)
        oc = o - mu
        var = jnp.mean(oc * oc, axis=-1, keepdims=True)
        on = oc * lax.rsqrt(var + GN_EPS) * gn_ref[:, vs]
        y_ref[:, vs] = (jax.nn.silu(gate_ref[:, vs].astype(F32)) * on).astype(BF16)


def _retention(cdec, q, kt, v, gate, dintra, qdec, kdec, gn, batch, seq):
    n = q.shape[0]
    c = RET_TILE
    nc = seq // c
    row = lambda w: pl.BlockSpec((c, w), lambda b, t: (b * nc + t, 0))
    return pl.pallas_call(
        _ret_kernel,
        grid=(batch, nc),
        in_specs=[pl.BlockSpec(memory_space=pltpu.SMEM),
                  row(RET_QK_WIDTH),
                  pl.BlockSpec((RET_QK_WIDTH, c), lambda b, t: (0, b * nc + t)),
                  row(RET_V_WIDTH), row(RET_V_WIDTH),
                  _full(dintra.shape), _full(qdec.shape), _full(kdec.shape), _full(gn.shape)],
        out_specs=row(RET_V_WIDTH),
        out_shape=jax.ShapeDtypeStruct((n, RET_V_WIDTH), BF16),
        scratch_shapes=[pltpu.VMEM((RET_HEADS, RET_QK_DIM, RET_V_DIM), F32)],
        compiler_params=_cparams(("parallel", "arbitrary")),
        name="retention",
    )(cdec, q, kt, v, gate, dintra, qdec, kdec, gn)


def _t5_bucket(rel):
    half = N_BUCKETS // 2
    max_exact = half // 2
    ret = (rel > 0).astype(I32) * half
    n = jnp.abs(rel)
    nf = jnp.maximum(n, 1).astype(F32)
    large = max_exact + (jnp.log(nf / max_exact) / math.log(MAX_DISTANCE / max_exact)
                         * (half - max_exact)).astype(I32)
    large = jnp.minimum(large, half - 1)
    return ret + jnp.where(n < max_exact, n, large)


def _bias_tables(rel_bias):
    c = jnp.arange(BIAS_BLOCK, dtype=I32)[:, None]
    r = jnp.arange(Q_TILE, dtype=I32)[None, :]
    rels = jnp.stack([c + (b - 1) * BIAS_BLOCK - r for b in range(Q_TILE // BIAS_BLOCK + 1)])
    onehot = jax.nn.one_hot(_t5_bucket(rels), N_BUCKETS, dtype=F32)
    tab = jnp.einsum("abcn,nh->ahbc", onehot, rel_bias, precision=lax.Precision.HIGHEST)
    far = rel_bias[_t5_bucket(jnp.int32(-2 * MAX_DISTANCE))]
    return (tab * LOG2E).astype(F32), (far * LOG2E).astype(F32)


def _block_diag(w):
    nb, bs, _ = w.shape
    eye = jnp.eye(nb, dtype=w.dtype)
    return jnp.einsum("hij,hg->higj", w, eye).reshape(nb * bs, nb * bs)


def _router_weights(w_group, w_expert):
    wr = jnp.concatenate([w_expert, w_group, jnp.zeros((D_MODEL, LANES - N_EXPERTS - N_GROUPS), F32)], axis=1)
    hi = wr.astype(BF16)
    lo = (wr - hi.astype(F32)).astype(BF16)
    return hi, lo


def _rotary_tables(seq):
    half = RET_QK_DIM // 2
    inv = ROPE_BASE ** (-jnp.arange(half, dtype=F32) / half)
    ang = jnp.arange(seq, dtype=F32)[:, None] * inv[None, :]
    cos, sin = jnp.cos(ang), jnp.sin(ang)
    cos_row = jnp.concatenate([cos, cos], axis=1)
    sin_row = jnp.concatenate([-sin, sin], axis=1)
    cos_t = jnp.concatenate([cos.T, cos.T], axis=0)
    sin_t = jnp.concatenate([-sin.T, sin.T], axis=0)
    return cos_row, sin_row, cos_t, sin_t


def _retention_tables():
    c = RET_TILE
    log_g = jnp.log(1.0 - 2.0 ** (-5.0 - jnp.arange(RET_HEADS, dtype=F32)))
    pos = jnp.arange(c, dtype=F32)
    diff = pos[:, None] - pos[None, :]
    causal = diff >= 0
    dintra = jnp.where(causal[None], jnp.exp(jnp.where(causal, diff, 0.0)[None] * log_g[:, None, None]), 0.0)
    qdec = jnp.exp((pos + 1.0)[None, :, None] * log_g[:, None, None])
    kdec = jnp.exp((c - 1.0 - pos)[None, None, :] * log_g[:, None, None])
    cdec = jnp.exp(c * log_g)
    return dintra, qdec, kdec, cdec


def kernel(x, p, rel_bias, mix_norm_g, ffn_norm_g, ple_norm_g, ev_w_in, ev_conv_w, ev_conv_b, ev_lru_wa, ev_lru_ba, ev_lru_wx, ev_lru_bx, ev_lru_lambda, ev_q_norm_g, ev_k_norm_g, ev_w_out, od_w_in, od_gn_g, od_w_out, moe_w_group, moe_w_expert, moe_w_gate, moe_w_up, moe_w_down, ple_w_up, ple_w_gate):
    batch, seq, _ = x.shape
    n = batch * seq
    depth = p.shape[0]
    h = x.reshape(n, D_MODEL)
    row = lambda a: a.reshape(1, -1)

    tab, far = _bias_tables(rel_bias)
    cos_row, sin_row, cos_t, sin_t = _rotary_tables(seq)
    dintra, qdec, kdec, cdec = _retention_tables()
    head_ones = _block_diag(jnp.ones((ATT_HEADS, ATT_HEAD_DIM, ATT_HEAD_DIM), BF16))

    for i in range(depth):
        jdx = i // 2
        if i % 2 == 0:
            w = ev_w_in[jdx]
            o = [0, 512, 1024, 1536, 2048, 2560, 3072, 3136, 3144]
            xa_w, ga_w, q_w, k_w, v_w, iq_w, ik_w, iw_w = [w[:, o[t]:o[t + 1]] for t in range(8)]
            wrow = jnp.concatenate([xa_w, ga_w, k_w, ik_w, ik_w], axis=1).astype(BF16)
            wt = jnp.concatenate([q_w, iq_w, v_w, iw_w, jnp.zeros((D_MODEL, 16 - IDX_HEADS), F32)],
                                 axis=1).T.astype(BF16)
            qg = ev_q_norm_g[jdx].reshape(-1, 1) * (ATT_HEAD_DIM ** -0.5 * LOG2E)
            kg = jnp.tile(ev_k_norm_g[jdx], ATT_HEADS).reshape(1, -1)
            xg, k, ik, qt, iqt, vt, iwt = _even_in(h, row(mix_norm_g[i]), wrow, wt, head_ones, qg, kg)
            ya = _rglru(xg, ev_conv_w[jdx], row(ev_conv_b[jdx]),
                        _block_diag(ev_lru_wa[jdx]).astype(BF16), row(ev_lru_ba[jdx]),
                        _block_diag(ev_lru_wx[jdx]).astype(BF16), row(ev_lru_bx[jdx]),
                        row(ev_lru_lambda[jdx]), batch, seq)
            yb = _dsa(far, qt, iqt, iwt, k, ik, vt, tab, batch, seq)
            wo = ev_w_out[jdx].astype(BF16)
            ys, ws = [ya, yb], [wo[:LRU_WIDTH], wo[LRU_WIDTH:]]
        else:
            w = od_w_in[jdx]
            wrow = jnp.concatenate([w[:, :RET_QK_WIDTH], w[:, 2 * RET_QK_WIDTH:]], axis=1).astype(BF16)
            wkt = w[:, RET_QK_WIDTH:2 * RET_QK_WIDTH].T.astype(BF16)
            q, v, gate, kt = _odd_in(h, row(mix_norm_g[i]), wrow, wkt, cos_row, sin_row, cos_t, sin_t, seq)
            yc = _retention(cdec, q, kt, v, gate, dintra, qdec, kdec, row(od_gn_g[jdx]), batch, seq)
            ys, ws = [yc], [od_w_out[jdx].astype(BF16)]
        wr_hi, wr_lo = _router_weights(moe_w_group[i], moe_w_expert[i])
        h1, xn, route = _mix_out(h, ys, ws, row(ffn_norm_g[i]), wr_hi, wr_lo)
        yp = _moe_sparse(xn, route, moe_w_gate, moe_w_up, moe_w_down, i)
        h = _ple(h1, yp, route, p.reshape(depth * n, PLE_DIM), i, ple_w_up[i].astype(BF16),
                 ple_w_gate[i].astype(BF16), row(ple_norm_g[i]))
    return h.reshape(batch, seq, D_MODEL)
```

```python
import functools
import math

import jax
import jax.numpy as jnp
from jax import lax
from jax.experimental import pallas as pl
from jax.experimental.pallas import tpu as pltpu
from jax.experimental.pallas import tpu_sc as plsc

F32 = jnp.float32
BF16 = jnp.bfloat16
I32 = jnp.int32

D_MODEL = 1024
CHUNK = 64
PLE_DIM = 256
RMS_EPS = 1e-6

LRU_WIDTH = 512
LRU_BLOCKS = 8
LRU_C = 8.0

ATT_HEADS = 8
ATT_HEAD_DIM = 64
ATT_WIDTH = 512
IDX_HEADS = 8
IDX_DIM = 64
MAX_TOP_K = 256
N_BUCKETS = 32
MAX_DISTANCE = 128

RET_HEADS = 8
RET_QK_DIM = 128
RET_V_DIM = 256
RET_QK_WIDTH = 1024
RET_V_WIDTH = 2048
ROPE_BASE = 10000.0
GN_EPS = 1e-5

N_GROUPS = 4
EXPERTS_PER_GROUP = 8
N_EXPERTS = 32
EXPERT_FF = 512

LANES = 128
INT_MIN = -(2 ** 31)
NEG_BIG = -1e30
LOG2E = 1.4426950408889634
VMEM_LIMIT = 56 * 1024 * 1024

SC_CORES = 2
SC_SUBCORES = 16
SC_CHUNK = 64

MIX_SUB = 256
Q_TILE = 128
BIAS_BLOCK = 128
K_TILE = 512
RET_TILE = 256


def _cparams(sem):
    return pltpu.CompilerParams(dimension_semantics=sem, vmem_limit_bytes=VMEM_LIMIT)


def _full(shape):
    nd = len(shape)
    return pl.BlockSpec(shape, lambda *_: (0,) * nd)


def _rms(xf, g):
    return xf * lax.rsqrt(jnp.mean(xf * xf, axis=-1, keepdims=True) + RMS_EPS) * g


def _dot(a, b):
    return jnp.dot(a, b, preferred_element_type=F32)


def _dot_nt(a, b):
    return lax.dot_general(a, b, (((1,), (1,)), ((), ())), preferred_element_type=F32)


def _neg_expm1(x):
    series = -x * (1.0 + x * (0.5 + x * (1.0 / 6.0 + x * (1.0 / 24.0))))
    return jnp.where(x > -0.03, series, 1.0 - jnp.exp(x))


def _pack_halves(x):
    w = x.shape[1] // 2
    hi = pltpu.bitcast(x[:, :w].astype(BF16).astype(F32), I32)
    lo = pltpu.bitcast(x[:, w:].astype(BF16).astype(F32), I32)
    return hi | lax.shift_right_logical(lo, 16)


def _unpack_halves(p):
    hi = pltpu.bitcast(p & jnp.int32(-65536), F32)
    lo = pltpu.bitcast(lax.shift_left(p, 16), F32)
    return hi, lo


def _split_bf16(x):
    hi = x.astype(BF16)
    lo = (x - hi.astype(F32)).astype(BF16)
    return hi, lo


def _even_in_kernel(h_ref, g_ref, wrow_ref, wt_ref, bd_ref, qg_ref, kg_ref,
                    xg_ref, k_ref, ik_ref, qt_ref, iqt_ref, vt_ref, iwt_ref):
    tm = h_ref.shape[0]
    w = ATT_WIDTH
    hn = _rms(h_ref[...], g_ref[...]).astype(BF16)

    xg_ref[...] = _dot(hn, wrow_ref[:, :2 * LRU_WIDTH])
    kf = _dot(hn, wrow_ref[:, 2 * LRU_WIDTH:2 * LRU_WIDTH + w])
    hi, lo = _split_bf16(kf * kf)
    ss = _dot(hi, bd_ref[...]) + _dot(lo, bd_ref[...])
    k_ref[...] = (kf * lax.rsqrt(ss * (1.0 / ATT_HEAD_DIM) + RMS_EPS) * kg_ref[...]).astype(BF16)
    ik_ref[...] = _dot(hn, wrow_ref[:, 2 * LRU_WIDTH + w:]).astype(BF16)

    feat = _dot_nt(wt_ref[...], hn)
    q3 = feat[0:w].reshape(ATT_HEADS, ATT_HEAD_DIM, tm)
    ssq = jnp.sum(q3 * q3, axis=1, keepdims=True)
    qn = q3 * lax.rsqrt(ssq * (1.0 / ATT_HEAD_DIM) + RMS_EPS) * qg_ref[...][None]
    qt_ref[...] = qn.reshape(w, tm).astype(BF16)
    iqt_ref[...] = feat[w:2 * w].astype(BF16)
    vt_ref[...] = feat[2 * w:3 * w].astype(BF16)
    iwt_ref[...] = feat[3 * w:3 * w + IDX_HEADS] * (IDX_HEADS ** -0.5 * IDX_DIM ** -0.5)


def _even_in(h, g, wrow, wt, bd, qg, kg, tm=512):
    n = h.shape[0]
    row = lambda w: pl.BlockSpec((tm, w), lambda i: (i, 0))
    col = lambda r: pl.BlockSpec((r, tm), lambda i: (0, i))
    return pl.pallas_call(
        _even_in_kernel,
        grid=(n // tm,),
        in_specs=[row(D_MODEL), _full(g.shape), _full(wrow.shape), _full(wt.shape),
                  _full(bd.shape), _full(qg.shape), _full(kg.shape)],
        out_specs=[row(1024), row(512), row(LANES), col(512), col(512), col(512), col(IDX_HEADS)],
        out_shape=[jax.ShapeDtypeStruct((n, 1024), F32), jax.ShapeDtypeStruct((n, 512), BF16),
                   jax.ShapeDtypeStruct((n, LANES), BF16), jax.ShapeDtypeStruct((512, n), BF16),
                   jax.ShapeDtypeStruct((512, n), BF16), jax.ShapeDtypeStruct((512, n), BF16),
                   jax.ShapeDtypeStruct((IDX_HEADS, n), F32)],
        compiler_params=_cparams(("parallel",)),
        name="even_in_proj",
    )(h, g, wrow, wt, bd, qg, kg)


def _rglru_kernel(xg_ref, cw_ref, cb_ref, wa_ref, ba_ref, wx_ref, bx_ref, lam_ref, ya_ref,
                  ext_ref, hst_ref):
    ts = xg_ref.shape[0]
    w = LRU_WIDTH

    @pl.when(pl.program_id(1) == 0)
    def _():
        ext_ref[0:8, :] = jnp.zeros((8, w), F32)
        hst_ref[...] = jnp.zeros_like(hst_ref)

    xa = xg_ref[:, :w]
    ga = xg_ref[:, w:]
    row = lax.broadcasted_iota(I32, (ts, w), 0)
    ext_ref[8:, :] = xa
    xc = xa * cw_ref[3:4, :] + cb_ref[...]
    for d in (1, 2, 3):
        xc = xc + ext_ref[8 - d:8 - d + ts, :] * cw_ref[3 - d:4 - d, :]
    ext_ref[0:8, :] = xa[ts - 8:, :]

    xcb = xc.astype(BF16)
    r = jax.nn.sigmoid(_dot(xcb, wa_ref[...]) + ba_ref[...])
    gi = jax.nn.sigmoid(_dot(xcb, wx_ref[...]) + bx_ref[...])
    nl = -lam_ref[...]
    softplus = jnp.maximum(nl, 0.0) + jnp.log1p(jnp.exp(-jnp.abs(nl)))
    log_a = (-LRU_C) * r * softplus
    a = jnp.exp(log_a)
    u = jnp.sqrt(_neg_expm1(2.0 * log_a)) * (gi * xc)

    groups = ts // 8
    a3 = a.reshape(groups, 8, w)
    u3 = u.reshape(groups, 8, w)
    pos = lax.broadcasted_iota(I32, (groups, 8, w), 1)
    for d in (1, 2, 4):
        keep = pos >= d
        a_sh = jnp.where(keep, pltpu.roll(a3, d, 1), 1.0)
        u_sh = jnp.where(keep, pltpu.roll(u3, d, 1), 0.0)
        u3 = a3 * u_sh + u3
        a3 = a3 * a_sh
    carry = hst_ref[0:1, :]
    hs = []
    for g in range(groups):
        hg = a3[g] * carry + u3[g]
        carry = hg[7:8, :]
        hs.append(hg)
    hst_ref[0:1, :] = carry
    hseq = jnp.concatenate(hs, axis=0)
    ya_ref[...] = (hseq * jax.nn.gelu(ga)).astype(BF16)


def _rglru(xg, cw, cb, wa, ba, wx, bx, lam, batch, seq, ts=256):
    n = xg.shape[0]
    nt = seq // ts
    return pl.pallas_call(
        _rglru_kernel,
        grid=(batch, nt),
        in_specs=[pl.BlockSpec((ts, 1024), lambda b, t: (b * nt + t, 0)),
                  _full(cw.shape), _full(cb.shape), _full(wa.shape), _full(ba.shape),
                  _full(wx.shape), _full(bx.shape), _full(lam.shape)],
        out_specs=pl.BlockSpec((ts, LRU_WIDTH), lambda b, t: (b * nt + t, 0)),
        out_shape=jax.ShapeDtypeStruct((n, LRU_WIDTH), BF16),
        scratch_shapes=[pltpu.VMEM((ts + 8, LRU_WIDTH), F32), pltpu.VMEM((8, LRU_WIDTH), F32)],
        compiler_params=_cparams(("parallel", "arbitrary")),
        name="rglru",
    )(xg, cw, cb, wa, ba, wx, bx, lam)


def _dsa_kernel(far_ref, qt_ref, iqt_ref, iwt_ref, k_ref, ik_ref, vt_ref, tab_ref, o_ref,
                keys_ref, sel_ref, cut_ref, *, seq, top_k):
    tq = Q_TILE
    hd = ATT_HEAD_DIM
    j = pl.program_id(1)
    q0 = j * tq
    nkt = (q0 + tq + K_TILE - 1) // K_TILE
    qcol = q0 + lax.broadcasted_iota(I32, (1, tq), 1)
    qlim = (qcol // CHUNK + 1) * CHUNK
    zero_rows = jnp.zeros((hd, tq), BF16)

    def head_rows(ref, h):
        blk = ref[h * hd:(h + 1) * hd, :]
        return jnp.concatenate([blk, zero_rows] if h % 2 == 0 else [zero_rows, blk], axis=0)

    def ktile(kt):
        return pl.multiple_of(kt * K_TILE, K_TILE)

    def key_pos(off, rows):
        return off + lax.broadcasted_iota(I32, (rows, tq), 0)

    iq_m = [head_rows(iqt_ref, h) for h in range(IDX_HEADS)]

    def score_body(kt, c):
        off = ktile(kt)
        ik = ik_ref[pl.ds(off, K_TILE), :]
        s = jnp.zeros((K_TILE, tq), F32)
        for h in range(IDX_HEADS):
            s = s + jnp.maximum(_dot(ik, iq_m[h]), 0.0) * iwt_ref[h:h + 1, :]
        s = jnp.where(s == 0.0, 0.0, s)
        bits = pltpu.bitcast(s, I32)
        key = bits ^ ((bits >> 31) & 0x7FFFFFFF)
        keys_ref[pl.ds(off, K_TILE), :] = jnp.where(key_pos(off, K_TILE) < qlim, key, INT_MIN)
        return c

    lax.fori_loop(0, nkt, score_body, 0)

    def count(pred):
        def body(kt, acc):
            off = ktile(kt)
            ind = pred(keys_ref[pl.ds(off, K_TILE), :], key_pos(off, K_TILE)).reshape(K_TILE // 8, 8, tq)
            while ind.shape[0] > 1:
                half = ind.shape[0] // 2
                ind = ind[:half] + ind[half:]
            return acc + ind[0]
        acc = lax.fori_loop(0, nkt, body, jnp.zeros((8, tq), F32))
        return jnp.sum(acc, axis=0, keepdims=True)

    kf = float(top_k)

    def search_body(i, ans):
        cand = ans + jnp.left_shift(jnp.int32(1), 31 - i)
        cnt = count(lambda kk, kpos: jnp.where(kk >= cand, 1.0, 0.0))
        return jnp.where(cnt >= kf, cand, ans)

    ans = lax.fori_loop(0, 32, search_body, jnp.full((1, tq), INT_MIN, I32))

    cnt_ge = count(lambda kk, kpos: jnp.where(kk >= ans, 1.0, 0.0))
    excess = jnp.where(ans > INT_MIN, cnt_ge - kf, 0.0)
    cut_ref[...] = jnp.full(cut_ref.shape, seq, I32)

    @pl.when(jnp.max(excess) > 0.0)
    def _():
        need = kf - count(lambda kk, kpos: jnp.where(kk > ans, 1.0, 0.0))

        def idx_body(i, pos):
            cand = pos + jnp.left_shift(jnp.int32(1), (seq.bit_length() - 2) - i)
            c = count(lambda kk, kpos: jnp.where(kk == ans, jnp.where(kpos < cand, 1.0, 0.0), 0.0))
            return jnp.where(c < need, cand, pos)

        pos = lax.fori_loop(0, seq.bit_length() - 1, idx_body, jnp.zeros((1, tq), I32))
        cut_ref[...] = jnp.broadcast_to(pos, cut_ref.shape)

    thr = jnp.maximum(ans, INT_MIN + 1)
    cut = cut_ref[0:1, :]

    near_w = tq + BIAS_BLOCK
    near_start = pl.multiple_of(jnp.maximum(q0 - BIAS_BLOCK, 0), BIAS_BLOCK)

    def selection(kk, kpos):
        tie = jnp.where(kpos <= cut, 0.0, NEG_BIG)
        return jnp.where(kk > thr, 0.0, jnp.where(kk == thr, tie, NEG_BIG))

    def sel_body(kt, c):
        off = ktile(kt)
        kpos = key_pos(off, K_TILE)
        s = selection(keys_ref[pl.ds(off, K_TILE), :], kpos)
        sel_ref[pl.ds(off, K_TILE), :] = jnp.where(kpos < near_start, s, NEG_BIG)
        return c

    n_far = (near_start + K_TILE - 1) // K_TILE
    lax.fori_loop(0, n_far, sel_body, 0)
    sel_near = selection(keys_ref[pl.ds(near_start, near_w), :], key_pos(near_start, near_w))

    def step(qm, kb, vte, add, carry):
        m, acc = carry
        s = _dot(kb, qm) + add
        m_new = jnp.maximum(m, jnp.max(s, axis=0, keepdims=True))
        p = jnp.exp2(s - m_new).astype(BF16)
        acc = jnp.exp2(m - m_new) * acc + _dot(vte, p)
        return m_new, acc

    qms = [head_rows(qt_ref, h) for h in range(ATT_HEADS)]
    pairs = [slice((h // 2) * LANES, (h // 2 + 1) * LANES) for h in range(ATT_HEADS)]

    def values(h, off, width):
        return jnp.concatenate([vt_ref[h * hd:(h + 1) * hd, pl.ds(off, width)],
                                jnp.ones((hd, width), BF16)], axis=0)

    def far_body(kt, carries):
        off = ktile(kt)
        sel = sel_ref[pl.ds(off, K_TILE), :]
        return tuple(step(qms[h], k_ref[pl.ds(off, K_TILE), pairs[h]], values(h, off, K_TILE), sel, carries[h])
                     for h in range(ATT_HEADS))

    init = (jnp.full((1, tq), NEG_BIG, F32), jnp.zeros((2 * hd, tq), F32))
    carries = lax.fori_loop(0, n_far, far_body, (init,) * ATT_HEADS)

    n_blocks = near_w // BIAS_BLOCK
    first = jnp.where(j == 0, 1, 0)
    res = []
    for h in range(ATT_HEADS):
        m, acc = carries[h]
        bias = jnp.concatenate([tab_ref[jnp.minimum(first + b, n_blocks - 1), h] for b in range(n_blocks)],
                               axis=0)
        m, acc = step(qms[h], k_ref[pl.ds(near_start, near_w), pairs[h]], values(h, near_start, near_w),
                      sel_near + bias, (m + far_ref[h], acc))
        res.append(acc[:hd, :] / acc[hd:hd + 1, :])

    for p in range(ATT_HEADS // 2):
        pair_t = jnp.concatenate([res[2 * p], res[2 * p + 1]], axis=0)
        o_ref[:, p * LANES:(p + 1) * LANES] = pair_t.T.astype(BF16)


def _dsa(far, qt, iqt, iwt, k, ik, vt, tab, batch, seq):
    n = k.shape[0]
    nb = seq // Q_TILE
    top_k = min(MAX_TOP_K, seq // 4)
    qspec = lambda r: pl.BlockSpec((r, Q_TILE), lambda b, j: (0, b * nb + j))
    return pl.pallas_call(
        functools.partial(_dsa_kernel, seq=seq, top_k=top_k),
        grid=(batch, nb),
        in_specs=[pl.BlockSpec(memory_space=pltpu.SMEM),
                  qspec(512), qspec(512), qspec(IDX_HEADS),
                  pl.BlockSpec((seq, 512), lambda b, j: (b, 0)),
                  pl.BlockSpec((seq, LANES), lambda b, j: (b, 0)),
                  pl.BlockSpec((512, seq), lambda b, j: (0, b)),
                  _full(tab.shape)],
        out_specs=pl.BlockSpec((Q_TILE, 512), lambda b, j: (b * nb + j, 0)),
        out_shape=jax.ShapeDtypeStruct((n, 512), BF16),
        scratch_shapes=[pltpu.VMEM((seq, Q_TILE), I32), pltpu.VMEM((seq, Q_TILE), F32),
                        pltpu.VMEM((8, Q_TILE), I32)],
        compiler_params=_cparams(("parallel", "arbitrary")),
        name="dsa",
    )(far, qt, iqt, iwt, k, ik, vt, tab)


def _mix_out_kernel(*refs, n_y):
    h_ref = refs[0]
    y_refs = refs[1:1 + n_y]
    w_refs = refs[1 + n_y:1 + 2 * n_y]
    g_ref, wr_hi_ref, wr_lo_ref = refs[1 + 2 * n_y:4 + 2 * n_y]
    h1_ref, xn_ref, route_ref = refs[4 + 2 * n_y:]
    for sub in range(h_ref.shape[0] // MIX_SUB):
        rs = slice(sub * MIX_SUB, (sub + 1) * MIX_SUB)
        _mix_out_rows(rs, h_ref, y_refs, w_refs, g_ref, wr_hi_ref, wr_lo_ref, h1_ref, xn_ref, route_ref)


def _mix_out_rows(rs, h_ref, y_refs, w_refs, g_ref, wr_hi_ref, wr_lo_ref, h1_ref, xn_ref, route_ref):
    tm = rs.stop - rs.start
    mix = _dot(y_refs[0][rs, :], w_refs[0][...])
    for y_ref, w_ref in zip(y_refs[1:], w_refs[1:]):
        mix = mix + _dot(y_ref[rs, :], w_ref[...])
    h1 = h_ref[rs, :] + mix
    h1_ref[rs, :] = h1
    xn = _rms(h1, g_ref[...])
    hi, lo = _split_bf16(xn)
    xn_ref[rs, :] = _pack_halves(xn)
    logits = _dot(hi, wr_hi_ref[...]) + _dot(lo, wr_hi_ref[...]) + _dot(hi, wr_lo_ref[...])

    lane = lax.broadcasted_iota(I32, (tm, LANES), 1).astype(F32)
    big = float(LANES)
    ninf = -jnp.inf
    is_g = (lane >= N_EXPERTS) & (lane < N_EXPERTS + N_GROUPS)
    glog = jnp.where(is_g, logits, ninf)
    gmax = jnp.max(glog, axis=1, keepdims=True)
    gsel = jnp.min(jnp.where(glog == gmax, lane, big), axis=1, keepdims=True) - N_EXPERTS
    gprob = 1.0 / jnp.sum(jnp.exp(glog - gmax), axis=1, keepdims=True)
    lo_l = gsel * EXPERTS_PER_GROUP
    within = jnp.where((lane >= lo_l) & (lane < lo_l + EXPERTS_PER_GROUP), logits, ninf)
    v1 = jnp.max(within, axis=1, keepdims=True)
    i1 = jnp.min(jnp.where(within == v1, lane, big), axis=1, keepdims=True)
    rest = jnp.where(lane == i1, ninf, within)
    v2 = jnp.max(rest, axis=1, keepdims=True)
    i2 = jnp.min(jnp.where(rest == v2, lane, big), axis=1, keepdims=True)
    e2 = jnp.exp(v2 - v1)
    w1 = gprob / (1.0 + e2)
    w2 = gprob * e2 / (1.0 + e2)
    route_ref[rs, :] = jnp.where(lane == 0, i1, jnp.where(lane == 1, i2,
                                 jnp.where(lane == 2, w1, jnp.where(lane == 3, w2, 0.0))))


def _mix_out(h, ys, ws, g, wr_hi, wr_lo, tm=512):
    n = h.shape[0]
    n_y = len(ys)
    row = lambda w: pl.BlockSpec((tm, w), lambda i: (i, 0))
    return pl.pallas_call(
        functools.partial(_mix_out_kernel, n_y=n_y),
        grid=(n // tm,),
        in_specs=[row(D_MODEL)] + [row(y.shape[1]) for y in ys] + [_full(w.shape) for w in ws]
                 + [_full(g.shape), _full(wr_hi.shape), _full(wr_lo.shape)],
        out_specs=[row(D_MODEL), row(D_MODEL // 2), row(LANES)],
        out_shape=[jax.ShapeDtypeStruct((n, D_MODEL), F32), jax.ShapeDtypeStruct((n, D_MODEL // 2), I32),
                   jax.ShapeDtypeStruct((n, LANES), F32)],
        compiler_params=_cparams(("parallel",)),
        name="mix_out_router",
    )(h, *ys, *ws, g, wr_hi, wr_lo)


def _sc_rows(src, idx, scatter):
    r, d = idx.shape[0], src.shape[1]
    n_src = src.shape[0]
    nw = SC_CORES * SC_SUBCORES
    per_w = r // nw
    n_chunks = per_w // SC_CHUNK
    assert r % nw == 0 and per_w % SC_CHUNK == 0 and (not scatter or n_src % per_w == 0)
    idx3 = idx.reshape(nw, n_chunks, SC_CHUNK)
    mesh = plsc.VectorSubcoreMesh(core_axis_name="c", subcore_axis_name="s")

    row_buf = pltpu.VMEM((SC_CHUNK, d), src.dtype)

    @functools.partial(
        pl.kernel, mesh=mesh, out_type=jax.ShapeDtypeStruct((r, d), src.dtype),
        scratch_types=[pltpu.VMEM((n_chunks, SC_CHUNK), I32), row_buf, row_buf] + [pltpu.SemaphoreType.DMA] * 4)
    def permute(src_hbm, idx_hbm, out_hbm, idx_v, rows0, rows1, in0, in1, out0, out1):
        wid = lax.axis_index("s") * SC_CORES + lax.axis_index("c")
        pltpu.sync_copy(idx_hbm.at[wid], idx_v)
        bufs, in_sems, out_sems = (rows0, rows1), (in0, in1), (out0, out1)

        def linear(c, n_rows):
            start = lax.rem(wid * per_w + c * SC_CHUNK, n_rows)
            return pl.ds(pl.multiple_of(start, SC_CHUNK), SC_CHUNK)

        def load(c):
            ref = src_hbm.at[linear(c, n_src)] if scatter else src_hbm.at[idx_v.at[c]]
            return pltpu.async_copy(ref, bufs[c % 2], in_sems[c % 2])

        def store(c):
            ref = out_hbm.at[idx_v.at[c]] if scatter else out_hbm.at[linear(c, r)]
            return pltpu.async_copy(bufs[c % 2], ref, out_sems[c % 2])

        loads, stores = {0: load(0)}, {}
        for c in range(n_chunks):
            if c + 1 < n_chunks:
                if c >= 1:
                    stores[c - 1].wait()
                loads[c + 1] = load(c + 1)
            loads[c].wait()
            stores[c] = store(c)
        for c in range(max(n_chunks - 2, 0), n_chunks):
            stores[c].wait()

    return permute(src, idx3)


def _moe_kernel(tile_ref, exp_ref, flag_ref, off_ref, xs_ref, wg_ref, wu_ref, wd_ref, ys_ref,
                wgu_ref, wdb_ref, acc_ref):
    w = pl.program_id(0)
    tm = xs_ref.shape[0]
    e = exp_ref[w]
    flags = flag_ref[w]

    @pl.when((flags & 4) != 0)
    def _():
        wgu_ref[:, :EXPERT_FF] = wg_ref[0, 0].astype(BF16)
        wgu_ref[:, EXPERT_FF:] = wu_ref[0, 0].astype(BF16)
        wdb_ref[...] = wd_ref[0, 0].astype(BF16)

    @pl.when((flags & 1) != 0)
    def _():
        xa, xb = _unpack_halves(xs_ref[...])
        xa = xa.astype(BF16)
        xb = xb.astype(BF16)
        half = D_MODEL // 2
        gate_up = _dot(xa, wgu_ref[:half, :]) + _dot(xb, wgu_ref[half:, :])
        hid = (jax.nn.silu(gate_up[:, :EXPERT_FF]) * gate_up[:, EXPERT_FF:]).astype(BF16)
        y = _dot(hid, wdb_ref[...])
        rows = tile_ref[w] * tm + lax.broadcasted_iota(I32, (tm, 1), 0)
        mine = (rows >= off_ref[e]) & (rows < off_ref[e + 1])
        y = jnp.where(mine, y, 0.0)

        @pl.when((flags & 2) != 0)
        def _():
            acc_ref[...] = y

        @pl.when((flags & 2) == 0)
        def _():
            acc_ref[...] += y

        ys_ref[...] = _pack_halves(acc_ref[...])


def _moe_plan(off, n_rows, tm):
    nt = n_rows // tm
    n_items = nt + N_EXPERTS - 1
    row_lo = jnp.arange(nt, dtype=I32) * tm
    ends = off[1:N_EXPERTS][None, :]
    e_lo = jnp.sum((ends <= row_lo[:, None]).astype(I32), axis=1)
    e_hi = jnp.sum((ends <= (row_lo + tm - 1)[:, None]).astype(I32), axis=1)
    span = e_hi - e_lo + 1
    start = jnp.cumsum(span) - span
    total = jnp.sum(span)
    w = jnp.arange(n_items, dtype=I32)
    valid = w < total
    tile = jnp.clip(jnp.sum((start[None, :] <= w[:, None]).astype(I32), axis=1) - 1, 0, nt - 1)
    expert = jnp.where(valid, e_lo[tile] + (w - start[tile]), e_hi[nt - 1])
    first = valid & (w == start[tile])
    new_e = jnp.concatenate([jnp.ones((1,), bool), expert[1:] != expert[:-1]])
    flags = valid.astype(I32) + 2 * first.astype(I32) + 4 * new_e.astype(I32)
    return tile, expert, flags


def _moe(xs, tile, expert, flags, off, wg, wu, wd, layer, tm):
    n_rows, half = xs.shape
    n_items = tile.shape[0]
    grid_spec = pltpu.PrefetchScalarGridSpec(
        num_scalar_prefetch=4,
        grid=(n_items,),
        in_specs=[pl.BlockSpec((tm, half), lambda w, t, e, f, o: (t[w], 0)),
                  pl.BlockSpec((1, 1, D_MODEL, EXPERT_FF), lambda w, t, e, f, o: (layer, e[w], 0, 0)),
                  pl.BlockSpec((1, 1, D_MODEL, EXPERT_FF), lambda w, t, e, f, o: (layer, e[w], 0, 0)),
                  pl.BlockSpec((1, 1, EXPERT_FF, D_MODEL), lambda w, t, e, f, o: (layer, e[w], 0, 0))],
        out_specs=pl.BlockSpec((tm, half), lambda w, t, e, f, o: (t[w], 0)),
        scratch_shapes=[pltpu.VMEM((D_MODEL, 2 * EXPERT_FF), BF16),
                        pltpu.VMEM((EXPERT_FF, D_MODEL), BF16), pltpu.VMEM((tm, D_MODEL), F32)])
    return pl.pallas_call(
        _moe_kernel,
        grid_spec=grid_spec,
        out_shape=jax.ShapeDtypeStruct((n_rows, half), I32),
        compiler_params=_cparams(("arbitrary",)),
        name="moe_experts",
    )(tile, expert, flags, off, xs, wg, wu, wd)


def _moe_sparse(xn_packed, route, wg, wu, wd, layer, tm=256):
    n = xn_packed.shape[0]
    eid = jnp.concatenate([route[:, 0], route[:, 1]]).astype(I32)
    sorted_e, sorted_pair = lax.sort_key_val(eid, jnp.arange(2 * n, dtype=I32))
    token = jnp.where(sorted_pair >= n, sorted_pair - n, sorted_pair)
    bounds = jnp.arange(N_EXPERTS + 1, dtype=I32)
    off = jnp.sum((eid[None, :] < bounds[:, None]).astype(I32), axis=1)
    tile, expert, flags = _moe_plan(off, 2 * n, tm)
    xs = _sc_rows(xn_packed, token, scatter=False)
    ys = _moe(xs, tile, expert, flags, off, wg, wu, wd, layer, tm)
    return _sc_rows(ys, sorted_pair, scatter=True)


def _ple_kernel(h1_ref, y0_ref, y1_ref, route_ref, p_ref, wup_ref, wgate_ref, g_ref, o_ref):
    w1 = route_ref[:, 2:3]
    w2 = route_ref[:, 3:4]
    a0, b0 = _unpack_halves(y0_ref[...])
    a1, b1 = _unpack_halves(y1_ref[...])
    moe = jnp.concatenate([w1 * a0 + w2 * a1, w1 * b0 + w2 * b1], axis=1)
    h2 = h1_ref[...] + moe
    e = _rms(_dot(p_ref[...].astype(BF16), wup_ref[...]), g_ref[...])
    gate = jax.nn.sigmoid(_dot(h2.astype(BF16), wgate_ref[...]))
    o_ref[...] = h2 + e * gate


def _ple(h1, yp, route, p_all, layer, wup, wgate, g, tm=512):
    n = h1.shape[0]
    nt = n // tm
    row = lambda w: pl.BlockSpec((tm, w), lambda i: (i, 0))
    return pl.pallas_call(
        _ple_kernel,
        grid=(nt,),
        in_specs=[row(D_MODEL), row(D_MODEL // 2),
                  pl.BlockSpec((tm, D_MODEL // 2), lambda i: (i + nt, 0)),
                  row(LANES), pl.BlockSpec((tm, PLE_DIM), lambda i: (i + layer * nt, 0)),
                  _full(wup.shape), _full(wgate.shape), _full(g.shape)],
        out_specs=row(D_MODEL),
        out_shape=jax.ShapeDtypeStruct((n, D_MODEL), F32),
        compiler_params=_cparams(("parallel",)),
        name="ple",
    )(h1, yp, yp, route, p_all, wup, wgate, g)


def _odd_in_kernel(h_ref, g_ref, wrow_ref, wkt_ref, cos_ref, sin_ref, cost_ref, sint_ref,
                   q_ref, v_ref, gate_ref, kt_ref):
    tm = h_ref.shape[0]
    hn = _rms(h_ref[...], g_ref[...]).astype(BF16)
    cos = cos_ref[...]
    sin = sin_ref[...]
    half = RET_QK_DIM // 2
    q = _dot(hn, wrow_ref[:, :RET_QK_WIDTH])
    for hd in range(RET_HEADS):
        sl = slice(hd * RET_QK_DIM, (hd + 1) * RET_QK_DIM)
        qh = q[:, sl]
        q_ref[:, sl] = (qh * cos + pltpu.roll(qh, half, 1) * sin).astype(BF16)
    v_ref[...] = _dot(hn, wrow_ref[:, RET_QK_WIDTH:RET_QK_WIDTH + RET_V_WIDTH]).astype(BF16)
    gate_ref[...] = _dot(hn, wrow_ref[:, RET_QK_WIDTH + RET_V_WIDTH:]).astype(BF16)

    cost = cost_ref[...] * (RET_QK_DIM ** -0.5)
    sint = sint_ref[...] * (RET_QK_DIM ** -0.5)
    kt = _dot_nt(wkt_ref[...], hn)
    for hd in range(RET_HEADS):
        sl = slice(hd * RET_QK_DIM, (hd + 1) * RET_QK_DIM)
        kh = kt[sl, :]
        swapped = jnp.concatenate([kh[half:], kh[:half]], axis=0)
        kt_ref[sl, :] = (kh * cost + swapped * sint).astype(BF16)


def _odd_in(h, g, wrow, wkt, cos, sin, cost, sint, seq, tm=512):
    n = h.shape[0]
    nt = seq // tm
    row = lambda w: pl.BlockSpec((tm, w), lambda i: (i, 0))
    return pl.pallas_call(
        _odd_in_kernel,
        grid=(n // tm,),
        in_specs=[row(D_MODEL), _full(g.shape), _full(wrow.shape), _full(wkt.shape),
                  pl.BlockSpec((tm, RET_QK_DIM), lambda i: (i % nt, 0)),
                  pl.BlockSpec((tm, RET_QK_DIM), lambda i: (i % nt, 0)),
                  pl.BlockSpec((RET_QK_DIM, tm), lambda i: (0, i % nt)),
                  pl.BlockSpec((RET_QK_DIM, tm), lambda i: (0, i % nt))],
        out_specs=[row(RET_QK_WIDTH), row(RET_V_WIDTH), row(RET_V_WIDTH),
                   pl.BlockSpec((RET_QK_WIDTH, tm), lambda i: (0, i))],
        out_shape=[jax.ShapeDtypeStruct((n, RET_QK_WIDTH), BF16), jax.ShapeDtypeStruct((n, RET_V_WIDTH), BF16),
                   jax.ShapeDtypeStruct((n, RET_V_WIDTH), BF16), jax.ShapeDtypeStruct((RET_QK_WIDTH, n), BF16)],
        compiler_params=_cparams(("parallel",)),
        name="odd_in_proj",
    )(h, g, wrow, wkt, cos, sin, cost, sint)


def _ret_kernel(cdec_ref, q_ref, kt_ref, v_ref, gate_ref, dintra_ref, qdec_ref, kdec_ref, gn_ref,
                y_ref, state_ref):
    @pl.when(pl.program_id(1) == 0)
    def _():
        state_ref[...] = jnp.zeros_like(state_ref)

    for h in range(RET_HEADS):
        ks = slice(h * RET_QK_DIM, (h + 1) * RET_QK_DIM)
        vs = slice(h * RET_V_DIM, (h + 1) * RET_V_DIM)
        qh = q_ref[:, ks]
        kth = kt_ref[ks, :]
        vh = v_ref[:, vs]
        state = state_ref[h]
        inner = (_dot(qh, kth) * dintra_ref[h]).astype(BF16)
        o = _dot(inner, vh) + _dot(qh, state.astype(BF16)) * qdec_ref[h]
        kd = (kth.astype(F32) * kdec_ref[h]).astype(BF16)
        state_ref[h] = state * cdec_ref[h] + _dot(kd, vh)
        mu = jnp.mean(o, axis=-1, keepdims=True)
        oc = o - mu
        var = jnp.mean(oc * oc, axis=-1, keepdims=True)
        on = oc * lax.rsqrt(var + GN_EPS) * gn_ref[:, vs]
        y_ref[:, vs] = (jax.nn.silu(gate_ref[:, vs].astype(F32)) * on).astype(BF16)


def _retention(cdec, q, kt, v, gate, dintra, qdec, kdec, gn, batch, seq):
    n = q.shape[0]
    c = RET_TILE
    nc = seq // c
    row = lambda w: pl.BlockSpec((c, w), lambda b, t: (b * nc + t, 0))
    return pl.pallas_call(
        _ret_kernel,
        grid=(batch, nc),
        in_specs=[pl.BlockSpec(memory_space=pltpu.SMEM),
                  row(RET_QK_WIDTH),
                  pl.BlockSpec((RET_QK_WIDTH, c), lambda b, t: (0, b * nc + t)),
                  row(RET_V_WIDTH), row(RET_V_WIDTH),
                  _full(dintra.shape), _full(qdec.shape), _full(kdec.shape), _full(gn.shape)],
        out_specs=row(RET_V_WIDTH),
        out_shape=jax.ShapeDtypeStruct((n, RET_V_WIDTH), BF16),
        scratch_shapes=[pltpu.VMEM((RET_HEADS, RET_QK_DIM, RET_V_DIM), F32)],
        compiler_params=_cparams(("parallel", "arbitrary")),
        name="retention",
    )(cdec, q, kt, v, gate, dintra, qdec, kdec, gn)


def _t5_bucket(rel):
    half = N_BUCKETS // 2
    max_exact = half // 2
    ret = (rel > 0).astype(I32) * half
    n = jnp.abs(rel)
    nf = jnp.maximum(n, 1).astype(F32)
    large = max_exact + (jnp.log(nf / max_exact) / math.log(MAX_DISTANCE / max_exact)
                         * (half - max_exact)).astype(I32)
    large = jnp.minimum(large, half - 1)
    return ret + jnp.where(n < max_exact, n, large)


def _bias_tables(rel_bias):
    c = jnp.arange(BIAS_BLOCK, dtype=I32)[:, None]
    r = jnp.arange(Q_TILE, dtype=I32)[None, :]
    rels = jnp.stack([c + (b - 1) * BIAS_BLOCK - r for b in range(Q_TILE // BIAS_BLOCK + 1)])
    onehot = jax.nn.one_hot(_t5_bucket(rels), N_BUCKETS, dtype=F32)
    tab = jnp.einsum("abcn,nh->ahbc", onehot, rel_bias, precision=lax.Precision.HIGHEST)
    far = rel_bias[_t5_bucket(jnp.int32(-2 * MAX_DISTANCE))]
    return (tab * LOG2E).astype(F32), (far * LOG2E).astype(F32)


def _transpose_bf16(w):
    wb = w.astype(BF16)
    eye = jnp.eye(w.shape[0], dtype=BF16)
    return lax.dot_general(wb, eye, (((0,), (0,)), ((), ())), preferred_element_type=BF16)


def _block_diag(w):
    nb, bs, _ = w.shape
    eye = jnp.eye(nb, dtype=w.dtype)
    return jnp.einsum("hij,hg->higj", w, eye).reshape(nb * bs, nb * bs)


def _router_weights(w_group, w_expert):
    wr = jnp.concatenate([w_expert, w_group, jnp.zeros((D_MODEL, LANES - N_EXPERTS - N_GROUPS), F32)], axis=1)
    hi = wr.astype(BF16)
    lo = (wr - hi.astype(F32)).astype(BF16)
    return hi, lo


def _rotary_tables(seq):
    half = RET_QK_DIM // 2
    inv = ROPE_BASE ** (-jnp.arange(half, dtype=F32) / half)
    ang = jnp.arange(seq, dtype=F32)[:, None] * inv[None, :]
    cos, sin = jnp.cos(ang), jnp.sin(ang)
    cos_row = jnp.concatenate([cos, cos], axis=1)
    sin_row = jnp.concatenate([-sin, sin], axis=1)
    cos_t = jnp.concatenate([cos.T, cos.T], axis=0)
    sin_t = jnp.concatenate([-sin.T, sin.T], axis=0)
    return cos_row, sin_row, cos_t, sin_t


def _retention_tables():
    c = RET_TILE
    log_g = jnp.log(1.0 - 2.0 ** (-5.0 - jnp.arange(RET_HEADS, dtype=F32)))
    pos = jnp.arange(c, dtype=F32)
    diff = pos[:, None] - pos[None, :]
    causal = diff >= 0
    dintra = jnp.where(causal[None], jnp.exp(jnp.where(causal, diff, 0.0)[None] * log_g[:, None, None]), 0.0)
    qdec = jnp.exp((pos + 1.0)[None, :, None] * log_g[:, None, None])
    kdec = jnp.exp((c - 1.0 - pos)[None, None, :] * log_g[:, None, None])
    cdec = jnp.exp(c * log_g)
    return dintra, qdec, kdec, cdec


def kernel(x, p, rel_bias, mix_norm_g, ffn_norm_g, ple_norm_g, ev_w_in, ev_conv_w, ev_conv_b, ev_lru_wa, ev_lru_ba, ev_lru_wx, ev_lru_bx, ev_lru_lambda, ev_q_norm_g, ev_k_norm_g, ev_w_out, od_w_in, od_gn_g, od_w_out, moe_w_group, moe_w_expert, moe_w_gate, moe_w_up, moe_w_down, ple_w_up, ple_w_gate):
    batch, seq, _ = x.shape
    n = batch * seq
    depth = p.shape[0]
    h = x.reshape(n, D_MODEL)
    row = lambda a: a.reshape(1, -1)

    tab, far = _bias_tables(rel_bias)
    cos_row, sin_row, cos_t, sin_t = _rotary_tables(seq)
    dintra, qdec, kdec, cdec = _retention_tables()
    head_ones = _block_diag(jnp.ones((ATT_HEADS, ATT_HEAD_DIM, ATT_HEAD_DIM), BF16))

    for i in range(depth):
        jdx = i // 2
        if i % 2 == 0:
            w = ev_w_in[jdx]
            o = [0, 512, 1024, 1536, 2048, 2560, 3072, 3136, 3144]
            xa_w, ga_w, q_w, k_w, v_w, iq_w, ik_w, iw_w = [w[:, o[t]:o[t + 1]] for t in range(8)]
            wrow = jnp.concatenate([xa_w, ga_w, k_w, ik_w, ik_w], axis=1).astype(BF16)
            wt = _transpose_bf16(jnp.concatenate(
                [q_w, iq_w, v_w, iw_w, jnp.zeros((D_MODEL, 16 - IDX_HEADS), F32)], axis=1))
            qg = ev_q_norm_g[jdx].reshape(-1, 1) * (ATT_HEAD_DIM ** -0.5 * LOG2E)
            kg = jnp.tile(ev_k_norm_g[jdx], ATT_HEADS).reshape(1, -1)
            xg, k, ik, qt, iqt, vt, iwt = _even_in(h, row(mix_norm_g[i]), wrow, wt, head_ones, qg, kg)
            ya = _rglru(xg, ev_conv_w[jdx], row(ev_conv_b[jdx]),
                        _block_diag(ev_lru_wa[jdx]).astype(BF16), row(ev_lru_ba[jdx]),
                        _block_diag(ev_lru_wx[jdx]).astype(BF16), row(ev_lru_bx[jdx]),
                        row(ev_lru_lambda[jdx]), batch, seq)
            yb = _dsa(far, qt, iqt, iwt, k, ik, vt, tab, batch, seq)
            wo = ev_w_out[jdx].astype(BF16)
            ys, ws = [ya, yb], [wo[:LRU_WIDTH], wo[LRU_WIDTH:]]
        else:
            w = od_w_in[jdx]
            wrow = jnp.concatenate([w[:, :RET_QK_WIDTH], w[:, 2 * RET_QK_WIDTH:]], axis=1).astype(BF16)
            wkt = _transpose_bf16(w[:, RET_QK_WIDTH:2 * RET_QK_WIDTH])
            q, v, gate, kt = _odd_in(h, row(mix_norm_g[i]), wrow, wkt, cos_row, sin_row, cos_t, sin_t, seq)
            yc = _retention(cdec, q, kt, v, gate, dintra, qdec, kdec, row(od_gn_g[jdx]), batch, seq)
            ys, ws = [yc], [od_w_out[jdx].astype(BF16)]
        wr_hi, wr_lo = _router_weights(moe_w_group[i], moe_w_expert[i])
        h1, xn, route = _mix_out(h, ys, ws, row(ffn_norm_g[i]), wr_hi, wr_lo)
        yp = _moe_sparse(xn, route, moe_w_gate, moe_w_up, moe_w_down, i)
        h = _ple(h1, yp, route, p.reshape(depth * n, PLE_DIM), i, ple_w_up[i].astype(BF16),
                 ple_w_gate[i].astype(BF16), row(ple_norm_g[i]))
    return h.reshape(batch, seq, D_MODEL)
```

```python
import functools
import math

import jax
import jax.numpy as jnp
from jax import lax
from jax.experimental import pallas as pl
from jax.experimental.pallas import tpu as pltpu
from jax.experimental.pallas import tpu_sc as plsc

F32 = jnp.float32
BF16 = jnp.bfloat16
I32 = jnp.int32

D_MODEL = 1024
CHUNK = 64
PLE_DIM = 256
RMS_EPS = 1e-6

LRU_WIDTH = 512
LRU_BLOCKS = 8
LRU_C = 8.0

ATT_HEADS = 8
ATT_HEAD_DIM = 64
ATT_WIDTH = 512
IDX_HEADS = 8
IDX_DIM = 64
MAX_TOP_K = 256
N_BUCKETS = 32
MAX_DISTANCE = 128

RET_HEADS = 8
RET_QK_DIM = 128
RET_V_DIM = 256
RET_QK_WIDTH = 1024
RET_V_WIDTH = 2048
ROPE_BASE = 10000.0
GN_EPS = 1e-5

N_GROUPS = 4
EXPERTS_PER_GROUP = 8
N_EXPERTS = 32
EXPERT_FF = 512

LANES = 128
INT_MIN = -(2 ** 31)
NEG_BIG = -1e30
LOG2E = 1.4426950408889634
VMEM_LIMIT = 56 * 1024 * 1024

SC_CORES = 2
SC_SUBCORES = 16
SC_CHUNK = 64

MIX_SUB = 256
Q_TILE = 128
BIAS_BLOCK = 128
K_TILE = 512
RET_TILE = 256


def _cparams(sem):
    return pltpu.CompilerParams(dimension_semantics=sem, vmem_limit_bytes=VMEM_LIMIT)


def _full(shape):
    nd = len(shape)
    return pl.BlockSpec(shape, lambda *_: (0,) * nd)


def _rms(xf, g):
    return xf * lax.rsqrt(jnp.mean(xf * xf, axis=-1, keepdims=True) + RMS_EPS) * g


def _dot(a, b):
    return jnp.dot(a, b, preferred_element_type=F32)


def _dot_nt(a, b):
    return lax.dot_general(a, b, (((1,), (1,)), ((), ())), preferred_element_type=F32)


def _neg_expm1(x):
    series = -x * (1.0 + x * (0.5 + x * (1.0 / 6.0 + x * (1.0 / 24.0))))
    return jnp.where(x > -0.03, series, 1.0 - jnp.exp(x))


def _pack_halves(x):
    w = x.shape[1] // 2
    hi = pltpu.bitcast(x[:, :w].astype(BF16).astype(F32), I32)
    lo = pltpu.bitcast(x[:, w:].astype(BF16).astype(F32), I32)
    return hi | lax.shift_right_logical(lo, 16)


def _unpack_halves(p):
    hi = pltpu.bitcast(p & jnp.int32(-65536), F32)
    lo = pltpu.bitcast(lax.shift_left(p, 16), F32)
    return hi, lo


def _split_bf16(x):
    hi = x.astype(BF16)
    lo = (x - hi.astype(F32)).astype(BF16)
    return hi, lo


def _even_in_kernel(h_ref, g_ref, wrow_ref, wt_ref, bd_ref, qg_ref, kg_ref,
                    xg_ref, k_ref, ik_ref, qt_ref, iqt_ref, vt_ref, iwt_ref):
    tm = h_ref.shape[0]
    w = ATT_WIDTH
    hn = _rms(h_ref[...], g_ref[...]).astype(BF16)

    xg_ref[...] = _dot(hn, wrow_ref[:, :2 * LRU_WIDTH])
    kf = _dot(hn, wrow_ref[:, 2 * LRU_WIDTH:2 * LRU_WIDTH + w])
    hi, lo = _split_bf16(kf * kf)
    ss = _dot(hi, bd_ref[...]) + _dot(lo, bd_ref[...])
    k_ref[...] = (kf * lax.rsqrt(ss * (1.0 / ATT_HEAD_DIM) + RMS_EPS) * kg_ref[...]).astype(BF16)
    ik_ref[...] = _dot(hn, wrow_ref[:, 2 * LRU_WIDTH + w:]).astype(BF16)

    feat = _dot_nt(wt_ref[...], hn)
    q3 = feat[0:w].reshape(ATT_HEADS, ATT_HEAD_DIM, tm)
    ssq = jnp.sum(q3 * q3, axis=1, keepdims=True)
    qn = q3 * lax.rsqrt(ssq * (1.0 / ATT_HEAD_DIM) + RMS_EPS) * qg_ref[...][None]
    qt_ref[...] = qn.reshape(w, tm).astype(BF16)
    iqt_ref[...] = feat[w:2 * w].astype(BF16)
    vt_ref[...] = feat[2 * w:3 * w].astype(BF16)
    iwt_ref[...] = feat[3 * w:3 * w + IDX_HEADS] * (IDX_HEADS ** -0.5 * IDX_DIM ** -0.5)


def _even_in(h, g, wrow, wt, bd, qg, kg, tm=512):
    n = h.shape[0]
    row = lambda w: pl.BlockSpec((tm, w), lambda i: (i, 0))
    col = lambda r: pl.BlockSpec((r, tm), lambda i: (0, i))
    return pl.pallas_call(
        _even_in_kernel,
        grid=(n // tm,),
        in_specs=[row(D_MODEL), _full(g.shape), _full(wrow.shape), _full(wt.shape),
                  _full(bd.shape), _full(qg.shape), _full(kg.shape)],
        out_specs=[row(2 * LRU_WIDTH), row(ATT_WIDTH), row(LANES), col(ATT_WIDTH), col(ATT_WIDTH),
                   col(ATT_WIDTH), col(IDX_HEADS)],
        out_shape=[jax.ShapeDtypeStruct((n, 2 * LRU_WIDTH), F32), jax.ShapeDtypeStruct((n, ATT_WIDTH), BF16),
                   jax.ShapeDtypeStruct((n, LANES), BF16), jax.ShapeDtypeStruct((ATT_WIDTH, n), BF16),
                   jax.ShapeDtypeStruct((ATT_WIDTH, n), BF16), jax.ShapeDtypeStruct((ATT_WIDTH, n), BF16),
                   jax.ShapeDtypeStruct((IDX_HEADS, n), F32)],
        compiler_params=_cparams(("parallel",)),
        name="even_in_proj",
    )(h, g, wrow, wt, bd, qg, kg)


def _rglru_kernel(xg_ref, cw_ref, cb_ref, wa_ref, ba_ref, wx_ref, bx_ref, lam_ref, ya_ref,
                  ext_ref, hst_ref):
    ts = xg_ref.shape[0]
    w = LRU_WIDTH

    @pl.when(pl.program_id(1) == 0)
    def _():
        ext_ref[0:8, :] = jnp.zeros((8, w), F32)
        hst_ref[...] = jnp.zeros_like(hst_ref)

    xa = xg_ref[:, :w]
    ga = xg_ref[:, w:]
    ext_ref[8:, :] = xa
    xc = xa * cw_ref[3:4, :] + cb_ref[...]
    for d in (1, 2, 3):
        xc = xc + ext_ref[8 - d:8 - d + ts, :] * cw_ref[3 - d:4 - d, :]
    ext_ref[0:8, :] = xa[ts - 8:, :]

    xcb = xc.astype(BF16)
    r = jax.nn.sigmoid(_dot(xcb, wa_ref[...]) + ba_ref[...])
    gi = jax.nn.sigmoid(_dot(xcb, wx_ref[...]) + bx_ref[...])
    nl = -lam_ref[...]
    softplus = jnp.maximum(nl, 0.0) + jnp.log1p(jnp.exp(-jnp.abs(nl)))
    log_a = (-LRU_C) * r * softplus
    a = jnp.exp(log_a)
    u = jnp.sqrt(_neg_expm1(2.0 * log_a)) * (gi * xc)

    groups = ts // 8
    a3 = a.reshape(groups, 8, w)
    u3 = u.reshape(groups, 8, w)
    pos = lax.broadcasted_iota(I32, (groups, 8, w), 1)
    for d in (1, 2, 4):
        keep = pos >= d
        a_sh = jnp.where(keep, pltpu.roll(a3, d, 1), 1.0)
        u_sh = jnp.where(keep, pltpu.roll(u3, d, 1), 0.0)
        u3 = a3 * u_sh + u3
        a3 = a3 * a_sh
    carry = hst_ref[0:1, :]
    hs = []
    for g in range(groups):
        hg = a3[g] * carry + u3[g]
        carry = hg[7:8, :]
        hs.append(hg)
    hst_ref[0:1, :] = carry
    hseq = jnp.concatenate(hs, axis=0)
    ya_ref[...] = (hseq * jax.nn.gelu(ga)).astype(BF16)


def _rglru(xg, cw, cb, wa, ba, wx, bx, lam, batch, seq, ts=512):
    n = xg.shape[0]
    nt = seq // ts
    return pl.pallas_call(
        _rglru_kernel,
        grid=(batch, nt),
        in_specs=[pl.BlockSpec((ts, 2 * LRU_WIDTH), lambda b, t: (b * nt + t, 0)),
                  _full(cw.shape), _full(cb.shape), _full(wa.shape), _full(ba.shape),
                  _full(wx.shape), _full(bx.shape), _full(lam.shape)],
        out_specs=pl.BlockSpec((ts, LRU_WIDTH), lambda b, t: (b * nt + t, 0)),
        out_shape=jax.ShapeDtypeStruct((n, LRU_WIDTH), BF16),
        scratch_shapes=[pltpu.VMEM((ts + 8, LRU_WIDTH), F32), pltpu.VMEM((8, LRU_WIDTH), F32)],
        compiler_params=_cparams(("parallel", "arbitrary")),
        name="rglru",
    )(xg, cw, cb, wa, ba, wx, bx, lam)


def _dsa_kernel(far_ref, qt_ref, iqt_ref, iwt_ref, k_ref, ik_ref, vt_ref, tab_ref, o_ref,
                keys_ref, sel_ref, cut_ref, *, seq, top_k):
    tq = Q_TILE
    hd = ATT_HEAD_DIM
    j = pl.program_id(1)
    q0 = j * tq
    nkt = (q0 + tq + K_TILE - 1) // K_TILE
    qcol = q0 + lax.broadcasted_iota(I32, (1, tq), 1)
    qlim = (qcol // CHUNK + 1) * CHUNK
    zero_rows = jnp.zeros((hd, tq), BF16)

    def head_rows(ref, h):
        blk = ref[h * hd:(h + 1) * hd, :]
        return jnp.concatenate([blk, zero_rows] if h % 2 == 0 else [zero_rows, blk], axis=0)

    def ktile(kt):
        return pl.multiple_of(kt * K_TILE, K_TILE)

    def key_pos(off, rows):
        return off + lax.broadcasted_iota(I32, (rows, tq), 0)

    iq_m = [head_rows(iqt_ref, h) for h in range(IDX_HEADS)]

    def score_body(kt, c):
        off = ktile(kt)
        ik = ik_ref[pl.ds(off, K_TILE), :]
        s = jnp.zeros((K_TILE, tq), F32)
        for h in range(IDX_HEADS):
            s = s + jnp.maximum(_dot(ik, iq_m[h]), 0.0) * iwt_ref[h:h + 1, :]
        s = jnp.where(s == 0.0, 0.0, s)
        bits = pltpu.bitcast(s, I32)
        key = bits ^ ((bits >> 31) & 0x7FFFFFFF)
        keys_ref[pl.ds(off, K_TILE), :] = jnp.where(key_pos(off, K_TILE) < qlim, key, INT_MIN)
        return c

    lax.fori_loop(0, nkt, score_body, 0)

    def count(pred):
        def body(kt, acc):
            off = ktile(kt)
            ind = pred(keys_ref[pl.ds(off, K_TILE), :], key_pos(off, K_TILE)).reshape(K_TILE // 8, 8, tq)
            while ind.shape[0] > 1:
                half = ind.shape[0] // 2
                ind = ind[:half] + ind[half:]
            return acc + ind[0]
        acc = lax.fori_loop(0, nkt, body, jnp.zeros((8, tq), F32))
        return jnp.sum(acc, axis=0, keepdims=True)

    kf = float(top_k)

    def search_body(i, ans):
        cand = ans + jnp.left_shift(jnp.int32(1), 31 - i)
        cnt = count(lambda kk, kpos: jnp.where(kk >= cand, 1.0, 0.0))
        return jnp.where(cnt >= kf, cand, ans)

    ans = lax.fori_loop(0, 32, search_body, jnp.full((1, tq), INT_MIN, I32))

    cnt_ge = count(lambda kk, kpos: jnp.where(kk >= ans, 1.0, 0.0))
    excess = jnp.where(ans > INT_MIN, cnt_ge - kf, 0.0)
    cut_ref[...] = jnp.full(cut_ref.shape, seq, I32)

    @pl.when(jnp.max(excess) > 0.0)
    def _():
        need = kf - count(lambda kk, kpos: jnp.where(kk > ans, 1.0, 0.0))

        def idx_body(i, pos):
            cand = pos + jnp.left_shift(jnp.int32(1), (seq.bit_length() - 2) - i)
            c = count(lambda kk, kpos: jnp.where(kk == ans, jnp.where(kpos < cand, 1.0, 0.0), 0.0))
            return jnp.where(c < need, cand, pos)

        pos = lax.fori_loop(0, seq.bit_length() - 1, idx_body, jnp.zeros((1, tq), I32))
        cut_ref[...] = jnp.broadcast_to(pos, cut_ref.shape)

    thr = jnp.maximum(ans, INT_MIN + 1)
    cut = cut_ref[0:1, :]

    near_w = tq + BIAS_BLOCK
    near_start = pl.multiple_of(jnp.maximum(q0 - BIAS_BLOCK, 0), BIAS_BLOCK)

    def selection(kk, kpos):
        tie = jnp.where(kpos <= cut, 0.0, NEG_BIG)
        return jnp.where(kk > thr, 0.0, jnp.where(kk == thr, tie, NEG_BIG))

    def sel_body(kt, c):
        off = ktile(kt)
        kpos = key_pos(off, K_TILE)
        s = selection(keys_ref[pl.ds(off, K_TILE), :], kpos)
        sel_ref[pl.ds(off, K_TILE), :] = jnp.where(kpos < near_start, s, NEG_BIG)
        return c

    n_far = (near_start + K_TILE - 1) // K_TILE
    lax.fori_loop(0, n_far, sel_body, 0)
    sel_near = selection(keys_ref[pl.ds(near_start, near_w), :], key_pos(near_start, near_w))

    def step(qm, kb, vte, add, carry):
        m, acc = carry
        s = _dot(kb, qm) + add
        m_new = jnp.maximum(m, jnp.max(s, axis=0, keepdims=True))
        p = jnp.exp2(s - m_new).astype(BF16)
        acc = jnp.exp2(m - m_new) * acc + _dot(vte, p)
        return m_new, acc

    qms = [head_rows(qt_ref, h) for h in range(ATT_HEADS)]
    pairs = [slice((h // 2) * LANES, (h // 2 + 1) * LANES) for h in range(ATT_HEADS)]

    def values(h, off, width):
        return jnp.concatenate([vt_ref[h * hd:(h + 1) * hd, pl.ds(off, width)],
                                jnp.ones((hd, width), BF16)], axis=0)

    def far_body(kt, carries):
        off = ktile(kt)
        sel = sel_ref[pl.ds(off, K_TILE), :]
        return tuple(step(qms[h], k_ref[pl.ds(off, K_TILE), pairs[h]], values(h, off, K_TILE), sel, carries[h])
                     for h in range(ATT_HEADS))

    init = (jnp.full((1, tq), NEG_BIG, F32), jnp.zeros((2 * hd, tq), F32))
    carries = lax.fori_loop(0, n_far, far_body, (init,) * ATT_HEADS)

    n_blocks = near_w // BIAS_BLOCK
    first = jnp.where(j == 0, 1, 0)
    res = []
    for h in range(ATT_HEADS):
        m, acc = carries[h]
        bias = jnp.concatenate([tab_ref[jnp.minimum(first + b, n_blocks - 1), h] for b in range(n_blocks)],
                               axis=0)
        m, acc = step(qms[h], k_ref[pl.ds(near_start, near_w), pairs[h]], values(h, near_start, near_w),
                      sel_near + bias, (m + far_ref[h], acc))
        res.append(acc[:hd, :] / acc[hd:hd + 1, :])

    for p in range(ATT_HEADS // 2):
        pair_t = jnp.concatenate([res[2 * p], res[2 * p + 1]], axis=0)
        o_ref[:, p * LANES:(p + 1) * LANES] = pair_t.T.astype(BF16)


def _dsa(far, qt, iqt, iwt, k, ik, vt, tab, batch, seq):
    n = k.shape[0]
    nb = seq // Q_TILE
    top_k = min(MAX_TOP_K, seq // 4)
    qspec = lambda r: pl.BlockSpec((r, Q_TILE), lambda b, j: (0, b * nb + j))
    return pl.pallas_call(
        functools.partial(_dsa_kernel, seq=seq, top_k=top_k),
        grid=(batch, nb),
        in_specs=[pl.BlockSpec(memory_space=pltpu.SMEM),
                  qspec(ATT_WIDTH), qspec(ATT_WIDTH), qspec(IDX_HEADS),
                  pl.BlockSpec((seq, ATT_WIDTH), lambda b, j: (b, 0)),
                  pl.BlockSpec((seq, LANES), lambda b, j: (b, 0)),
                  pl.BlockSpec((ATT_WIDTH, seq), lambda b, j: (0, b)),
                  _full(tab.shape)],
        out_specs=pl.BlockSpec((Q_TILE, ATT_WIDTH), lambda b, j: (b * nb + j, 0)),
        out_shape=jax.ShapeDtypeStruct((n, ATT_WIDTH), BF16),
        scratch_shapes=[pltpu.VMEM((seq, Q_TILE), I32), pltpu.VMEM((seq, Q_TILE), F32),
                        pltpu.VMEM((8, Q_TILE), I32)],
        compiler_params=_cparams(("parallel", "arbitrary")),
        name="dsa",
    )(far, qt, iqt, iwt, k, ik, vt, tab)


def _mix_out_kernel(*refs, n_y):
    h_ref = refs[0]
    y_refs = refs[1:1 + n_y]
    w_refs = refs[1 + n_y:1 + 2 * n_y]
    g_ref, wr_hi_ref, wr_lo_ref = refs[1 + 2 * n_y:4 + 2 * n_y]
    h1_ref, xn_ref, route_ref = refs[4 + 2 * n_y:]
    for sub in range(h_ref.shape[0] // MIX_SUB):
        rs = slice(sub * MIX_SUB, (sub + 1) * MIX_SUB)
        _mix_out_rows(rs, h_ref, y_refs, w_refs, g_ref, wr_hi_ref, wr_lo_ref, h1_ref, xn_ref, route_ref)


def _mix_out_rows(rs, h_ref, y_refs, w_refs, g_ref, wr_hi_ref, wr_lo_ref, h1_ref, xn_ref, route_ref):
    tm = rs.stop - rs.start
    mix = _dot(y_refs[0][rs, :], w_refs[0][...])
    for y_ref, w_ref in zip(y_refs[1:], w_refs[1:]):
        mix = mix + _dot(y_ref[rs, :], w_ref[...])
    h1 = h_ref[rs, :] + mix
    h1_ref[rs, :] = h1
    xn = _rms(h1, g_ref[...])
    hi, lo = _split_bf16(xn)
    xn_ref[rs, :] = _pack_halves(xn)
    logits = _dot(hi, wr_hi_ref[...]) + _dot(lo, wr_hi_ref[...]) + _dot(hi, wr_lo_ref[...])

    lane = lax.broadcasted_iota(I32, (tm, LANES), 1).astype(F32)
    big = float(LANES)
    ninf = -jnp.inf
    is_g = (lane >= N_EXPERTS) & (lane < N_EXPERTS + N_GROUPS)
    glog = jnp.where(is_g, logits, ninf)
    gmax = jnp.max(glog, axis=1, keepdims=True)
    gsel = jnp.min(jnp.where(glog == gmax, lane, big), axis=1, keepdims=True) - N_EXPERTS
    gprob = 1.0 / jnp.sum(jnp.exp(glog - gmax), axis=1, keepdims=True)
    lo_l = gsel * EXPERTS_PER_GROUP
    within = jnp.where((lane >= lo_l) & (lane < lo_l + EXPERTS_PER_GROUP), logits, ninf)
    v1 = jnp.max(within, axis=1, keepdims=True)
    i1 = jnp.min(jnp.where(within == v1, lane, big), axis=1, keepdims=True)
    rest = jnp.where(lane == i1, ninf, within)
    v2 = jnp.max(rest, axis=1, keepdims=True)
    i2 = jnp.min(jnp.where(rest == v2, lane, big), axis=1, keepdims=True)
    e2 = jnp.exp(v2 - v1)
    w1 = gprob / (1.0 + e2)
    w2 = gprob * e2 / (1.0 + e2)
    route_ref[rs, :] = jnp.where(lane == 0, i1, jnp.where(lane == 1, i2,
                                 jnp.where(lane == 2, w1, jnp.where(lane == 3, w2, 0.0))))


def _mix_out(h, ys, ws, g, wr_hi, wr_lo, tm=1024):
    n = h.shape[0]
    n_y = len(ys)
    row = lambda w: pl.BlockSpec((tm, w), lambda i: (i, 0))
    return pl.pallas_call(
        functools.partial(_mix_out_kernel, n_y=n_y),
        grid=(n // tm,),
        in_specs=[row(D_MODEL)] + [row(y.shape[1]) for y in ys] + [_full(w.shape) for w in ws]
                 + [_full(g.shape), _full(wr_hi.shape), _full(wr_lo.shape)],
        out_specs=[row(D_MODEL), row(D_MODEL // 2), row(LANES)],
        out_shape=[jax.ShapeDtypeStruct((n, D_MODEL), F32), jax.ShapeDtypeStruct((n, D_MODEL // 2), I32),
                   jax.ShapeDtypeStruct((n, LANES), F32)],
        compiler_params=_cparams(("parallel",)),
        name="mix_out_router",
    )(h, *ys, *ws, g, wr_hi, wr_lo)


def _sc_rows(src, idx, scatter):
    r, d = idx.shape[0], src.shape[1]
    n_src = src.shape[0]
    nw = SC_CORES * SC_SUBCORES
    per_w = r // nw
    n_chunks = per_w // SC_CHUNK
    assert r % nw == 0 and per_w % SC_CHUNK == 0 and (not scatter or n_src % per_w == 0)
    idx3 = idx.reshape(nw, n_chunks, SC_CHUNK)
    mesh = plsc.VectorSubcoreMesh(core_axis_name="c", subcore_axis_name="s")

    row_buf = pltpu.VMEM((SC_CHUNK, d), src.dtype)

    @functools.partial(
        pl.kernel, mesh=mesh, out_type=jax.ShapeDtypeStruct((r, d), src.dtype),
        scratch_types=[pltpu.VMEM((n_chunks, SC_CHUNK), I32), row_buf, row_buf] + [pltpu.SemaphoreType.DMA] * 4)
    def permute(src_hbm, idx_hbm, out_hbm, idx_v, rows0, rows1, in0, in1, out0, out1):
        wid = lax.axis_index("s") * SC_CORES + lax.axis_index("c")
        pltpu.sync_copy(idx_hbm.at[wid], idx_v)
        bufs, in_sems, out_sems = (rows0, rows1), (in0, in1), (out0, out1)

        def linear(c, n_rows):
            start = lax.rem(wid * per_w + c * SC_CHUNK, n_rows)
            return pl.ds(pl.multiple_of(start, SC_CHUNK), SC_CHUNK)

        def load(c):
            ref = src_hbm.at[linear(c, n_src)] if scatter else src_hbm.at[idx_v.at[c]]
            return pltpu.async_copy(ref, bufs[c % 2], in_sems[c % 2])

        def store(c):
            ref = out_hbm.at[idx_v.at[c]] if scatter else out_hbm.at[linear(c, r)]
            return pltpu.async_copy(bufs[c % 2], ref, out_sems[c % 2])

        loads, stores = {0: load(0)}, {}
        for c in range(n_chunks):
            if c + 1 < n_chunks:
                if c >= 1:
                    stores[c - 1].wait()
                loads[c + 1] = load(c + 1)
            loads[c].wait()
            stores[c] = store(c)
        for c in range(max(n_chunks - 2, 0), n_chunks):
            stores[c].wait()

    return permute(src, idx3)


def _moe_kernel(tile_ref, exp_ref, flag_ref, off_ref, xs_ref, wg_ref, wu_ref, wd_ref, ys_ref,
                wgu_ref, wdb_ref, acc_ref):
    w = pl.program_id(0)
    tm = xs_ref.shape[0]
    e = exp_ref[w]
    flags = flag_ref[w]

    @pl.when((flags & 4) != 0)
    def _():
        wgu_ref[:, :EXPERT_FF] = wg_ref[0, 0].astype(BF16)
        wgu_ref[:, EXPERT_FF:] = wu_ref[0, 0].astype(BF16)
        wdb_ref[...] = wd_ref[0, 0].astype(BF16)

    @pl.when((flags & 1) != 0)
    def _():
        xa, xb = _unpack_halves(xs_ref[...])
        xa = xa.astype(BF16)
        xb = xb.astype(BF16)
        half = D_MODEL // 2
        gate_up = _dot(xa, wgu_ref[:half, :]) + _dot(xb, wgu_ref[half:, :])
        hid = (jax.nn.silu(gate_up[:, :EXPERT_FF]) * gate_up[:, EXPERT_FF:]).astype(BF16)
        y = _dot(hid, wdb_ref[...])
        rows = tile_ref[w] * tm + lax.broadcasted_iota(I32, (tm, 1), 0)
        mine = (rows >= off_ref[e]) & (rows < off_ref[e + 1])
        y = jnp.where(mine, y, 0.0)

        @pl.when((flags & 2) != 0)
        def _():
            acc_ref[...] = y

        @pl.when((flags & 2) == 0)
        def _():
            acc_ref[...] += y

        ys_ref[...] = _pack_halves(acc_ref[...])


def _moe_plan(off, n_rows, tm):
    nt = n_rows // tm
    n_items = nt + N_EXPERTS - 1
    row_lo = jnp.arange(nt, dtype=I32) * tm
    ends = off[1:N_EXPERTS][None, :]
    e_lo = jnp.sum((ends <= row_lo[:, None]).astype(I32), axis=1)
    e_hi = jnp.sum((ends <= (row_lo + tm - 1)[:, None]).astype(I32), axis=1)
    span = e_hi - e_lo + 1
    start = jnp.cumsum(span) - span
    total = jnp.sum(span)
    w = jnp.arange(n_items, dtype=I32)
    valid = w < total
    tile = jnp.clip(jnp.sum((start[None, :] <= w[:, None]).astype(I32), axis=1) - 1, 0, nt - 1)
    expert = jnp.where(valid, e_lo[tile] + (w - start[tile]), e_hi[nt - 1])
    first = valid & (w == start[tile])
    new_e = jnp.concatenate([jnp.ones((1,), bool), expert[1:] != expert[:-1]])
    flags = valid.astype(I32) + 2 * first.astype(I32) + 4 * new_e.astype(I32)
    return tile, expert, flags


def _moe(xs, tile, expert, flags, off, wg, wu, wd, layer, tm):
    n_rows, half = xs.shape
    n_items = tile.shape[0]
    grid_spec = pltpu.PrefetchScalarGridSpec(
        num_scalar_prefetch=4,
        grid=(n_items,),
        in_specs=[pl.BlockSpec((tm, half), lambda w, t, e, f, o: (t[w], 0)),
                  pl.BlockSpec((1, 1, D_MODEL, EXPERT_FF), lambda w, t, e, f, o: (layer, e[w], 0, 0)),
                  pl.BlockSpec((1, 1, D_MODEL, EXPERT_FF), lambda w, t, e, f, o: (layer, e[w], 0, 0)),
                  pl.BlockSpec((1, 1, EXPERT_FF, D_MODEL), lambda w, t, e, f, o: (layer, e[w], 0, 0))],
        out_specs=pl.BlockSpec((tm, half), lambda w, t, e, f, o: (t[w], 0)),
        scratch_shapes=[pltpu.VMEM((D_MODEL, 2 * EXPERT_FF), BF16),
                        pltpu.VMEM((EXPERT_FF, D_MODEL), BF16), pltpu.VMEM((tm, D_MODEL), F32)])
    return pl.pallas_call(
        _moe_kernel,
        grid_spec=grid_spec,
        out_shape=jax.ShapeDtypeStruct((n_rows, half), I32),
        compiler_params=_cparams(("arbitrary",)),
        name="moe_experts",
    )(tile, expert, flags, off, xs, wg, wu, wd)


def _moe_sparse(xn_packed, route, wg, wu, wd, layer, tm=256):
    n = xn_packed.shape[0]
    eid = jnp.concatenate([route[:, 0], route[:, 1]]).astype(I32)
    _, sorted_pair = lax.sort_key_val(eid, jnp.arange(2 * n, dtype=I32))
    token = jnp.where(sorted_pair >= n, sorted_pair - n, sorted_pair)
    bounds = jnp.arange(N_EXPERTS + 1, dtype=I32)
    off = jnp.sum((eid[None, :] < bounds[:, None]).astype(I32), axis=1)
    tile, expert, flags = _moe_plan(off, 2 * n, tm)
    xs = _sc_rows(xn_packed, token, scatter=False)
    ys = _moe(xs, tile, expert, flags, off, wg, wu, wd, layer, tm)
    return _sc_rows(ys, sorted_pair, scatter=True)


def _ple_kernel(h1_ref, y0_ref, y1_ref, route_ref, p_ref, wup_ref, wgate_ref, g_ref, o_ref):
    w1 = route_ref[:, 2:3]
    w2 = route_ref[:, 3:4]
    a0, b0 = _unpack_halves(y0_ref[...])
    a1, b1 = _unpack_halves(y1_ref[...])
    moe = jnp.concatenate([w1 * a0 + w2 * a1, w1 * b0 + w2 * b1], axis=1)
    h2 = h1_ref[...] + moe
    e = _rms(_dot(p_ref[...].astype(BF16), wup_ref[...]), g_ref[...])
    gate = jax.nn.sigmoid(_dot(h2.astype(BF16), wgate_ref[...]))
    o_ref[...] = h2 + e * gate


def _ple(h1, yp, route, p_all, layer, wup, wgate, g, tm=1024):
    n = h1.shape[0]
    nt = n // tm
    row = lambda w: pl.BlockSpec((tm, w), lambda i: (i, 0))
    return pl.pallas_call(
        _ple_kernel,
        grid=(nt,),
        in_specs=[row(D_MODEL), row(D_MODEL // 2),
                  pl.BlockSpec((tm, D_MODEL // 2), lambda i: (i + nt, 0)),
                  row(LANES), pl.BlockSpec((tm, PLE_DIM), lambda i: (i + layer * nt, 0)),
                  _full(wup.shape), _full(wgate.shape), _full(g.shape)],
        out_specs=row(D_MODEL),
        out_shape=jax.ShapeDtypeStruct((n, D_MODEL), F32),
        compiler_params=_cparams(("parallel",)),
        name="ple",
    )(h1, yp, yp, route, p_all, wup, wgate, g)


def _odd_in_kernel(h_ref, g_ref, wrow_ref, wkt_ref, cos_ref, sin_ref, cost_ref, sint_ref,
                   q_ref, v_ref, gate_ref, kt_ref):
    hn = _rms(h_ref[...], g_ref[...]).astype(BF16)
    cos = cos_ref[...]
    sin = sin_ref[...]
    half = RET_QK_DIM // 2
    q = _dot(hn, wrow_ref[:, :RET_QK_WIDTH])
    for hd in range(RET_HEADS):
        sl = slice(hd * RET_QK_DIM, (hd + 1) * RET_QK_DIM)
        qh = q[:, sl]
        q_ref[:, sl] = (qh * cos + pltpu.roll(qh, half, 1) * sin).astype(BF16)
    v_ref[...] = _dot(hn, wrow_ref[:, RET_QK_WIDTH:RET_QK_WIDTH + RET_V_WIDTH]).astype(BF16)
    gate_ref[...] = _dot(hn, wrow_ref[:, RET_QK_WIDTH + RET_V_WIDTH:]).astype(BF16)

    cost = cost_ref[...] * (RET_QK_DIM ** -0.5)
    sint = sint_ref[...] * (RET_QK_DIM ** -0.5)
    kt = _dot_nt(wkt_ref[...], hn)
    for hd in range(RET_HEADS):
        sl = slice(hd * RET_QK_DIM, (hd + 1) * RET_QK_DIM)
        kh = kt[sl, :]
        swapped = jnp.concatenate([kh[half:], kh[:half]], axis=0)
        kt_ref[sl, :] = (kh * cost + swapped * sint).astype(BF16)


def _odd_in(h, g, wrow, wkt, cos, sin, cost, sint, seq, tm=512):
    n = h.shape[0]
    nt = seq // tm
    row = lambda w: pl.BlockSpec((tm, w), lambda i: (i, 0))
    return pl.pallas_call(
        _odd_in_kernel,
        grid=(n // tm,),
        in_specs=[row(D_MODEL), _full(g.shape), _full(wrow.shape), _full(wkt.shape),
                  pl.BlockSpec((tm, RET_QK_DIM), lambda i: (i % nt, 0)),
                  pl.BlockSpec((tm, RET_QK_DIM), lambda i: (i % nt, 0)),
                  pl.BlockSpec((RET_QK_DIM, tm), lambda i: (0, i % nt)),
                  pl.BlockSpec((RET_QK_DIM, tm), lambda i: (0, i % nt))],
        out_specs=[row(RET_QK_WIDTH), row(RET_V_WIDTH), row(RET_V_WIDTH),
                   pl.BlockSpec((RET_QK_WIDTH, tm), lambda i: (0, i))],
        out_shape=[jax.ShapeDtypeStruct((n, RET_QK_WIDTH), BF16), jax.ShapeDtypeStruct((n, RET_V_WIDTH), BF16),
                   jax.ShapeDtypeStruct((n, RET_V_WIDTH), BF16), jax.ShapeDtypeStruct((RET_QK_WIDTH, n), BF16)],
        compiler_params=_cparams(("parallel",)),
        name="odd_in_proj",
    )(h, g, wrow, wkt, cos, sin, cost, sint)


def _ret_kernel(cdec_ref, q_ref, kt_ref, v_ref, gate_ref, dintra_ref, qdec_ref, kdec_ref, gn_ref,
                y_ref, state_ref):
    @pl.when(pl.program_id(1) == 0)
    def _():
        state_ref[...] = jnp.zeros_like(state_ref)

    for h in range(RET_HEADS):
        ks = slice(h * RET_QK_DIM, (h + 1) * RET_QK_DIM)
        vs = slice(h * RET_V_DIM, (h + 1) * RET_V_DIM)
        qh = q_ref[:, ks]
        kth = kt_ref[ks, :]
        vh = v_ref[:, vs]
        state = state_ref[h]
        inner = (_dot(qh, kth) * dintra_ref[h]).astype(BF16)
        o = _dot(inner, vh) + _dot(qh, state.astype(BF16)) * qdec_ref[h]
        kd = (kth.astype(F32) * kdec_ref[h]).astype(BF16)
        state_ref[h] = state * cdec_ref[h] + _dot(kd, vh)
        mu = jnp.mean(o, axis=-1, keepdims=True)
        oc = o - mu
        var = jnp.mean(oc * oc, axis=-1, keepdims=True)
        on = oc * lax.rsqrt(var + GN_EPS) * gn_ref[:, vs]
        y_ref[:, vs] = (jax.nn.silu(gate_ref[:, vs].astype(F32)) * on).astype(BF16)


def _retention(cdec, q, kt, v, gate, dintra, qdec, kdec, gn, batch, seq):
    n = q.shape[0]
    c = RET_TILE
    nc = seq // c
    row = lambda w: pl.BlockSpec((c, w), lambda b, t: (b * nc + t, 0))
    return pl.pallas_call(
        _ret_kernel,
        grid=(batch, nc),
        in_specs=[pl.BlockSpec(memory_space=pltpu.SMEM),
                  row(RET_QK_WIDTH),
                  pl.BlockSpec((RET_QK_WIDTH, c), lambda b, t: (0, b * nc + t)),
                  row(RET_V_WIDTH), row(RET_V_WIDTH),
                  _full(dintra.shape), _full(qdec.shape), _full(kdec.shape), _full(gn.shape)],
        out_specs=row(RET_V_WIDTH),
        out_shape=jax.ShapeDtypeStruct((n, RET_V_WIDTH), BF16),
        scratch_shapes=[pltpu.VMEM((RET_HEADS, RET_QK_DIM, RET_V_DIM), F32)],
        compiler_params=_cparams(("parallel", "arbitrary")),
        name="retention",
    )(cdec, q, kt, v, gate, dintra, qdec, kdec, gn)


def _t5_bucket(rel):
    half = N_BUCKETS // 2
    max_exact = half // 2
    ret = (rel > 0).astype(I32) * half
    n = jnp.abs(rel)
    nf = jnp.maximum(n, 1).astype(F32)
    large = max_exact + (jnp.log(nf / max_exact) / math.log(MAX_DISTANCE / max_exact)
                         * (half - max_exact)).astype(I32)
    large = jnp.minimum(large, half - 1)
    return ret + jnp.where(n < max_exact, n, large)


def _bias_tables(rel_bias):
    c = jnp.arange(BIAS_BLOCK, dtype=I32)[:, None]
    r = jnp.arange(Q_TILE, dtype=I32)[None, :]
    rels = jnp.stack([c + (b - 1) * BIAS_BLOCK - r for b in range(Q_TILE // BIAS_BLOCK + 1)])
    onehot = jax.nn.one_hot(_t5_bucket(rels), N_BUCKETS, dtype=F32)
    tab = jnp.einsum("abcn,nh->ahbc", onehot, rel_bias, precision=lax.Precision.HIGHEST)
    far = rel_bias[_t5_bucket(jnp.int32(-2 * MAX_DISTANCE))]
    return (tab * LOG2E).astype(F32), (far * LOG2E).astype(F32)


def _transpose_bf16(w):
    wb = w.astype(BF16)
    eye = jnp.eye(w.shape[0], dtype=BF16)
    return lax.dot_general(wb, eye, (((0,), (0,)), ((), ())), preferred_element_type=BF16)


def _block_diag(w):
    nb, bs, _ = w.shape
    eye = jnp.eye(nb, dtype=w.dtype)
    return jnp.einsum("hij,hg->higj", w, eye).reshape(nb * bs, nb * bs)


def _router_weights(w_group, w_expert):
    wr = jnp.concatenate([w_expert, w_group, jnp.zeros((D_MODEL, LANES - N_EXPERTS - N_GROUPS), F32)], axis=1)
    hi = wr.astype(BF16)
    lo = (wr - hi.astype(F32)).astype(BF16)
    return hi, lo


def _rotary_tables(seq):
    half = RET_QK_DIM // 2
    inv = ROPE_BASE ** (-jnp.arange(half, dtype=F32) / half)
    ang = jnp.arange(seq, dtype=F32)[:, None] * inv[None, :]
    cos, sin = jnp.cos(ang), jnp.sin(ang)
    cos_row = jnp.concatenate([cos, cos], axis=1)
    sin_row = jnp.concatenate([-sin, sin], axis=1)
    cos_t = jnp.concatenate([cos.T, cos.T], axis=0)
    sin_t = jnp.concatenate([-sin.T, sin.T], axis=0)
    return cos_row, sin_row, cos_t, sin_t


def _retention_tables():
    c = RET_TILE
    log_g = jnp.log(1.0 - 2.0 ** (-5.0 - jnp.arange(RET_HEADS, dtype=F32)))
    pos = jnp.arange(c, dtype=F32)
    diff = pos[:, None] - pos[None, :]
    causal = diff >= 0
    dintra = jnp.where(causal[None], jnp.exp(jnp.where(causal, diff, 0.0)[None] * log_g[:, None, None]), 0.0)
    qdec = jnp.exp((pos + 1.0)[None, :, None] * log_g[:, None, None])
    kdec = jnp.exp((c - 1.0 - pos)[None, None, :] * log_g[:, None, None])
    cdec = jnp.exp(c * log_g)
    return dintra, qdec, kdec, cdec


def kernel(x, p, rel_bias, mix_norm_g, ffn_norm_g, ple_norm_g, ev_w_in, ev_conv_w, ev_conv_b, ev_lru_wa, ev_lru_ba, ev_lru_wx, ev_lru_bx, ev_lru_lambda, ev_q_norm_g, ev_k_norm_g, ev_w_out, od_w_in, od_gn_g, od_w_out, moe_w_group, moe_w_expert, moe_w_gate, moe_w_up, moe_w_down, ple_w_up, ple_w_gate):
    batch, seq, _ = x.shape
    n = batch * seq
    depth = p.shape[0]
    h = x.reshape(n, D_MODEL)
    row = lambda a: a.reshape(1, -1)

    tab, far = _bias_tables(rel_bias)
    cos_row, sin_row, cos_t, sin_t = _rotary_tables(seq)
    dintra, qdec, kdec, cdec = _retention_tables()
    head_ones = _block_diag(jnp.ones((ATT_HEADS, ATT_HEAD_DIM, ATT_HEAD_DIM), BF16))

    for i in range(depth):
        jdx = i // 2
        if i % 2 == 0:
            w = ev_w_in[jdx]
            widths = (LRU_WIDTH, LRU_WIDTH, ATT_WIDTH, ATT_WIDTH, ATT_WIDTH, IDX_HEADS * IDX_DIM, IDX_DIM, IDX_HEADS)
            o = [sum(widths[:t]) for t in range(len(widths) + 1)]
            xa_w, ga_w, q_w, k_w, v_w, iq_w, ik_w, iw_w = [w[:, o[t]:o[t + 1]] for t in range(8)]
            wrow = jnp.concatenate([xa_w, ga_w, k_w, ik_w, ik_w], axis=1).astype(BF16)
            wt = _transpose_bf16(jnp.concatenate(
                [q_w, iq_w, v_w, iw_w, jnp.zeros((D_MODEL, 16 - IDX_HEADS), F32)], axis=1))
            qg = ev_q_norm_g[jdx].reshape(-1, 1) * (ATT_HEAD_DIM ** -0.5 * LOG2E)
            kg = jnp.tile(ev_k_norm_g[jdx], ATT_HEADS).reshape(1, -1)
            xg, k, ik, qt, iqt, vt, iwt = _even_in(h, row(mix_norm_g[i]), wrow, wt, head_ones, qg, kg)
            ya = _rglru(xg, ev_conv_w[jdx], row(ev_conv_b[jdx]),
                        _block_diag(ev_lru_wa[jdx]).astype(BF16), row(ev_lru_ba[jdx]),
                        _block_diag(ev_lru_wx[jdx]).astype(BF16), row(ev_lru_bx[jdx]),
                        row(ev_lru_lambda[jdx]), batch, seq)
            yb = _dsa(far, qt, iqt, iwt, k, ik, vt, tab, batch, seq)
            wo = ev_w_out[jdx].astype(BF16)
            ys, ws = [ya, yb], [wo[:LRU_WIDTH], wo[LRU_WIDTH:]]
        else:
            w = od_w_in[jdx]
            wrow = jnp.concatenate([w[:, :RET_QK_WIDTH], w[:, 2 * RET_QK_WIDTH:]], axis=1).astype(BF16)
            wkt = _transpose_bf16(w[:, RET_QK_WIDTH:2 * RET_QK_WIDTH])
            q, v, gate, kt = _odd_in(h, row(mix_norm_g[i]), wrow, wkt, cos_row, sin_row, cos_t, sin_t, seq)
            yc = _retention(cdec, q, kt, v, gate, dintra, qdec, kdec, row(od_gn_g[jdx]), batch, seq)
            ys, ws = [yc], [od_w_out[jdx].astype(BF16)]
        wr_hi, wr_lo = _router_weights(moe_w_group[i], moe_w_expert[i])
        h1, xn, route = _mix_out(h, ys, ws, row(ffn_norm_g[i]), wr_hi, wr_lo)
        yp = _moe_sparse(xn, route, moe_w_gate, moe_w_up, moe_w_down, i)
        h = _ple(h1, yp, route, p.reshape(depth * n, PLE_DIM), i, ple_w_up[i].astype(BF16),
                 ple_w_gate[i].astype(BF16), row(ple_norm_g[i]))
    return h.reshape(batch, seq, D_MODEL)
```

```python
import functools
import math

import jax
import jax.numpy as jnp
from jax import lax
from jax.experimental import pallas as pl
from jax.experimental.pallas import tpu as pltpu
from jax.experimental.pallas import tpu_sc as plsc

F32 = jnp.float32
BF16 = jnp.bfloat16
I32 = jnp.int32

D_MODEL = 1024
CHUNK = 64
PLE_DIM = 256
RMS_EPS = 1e-6

LRU_WIDTH = 512
LRU_BLOCKS = 8
LRU_C = 8.0

ATT_HEADS = 8
ATT_HEAD_DIM = 64
ATT_WIDTH = 512
IDX_HEADS = 8
IDX_DIM = 64
MAX_TOP_K = 256
N_BUCKETS = 32
MAX_DISTANCE = 128

RET_HEADS = 8
RET_QK_DIM = 128
RET_V_DIM = 256
RET_QK_WIDTH = 1024
RET_V_WIDTH = 2048
ROPE_BASE = 10000.0
GN_EPS = 1e-5

N_GROUPS = 4
EXPERTS_PER_GROUP = 8
N_EXPERTS = 32
EXPERT_FF = 512

LANES = 128
INT_MIN = -(2 ** 31)
NEG_BIG = -1e30
LOG2E = 1.4426950408889634
VMEM_LIMIT = 56 * 1024 * 1024

SC_CORES = 2
SC_SUBCORES = 16
SC_CHUNK = 64

MIX_SUB = 256
Q_TILE = 128
BIAS_BLOCK = 128
K_TILE = 512
RET_TILE = 256


def _cparams(sem):
    return pltpu.CompilerParams(dimension_semantics=sem, vmem_limit_bytes=VMEM_LIMIT)


def _full(shape):
    nd = len(shape)
    return pl.BlockSpec(shape, lambda *_: (0,) * nd)


def _rms(xf, g):
    return xf * lax.rsqrt(jnp.mean(xf * xf, axis=-1, keepdims=True) + RMS_EPS) * g


def _dot(a, b):
    return jnp.dot(a, b, preferred_element_type=F32)


def _dot_nt(a, b):
    return lax.dot_general(a, b, (((1,), (1,)), ((), ())), preferred_element_type=F32)


def _neg_expm1(x):
    series = -x * (1.0 + x * (0.5 + x * (1.0 / 6.0 + x * (1.0 / 24.0))))
    return jnp.where(x > -0.03, series, 1.0 - jnp.exp(x))


def _pack_halves(x):
    w = x.shape[1] // 2
    hi = pltpu.bitcast(x[:, :w].astype(BF16).astype(F32), I32)
    lo = pltpu.bitcast(x[:, w:].astype(BF16).astype(F32), I32)
    return hi | lax.shift_right_logical(lo, 16)


def _unpack_halves(p):
    hi = pltpu.bitcast(p & jnp.int32(-65536), F32)
    lo = pltpu.bitcast(lax.shift_left(p, 16), F32)
    return hi, lo


def _split_bf16(x):
    hi = x.astype(BF16)
    lo = (x - hi.astype(F32)).astype(BF16)
    return hi, lo


def _even_in_kernel(h_ref, g_ref, wrow_ref, wt_ref, bd_ref, qg_ref, kg_ref,
                    xg_ref, k_ref, ik_ref, qt_ref, iqt_ref, vt_ref, iwt_ref):
    tm = h_ref.shape[0]
    w = ATT_WIDTH
    hn = _rms(h_ref[...], g_ref[...]).astype(BF16)

    xg_ref[...] = _dot(hn, wrow_ref[:, :2 * LRU_WIDTH])
    kf = _dot(hn, wrow_ref[:, 2 * LRU_WIDTH:2 * LRU_WIDTH + w])
    hi, lo = _split_bf16(kf * kf)
    ss = _dot(hi, bd_ref[...]) + _dot(lo, bd_ref[...])
    k_ref[...] = (kf * lax.rsqrt(ss * (1.0 / ATT_HEAD_DIM) + RMS_EPS) * kg_ref[...]).astype(BF16)
    ik_ref[...] = _dot(hn, wrow_ref[:, 2 * LRU_WIDTH + w:]).astype(BF16)

    feat = _dot_nt(wt_ref[...], hn)
    q3 = feat[0:w].reshape(ATT_HEADS, ATT_HEAD_DIM, tm)
    ssq = jnp.sum(q3 * q3, axis=1, keepdims=True)
    qn = q3 * lax.rsqrt(ssq * (1.0 / ATT_HEAD_DIM) + RMS_EPS) * qg_ref[...][None]
    qt_ref[...] = qn.reshape(w, tm).astype(BF16)
    iqt_ref[...] = feat[w:2 * w].astype(BF16)
    vt_ref[...] = feat[2 * w:3 * w].astype(BF16)
    iwt_ref[...] = feat[3 * w:3 * w + IDX_HEADS] * (IDX_HEADS ** -0.5 * IDX_DIM ** -0.5)


def _even_in(h, g, wrow, wt, bd, qg, kg, tm=512):
    n = h.shape[0]
    row = lambda w: pl.BlockSpec((tm, w), lambda i: (i, 0))
    col = lambda r: pl.BlockSpec((r, tm), lambda i: (0, i))
    return pl.pallas_call(
        _even_in_kernel,
        grid=(n // tm,),
        in_specs=[row(D_MODEL), _full(g.shape), _full(wrow.shape), _full(wt.shape),
                  _full(bd.shape), _full(qg.shape), _full(kg.shape)],
        out_specs=[row(2 * LRU_WIDTH), row(ATT_WIDTH), row(LANES), col(ATT_WIDTH), col(ATT_WIDTH),
                   col(ATT_WIDTH), col(IDX_HEADS)],
        out_shape=[jax.ShapeDtypeStruct((n, 2 * LRU_WIDTH), F32), jax.ShapeDtypeStruct((n, ATT_WIDTH), BF16),
                   jax.ShapeDtypeStruct((n, LANES), BF16), jax.ShapeDtypeStruct((ATT_WIDTH, n), BF16),
                   jax.ShapeDtypeStruct((ATT_WIDTH, n), BF16), jax.ShapeDtypeStruct((ATT_WIDTH, n), BF16),
                   jax.ShapeDtypeStruct((IDX_HEADS, n), F32)],
        compiler_params=_cparams(("parallel",)),
        name="even_in_proj",
    )(h, g, wrow, wt, bd, qg, kg)


def _rglru_kernel(xg_ref, cw_ref, cb_ref, wa_ref, ba_ref, wx_ref, bx_ref, lam_ref, ya_ref,
                  ext_ref, hst_ref):
    ts = xg_ref.shape[0]
    w = LRU_WIDTH

    @pl.when(pl.program_id(1) == 0)
    def _():
        ext_ref[0:8, :] = jnp.zeros((8, w), F32)
        hst_ref[...] = jnp.zeros_like(hst_ref)

    xa = xg_ref[:, :w]
    ga = xg_ref[:, w:]
    ext_ref[8:, :] = xa
    xc = xa * cw_ref[3:4, :] + cb_ref[...]
    for d in (1, 2, 3):
        xc = xc + ext_ref[8 - d:8 - d + ts, :] * cw_ref[3 - d:4 - d, :]
    ext_ref[0:8, :] = xa[ts - 8:, :]

    xcb = xc.astype(BF16)
    r = jax.nn.sigmoid(_dot(xcb, wa_ref[...]) + ba_ref[...])
    gi = jax.nn.sigmoid(_dot(xcb, wx_ref[...]) + bx_ref[...])
    nl = -lam_ref[...]
    softplus = jnp.maximum(nl, 0.0) + jnp.log1p(jnp.exp(-jnp.abs(nl)))
    log_a = (-LRU_C) * r * softplus
    a = jnp.exp(log_a)
    u = jnp.sqrt(_neg_expm1(2.0 * log_a)) * (gi * xc)

    groups = ts // 8
    a3 = a.reshape(groups, 8, w)
    u3 = u.reshape(groups, 8, w)
    pos = lax.broadcasted_iota(I32, (groups, 8, w), 1)
    for d in (1, 2, 4):
        keep = pos >= d
        a_sh = jnp.where(keep, pltpu.roll(a3, d, 1), 1.0)
        u_sh = jnp.where(keep, pltpu.roll(u3, d, 1), 0.0)
        u3 = a3 * u_sh + u3
        a3 = a3 * a_sh
    carry = hst_ref[0:1, :]
    hs = []
    for g in range(groups):
        hg = a3[g] * carry + u3[g]
        carry = hg[7:8, :]
        hs.append(hg)
    hst_ref[0:1, :] = carry
    hseq = jnp.concatenate(hs, axis=0)
    ya_ref[...] = (hseq * jax.nn.gelu(ga)).astype(BF16)


def _rglru(xg, cw, cb, wa, ba, wx, bx, lam, batch, seq, ts=512):
    n = xg.shape[0]
    nt = seq // ts
    return pl.pallas_call(
        _rglru_kernel,
        grid=(batch, nt),
        in_specs=[pl.BlockSpec((ts, 2 * LRU_WIDTH), lambda b, t: (b * nt + t, 0)),
                  _full(cw.shape), _full(cb.shape), _full(wa.shape), _full(ba.shape),
                  _full(wx.shape), _full(bx.shape), _full(lam.shape)],
        out_specs=pl.BlockSpec((ts, LRU_WIDTH), lambda b, t: (b * nt + t, 0)),
        out_shape=jax.ShapeDtypeStruct((n, LRU_WIDTH), BF16),
        scratch_shapes=[pltpu.VMEM((ts + 8, LRU_WIDTH), F32), pltpu.VMEM((8, LRU_WIDTH), F32)],
        compiler_params=_cparams(("parallel", "arbitrary")),
        name="rglru",
    )(xg, cw, cb, wa, ba, wx, bx, lam)


def _dsa_kernel(far_ref, qt_ref, iqt_ref, iwt_ref, k_ref, ik_ref, vt_ref, tab_ref, o_ref,
                keys_ref, sel_ref, cut_ref, *, seq, top_k):
    tq = Q_TILE
    hd = ATT_HEAD_DIM
    j = pl.program_id(1)
    q0 = j * tq
    nkt = (q0 + tq + K_TILE - 1) // K_TILE
    qcol = q0 + lax.broadcasted_iota(I32, (1, tq), 1)
    qlim = (qcol // CHUNK + 1) * CHUNK
    zero_rows = jnp.zeros((hd, tq), BF16)

    def head_rows(ref, h):
        blk = ref[h * hd:(h + 1) * hd, :]
        return jnp.concatenate([blk, zero_rows] if h % 2 == 0 else [zero_rows, blk], axis=0)

    def ktile(kt):
        return pl.multiple_of(kt * K_TILE, K_TILE)

    def key_pos(off, rows):
        return off + lax.broadcasted_iota(I32, (rows, tq), 0)

    iq_m = [head_rows(iqt_ref, h) for h in range(IDX_HEADS)]

    def score_body(kt, c):
        off = ktile(kt)
        ik = ik_ref[pl.ds(off, K_TILE), :]
        s = jnp.zeros((K_TILE, tq), F32)
        for h in range(IDX_HEADS):
            s = s + jnp.maximum(_dot(ik, iq_m[h]), 0.0) * iwt_ref[h:h + 1, :]
        s = jnp.where(s == 0.0, 0.0, s)
        bits = pltpu.bitcast(s, I32)
        key = bits ^ ((bits >> 31) & 0x7FFFFFFF)
        keys_ref[pl.ds(off, K_TILE), :] = jnp.where(key_pos(off, K_TILE) < qlim, key, INT_MIN)
        return c

    lax.fori_loop(0, nkt, score_body, 0)

    def count(pred):
        def body(kt, acc):
            off = ktile(kt)
            ind = pred(keys_ref[pl.ds(off, K_TILE), :], key_pos(off, K_TILE)).reshape(K_TILE // 8, 8, tq)
            while ind.shape[0] > 1:
                half = ind.shape[0] // 2
                ind = ind[:half] + ind[half:]
            return acc + ind[0]
        acc = lax.fori_loop(0, nkt, body, jnp.zeros((8, tq), F32))
        return jnp.sum(acc, axis=0, keepdims=True)

    kf = float(top_k)

    def search_body(i, ans):
        cand = ans + jnp.left_shift(jnp.int32(1), 31 - i)
        cnt = count(lambda kk, kpos: jnp.where(kk >= cand, 1.0, 0.0))
        return jnp.where(cnt >= kf, cand, ans)

    ans = lax.fori_loop(0, 32, search_body, jnp.full((1, tq), INT_MIN, I32))

    cnt_ge = count(lambda kk, kpos: jnp.where(kk >= ans, 1.0, 0.0))
    excess = jnp.where(ans > INT_MIN, cnt_ge - kf, 0.0)
    cut_ref[...] = jnp.full(cut_ref.shape, seq, I32)

    @pl.when(jnp.max(excess) > 0.0)
    def _():
        need = kf - count(lambda kk, kpos: jnp.where(kk > ans, 1.0, 0.0))

        def idx_body(i, pos):
            cand = pos + jnp.left_shift(jnp.int32(1), (seq.bit_length() - 2) - i)
            c = count(lambda kk, kpos: jnp.where(kk == ans, jnp.where(kpos < cand, 1.0, 0.0), 0.0))
            return jnp.where(c < need, cand, pos)

        pos = lax.fori_loop(0, seq.bit_length() - 1, idx_body, jnp.zeros((1, tq), I32))
        cut_ref[...] = jnp.broadcast_to(pos, cut_ref.shape)

    thr = jnp.maximum(ans, INT_MIN + 1)
    cut = cut_ref[0:1, :]

    near_w = tq + BIAS_BLOCK
    near_start = pl.multiple_of(jnp.maximum(q0 - BIAS_BLOCK, 0), BIAS_BLOCK)

    def selection(kk, kpos):
        tie = jnp.where(kpos <= cut, 0.0, NEG_BIG)
        return jnp.where(kk > thr, 0.0, jnp.where(kk == thr, tie, NEG_BIG))

    def sel_body(kt, c):
        off = ktile(kt)
        kpos = key_pos(off, K_TILE)
        s = selection(keys_ref[pl.ds(off, K_TILE), :], kpos)
        sel_ref[pl.ds(off, K_TILE), :] = jnp.where(kpos < near_start, s, NEG_BIG)
        return c

    n_far = (near_start + K_TILE - 1) // K_TILE
    lax.fori_loop(0, n_far, sel_body, 0)
    sel_near = selection(keys_ref[pl.ds(near_start, near_w), :], key_pos(near_start, near_w))

    def step(qm, kb, vte, add, carry):
        m, acc = carry
        s = _dot(kb, qm) + add
        m_new = jnp.maximum(m, jnp.max(s, axis=0, keepdims=True))
        p = jnp.exp2(s - m_new).astype(BF16)
        acc = jnp.exp2(m - m_new) * acc + _dot(vte, p)
        return m_new, acc

    qms = [head_rows(qt_ref, h) for h in range(ATT_HEADS)]
    pairs = [slice((h // 2) * LANES, (h // 2 + 1) * LANES) for h in range(ATT_HEADS)]

    def values(h, off, width):
        return jnp.concatenate([vt_ref[h * hd:(h + 1) * hd, pl.ds(off, width)],
                                jnp.ones((hd, width), BF16)], axis=0)

    def far_body(kt, carries):
        off = ktile(kt)
        sel = sel_ref[pl.ds(off, K_TILE), :]
        return tuple(step(qms[h], k_ref[pl.ds(off, K_TILE), pairs[h]], values(h, off, K_TILE), sel, carries[h])
                     for h in range(ATT_HEADS))

    init = (jnp.full((1, tq), NEG_BIG, F32), jnp.zeros((2 * hd, tq), F32))
    carries = lax.fori_loop(0, n_far, far_body, (init,) * ATT_HEADS)

    n_blocks = near_w // BIAS_BLOCK
    first = jnp.where(j == 0, 1, 0)
    res = []
    for h in range(ATT_HEADS):
        m, acc = carries[h]
        bias = jnp.concatenate([tab_ref[jnp.minimum(first + b, n_blocks - 1), h] for b in range(n_blocks)],
                               axis=0)
        m, acc = step(qms[h], k_ref[pl.ds(near_start, near_w), pairs[h]], values(h, near_start, near_w),
                      sel_near + bias, (m + far_ref[h], acc))
        res.append(acc[:hd, :] / acc[hd:hd + 1, :])

    for p in range(ATT_HEADS // 2):
        pair_t = jnp.concatenate([res[2 * p], res[2 * p + 1]], axis=0)
        o_ref[:, p * LANES:(p + 1) * LANES] = pair_t.T.astype(BF16)


def _dsa(far, qt, iqt, iwt, k, ik, vt, tab, batch, seq):
    n = k.shape[0]
    nb = seq // Q_TILE
    top_k = min(MAX_TOP_K, seq // 4)
    qspec = lambda r: pl.BlockSpec((r, Q_TILE), lambda b, j: (0, b * nb + j))
    return pl.pallas_call(
        functools.partial(_dsa_kernel, seq=seq, top_k=top_k),
        grid=(batch, nb),
        in_specs=[pl.BlockSpec(memory_space=pltpu.SMEM),
                  qspec(ATT_WIDTH), qspec(ATT_WIDTH), qspec(IDX_HEADS),
                  pl.BlockSpec((seq, ATT_WIDTH), lambda b, j: (b, 0)),
                  pl.BlockSpec((seq, LANES), lambda b, j: (b, 0)),
                  pl.BlockSpec((ATT_WIDTH, seq), lambda b, j: (0, b)),
                  _full(tab.shape)],
        out_specs=pl.BlockSpec((Q_TILE, ATT_WIDTH), lambda b, j: (b * nb + j, 0)),
        out_shape=jax.ShapeDtypeStruct((n, ATT_WIDTH), BF16),
        scratch_shapes=[pltpu.VMEM((seq, Q_TILE), I32), pltpu.VMEM((seq, Q_TILE), F32),
                        pltpu.VMEM((8, Q_TILE), I32)],
        compiler_params=_cparams(("parallel", "arbitrary")),
        name="dsa",
    )(far, qt, iqt, iwt, k, ik, vt, tab)


def _mix_out_kernel(*refs, n_y):
    h_ref = refs[0]
    y_refs = refs[1:1 + n_y]
    w_refs = refs[1 + n_y:1 + 2 * n_y]
    g_ref, wr_hi_ref, wr_lo_ref = refs[1 + 2 * n_y:4 + 2 * n_y]
    h1_ref, xn_ref, route_ref = refs[4 + 2 * n_y:]
    for sub in range(h_ref.shape[0] // MIX_SUB):
        rs = slice(sub * MIX_SUB, (sub + 1) * MIX_SUB)
        _mix_out_rows(rs, h_ref, y_refs, w_refs, g_ref, wr_hi_ref, wr_lo_ref, h1_ref, xn_ref, route_ref)


def _mix_out_rows(rs, h_ref, y_refs, w_refs, g_ref, wr_hi_ref, wr_lo_ref, h1_ref, xn_ref, route_ref):
    tm = rs.stop - rs.start
    mix = _dot(y_refs[0][rs, :], w_refs[0][...])
    for y_ref, w_ref in zip(y_refs[1:], w_refs[1:]):
        mix = mix + _dot(y_ref[rs, :], w_ref[...])
    h1 = h_ref[rs, :] + mix
    h1_ref[rs, :] = h1
    xn = _rms(h1, g_ref[...])
    hi, lo = _split_bf16(xn)
    xn_ref[rs, :] = _pack_halves(xn)
    logits = _dot(hi, wr_hi_ref[...]) + _dot(lo, wr_hi_ref[...]) + _dot(hi, wr_lo_ref[...])

    lane = lax.broadcasted_iota(I32, (tm, LANES), 1).astype(F32)
    big = float(LANES)
    ninf = -jnp.inf
    is_g = (lane >= N_EXPERTS) & (lane < N_EXPERTS + N_GROUPS)
    glog = jnp.where(is_g, logits, ninf)
    gmax = jnp.max(glog, axis=1, keepdims=True)
    gsel = jnp.min(jnp.where(glog == gmax, lane, big), axis=1, keepdims=True) - N_EXPERTS
    gprob = 1.0 / jnp.sum(jnp.exp(glog - gmax), axis=1, keepdims=True)
    lo_l = gsel * EXPERTS_PER_GROUP
    within = jnp.where((lane >= lo_l) & (lane < lo_l + EXPERTS_PER_GROUP), logits, ninf)
    v1 = jnp.max(within, axis=1, keepdims=True)
    i1 = jnp.min(jnp.where(within == v1, lane, big), axis=1, keepdims=True)
    rest = jnp.where(lane == i1, ninf, within)
    v2 = jnp.max(rest, axis=1, keepdims=True)
    i2 = jnp.min(jnp.where(rest == v2, lane, big), axis=1, keepdims=True)
    e2 = jnp.exp(v2 - v1)
    w1 = gprob / (1.0 + e2)
    w2 = gprob * e2 / (1.0 + e2)
    route_ref[rs, :] = jnp.where(lane == 0, i1, jnp.where(lane == 1, i2,
                                 jnp.where(lane == 2, w1, jnp.where(lane == 3, w2, 0.0))))


def _mix_out(h, ys, ws, g, wr_hi, wr_lo, tm=1024):
    n = h.shape[0]
    n_y = len(ys)
    row = lambda w: pl.BlockSpec((tm, w), lambda i: (i, 0))
    return pl.pallas_call(
        functools.partial(_mix_out_kernel, n_y=n_y),
        grid=(n // tm,),
        in_specs=[row(D_MODEL)] + [row(y.shape[1]) for y in ys] + [_full(w.shape) for w in ws]
                 + [_full(g.shape), _full(wr_hi.shape), _full(wr_lo.shape)],
        out_specs=[row(D_MODEL), row(D_MODEL // 2), row(LANES)],
        out_shape=[jax.ShapeDtypeStruct((n, D_MODEL), F32), jax.ShapeDtypeStruct((n, D_MODEL // 2), I32),
                   jax.ShapeDtypeStruct((n, LANES), F32)],
        compiler_params=_cparams(("parallel",)),
        name="mix_out_router",
    )(h, *ys, *ws, g, wr_hi, wr_lo)


def _sc_rows(src, idx, scatter):
    r, d = idx.shape[0], src.shape[1]
    n_src = src.shape[0]
    nw = SC_CORES * SC_SUBCORES
    per_w = r // nw
    n_chunks = per_w // SC_CHUNK
    assert r % nw == 0 and per_w % SC_CHUNK == 0 and (not scatter or n_src % per_w == 0)
    idx3 = idx.reshape(nw, n_chunks, SC_CHUNK)
    mesh = plsc.VectorSubcoreMesh(core_axis_name="c", subcore_axis_name="s")

    row_buf = pltpu.VMEM((SC_CHUNK, d), src.dtype)

    @functools.partial(
        pl.kernel, mesh=mesh, out_type=jax.ShapeDtypeStruct((r, d), src.dtype),
        scratch_types=[pltpu.VMEM((n_chunks, SC_CHUNK), I32), row_buf, row_buf] + [pltpu.SemaphoreType.DMA] * 4)
    def permute(src_hbm, idx_hbm, out_hbm, idx_v, rows0, rows1, in0, in1, out0, out1):
        wid = lax.axis_index("s") * SC_CORES + lax.axis_index("c")
        pltpu.sync_copy(idx_hbm.at[wid], idx_v)
        bufs, in_sems, out_sems = (rows0, rows1), (in0, in1), (out0, out1)

        def linear(c, n_rows):
            start = lax.rem(wid * per_w + c * SC_CHUNK, n_rows)
            return pl.ds(pl.multiple_of(start, SC_CHUNK), SC_CHUNK)

        def load(c):
            ref = src_hbm.at[linear(c, n_src)] if scatter else src_hbm.at[idx_v.at[c]]
            return pltpu.async_copy(ref, bufs[c % 2], in_sems[c % 2])

        def store(c):
            ref = out_hbm.at[idx_v.at[c]] if scatter else out_hbm.at[linear(c, r)]
            return pltpu.async_copy(bufs[c % 2], ref, out_sems[c % 2])

        loads, stores = {0: load(0)}, {}
        for c in range(n_chunks):
            if c + 1 < n_chunks:
                if c >= 1:
                    stores[c - 1].wait()
                loads[c + 1] = load(c + 1)
            loads[c].wait()
            stores[c] = store(c)
        for c in range(max(n_chunks - 2, 0), n_chunks):
            stores[c].wait()

    return permute(src, idx3)


def _moe_kernel(tile_ref, exp_ref, flag_ref, off_ref, xs_ref, wg_ref, wu_ref, wd_ref, ys_ref,
                wgu_ref, wdb_ref, acc_ref):
    w = pl.program_id(0)
    tm = xs_ref.shape[0]
    e = exp_ref[w]
    flags = flag_ref[w]

    @pl.when(w == 0)
    def _():
        acc_ref[...] = jnp.zeros_like(acc_ref)

    @pl.when((flags & 4) != 0)
    def _():
        wgu_ref[:, :EXPERT_FF] = wg_ref[0, 0].astype(BF16)
        wgu_ref[:, EXPERT_FF:] = wu_ref[0, 0].astype(BF16)
        wdb_ref[...] = wd_ref[0, 0].astype(BF16)

    @pl.when((flags & 1) != 0)
    def _():
        xa, xb = _unpack_halves(xs_ref[...])
        xa = xa.astype(BF16)
        xb = xb.astype(BF16)
        half = D_MODEL // 2
        gate_up = _dot(xa, wgu_ref[:half, :]) + _dot(xb, wgu_ref[half:, :])
        hid = (jax.nn.silu(gate_up[:, :EXPERT_FF]) * gate_up[:, EXPERT_FF:]).astype(BF16)
        y = _dot(hid, wdb_ref[...])
        rows = tile_ref[w] * tm + lax.broadcasted_iota(I32, (tm, 1), 0)
        mine = (rows >= off_ref[e]) & (rows < off_ref[e + 1])
        y = jnp.where(mine, y, 0.0)
        acc = jnp.where((flags & 2) != 0, y, acc_ref[...] + y)
        acc_ref[...] = acc
        ys_ref[...] = _pack_halves(acc)


def _moe_plan(off, n_rows, tm):
    nt = n_rows // tm
    n_items = nt + N_EXPERTS - 1
    row_lo = jnp.arange(nt, dtype=I32) * tm
    ends = off[1:N_EXPERTS][None, :]
    e_lo = jnp.sum((ends <= row_lo[:, None]).astype(I32), axis=1)
    e_hi = jnp.sum((ends <= (row_lo + tm - 1)[:, None]).astype(I32), axis=1)
    span = e_hi - e_lo + 1
    start = jnp.cumsum(span) - span
    total = jnp.sum(span)
    w = jnp.arange(n_items, dtype=I32)
    valid = w < total
    tile = jnp.clip(jnp.sum((start[None, :] <= w[:, None]).astype(I32), axis=1) - 1, 0, nt - 1)
    expert = jnp.where(valid, e_lo[tile] + (w - start[tile]), e_hi[nt - 1])
    first = valid & (w == start[tile])
    new_e = jnp.concatenate([jnp.ones((1,), bool), expert[1:] != expert[:-1]])
    flags = valid.astype(I32) + 2 * first.astype(I32) + 4 * new_e.astype(I32)
    return tile, expert, flags


def _moe(xs, tile, expert, flags, off, wg, wu, wd, layer, tm):
    n_rows, half = xs.shape
    n_items = tile.shape[0]
    grid_spec = pltpu.PrefetchScalarGridSpec(
        num_scalar_prefetch=4,
        grid=(n_items,),
        in_specs=[pl.BlockSpec((tm, half), lambda w, t, e, f, o: (t[w], 0)),
                  pl.BlockSpec((1, 1, D_MODEL, EXPERT_FF), lambda w, t, e, f, o: (layer, e[w], 0, 0)),
                  pl.BlockSpec((1, 1, D_MODEL, EXPERT_FF), lambda w, t, e, f, o: (layer, e[w], 0, 0)),
                  pl.BlockSpec((1, 1, EXPERT_FF, D_MODEL), lambda w, t, e, f, o: (layer, e[w], 0, 0))],
        out_specs=pl.BlockSpec((tm, half), lambda w, t, e, f, o: (t[w], 0)),
        scratch_shapes=[pltpu.VMEM((D_MODEL, 2 * EXPERT_FF), BF16),
                        pltpu.VMEM((EXPERT_FF, D_MODEL), BF16), pltpu.VMEM((tm, D_MODEL), F32)])
    return pl.pallas_call(
        _moe_kernel,
        grid_spec=grid_spec,
        out_shape=jax.ShapeDtypeStruct((n_rows, half), I32),
        compiler_params=_cparams(("arbitrary",)),
        name="moe_experts",
    )(tile, expert, flags, off, xs, wg, wu, wd)


def _moe_sparse(xn_packed, route, wg, wu, wd, layer, tm=256):
    n = xn_packed.shape[0]
    eid = jnp.concatenate([route[:, 0], route[:, 1]]).astype(I32)
    _, sorted_pair = lax.sort_key_val(eid, jnp.arange(2 * n, dtype=I32))
    token = jnp.where(sorted_pair >= n, sorted_pair - n, sorted_pair)
    bounds = jnp.arange(N_EXPERTS + 1, dtype=I32)
    off = jnp.sum((eid[None, :] < bounds[:, None]).astype(I32), axis=1)
    tile, expert, flags = _moe_plan(off, 2 * n, tm)
    xs = _sc_rows(xn_packed, token, scatter=False)
    ys = _moe(xs, tile, expert, flags, off, wg, wu, wd, layer, tm)
    return _sc_rows(ys, sorted_pair, scatter=True)


def _ple_kernel(h1_ref, y0_ref, y1_ref, route_ref, p_ref, wup_ref, wgate_ref, g_ref, o_ref):
    w1 = route_ref[:, 2:3]
    w2 = route_ref[:, 3:4]
    a0, b0 = _unpack_halves(y0_ref[...])
    a1, b1 = _unpack_halves(y1_ref[...])
    moe = jnp.concatenate([w1 * a0 + w2 * a1, w1 * b0 + w2 * b1], axis=1)
    h2 = h1_ref[...] + moe
    e = _rms(_dot(p_ref[...].astype(BF16), wup_ref[...]), g_ref[...])
    gate = jax.nn.sigmoid(_dot(h2.astype(BF16), wgate_ref[...]))
    o_ref[...] = h2 + e * gate


def _ple(h1, yp, route, p_all, layer, wup, wgate, g, tm=1024):
    n = h1.shape[0]
    nt = n // tm
    row = lambda w: pl.BlockSpec((tm, w), lambda i: (i, 0))
    return pl.pallas_call(
        _ple_kernel,
        grid=(nt,),
        in_specs=[row(D_MODEL), row(D_MODEL // 2),
                  pl.BlockSpec((tm, D_MODEL // 2), lambda i: (i + nt, 0)),
                  row(LANES), pl.BlockSpec((tm, PLE_DIM), lambda i: (i + layer * nt, 0)),
                  _full(wup.shape), _full(wgate.shape), _full(g.shape)],
        out_specs=row(D_MODEL),
        out_shape=jax.ShapeDtypeStruct((n, D_MODEL), F32),
        compiler_params=_cparams(("parallel",)),
        name="ple",
    )(h1, yp, yp, route, p_all, wup, wgate, g)


def _odd_in_kernel(h_ref, g_ref, wrow_ref, wkt_ref, cos_ref, sin_ref, cost_ref, sint_ref,
                   q_ref, v_ref, gate_ref, kt_ref):
    hn = _rms(h_ref[...], g_ref[...]).astype(BF16)
    cos = cos_ref[...]
    sin = sin_ref[...]
    half = RET_QK_DIM // 2
    q = _dot(hn, wrow_ref[:, :RET_QK_WIDTH])
    for hd in range(RET_HEADS):
        sl = slice(hd * RET_QK_DIM, (hd + 1) * RET_QK_DIM)
        qh = q[:, sl]
        q_ref[:, sl] = (qh * cos + pltpu.roll(qh, half, 1) * sin).astype(BF16)
    v_ref[...] = _dot(hn, wrow_ref[:, RET_QK_WIDTH:RET_QK_WIDTH + RET_V_WIDTH]).astype(BF16)
    gate_ref[...] = _dot(hn, wrow_ref[:, RET_QK_WIDTH + RET_V_WIDTH:]).astype(BF16)

    cost = cost_ref[...] * (RET_QK_DIM ** -0.5)
    sint = sint_ref[...] * (RET_QK_DIM ** -0.5)
    kt = _dot_nt(wkt_ref[...], hn)
    for hd in range(RET_HEADS):
        sl = slice(hd * RET_QK_DIM, (hd + 1) * RET_QK_DIM)
        kh = kt[sl, :]
        swapped = jnp.concatenate([kh[half:], kh[:half]], axis=0)
        kt_ref[sl, :] = (kh * cost + swapped * sint).astype(BF16)


def _odd_in(h, g, wrow, wkt, cos, sin, cost, sint, seq, tm=512):
    n = h.shape[0]
    nt = seq // tm
    row = lambda w: pl.BlockSpec((tm, w), lambda i: (i, 0))
    return pl.pallas_call(
        _odd_in_kernel,
        grid=(n // tm,),
        in_specs=[row(D_MODEL), _full(g.shape), _full(wrow.shape), _full(wkt.shape),
                  pl.BlockSpec((tm, RET_QK_DIM), lambda i: (i % nt, 0)),
                  pl.BlockSpec((tm, RET_QK_DIM), lambda i: (i % nt, 0)),
                  pl.BlockSpec((RET_QK_DIM, tm), lambda i: (0, i % nt)),
                  pl.BlockSpec((RET_QK_DIM, tm), lambda i: (0, i % nt))],
        out_specs=[row(RET_QK_WIDTH), row(RET_V_WIDTH), row(RET_V_WIDTH),
                   pl.BlockSpec((RET_QK_WIDTH, tm), lambda i: (0, i))],
        out_shape=[jax.ShapeDtypeStruct((n, RET_QK_WIDTH), BF16), jax.ShapeDtypeStruct((n, RET_V_WIDTH), BF16),
                   jax.ShapeDtypeStruct((n, RET_V_WIDTH), BF16), jax.ShapeDtypeStruct((RET_QK_WIDTH, n), BF16)],
        compiler_params=_cparams(("parallel",)),
        name="odd_in_proj",
    )(h, g, wrow, wkt, cos, sin, cost, sint)


def _ret_kernel(cdec_ref, q_ref, kt_ref, v_ref, gate_ref, dintra_ref, qdec_ref, kdec_ref, gn_ref,
                y_ref, state_ref):
    @pl.when(pl.program_id(1) == 0)
    def _():
        state_ref[...] = jnp.zeros_like(state_ref)

    for h in range(RET_HEADS):
        ks = slice(h * RET_QK_DIM, (h + 1) * RET_QK_DIM)
        vs = slice(h * RET_V_DIM, (h + 1) * RET_V_DIM)
        qh = q_ref[:, ks]
        kth = kt_ref[ks, :]
        vh = v_ref[:, vs]
        state = state_ref[h]
        inner = (_dot(qh, kth) * dintra_ref[h]).astype(BF16)
        o = _dot(inner, vh) + _dot(qh, state.astype(BF16)) * qdec_ref[h]
        kd = (kth.astype(F32) * kdec_ref[h]).astype(BF16)
        state_ref[h] = state * cdec_ref[h] + _dot(kd, vh)
        mu = jnp.mean(o, axis=-1, keepdims=True)
        oc = o - mu
        var = jnp.mean(oc * oc, axis=-1, keepdims=True)
        on = oc * lax.rsqrt(var + GN_EPS) * gn_ref[:, vs]
        y_ref[:, vs] = (jax.nn.silu(gate_ref[:, vs].astype(F32)) * on).astype(BF16)


def _retention(cdec, q, kt, v, gate, dintra, qdec, kdec, gn, batch, seq):
    n = q.shape[0]
    c = RET_TILE
    nc = seq // c
    row = lambda w: pl.BlockSpec((c, w), lambda b, t: (b * nc + t, 0))
    return pl.pallas_call(
        _ret_kernel,
        grid=(batch, nc),
        in_specs=[pl.BlockSpec(memory_space=pltpu.SMEM),
                  row(RET_QK_WIDTH),
                  pl.BlockSpec((RET_QK_WIDTH, c), lambda b, t: (0, b * nc + t)),
                  row(RET_V_WIDTH), row(RET_V_WIDTH),
                  _full(dintra.shape), _full(qdec.shape), _full(kdec.shape), _full(gn.shape)],
        out_specs=row(RET_V_WIDTH),
        out_shape=jax.ShapeDtypeStruct((n, RET_V_WIDTH), BF16),
        scratch_shapes=[pltpu.VMEM((RET_HEADS, RET_QK_DIM, RET_V_DIM), F32)],
        compiler_params=_cparams(("parallel", "arbitrary")),
        name="retention",
    )(cdec, q, kt, v, gate, dintra, qdec, kdec, gn)


def _t5_bucket(rel):
    half = N_BUCKETS // 2
    max_exact = half // 2
    ret = (rel > 0).astype(I32) * half
    n = jnp.abs(rel)
    nf = jnp.maximum(n, 1).astype(F32)
    large = max_exact + (jnp.log(nf / max_exact) / math.log(MAX_DISTANCE / max_exact)
                         * (half - max_exact)).astype(I32)
    large = jnp.minimum(large, half - 1)
    return ret + jnp.where(n < max_exact, n, large)


def _bias_tables(rel_bias):
    c = jnp.arange(BIAS_BLOCK, dtype=I32)[:, None]
    r = jnp.arange(Q_TILE, dtype=I32)[None, :]
    rels = jnp.stack([c + (b - 1) * BIAS_BLOCK - r for b in range(Q_TILE // BIAS_BLOCK + 1)])
    onehot = jax.nn.one_hot(_t5_bucket(rels), N_BUCKETS, dtype=F32)
    tab = jnp.einsum("abcn,nh->ahbc", onehot, rel_bias, precision=lax.Precision.HIGHEST)
    far = rel_bias[_t5_bucket(jnp.int32(-2 * MAX_DISTANCE))]
    return (tab * LOG2E).astype(F32), (far * LOG2E).astype(F32)


def _transpose_bf16(w):
    wb = w.astype(BF16)
    eye = jnp.eye(w.shape[0], dtype=BF16)
    return lax.dot_general(wb, eye, (((0,), (0,)), ((), ())), preferred_element_type=BF16)


def _block_diag(w):
    nb, bs, _ = w.shape
    eye = jnp.eye(nb, dtype=w.dtype)
    return jnp.einsum("hij,hg->higj", w, eye).reshape(nb * bs, nb * bs)


def _router_weights(w_group, w_expert):
    wr = jnp.concatenate([w_expert, w_group, jnp.zeros((D_MODEL, LANES - N_EXPERTS - N_GROUPS), F32)], axis=1)
    hi = wr.astype(BF16)
    lo = (wr - hi.astype(F32)).astype(BF16)
    return hi, lo


def _rotary_tables(seq):
    half = RET_QK_DIM // 2
    inv = ROPE_BASE ** (-jnp.arange(half, dtype=F32) / half)
    ang = jnp.arange(seq, dtype=F32)[:, None] * inv[None, :]
    cos, sin = jnp.cos(ang), jnp.sin(ang)
    cos_row = jnp.concatenate([cos, cos], axis=1)
    sin_row = jnp.concatenate([-sin, sin], axis=1)
    cos_t = jnp.concatenate([cos.T, cos.T], axis=0)
    sin_t = jnp.concatenate([-sin.T, sin.T], axis=0)
    return cos_row, sin_row, cos_t, sin_t


def _retention_tables():
    c = RET_TILE
    log_g = jnp.log(1.0 - 2.0 ** (-5.0 - jnp.arange(RET_HEADS, dtype=F32)))
    pos = jnp.arange(c, dtype=F32)
    diff = pos[:, None] - pos[None, :]
    causal = diff >= 0
    dintra = jnp.where(causal[None], jnp.exp(jnp.where(causal, diff, 0.0)[None] * log_g[:, None, None]), 0.0)
    qdec = jnp.exp((pos + 1.0)[None, :, None] * log_g[:, None, None])
    kdec = jnp.exp((c - 1.0 - pos)[None, None, :] * log_g[:, None, None])
    cdec = jnp.exp(c * log_g)
    return dintra, qdec, kdec, cdec


def kernel(x, p, rel_bias, mix_norm_g, ffn_norm_g, ple_norm_g, ev_w_in, ev_conv_w, ev_conv_b, ev_lru_wa, ev_lru_ba, ev_lru_wx, ev_lru_bx, ev_lru_lambda, ev_q_norm_g, ev_k_norm_g, ev_w_out, od_w_in, od_gn_g, od_w_out, moe_w_group, moe_w_expert, moe_w_gate, moe_w_up, moe_w_down, ple_w_up, ple_w_gate):
    batch, seq, _ = x.shape
    n = batch * seq
    depth = p.shape[0]
    h = x.reshape(n, D_MODEL)
    row = lambda a: a.reshape(1, -1)

    tab, far = _bias_tables(rel_bias)
    cos_row, sin_row, cos_t, sin_t = _rotary_tables(seq)
    dintra, qdec, kdec, cdec = _retention_tables()
    head_ones = _block_diag(jnp.ones((ATT_HEADS, ATT_HEAD_DIM, ATT_HEAD_DIM), BF16))

    for i in range(depth):
        jdx = i // 2
        if i % 2 == 0:
            w = ev_w_in[jdx]
            widths = (LRU_WIDTH, LRU_WIDTH, ATT_WIDTH, ATT_WIDTH, ATT_WIDTH, IDX_HEADS * IDX_DIM, IDX_DIM, IDX_HEADS)
            o = [sum(widths[:t]) for t in range(len(widths) + 1)]
            xa_w, ga_w, q_w, k_w, v_w, iq_w, ik_w, iw_w = [w[:, o[t]:o[t + 1]] for t in range(8)]
            wrow = jnp.concatenate([xa_w, ga_w, k_w, ik_w, ik_w], axis=1).astype(BF16)
            wt = _transpose_bf16(jnp.concatenate(
                [q_w, iq_w, v_w, iw_w, jnp.zeros((D_MODEL, 16 - IDX_HEADS), F32)], axis=1))
            qg = ev_q_norm_g[jdx].reshape(-1, 1) * (ATT_HEAD_DIM ** -0.5 * LOG2E)
            kg = jnp.tile(ev_k_norm_g[jdx], ATT_HEADS).reshape(1, -1)
            xg, k, ik, qt, iqt, vt, iwt = _even_in(h, row(mix_norm_g[i]), wrow, wt, head_ones, qg, kg)
            ya = _rglru(xg, ev_conv_w[jdx], row(ev_conv_b[jdx]),
                        _block_diag(ev_lru_wa[jdx]).astype(BF16), row(ev_lru_ba[jdx]),
                        _block_diag(ev_lru_wx[jdx]).astype(BF16), row(ev_lru_bx[jdx]),
                        row(ev_lru_lambda[jdx]), batch, seq)
            yb = _dsa(far, qt, iqt, iwt, k, ik, vt, tab, batch, seq)
            wo = ev_w_out[jdx].astype(BF16)
            ys, ws = [ya, yb], [wo[:LRU_WIDTH], wo[LRU_WIDTH:]]
        else:
            w = od_w_in[jdx]
            wrow = jnp.concatenate([w[:, :RET_QK_WIDTH], w[:, 2 * RET_QK_WIDTH:]], axis=1).astype(BF16)
            wkt = _transpose_bf16(w[:, RET_QK_WIDTH:2 * RET_QK_WIDTH])
            q, v, gate, kt = _odd_in(h, row(mix_norm_g[i]), wrow, wkt, cos_row, sin_row, cos_t, sin_t, seq)
            yc = _retention(cdec, q, kt, v, gate, dintra, qdec, kdec, row(od_gn_g[jdx]), batch, seq)
            ys, ws = [yc], [od_w_out[jdx].astype(BF16)]
        wr_hi, wr_lo = _router_weights(moe_w_group[i], moe_w_expert[i])
        h1, xn, route = _mix_out(h, ys, ws, row(ffn_norm_g[i]), wr_hi, wr_lo)
        yp = _moe_sparse(xn, route, moe_w_gate, moe_w_up, moe_w_down, i)
        h = _ple(h1, yp, route, p.reshape(depth * n, PLE_DIM), i, ple_w_up[i].astype(BF16),
                 ple_w_gate[i].astype(BF16), row(ple_norm_g[i]))
    return h.reshape(batch, seq, D_MODEL)
```

```python
import functools
import math

import jax
import jax.numpy as jnp
from jax import lax
from jax.experimental import pallas as pl
from jax.experimental.pallas import tpu as pltpu
from jax.experimental.pallas import tpu_sc as plsc

F32 = jnp.float32
BF16 = jnp.bfloat16
I32 = jnp.int32

D_MODEL = 1024
CHUNK = 64
PLE_DIM = 256
RMS_EPS = 1e-6

LRU_WIDTH = 512
LRU_BLOCKS = 8
LRU_C = 8.0

ATT_HEADS = 8
ATT_HEAD_DIM = 64
ATT_WIDTH = 512
IDX_HEADS = 8
IDX_DIM = 64
MAX_TOP_K = 256
N_BUCKETS = 32
MAX_DISTANCE = 128

RET_HEADS = 8
RET_QK_DIM = 128
RET_V_DIM = 256
RET_QK_WIDTH = 1024
RET_V_WIDTH = 2048
ROPE_BASE = 10000.0
GN_EPS = 1e-5

N_GROUPS = 4
EXPERTS_PER_GROUP = 8
N_EXPERTS = 32
EXPERT_FF = 512

LANES = 128
INT_MIN = -(2 ** 31)
NEG_BIG = -1e30
LOG2E = 1.4426950408889634
VMEM_LIMIT = 56 * 1024 * 1024

SC_CORES = 2
SC_SUBCORES = 16
SC_CHUNK = 64

MIX_SUB = 256
Q_TILE = 128
BIAS_BLOCK = 128
K_TILE = 512
RET_TILE = 256


def _cparams(sem):
    return pltpu.CompilerParams(dimension_semantics=sem, vmem_limit_bytes=VMEM_LIMIT)


def _full(shape):
    nd = len(shape)
    return pl.BlockSpec(shape, lambda *_: (0,) * nd)


def _rms(xf, g):
    return xf * lax.rsqrt(jnp.mean(xf * xf, axis=-1, keepdims=True) + RMS_EPS) * g


def _dot(a, b):
    return jnp.dot(a, b, preferred_element_type=F32)


def _dot_nt(a, b):
    return lax.dot_general(a, b, (((1,), (1,)), ((), ())), preferred_element_type=F32)


def _neg_expm1(x):
    series = -x * (1.0 + x * (0.5 + x * (1.0 / 6.0 + x * (1.0 / 24.0))))
    return jnp.where(x > -0.03, series, 1.0 - jnp.exp(x))


def _pack_halves(x):
    w = x.shape[1] // 2
    hi = pltpu.bitcast(x[:, :w].astype(BF16).astype(F32), I32)
    lo = pltpu.bitcast(x[:, w:].astype(BF16).astype(F32), I32)
    return hi | lax.shift_right_logical(lo, 16)


def _unpack_halves(p):
    hi = pltpu.bitcast(p & jnp.int32(-65536), F32)
    lo = pltpu.bitcast(lax.shift_left(p, 16), F32)
    return hi, lo


def _split_bf16(x):
    hi = x.astype(BF16)
    lo = (x - hi.astype(F32)).astype(BF16)
    return hi, lo


def _even_in_kernel(h_ref, g_ref, wrow_ref, wt_ref, bd_ref, qg_ref, kg_ref,
                    xg_ref, k_ref, ik_ref, qt_ref, iqt_ref, vt_ref, iwt_ref):
    tm = h_ref.shape[0]
    w = ATT_WIDTH
    hn = _rms(h_ref[...], g_ref[...]).astype(BF16)

    xg_ref[...] = _dot(hn, wrow_ref[:, :2 * LRU_WIDTH])
    kf = _dot(hn, wrow_ref[:, 2 * LRU_WIDTH:2 * LRU_WIDTH + w])
    hi, lo = _split_bf16(kf * kf)
    ss = _dot(hi, bd_ref[...]) + _dot(lo, bd_ref[...])
    k_ref[...] = (kf * lax.rsqrt(ss * (1.0 / ATT_HEAD_DIM) + RMS_EPS) * kg_ref[...]).astype(BF16)
    ik_ref[...] = _dot(hn, wrow_ref[:, 2 * LRU_WIDTH + w:]).astype(BF16)

    feat = _dot_nt(wt_ref[...], hn)
    q3 = feat[0:w].reshape(ATT_HEADS, ATT_HEAD_DIM, tm)
    ssq = jnp.sum(q3 * q3, axis=1, keepdims=True)
    qn = q3 * lax.rsqrt(ssq * (1.0 / ATT_HEAD_DIM) + RMS_EPS) * qg_ref[...][None]
    qt_ref[...] = qn.reshape(w, tm).astype(BF16)
    iqt_ref[...] = feat[w:2 * w].astype(BF16)
    vt_ref[...] = feat[2 * w:3 * w].astype(BF16)
    iwt_ref[...] = feat[3 * w:3 * w + IDX_HEADS] * (IDX_HEADS ** -0.5 * IDX_DIM ** -0.5)


def _even_in(h, g, wrow, wt, bd, qg, kg, tm=512):
    n = h.shape[0]
    row = lambda w: pl.BlockSpec((tm, w), lambda i: (i, 0))
    col = lambda r: pl.BlockSpec((r, tm), lambda i: (0, i))
    return pl.pallas_call(
        _even_in_kernel,
        grid=(n // tm,),
        in_specs=[row(D_MODEL), _full(g.shape), _full(wrow.shape), _full(wt.shape),
                  _full(bd.shape), _full(qg.shape), _full(kg.shape)],
        out_specs=[row(2 * LRU_WIDTH), row(ATT_WIDTH), row(LANES), col(ATT_WIDTH), col(ATT_WIDTH),
                   col(ATT_WIDTH), col(IDX_HEADS)],
        out_shape=[jax.ShapeDtypeStruct((n, 2 * LRU_WIDTH), F32), jax.ShapeDtypeStruct((n, ATT_WIDTH), BF16),
                   jax.ShapeDtypeStruct((n, LANES), BF16), jax.ShapeDtypeStruct((ATT_WIDTH, n), BF16),
                   jax.ShapeDtypeStruct((ATT_WIDTH, n), BF16), jax.ShapeDtypeStruct((ATT_WIDTH, n), BF16),
                   jax.ShapeDtypeStruct((IDX_HEADS, n), F32)],
        compiler_params=_cparams(("parallel",)),
        name="even_in_proj",
    )(h, g, wrow, wt, bd, qg, kg)


def _rglru_kernel(xg_ref, cw_ref, cb_ref, wa_ref, ba_ref, wx_ref, bx_ref, lam_ref, ya_ref,
                  ext_ref, hst_ref):
    ts = xg_ref.shape[0]
    w = LRU_WIDTH

    @pl.when(pl.program_id(1) == 0)
    def _():
        ext_ref[0:8, :] = jnp.zeros((8, w), F32)
        hst_ref[...] = jnp.zeros_like(hst_ref)

    xa = xg_ref[:, :w]
    ga = xg_ref[:, w:]
    ext_ref[8:, :] = xa
    xc = xa * cw_ref[3:4, :] + cb_ref[...]
    for d in (1, 2, 3):
        xc = xc + ext_ref[8 - d:8 - d + ts, :] * cw_ref[3 - d:4 - d, :]
    ext_ref[0:8, :] = xa[ts - 8:, :]

    xcb = xc.astype(BF16)
    r = jax.nn.sigmoid(_dot(xcb, wa_ref[...]) + ba_ref[...])
    gi = jax.nn.sigmoid(_dot(xcb, wx_ref[...]) + bx_ref[...])
    nl = -lam_ref[...]
    softplus = jnp.maximum(nl, 0.0) + jnp.log1p(jnp.exp(-jnp.abs(nl)))
    log_a = (-LRU_C) * r * softplus
    a = jnp.exp(log_a)
    u = jnp.sqrt(_neg_expm1(2.0 * log_a)) * (gi * xc)

    groups = ts // 8
    a3 = a.reshape(groups, 8, w)
    u3 = u.reshape(groups, 8, w)
    pos = lax.broadcasted_iota(I32, (groups, 8, w), 1)
    for d in (1, 2, 4):
        keep = pos >= d
        a_sh = jnp.where(keep, pltpu.roll(a3, d, 1), 1.0)
        u_sh = jnp.where(keep, pltpu.roll(u3, d, 1), 0.0)
        u3 = a3 * u_sh + u3
        a3 = a3 * a_sh
    carry = hst_ref[0:1, :]
    hs = []
    for g in range(groups):
        hg = a3[g] * carry + u3[g]
        carry = hg[7:8, :]
        hs.append(hg)
    hst_ref[0:1, :] = carry
    hseq = jnp.concatenate(hs, axis=0)
    ya_ref[...] = (hseq * jax.nn.gelu(ga)).astype(BF16)


def _rglru(xg, cw, cb, wa, ba, wx, bx, lam, batch, seq, ts=512):
    n = xg.shape[0]
    nt = seq // ts
    return pl.pallas_call(
        _rglru_kernel,
        grid=(batch, nt),
        in_specs=[pl.BlockSpec((ts, 2 * LRU_WIDTH), lambda b, t: (b * nt + t, 0)),
                  _full(cw.shape), _full(cb.shape), _full(wa.shape), _full(ba.shape),
                  _full(wx.shape), _full(bx.shape), _full(lam.shape)],
        out_specs=pl.BlockSpec((ts, LRU_WIDTH), lambda b, t: (b * nt + t, 0)),
        out_shape=jax.ShapeDtypeStruct((n, LRU_WIDTH), BF16),
        scratch_shapes=[pltpu.VMEM((ts + 8, LRU_WIDTH), F32), pltpu.VMEM((8, LRU_WIDTH), F32)],
        compiler_params=_cparams(("parallel", "arbitrary")),
        name="rglru",
    )(xg, cw, cb, wa, ba, wx, bx, lam)


def _dsa_kernel(far_ref, qt_ref, iqt_ref, iwt_ref, k_ref, ik_ref, vt_ref, tab_ref, o_ref,
                keys_ref, sel_ref, cut_ref, *, seq, top_k):
    tq = Q_TILE
    hd = ATT_HEAD_DIM
    j = pl.program_id(1)
    q0 = j * tq
    nkt = (q0 + tq + K_TILE - 1) // K_TILE
    qcol = q0 + lax.broadcasted_iota(I32, (1, tq), 1)
    qlim = (qcol // CHUNK + 1) * CHUNK
    zero_rows = jnp.zeros((hd, tq), BF16)

    def head_rows(ref, h):
        blk = ref[h * hd:(h + 1) * hd, :]
        return jnp.concatenate([blk, zero_rows] if h % 2 == 0 else [zero_rows, blk], axis=0)

    def ktile(kt):
        return pl.multiple_of(kt * K_TILE, K_TILE)

    def key_pos(off, rows):
        return off + lax.broadcasted_iota(I32, (rows, tq), 0)

    iq_m = [head_rows(iqt_ref, h) for h in range(IDX_HEADS)]

    def score_tile(kt, mask_inadmissible):
        off = ktile(kt)
        ik = ik_ref[pl.ds(off, K_TILE), :]
        s = jnp.zeros((K_TILE, tq), F32)
        for h in range(IDX_HEADS):
            s = s + jnp.maximum(_dot(ik, iq_m[h]), 0.0) * iwt_ref[h:h + 1, :]
        bits = pltpu.bitcast(s, I32)
        neg = bits >> 31
        key = (bits ^ (neg & 0x7FFFFFFF)) - neg
        if mask_inadmissible:
            key = jnp.where(key_pos(off, K_TILE) < qlim, key, INT_MIN)
        keys_ref[pl.ds(off, K_TILE), :] = key

    lax.fori_loop(0, nkt - 1, lambda kt, c: (score_tile(kt, False), c)[1], 0)
    score_tile(nkt - 1, True)

    def count(pred):
        def body(kt, acc):
            off = ktile(kt)
            ind = pred(keys_ref[pl.ds(off, K_TILE), :], key_pos(off, K_TILE)).reshape(K_TILE // 8, 8, tq)
            while ind.shape[0] > 1:
                half = ind.shape[0] // 2
                ind = ind[:half] + ind[half:]
            return acc + ind[0]
        acc = lax.fori_loop(0, nkt, body, jnp.zeros((8, tq), F32))
        return jnp.sum(acc, axis=0, keepdims=True)

    kf = float(top_k)

    def search_body(i, ans):
        cand = ans + jnp.left_shift(jnp.int32(1), 31 - i)
        cnt = count(lambda kk, kpos: jnp.where(kk >= cand, 1.0, 0.0))
        return jnp.where(cnt >= kf, cand, ans)

    ans = lax.fori_loop(0, 32, search_body, jnp.full((1, tq), INT_MIN, I32))

    cnt_ge = count(lambda kk, kpos: jnp.where(kk >= ans, 1.0, 0.0))
    excess = jnp.where(ans > INT_MIN, cnt_ge - kf, 0.0)
    cut_ref[...] = jnp.full(cut_ref.shape, seq, I32)

    @pl.when(jnp.max(excess) > 0.0)
    def _():
        need = kf - count(lambda kk, kpos: jnp.where(kk > ans, 1.0, 0.0))

        def idx_body(i, pos):
            cand = pos + jnp.left_shift(jnp.int32(1), (seq.bit_length() - 2) - i)
            c = count(lambda kk, kpos: jnp.where(kk == ans, jnp.where(kpos < cand, 1.0, 0.0), 0.0))
            return jnp.where(c < need, cand, pos)

        pos = lax.fori_loop(0, seq.bit_length() - 1, idx_body, jnp.zeros((1, tq), I32))
        cut_ref[...] = jnp.broadcast_to(pos, cut_ref.shape)

    thr = jnp.maximum(ans, INT_MIN + 1)
    cut = cut_ref[0:1, :]

    near_w = tq + BIAS_BLOCK
    near_start = pl.multiple_of(jnp.maximum(q0 - BIAS_BLOCK, 0), BIAS_BLOCK)

    def selection(kk, kpos):
        tie = jnp.where(kpos <= cut, 0.0, NEG_BIG)
        return jnp.where(kk > thr, 0.0, jnp.where(kk == thr, tie, NEG_BIG))

    def sel_body(kt, c):
        off = ktile(kt)
        kpos = key_pos(off, K_TILE)
        s = selection(keys_ref[pl.ds(off, K_TILE), :], kpos)
        sel_ref[pl.ds(off, K_TILE), :] = jnp.where(kpos < near_start, s, NEG_BIG)
        return c

    n_far = (near_start + K_TILE - 1) // K_TILE
    lax.fori_loop(0, n_far, sel_body, 0)
    sel_near = selection(keys_ref[pl.ds(near_start, near_w), :], key_pos(near_start, near_w))

    def step(qm, kb, vte, add, carry):
        m, acc = carry
        s = _dot(kb, qm) + add
        m_new = jnp.maximum(m, jnp.max(s, axis=0, keepdims=True))
        p = jnp.exp2(s - m_new).astype(BF16)
        acc = jnp.exp2(m - m_new) * acc + _dot(vte, p)
        return m_new, acc

    qms = [head_rows(qt_ref, h) for h in range(ATT_HEADS)]
    pairs = [slice((h // 2) * LANES, (h // 2 + 1) * LANES) for h in range(ATT_HEADS)]

    def values(h, off, width):
        return jnp.concatenate([vt_ref[h * hd:(h + 1) * hd, pl.ds(off, width)],
                                jnp.ones((hd, width), BF16)], axis=0)

    def far_body(kt, carries):
        off = ktile(kt)
        sel = sel_ref[pl.ds(off, K_TILE), :]
        return tuple(step(qms[h], k_ref[pl.ds(off, K_TILE), pairs[h]], values(h, off, K_TILE), sel, carries[h])
                     for h in range(ATT_HEADS))

    init = (jnp.full((1, tq), NEG_BIG, F32), jnp.zeros((2 * hd, tq), F32))
    carries = lax.fori_loop(0, n_far, far_body, (init,) * ATT_HEADS)

    n_blocks = near_w // BIAS_BLOCK
    first = jnp.where(j == 0, 1, 0)
    res = []
    for h in range(ATT_HEADS):
        m, acc = carries[h]
        bias = jnp.concatenate([tab_ref[jnp.minimum(first + b, n_blocks - 1), h] for b in range(n_blocks)],
                               axis=0)
        m, acc = step(qms[h], k_ref[pl.ds(near_start, near_w), pairs[h]], values(h, near_start, near_w),
                      sel_near + bias, (m + far_ref[h], acc))
        res.append(acc[:hd, :] / acc[hd:hd + 1, :])

    for p in range(ATT_HEADS // 2):
        pair_t = jnp.concatenate([res[2 * p], res[2 * p + 1]], axis=0)
        o_ref[:, p * LANES:(p + 1) * LANES] = pair_t.T.astype(BF16)


def _dsa(far, qt, iqt, iwt, k, ik, vt, tab, batch, seq):
    n = k.shape[0]
    nb = seq // Q_TILE
    top_k = min(MAX_TOP_K, seq // 4)
    qspec = lambda r: pl.BlockSpec((r, Q_TILE), lambda b, j: (0, b * nb + j))
    return pl.pallas_call(
        functools.partial(_dsa_kernel, seq=seq, top_k=top_k),
        grid=(batch, nb),
        in_specs=[pl.BlockSpec(memory_space=pltpu.SMEM),
                  qspec(ATT_WIDTH), qspec(ATT_WIDTH), qspec(IDX_HEADS),
                  pl.BlockSpec((seq, ATT_WIDTH), lambda b, j: (b, 0)),
                  pl.BlockSpec((seq, LANES), lambda b, j: (b, 0)),
                  pl.BlockSpec((ATT_WIDTH, seq), lambda b, j: (0, b)),
                  _full(tab.shape)],
        out_specs=pl.BlockSpec((Q_TILE, ATT_WIDTH), lambda b, j: (b * nb + j, 0)),
        out_shape=jax.ShapeDtypeStruct((n, ATT_WIDTH), BF16),
        scratch_shapes=[pltpu.VMEM((seq, Q_TILE), I32), pltpu.VMEM((seq, Q_TILE), F32),
                        pltpu.VMEM((8, Q_TILE), I32)],
        compiler_params=_cparams(("parallel", "arbitrary")),
        name="dsa",
    )(far, qt, iqt, iwt, k, ik, vt, tab)


def _mix_out_kernel(*refs, n_y):
    h_ref = refs[0]
    y_refs = refs[1:1 + n_y]
    w_refs = refs[1 + n_y:1 + 2 * n_y]
    g_ref, wr_hi_ref, wr_lo_ref = refs[1 + 2 * n_y:4 + 2 * n_y]
    h1_ref, xn_ref, route_ref = refs[4 + 2 * n_y:]
    for sub in range(h_ref.shape[0] // MIX_SUB):
        rs = slice(sub * MIX_SUB, (sub + 1) * MIX_SUB)
        _mix_out_rows(rs, h_ref, y_refs, w_refs, g_ref, wr_hi_ref, wr_lo_ref, h1_ref, xn_ref, route_ref)


def _mix_out_rows(rs, h_ref, y_refs, w_refs, g_ref, wr_hi_ref, wr_lo_ref, h1_ref, xn_ref, route_ref):
    tm = rs.stop - rs.start
    mix = _dot(y_refs[0][rs, :], w_refs[0][...])
    for y_ref, w_ref in zip(y_refs[1:], w_refs[1:]):
        mix = mix + _dot(y_ref[rs, :], w_ref[...])
    h1 = h_ref[rs, :] + mix
    h1_ref[rs, :] = h1
    xn = _rms(h1, g_ref[...])
    hi, lo = _split_bf16(xn)
    xn_ref[rs, :] = _pack_halves(xn)
    logits = _dot(hi, wr_hi_ref[...]) + _dot(lo, wr_hi_ref[...]) + _dot(hi, wr_lo_ref[...])

    lane = lax.broadcasted_iota(I32, (tm, LANES), 1).astype(F32)
    big = float(LANES)
    ninf = -jnp.inf
    is_g = (lane >= N_EXPERTS) & (lane < N_EXPERTS + N_GROUPS)
    glog = jnp.where(is_g, logits, ninf)
    gmax = jnp.max(glog, axis=1, keepdims=True)
    gsel = jnp.min(jnp.where(glog == gmax, lane, big), axis=1, keepdims=True) - N_EXPERTS
    gprob = 1.0 / jnp.sum(jnp.exp(glog - gmax), axis=1, keepdims=True)
    lo_l = gsel * EXPERTS_PER_GROUP
    within = jnp.where((lane >= lo_l) & (lane < lo_l + EXPERTS_PER_GROUP), logits, ninf)
    v1 = jnp.max(within, axis=1, keepdims=True)
    i1 = jnp.min(jnp.where(within == v1, lane, big), axis=1, keepdims=True)
    rest = jnp.where(lane == i1, ninf, within)
    v2 = jnp.max(rest, axis=1, keepdims=True)
    i2 = jnp.min(jnp.where(rest == v2, lane, big), axis=1, keepdims=True)
    e2 = jnp.exp(v2 - v1)
    w1 = gprob / (1.0 + e2)
    w2 = gprob * e2 / (1.0 + e2)
    route_ref[rs, :] = jnp.where(lane == 0, i1, jnp.where(lane == 1, i2,
                                 jnp.where(lane == 2, w1, jnp.where(lane == 3, w2, 0.0))))


def _mix_out(h, ys, ws, g, wr_hi, wr_lo, tm=1024):
    n = h.shape[0]
    n_y = len(ys)
    row = lambda w: pl.BlockSpec((tm, w), lambda i: (i, 0))
    return pl.pallas_call(
        functools.partial(_mix_out_kernel, n_y=n_y),
        grid=(n // tm,),
        in_specs=[row(D_MODEL)] + [row(y.shape[1]) for y in ys] + [_full(w.shape) for w in ws]
                 + [_full(g.shape), _full(wr_hi.shape), _full(wr_lo.shape)],
        out_specs=[row(D_MODEL), row(D_MODEL // 2), row(LANES)],
        out_shape=[jax.ShapeDtypeStruct((n, D_MODEL), F32), jax.ShapeDtypeStruct((n, D_MODEL // 2), I32),
                   jax.ShapeDtypeStruct((n, LANES), F32)],
        compiler_params=_cparams(("parallel",)),
        name="mix_out_router",
    )(h, *ys, *ws, g, wr_hi, wr_lo)


def _sc_rows(src, idx, scatter):
    r, d = idx.shape[0], src.shape[1]
    n_src = src.shape[0]
    nw = SC_CORES * SC_SUBCORES
    per_w = r // nw
    n_chunks = per_w // SC_CHUNK
    assert r % nw == 0 and per_w % SC_CHUNK == 0 and (not scatter or n_src % per_w == 0)
    idx3 = idx.reshape(nw, n_chunks, SC_CHUNK)
    mesh = plsc.VectorSubcoreMesh(core_axis_name="c", subcore_axis_name="s")

    row_buf = pltpu.VMEM((SC_CHUNK, d), src.dtype)

    @functools.partial(
        pl.kernel, mesh=mesh, out_type=jax.ShapeDtypeStruct((r, d), src.dtype),
        scratch_types=[pltpu.VMEM((n_chunks, SC_CHUNK), I32), row_buf, row_buf] + [pltpu.SemaphoreType.DMA] * 4)
    def permute(src_hbm, idx_hbm, out_hbm, idx_v, rows0, rows1, in0, in1, out0, out1):
        wid = lax.axis_index("s") * SC_CORES + lax.axis_index("c")
        pltpu.sync_copy(idx_hbm.at[wid], idx_v)
        bufs, in_sems, out_sems = (rows0, rows1), (in0, in1), (out0, out1)

        def linear(c, n_rows):
            start = lax.rem(wid * per_w + c * SC_CHUNK, n_rows)
            return pl.ds(pl.multiple_of(start, SC_CHUNK), SC_CHUNK)

        def load(c):
            ref = src_hbm.at[linear(c, n_src)] if scatter else src_hbm.at[idx_v.at[c]]
            return pltpu.async_copy(ref, bufs[c % 2], in_sems[c % 2])

        def store(c):
            ref = out_hbm.at[idx_v.at[c]] if scatter else out_hbm.at[linear(c, r)]
            return pltpu.async_copy(bufs[c % 2], ref, out_sems[c % 2])

        loads, stores = {0: load(0)}, {}
        for c in range(n_chunks):
            if c + 1 < n_chunks:
                if c >= 1:
                    stores[c - 1].wait()
                loads[c + 1] = load(c + 1)
            loads[c].wait()
            stores[c] = store(c)
        for c in range(max(n_chunks - 2, 0), n_chunks):
            stores[c].wait()

    return permute(src, idx3)


def _moe_kernel(tile_ref, exp_ref, flag_ref, off_ref, xs_ref, wg_ref, wu_ref, wd_ref, ys_ref,
                wgu_ref, wdb_ref, acc_ref):
    w = pl.program_id(0)
    tm = xs_ref.shape[0]
    e = exp_ref[w]
    flags = flag_ref[w]

    @pl.when(w == 0)
    def _():
        acc_ref[...] = jnp.zeros_like(acc_ref)

    @pl.when((flags & 4) != 0)
    def _():
        wgu_ref[:, :EXPERT_FF] = wg_ref[0, 0].astype(BF16)
        wgu_ref[:, EXPERT_FF:] = wu_ref[0, 0].astype(BF16)
        wdb_ref[...] = wd_ref[0, 0].astype(BF16)

    @pl.when((flags & 1) != 0)
    def _():
        xa, xb = _unpack_halves(xs_ref[...])
        xa = xa.astype(BF16)
        xb = xb.astype(BF16)
        half = D_MODEL // 2
        gate_up = _dot(xa, wgu_ref[:half, :]) + _dot(xb, wgu_ref[half:, :])
        hid = (jax.nn.silu(gate_up[:, :EXPERT_FF]) * gate_up[:, EXPERT_FF:]).astype(BF16)
        y = _dot(hid, wdb_ref[...])
        rows = tile_ref[w] * tm + lax.broadcasted_iota(I32, (tm, 1), 0)
        mine = (rows >= off_ref[e]) & (rows < off_ref[e + 1])
        y = jnp.where(mine, y, 0.0)
        acc = jnp.where((flags & 2) != 0, y, acc_ref[...] + y)
        acc_ref[...] = acc
        ys_ref[...] = _pack_halves(acc)


def _moe_plan(off, n_rows, tm):
    nt = n_rows // tm
    n_items = nt + N_EXPERTS - 1
    row_lo = jnp.arange(nt, dtype=I32) * tm
    ends = off[1:N_EXPERTS][None, :]
    e_lo = jnp.sum((ends <= row_lo[:, None]).astype(I32), axis=1)
    e_hi = jnp.sum((ends <= (row_lo + tm - 1)[:, None]).astype(I32), axis=1)
    span = e_hi - e_lo + 1
    start = jnp.cumsum(span) - span
    total = jnp.sum(span)
    w = jnp.arange(n_items, dtype=I32)
    valid = w < total
    tile = jnp.clip(jnp.sum((start[None, :] <= w[:, None]).astype(I32), axis=1) - 1, 0, nt - 1)
    expert = jnp.where(valid, e_lo[tile] + (w - start[tile]), e_hi[nt - 1])
    first = valid & (w == start[tile])
    new_e = jnp.concatenate([jnp.ones((1,), bool), expert[1:] != expert[:-1]])
    flags = valid.astype(I32) + 2 * first.astype(I32) + 4 * new_e.astype(I32)
    return tile, expert, flags


def _moe(xs, tile, expert, flags, off, wg, wu, wd, layer, tm):
    n_rows, half = xs.shape
    n_items = tile.shape[0]
    grid_spec = pltpu.PrefetchScalarGridSpec(
        num_scalar_prefetch=4,
        grid=(n_items,),
        in_specs=[pl.BlockSpec((tm, half), lambda w, t, e, f, o: (t[w], 0)),
                  pl.BlockSpec((1, 1, D_MODEL, EXPERT_FF), lambda w, t, e, f, o: (layer, e[w], 0, 0)),
                  pl.BlockSpec((1, 1, D_MODEL, EXPERT_FF), lambda w, t, e, f, o: (layer, e[w], 0, 0)),
                  pl.BlockSpec((1, 1, EXPERT_FF, D_MODEL), lambda w, t, e, f, o: (layer, e[w], 0, 0))],
        out_specs=pl.BlockSpec((tm, half), lambda w, t, e, f, o: (t[w], 0)),
        scratch_shapes=[pltpu.VMEM((D_MODEL, 2 * EXPERT_FF), BF16),
                        pltpu.VMEM((EXPERT_FF, D_MODEL), BF16), pltpu.VMEM((tm, D_MODEL), F32)])
    return pl.pallas_call(
        _moe_kernel,
        grid_spec=grid_spec,
        out_shape=jax.ShapeDtypeStruct((n_rows, half), I32),
        compiler_params=_cparams(("arbitrary",)),
        name="moe_experts",
    )(tile, expert, flags, off, xs, wg, wu, wd)


def _moe_sparse(xn_packed, route, wg, wu, wd, layer, tm=256):
    n = xn_packed.shape[0]
    eid = jnp.concatenate([route[:, 0], route[:, 1]]).astype(I32)
    _, sorted_pair = lax.sort_key_val(eid, jnp.arange(2 * n, dtype=I32))
    token = jnp.where(sorted_pair >= n, sorted_pair - n, sorted_pair)
    bounds = jnp.arange(N_EXPERTS + 1, dtype=I32)
    off = jnp.sum((eid[None, :] < bounds[:, None]).astype(I32), axis=1)
    tile, expert, flags = _moe_plan(off, 2 * n, tm)
    xs = _sc_rows(xn_packed, token, scatter=False)
    ys = _moe(xs, tile, expert, flags, off, wg, wu, wd, layer, tm)
    return _sc_rows(ys, sorted_pair, scatter=True)


def _ple_kernel(h1_ref, y0_ref, y1_ref, route_ref, p_ref, wup_ref, wgate_ref, g_ref, o_ref):
    w1 = route_ref[:, 2:3]
    w2 = route_ref[:, 3:4]
    a0, b0 = _unpack_halves(y0_ref[...])
    a1, b1 = _unpack_halves(y1_ref[...])
    moe = jnp.concatenate([w1 * a0 + w2 * a1, w1 * b0 + w2 * b1], axis=1)
    h2 = h1_ref[...] + moe
    e = _rms(_dot(p_ref[...].astype(BF16), wup_ref[...]), g_ref[...])
    gate = jax.nn.sigmoid(_dot(h2.astype(BF16), wgate_ref[...]))
    o_ref[...] = h2 + e * gate


def _ple(h1, yp, route, p_all, layer, wup, wgate, g, tm=1024):
    n = h1.shape[0]
    nt = n // tm
    row = lambda w: pl.BlockSpec((tm, w), lambda i: (i, 0))
    return pl.pallas_call(
        _ple_kernel,
        grid=(nt,),
        in_specs=[row(D_MODEL), row(D_MODEL // 2),
                  pl.BlockSpec((tm, D_MODEL // 2), lambda i: (i + nt, 0)),
                  row(LANES), pl.BlockSpec((tm, PLE_DIM), lambda i: (i + layer * nt, 0)),
                  _full(wup.shape), _full(wgate.shape), _full(g.shape)],
        out_specs=row(D_MODEL),
        out_shape=jax.ShapeDtypeStruct((n, D_MODEL), F32),
        compiler_params=_cparams(("parallel",)),
        name="ple",
    )(h1, yp, yp, route, p_all, wup, wgate, g)


def _odd_in_kernel(h_ref, g_ref, wrow_ref, wkt_ref, cos_ref, sin_ref, cost_ref, sint_ref,
                   q_ref, v_ref, gate_ref, kt_ref):
    hn = _rms(h_ref[...], g_ref[...]).astype(BF16)
    cos = cos_ref[...]
    sin = sin_ref[...]
    half = RET_QK_DIM // 2
    q = _dot(hn, wrow_ref[:, :RET_QK_WIDTH])
    for hd in range(RET_HEADS):
        sl = slice(hd * RET_QK_DIM, (hd + 1) * RET_QK_DIM)
        qh = q[:, sl]
        q_ref[:, sl] = (qh * cos + pltpu.roll(qh, half, 1) * sin).astype(BF16)
    v_ref[...] = _dot(hn, wrow_ref[:, RET_QK_WIDTH:RET_QK_WIDTH + RET_V_WIDTH]).astype(BF16)
    gate_ref[...] = _dot(hn, wrow_ref[:, RET_QK_WIDTH + RET_V_WIDTH:]).astype(BF16)

    cost = cost_ref[...] * (RET_QK_DIM ** -0.5)
    sint = sint_ref[...] * (RET_QK_DIM ** -0.5)
    kt = _dot_nt(wkt_ref[...], hn)
    for hd in range(RET_HEADS):
        sl = slice(hd * RET_QK_DIM, (hd + 1) * RET_QK_DIM)
        kh = kt[sl, :]
        swapped = jnp.concatenate([kh[half:], kh[:half]], axis=0)
        kt_ref[sl, :] = (kh * cost + swapped * sint).astype(BF16)


def _odd_in(h, g, wrow, wkt, cos, sin, cost, sint, seq, tm=512):
    n = h.shape[0]
    nt = seq // tm
    row = lambda w: pl.BlockSpec((tm, w), lambda i: (i, 0))
    return pl.pallas_call(
        _odd_in_kernel,
        grid=(n // tm,),
        in_specs=[row(D_MODEL), _full(g.shape), _full(wrow.shape), _full(wkt.shape),
                  pl.BlockSpec((tm, RET_QK_DIM), lambda i: (i % nt, 0)),
                  pl.BlockSpec((tm, RET_QK_DIM), lambda i: (i % nt, 0)),
                  pl.BlockSpec((RET_QK_DIM, tm), lambda i: (0, i % nt)),
                  pl.BlockSpec((RET_QK_DIM, tm), lambda i: (0, i % nt))],
        out_specs=[row(RET_QK_WIDTH), row(RET_V_WIDTH), row(RET_V_WIDTH),
                   pl.BlockSpec((RET_QK_WIDTH, tm), lambda i: (0, i))],
        out_shape=[jax.ShapeDtypeStruct((n, RET_QK_WIDTH), BF16), jax.ShapeDtypeStruct((n, RET_V_WIDTH), BF16),
                   jax.ShapeDtypeStruct((n, RET_V_WIDTH), BF16), jax.ShapeDtypeStruct((RET_QK_WIDTH, n), BF16)],
        compiler_params=_cparams(("parallel",)),
        name="odd_in_proj",
    )(h, g, wrow, wkt, cos, sin, cost, sint)


def _ret_kernel(cdec_ref, q_ref, kt_ref, v_ref, gate_ref, dintra_ref, qdec_ref, kdec_ref, gn_ref,
                y_ref, state_ref):
    @pl.when(pl.program_id(1) == 0)
    def _():
        state_ref[...] = jnp.zeros_like(state_ref)

    for h in range(RET_HEADS):
        ks = slice(h * RET_QK_DIM, (h + 1) * RET_QK_DIM)
        vs = slice(h * RET_V_DIM, (h + 1) * RET_V_DIM)
        qh = q_ref[:, ks]
        kth = kt_ref[ks, :]
        vh = v_ref[:, vs]
        state = state_ref[h]
        inner = (_dot(qh, kth) * dintra_ref[h]).astype(BF16)
        o = _dot(inner, vh) + _dot(qh, state.astype(BF16)) * qdec_ref[h]
        kd = (kth.astype(F32) * kdec_ref[h]).astype(BF16)
        state_ref[h] = state * cdec_ref[h] + _dot(kd, vh)
        mu = jnp.mean(o, axis=-1, keepdims=True)
        oc = o - mu
        var = jnp.mean(oc * oc, axis=-1, keepdims=True)
        on = oc * lax.rsqrt(var + GN_EPS) * gn_ref[:, vs]
        y_ref[:, vs] = (jax.nn.silu(gate_ref[:, vs].astype(F32)) * on).astype(BF16)


def _retention(cdec, q, kt, v, gate, dintra, qdec, kdec, gn, batch, seq):
    n = q.shape[0]
    c = RET_TILE
    nc = seq // c
    row = lambda w: pl.BlockSpec((c, w), lambda b, t: (b * nc + t, 0))
    return pl.pallas_call(
        _ret_kernel,
        grid=(batch, nc),
        in_specs=[pl.BlockSpec(memory_space=pltpu.SMEM),
                  row(RET_QK_WIDTH),
                  pl.BlockSpec((RET_QK_WIDTH, c), lambda b, t: (0, b * nc + t)),
                  row(RET_V_WIDTH), row(RET_V_WIDTH),
                  _full(dintra.shape), _full(qdec.shape), _full(kdec.shape), _full(gn.shape)],
        out_specs=row(RET_V_WIDTH),
        out_shape=jax.ShapeDtypeStruct((n, RET_V_WIDTH), BF16),
        scratch_shapes=[pltpu.VMEM((RET_HEADS, RET_QK_DIM, RET_V_DIM), F32)],
        compiler_params=_cparams(("parallel", "arbitrary")),
        name="retention",
    )(cdec, q, kt, v, gate, dintra, qdec, kdec, gn)


def _t5_bucket(rel):
    half = N_BUCKETS // 2
    max_exact = half // 2
    ret = (rel > 0).astype(I32) * half
    n = jnp.abs(rel)
    nf = jnp.maximum(n, 1).astype(F32)
    large = max_exact + (jnp.log(nf / max_exact) / math.log(MAX_DISTANCE / max_exact)
                         * (half - max_exact)).astype(I32)
    large = jnp.minimum(large, half - 1)
    return ret + jnp.where(n < max_exact, n, large)


def _bias_tables(rel_bias):
    c = jnp.arange(BIAS_BLOCK, dtype=I32)[:, None]
    r = jnp.arange(Q_TILE, dtype=I32)[None, :]
    rels = jnp.stack([c + (b - 1) * BIAS_BLOCK - r for b in range(Q_TILE // BIAS_BLOCK + 1)])
    onehot = jax.nn.one_hot(_t5_bucket(rels), N_BUCKETS, dtype=F32)
    tab = jnp.einsum("abcn,nh->ahbc", onehot, rel_bias, precision=lax.Precision.HIGHEST)
    far = rel_bias[_t5_bucket(jnp.int32(-2 * MAX_DISTANCE))]
    return (tab * LOG2E).astype(F32), (far * LOG2E).astype(F32)


def _transpose_bf16(w):
    wb = w.astype(BF16)
    eye = jnp.eye(w.shape[0], dtype=BF16)
    return lax.dot_general(wb, eye, (((0,), (0,)), ((), ())), preferred_element_type=BF16)


def _block_diag(w):
    nb, bs, _ = w.shape
    eye = jnp.eye(nb, dtype=w.dtype)
    return jnp.einsum("hij,hg->higj", w, eye).reshape(nb * bs, nb * bs)


def _router_weights(w_group, w_expert):
    wr = jnp.concatenate([w_expert, w_group, jnp.zeros((D_MODEL, LANES - N_EXPERTS - N_GROUPS), F32)], axis=1)
    hi = wr.astype(BF16)
    lo = (wr - hi.astype(F32)).astype(BF16)
    return hi, lo


def _rotary_tables(seq):
    half = RET_QK_DIM // 2
    inv = ROPE_BASE ** (-jnp.arange(half, dtype=F32) / half)
    ang = jnp.arange(seq, dtype=F32)[:, None] * inv[None, :]
    cos, sin = jnp.cos(ang), jnp.sin(ang)
    cos_row = jnp.concatenate([cos, cos], axis=1)
    sin_row = jnp.concatenate([-sin, sin], axis=1)
    cos_t = jnp.concatenate([cos.T, cos.T], axis=0)
    sin_t = jnp.concatenate([-sin.T, sin.T], axis=0)
    return cos_row, sin_row, cos_t, sin_t


def _retention_tables():
    c = RET_TILE
    log_g = jnp.log(1.0 - 2.0 ** (-5.0 - jnp.arange(RET_HEADS, dtype=F32)))
    pos = jnp.arange(c, dtype=F32)
    diff = pos[:, None] - pos[None, :]
    causal = diff >= 0
    dintra = jnp.where(causal[None], jnp.exp(jnp.where(causal, diff, 0.0)[None] * log_g[:, None, None]), 0.0)
    qdec = jnp.exp((pos + 1.0)[None, :, None] * log_g[:, None, None])
    kdec = jnp.exp((c - 1.0 - pos)[None, None, :] * log_g[:, None, None])
    cdec = jnp.exp(c * log_g)
    return dintra, qdec, kdec, cdec


def kernel(x, p, rel_bias, mix_norm_g, ffn_norm_g, ple_norm_g, ev_w_in, ev_conv_w, ev_conv_b, ev_lru_wa, ev_lru_ba, ev_lru_wx, ev_lru_bx, ev_lru_lambda, ev_q_norm_g, ev_k_norm_g, ev_w_out, od_w_in, od_gn_g, od_w_out, moe_w_group, moe_w_expert, moe_w_gate, moe_w_up, moe_w_down, ple_w_up, ple_w_gate):
    batch, seq, _ = x.shape
    n = batch * seq
    depth = p.shape[0]
    h = x.reshape(n, D_MODEL)
    row = lambda a: a.reshape(1, -1)

    tab, far = _bias_tables(rel_bias)
    cos_row, sin_row, cos_t, sin_t = _rotary_tables(seq)
    dintra, qdec, kdec, cdec = _retention_tables()
    head_ones = _block_diag(jnp.ones((ATT_HEADS, ATT_HEAD_DIM, ATT_HEAD_DIM), BF16))

    for i in range(depth):
        jdx = i // 2
        if i % 2 == 0:
            w = ev_w_in[jdx]
            widths = (LRU_WIDTH, LRU_WIDTH, ATT_WIDTH, ATT_WIDTH, ATT_WIDTH, IDX_HEADS * IDX_DIM, IDX_DIM, IDX_HEADS)
            o = [sum(widths[:t]) for t in range(len(widths) + 1)]
            xa_w, ga_w, q_w, k_w, v_w, iq_w, ik_w, iw_w = [w[:, o[t]:o[t + 1]] for t in range(8)]
            wrow = jnp.concatenate([xa_w, ga_w, k_w, ik_w, ik_w], axis=1).astype(BF16)
            wt = _transpose_bf16(jnp.concatenate(
                [q_w, iq_w, v_w, iw_w, jnp.zeros((D_MODEL, 16 - IDX_HEADS), F32)], axis=1))
            qg = ev_q_norm_g[jdx].reshape(-1, 1) * (ATT_HEAD_DIM ** -0.5 * LOG2E)
            kg = jnp.tile(ev_k_norm_g[jdx], ATT_HEADS).reshape(1, -1)
            xg, k, ik, qt, iqt, vt, iwt = _even_in(h, row(mix_norm_g[i]), wrow, wt, head_ones, qg, kg)
            ya = _rglru(xg, ev_conv_w[jdx], row(ev_conv_b[jdx]),
                        _block_diag(ev_lru_wa[jdx]).astype(BF16), row(ev_lru_ba[jdx]),
                        _block_diag(ev_lru_wx[jdx]).astype(BF16), row(ev_lru_bx[jdx]),
                        row(ev_lru_lambda[jdx]), batch, seq)
            yb = _dsa(far, qt, iqt, iwt, k, ik, vt, tab, batch, seq)
            wo = ev_w_out[jdx].astype(BF16)
            ys, ws = [ya, yb], [wo[:LRU_WIDTH], wo[LRU_WIDTH:]]
        else:
            w = od_w_in[jdx]
            wrow = jnp.concatenate([w[:, :RET_QK_WIDTH], w[:, 2 * RET_QK_WIDTH:]], axis=1).astype(BF16)
            wkt = _transpose_bf16(w[:, RET_QK_WIDTH:2 * RET_QK_WIDTH])
            q, v, gate, kt = _odd_in(h, row(mix_norm_g[i]), wrow, wkt, cos_row, sin_row, cos_t, sin_t, seq)
            yc = _retention(cdec, q, kt, v, gate, dintra, qdec, kdec, row(od_gn_g[jdx]), batch, seq)
            ys, ws = [yc], [od_w_out[jdx].astype(BF16)]
        wr_hi, wr_lo = _router_weights(moe_w_group[i], moe_w_expert[i])
        h1, xn, route = _mix_out(h, ys, ws, row(ffn_norm_g[i]), wr_hi, wr_lo)
        yp = _moe_sparse(xn, route, moe_w_gate, moe_w_up, moe_w_down, i)
        h = _ple(h1, yp, route, p.reshape(depth * n, PLE_DIM), i, ple_w_up[i].astype(BF16),
                 ple_w_gate[i].astype(BF16), row(ple_norm_g[i]))
    return h.reshape(batch, seq, D_MODEL)
```

```python
import functools
import math

import jax
import jax.numpy as jnp
from jax import lax
from jax.experimental import pallas as pl
from jax.experimental.pallas import tpu as pltpu
from jax.experimental.pallas import tpu_sc as plsc

F32 = jnp.float32
BF16 = jnp.bfloat16
I32 = jnp.int32

D_MODEL = 1024
CHUNK = 64
PLE_DIM = 256
RMS_EPS = 1e-6

LRU_WIDTH = 512
LRU_BLOCKS = 8
LRU_C = 8.0

ATT_HEADS = 8
ATT_HEAD_DIM = 64
ATT_WIDTH = 512
IDX_HEADS = 8
IDX_DIM = 64
MAX_TOP_K = 256
N_BUCKETS = 32
MAX_DISTANCE = 128

RET_HEADS = 8
RET_QK_DIM = 128
RET_V_DIM = 256
RET_QK_WIDTH = 1024
RET_V_WIDTH = 2048
ROPE_BASE = 10000.0
GN_EPS = 1e-5

N_GROUPS = 4
EXPERTS_PER_GROUP = 8
N_EXPERTS = 32
EXPERT_FF = 512

LANES = 128
INT_MIN = -(2 ** 31)
NEG_BIG = -1e30
LOG2E = 1.4426950408889634
VMEM_LIMIT = 56 * 1024 * 1024

SC_CORES = 2
SC_SUBCORES = 16
SC_CHUNK = 64

MIX_SUB = 256
Q_TILE = 128
BIAS_BLOCK = 128
K_TILE = 512
RET_TILE = 256


def _cparams(sem):
    return pltpu.CompilerParams(dimension_semantics=sem, vmem_limit_bytes=VMEM_LIMIT)


def _full(shape):
    nd = len(shape)
    return pl.BlockSpec(shape, lambda *_: (0,) * nd)


def _rms(xf, g):
    return xf * lax.rsqrt(jnp.mean(xf * xf, axis=-1, keepdims=True) + RMS_EPS) * g


def _dot(a, b):
    return jnp.dot(a, b, preferred_element_type=F32)


def _dot_nt(a, b):
    return lax.dot_general(a, b, (((1,), (1,)), ((), ())), preferred_element_type=F32)


def _neg_expm1(x):
    series = -x * (1.0 + x * (0.5 + x * (1.0 / 6.0 + x * (1.0 / 24.0))))
    return jnp.where(x > -0.03, series, 1.0 - jnp.exp(x))


def _pack_halves(x):
    w = x.shape[1] // 2
    hi = pltpu.bitcast(x[:, :w].astype(BF16).astype(F32), I32)
    lo = pltpu.bitcast(x[:, w:].astype(BF16).astype(F32), I32)
    return hi | lax.shift_right_logical(lo, 16)


def _unpack_halves(p):
    hi = pltpu.bitcast(p & jnp.int32(-65536), F32)
    lo = pltpu.bitcast(lax.shift_left(p, 16), F32)
    return hi, lo


def _split_bf16(x):
    hi = x.astype(BF16)
    lo = (x - hi.astype(F32)).astype(BF16)
    return hi, lo


def _even_in_kernel(h_ref, g_ref, wrow_ref, wt_ref, bd_ref, qg_ref, kg_ref,
                    xg_ref, k_ref, ik_ref, qt_ref, iqt_ref, vt_ref, iwt_ref):
    tm = h_ref.shape[0]
    w = ATT_WIDTH
    hn = _rms(h_ref[...], g_ref[...]).astype(BF16)

    xg_ref[...] = _dot(hn, wrow_ref[:, :2 * LRU_WIDTH])
    kf = _dot(hn, wrow_ref[:, 2 * LRU_WIDTH:2 * LRU_WIDTH + w])
    hi, lo = _split_bf16(kf * kf)
    ss = _dot(hi, bd_ref[...]) + _dot(lo, bd_ref[...])
    k_ref[...] = (kf * lax.rsqrt(ss * (1.0 / ATT_HEAD_DIM) + RMS_EPS) * kg_ref[...]).astype(BF16)
    ik_ref[...] = _dot(hn, wrow_ref[:, 2 * LRU_WIDTH + w:]).astype(BF16)

    feat = _dot_nt(wt_ref[...], hn)
    q3 = feat[0:w].reshape(ATT_HEADS, ATT_HEAD_DIM, tm)
    ssq = jnp.sum(q3 * q3, axis=1, keepdims=True)
    qn = q3 * lax.rsqrt(ssq * (1.0 / ATT_HEAD_DIM) + RMS_EPS) * qg_ref[...][None]
    qt_ref[...] = qn.reshape(w, tm).astype(BF16)
    iqt_ref[...] = feat[w:2 * w].astype(BF16)
    vt_ref[...] = feat[2 * w:3 * w].astype(BF16)
    iwt_ref[...] = feat[3 * w:3 * w + IDX_HEADS] * (IDX_HEADS ** -0.5 * IDX_DIM ** -0.5)


def _even_in(h, g, wrow, wt, bd, qg, kg, tm=512):
    n = h.shape[0]
    row = lambda w: pl.BlockSpec((tm, w), lambda i: (i, 0))
    col = lambda r: pl.BlockSpec((r, tm), lambda i: (0, i))
    return pl.pallas_call(
        _even_in_kernel,
        grid=(n // tm,),
        in_specs=[row(D_MODEL), _full(g.shape), _full(wrow.shape), _full(wt.shape),
                  _full(bd.shape), _full(qg.shape), _full(kg.shape)],
        out_specs=[row(2 * LRU_WIDTH), row(ATT_WIDTH), row(LANES), col(ATT_WIDTH), col(ATT_WIDTH),
                   col(ATT_WIDTH), col(IDX_HEADS)],
        out_shape=[jax.ShapeDtypeStruct((n, 2 * LRU_WIDTH), F32), jax.ShapeDtypeStruct((n, ATT_WIDTH), BF16),
                   jax.ShapeDtypeStruct((n, LANES), BF16), jax.ShapeDtypeStruct((ATT_WIDTH, n), BF16),
                   jax.ShapeDtypeStruct((ATT_WIDTH, n), BF16), jax.ShapeDtypeStruct((ATT_WIDTH, n), BF16),
                   jax.ShapeDtypeStruct((IDX_HEADS, n), F32)],
        compiler_params=_cparams(("parallel",)),
        name="even_in_proj",
    )(h, g, wrow, wt, bd, qg, kg)


def _rglru_kernel(xg_ref, cw_ref, cb_ref, wa_ref, ba_ref, wx_ref, bx_ref, lam_ref, ya_ref,
                  ext_ref, hst_ref):
    ts = xg_ref.shape[0]
    w = LRU_WIDTH

    @pl.when(pl.program_id(1) == 0)
    def _():
        ext_ref[0:8, :] = jnp.zeros((8, w), F32)
        hst_ref[...] = jnp.zeros_like(hst_ref)

    xa = xg_ref[:, :w]
    ga = xg_ref[:, w:]
    ext_ref[8:, :] = xa
    xc = xa * cw_ref[3:4, :] + cb_ref[...]
    for d in (1, 2, 3):
        xc = xc + ext_ref[8 - d:8 - d + ts, :] * cw_ref[3 - d:4 - d, :]
    ext_ref[0:8, :] = xa[ts - 8:, :]

    xcb = xc.astype(BF16)
    r = jax.nn.sigmoid(_dot(xcb, wa_ref[...]) + ba_ref[...])
    gi = jax.nn.sigmoid(_dot(xcb, wx_ref[...]) + bx_ref[...])
    nl = -lam_ref[...]
    softplus = jnp.maximum(nl, 0.0) + jnp.log1p(jnp.exp(-jnp.abs(nl)))
    log_a = (-LRU_C) * r * softplus
    a = jnp.exp(log_a)
    u = jnp.sqrt(_neg_expm1(2.0 * log_a)) * (gi * xc)

    groups = ts // 8
    a3 = a.reshape(groups, 8, w)
    u3 = u.reshape(groups, 8, w)
    pos = lax.broadcasted_iota(I32, (groups, 8, w), 1)
    for d in (1, 2, 4):
        keep = pos >= d
        a_sh = jnp.where(keep, pltpu.roll(a3, d, 1), 1.0)
        u_sh = jnp.where(keep, pltpu.roll(u3, d, 1), 0.0)
        u3 = a3 * u_sh + u3
        a3 = a3 * a_sh
    carry = hst_ref[0:1, :]
    hs = []
    for g in range(groups):
        hg = a3[g] * carry + u3[g]
        carry = hg[7:8, :]
        hs.append(hg)
    hst_ref[0:1, :] = carry
    hseq = jnp.concatenate(hs, axis=0)
    ya_ref[...] = (hseq * jax.nn.gelu(ga)).astype(BF16)


def _rglru(xg, cw, cb, wa, ba, wx, bx, lam, batch, seq, ts=512):
    n = xg.shape[0]
    nt = seq // ts
    return pl.pallas_call(
        _rglru_kernel,
        grid=(batch, nt),
        in_specs=[pl.BlockSpec((ts, 2 * LRU_WIDTH), lambda b, t: (b * nt + t, 0)),
                  _full(cw.shape), _full(cb.shape), _full(wa.shape), _full(ba.shape),
                  _full(wx.shape), _full(bx.shape), _full(lam.shape)],
        out_specs=pl.BlockSpec((ts, LRU_WIDTH), lambda b, t: (b * nt + t, 0)),
        out_shape=jax.ShapeDtypeStruct((n, LRU_WIDTH), BF16),
        scratch_shapes=[pltpu.VMEM((ts + 8, LRU_WIDTH), F32), pltpu.VMEM((8, LRU_WIDTH), F32)],
        compiler_params=_cparams(("parallel", "arbitrary")),
        name="rglru",
    )(xg, cw, cb, wa, ba, wx, bx, lam)


def _dsa_kernel(far_ref, qt_ref, iqt_ref, iwt_ref, k_ref, ik_ref, vt_ref, tab_ref, o_ref,
                keys_ref, sel_ref, cut_ref, *, seq, top_k):
    tq = Q_TILE
    hd = ATT_HEAD_DIM
    j = pl.program_id(1)
    q0 = j * tq
    nkt = (q0 + tq + K_TILE - 1) // K_TILE
    qcol = q0 + lax.broadcasted_iota(I32, (1, tq), 1)
    qlim = (qcol // CHUNK + 1) * CHUNK
    zero_rows = jnp.zeros((hd, tq), BF16)

    def head_rows(ref, h):
        blk = ref[h * hd:(h + 1) * hd, :]
        return jnp.concatenate([blk, zero_rows] if h % 2 == 0 else [zero_rows, blk], axis=0)

    def ktile(kt):
        return pl.multiple_of(kt * K_TILE, K_TILE)

    def key_pos(off, rows):
        return off + lax.broadcasted_iota(I32, (rows, tq), 0)

    iq_m = [head_rows(iqt_ref, h) for h in range(IDX_HEADS)]

    def score_tile(kt, mask_inadmissible):
        off = ktile(kt)
        ik = ik_ref[pl.ds(off, K_TILE), :]
        s = jnp.zeros((K_TILE, tq), F32)
        for h in range(IDX_HEADS):
            s = s + jnp.maximum(_dot(ik, iq_m[h]), 0.0) * iwt_ref[h:h + 1, :]
        bits = pltpu.bitcast(s, I32)
        neg = bits >> 31
        key = (bits ^ (neg & 0x7FFFFFFF)) - neg
        if mask_inadmissible:
            key = jnp.where(key_pos(off, K_TILE) < qlim, key, INT_MIN)
        keys_ref[pl.ds(off, K_TILE), :] = key

    lax.fori_loop(0, nkt - 1, lambda kt, c: (score_tile(kt, False), c)[1], 0)
    score_tile(nkt - 1, True)

    def count(pred):
        def body(kt, acc):
            off = ktile(kt)
            ind = pred(keys_ref[pl.ds(off, K_TILE), :], key_pos(off, K_TILE)).reshape(K_TILE // 8, 8, tq)
            while ind.shape[0] > 1:
                half = ind.shape[0] // 2
                ind = ind[:half] + ind[half:]
            return acc + ind[0]
        acc = lax.fori_loop(0, nkt, body, jnp.zeros((8, tq), F32))
        return jnp.sum(acc, axis=0, keepdims=True)

    kf = float(top_k)

    def search_body(i, carry):
        ans, cnt_ans = carry
        cand = ans + jnp.left_shift(jnp.int32(1), 31 - i)
        cnt = count(lambda kk, kpos: jnp.where(kk >= cand, 1.0, 0.0))
        take = cnt >= kf
        return jnp.where(take, cand, ans), jnp.where(take, cnt, cnt_ans)

    covered = (nkt * K_TILE).astype(F32)
    ans, cnt_ge = lax.fori_loop(0, 32, search_body,
                                (jnp.full((1, tq), INT_MIN, I32), jnp.full((1, tq), covered, F32)))

    excess = jnp.where(ans > INT_MIN, cnt_ge - kf, 0.0)
    cut_ref[...] = jnp.full(cut_ref.shape, seq, I32)

    @pl.when(jnp.max(excess) > 0.0)
    def _():
        need = kf - count(lambda kk, kpos: jnp.where(kk > ans, 1.0, 0.0))

        def idx_body(i, pos):
            cand = pos + jnp.left_shift(jnp.int32(1), (seq.bit_length() - 2) - i)
            c = count(lambda kk, kpos: jnp.where(kk == ans, jnp.where(kpos < cand, 1.0, 0.0), 0.0))
            return jnp.where(c < need, cand, pos)

        pos = lax.fori_loop(0, seq.bit_length() - 1, idx_body, jnp.zeros((1, tq), I32))
        cut_ref[...] = jnp.broadcast_to(pos, cut_ref.shape)

    thr = jnp.maximum(ans, INT_MIN + 1)
    cut = cut_ref[0:1, :]

    near_w = tq + BIAS_BLOCK
    near_start = pl.multiple_of(jnp.maximum(q0 - BIAS_BLOCK, 0), BIAS_BLOCK)

    def selection(kk, kpos):
        tie = jnp.where(kpos <= cut, 0.0, NEG_BIG)
        return jnp.where(kk > thr, 0.0, jnp.where(kk == thr, tie, NEG_BIG))

    def sel_body(kt, c):
        off = ktile(kt)
        kpos = key_pos(off, K_TILE)
        s = selection(keys_ref[pl.ds(off, K_TILE), :], kpos)
        sel_ref[pl.ds(off, K_TILE), :] = jnp.where(kpos < near_start, s, NEG_BIG)
        return c

    n_far = (near_start + K_TILE - 1) // K_TILE
    lax.fori_loop(0, n_far, sel_body, 0)
    sel_near = selection(keys_ref[pl.ds(near_start, near_w), :], key_pos(near_start, near_w))

    def step(qm, kb, vte, add, carry):
        m, acc = carry
        s = _dot(kb, qm) + add
        m_new = jnp.maximum(m, jnp.max(s, axis=0, keepdims=True))
        p = jnp.exp2(s - m_new).astype(BF16)
        acc = jnp.exp2(m - m_new) * acc + _dot(vte, p)
        return m_new, acc

    qms = [head_rows(qt_ref, h) for h in range(ATT_HEADS)]
    pairs = [slice((h // 2) * LANES, (h // 2 + 1) * LANES) for h in range(ATT_HEADS)]

    def values(h, off, width):
        return jnp.concatenate([vt_ref[h * hd:(h + 1) * hd, pl.ds(off, width)],
                                jnp.ones((hd, width), BF16)], axis=0)

    def far_body(kt, carries):
        off = ktile(kt)
        sel = sel_ref[pl.ds(off, K_TILE), :]
        return tuple(step(qms[h], k_ref[pl.ds(off, K_TILE), pairs[h]], values(h, off, K_TILE), sel, carries[h])
                     for h in range(ATT_HEADS))

    init = (jnp.full((1, tq), NEG_BIG, F32), jnp.zeros((2 * hd, tq), F32))
    carries = lax.fori_loop(0, n_far, far_body, (init,) * ATT_HEADS)

    n_blocks = near_w // BIAS_BLOCK
    first = jnp.where(j == 0, 1, 0)
    res = []
    for h in range(ATT_HEADS):
        m, acc = carries[h]
        bias = jnp.concatenate([tab_ref[jnp.minimum(first + b, n_blocks - 1), h] for b in range(n_blocks)],
                               axis=0)
        m, acc = step(qms[h], k_ref[pl.ds(near_start, near_w), pairs[h]], values(h, near_start, near_w),
                      sel_near + bias, (m + far_ref[h], acc))
        res.append(acc[:hd, :] / acc[hd:hd + 1, :])

    for p in range(ATT_HEADS // 2):
        pair_t = jnp.concatenate([res[2 * p], res[2 * p + 1]], axis=0)
        o_ref[:, p * LANES:(p + 1) * LANES] = pair_t.T.astype(BF16)


def _dsa(far, qt, iqt, iwt, k, ik, vt, tab, batch, seq):
    n = k.shape[0]
    nb = seq // Q_TILE
    top_k = min(MAX_TOP_K, seq // 4)
    qspec = lambda r: pl.BlockSpec((r, Q_TILE), lambda b, j: (0, b * nb + j))
    return pl.pallas_call(
        functools.partial(_dsa_kernel, seq=seq, top_k=top_k),
        grid=(batch, nb),
        in_specs=[pl.BlockSpec(memory_space=pltpu.SMEM),
                  qspec(ATT_WIDTH), qspec(ATT_WIDTH), qspec(IDX_HEADS),
                  pl.BlockSpec((seq, ATT_WIDTH), lambda b, j: (b, 0)),
                  pl.BlockSpec((seq, LANES), lambda b, j: (b, 0)),
                  pl.BlockSpec((ATT_WIDTH, seq), lambda b, j: (0, b)),
                  _full(tab.shape)],
        out_specs=pl.BlockSpec((Q_TILE, ATT_WIDTH), lambda b, j: (b * nb + j, 0)),
        out_shape=jax.ShapeDtypeStruct((n, ATT_WIDTH), BF16),
        scratch_shapes=[pltpu.VMEM((seq, Q_TILE), I32), pltpu.VMEM((seq, Q_TILE), F32),
                        pltpu.VMEM((8, Q_TILE), I32)],
        compiler_params=_cparams(("parallel", "arbitrary")),
        name="dsa",
    )(far, qt, iqt, iwt, k, ik, vt, tab)


def _mix_out_kernel(*refs, n_y):
    h_ref = refs[0]
    y_refs = refs[1:1 + n_y]
    w_refs = refs[1 + n_y:1 + 2 * n_y]
    g_ref, wr_hi_ref, wr_lo_ref = refs[1 + 2 * n_y:4 + 2 * n_y]
    h1_ref, xn_ref, route_ref = refs[4 + 2 * n_y:]
    for sub in range(h_ref.shape[0] // MIX_SUB):
        rs = slice(sub * MIX_SUB, (sub + 1) * MIX_SUB)
        _mix_out_rows(rs, h_ref, y_refs, w_refs, g_ref, wr_hi_ref, wr_lo_ref, h1_ref, xn_ref, route_ref)


def _mix_out_rows(rs, h_ref, y_refs, w_refs, g_ref, wr_hi_ref, wr_lo_ref, h1_ref, xn_ref, route_ref):
    tm = rs.stop - rs.start
    mix = _dot(y_refs[0][rs, :], w_refs[0][...])
    for y_ref, w_ref in zip(y_refs[1:], w_refs[1:]):
        mix = mix + _dot(y_ref[rs, :], w_ref[...])
    h1 = h_ref[rs, :] + mix
    h1_ref[rs, :] = h1
    xn = _rms(h1, g_ref[...])
    hi, lo = _split_bf16(xn)
    xn_ref[rs, :] = _pack_halves(xn)
    logits = _dot(hi, wr_hi_ref[...]) + _dot(lo, wr_hi_ref[...]) + _dot(hi, wr_lo_ref[...])

    lane = lax.broadcasted_iota(I32, (tm, LANES), 1).astype(F32)
    big = float(LANES)
    ninf = -jnp.inf
    is_g = (lane >= N_EXPERTS) & (lane < N_EXPERTS + N_GROUPS)
    glog = jnp.where(is_g, logits, ninf)
    gmax = jnp.max(glog, axis=1, keepdims=True)
    gsel = jnp.min(jnp.where(glog == gmax, lane, big), axis=1, keepdims=True) - N_EXPERTS
    gprob = 1.0 / jnp.sum(jnp.exp(glog - gmax), axis=1, keepdims=True)
    lo_l = gsel * EXPERTS_PER_GROUP
    within = jnp.where((lane >= lo_l) & (lane < lo_l + EXPERTS_PER_GROUP), logits, ninf)
    v1 = jnp.max(within, axis=1, keepdims=True)
    i1 = jnp.min(jnp.where(within == v1, lane, big), axis=1, keepdims=True)
    rest = jnp.where(lane == i1, ninf, within)
    v2 = jnp.max(rest, axis=1, keepdims=True)
    i2 = jnp.min(jnp.where(rest == v2, lane, big), axis=1, keepdims=True)
    e2 = jnp.exp(v2 - v1)
    w1 = gprob / (1.0 + e2)
    w2 = gprob * e2 / (1.0 + e2)
    route_ref[rs, :] = jnp.where(lane == 0, i1, jnp.where(lane == 1, i2,
                                 jnp.where(lane == 2, w1, jnp.where(lane == 3, w2, 0.0))))


def _mix_out(h, ys, ws, g, wr_hi, wr_lo, tm=1024):
    n = h.shape[0]
    n_y = len(ys)
    row = lambda w: pl.BlockSpec((tm, w), lambda i: (i, 0))
    return pl.pallas_call(
        functools.partial(_mix_out_kernel, n_y=n_y),
        grid=(n // tm,),
        in_specs=[row(D_MODEL)] + [row(y.shape[1]) for y in ys] + [_full(w.shape) for w in ws]
                 + [_full(g.shape), _full(wr_hi.shape), _full(wr_lo.shape)],
        out_specs=[row(D_MODEL), row(D_MODEL // 2), row(LANES)],
        out_shape=[jax.ShapeDtypeStruct((n, D_MODEL), F32), jax.ShapeDtypeStruct((n, D_MODEL // 2), I32),
                   jax.ShapeDtypeStruct((n, LANES), F32)],
        compiler_params=_cparams(("parallel",)),
        name="mix_out_router",
    )(h, *ys, *ws, g, wr_hi, wr_lo)


def _sc_rows(src, idx, scatter):
    r, d = idx.shape[0], src.shape[1]
    n_src = src.shape[0]
    nw = SC_CORES * SC_SUBCORES
    per_w = r // nw
    n_chunks = per_w // SC_CHUNK
    assert r % nw == 0 and per_w % SC_CHUNK == 0 and (not scatter or n_src % per_w == 0)
    idx3 = idx.reshape(nw, n_chunks, SC_CHUNK)
    mesh = plsc.VectorSubcoreMesh(core_axis_name="c", subcore_axis_name="s")

    row_buf = pltpu.VMEM((SC_CHUNK, d), src.dtype)

    @functools.partial(
        pl.kernel, mesh=mesh, out_type=jax.ShapeDtypeStruct((r, d), src.dtype),
        scratch_types=[pltpu.VMEM((n_chunks, SC_CHUNK), I32), row_buf, row_buf] + [pltpu.SemaphoreType.DMA] * 4)
    def permute(src_hbm, idx_hbm, out_hbm, idx_v, rows0, rows1, in0, in1, out0, out1):
        wid = lax.axis_index("s") * SC_CORES + lax.axis_index("c")
        pltpu.sync_copy(idx_hbm.at[wid], idx_v)
        bufs, in_sems, out_sems = (rows0, rows1), (in0, in1), (out0, out1)

        def linear(c, n_rows):
            start = lax.rem(wid * per_w + c * SC_CHUNK, n_rows)
            return pl.ds(pl.multiple_of(start, SC_CHUNK), SC_CHUNK)

        def load(c):
            ref = src_hbm.at[linear(c, n_src)] if scatter else src_hbm.at[idx_v.at[c]]
            return pltpu.async_copy(ref, bufs[c % 2], in_sems[c % 2])

        def store(c):
            ref = out_hbm.at[idx_v.at[c]] if scatter else out_hbm.at[linear(c, r)]
            return pltpu.async_copy(bufs[c % 2], ref, out_sems[c % 2])

        loads, stores = {0: load(0)}, {}
        for c in range(n_chunks):
            if c + 1 < n_chunks:
                if c >= 1:
                    stores[c - 1].wait()
                loads[c + 1] = load(c + 1)
            loads[c].wait()
            stores[c] = store(c)
        for c in range(max(n_chunks - 2, 0), n_chunks):
            stores[c].wait()

    return permute(src, idx3)


def _moe_kernel(tile_ref, exp_ref, flag_ref, off_ref, xs_ref, wg_ref, wu_ref, wd_ref, ys_ref,
                wgu_ref, wdb_ref, acc_ref):
    w = pl.program_id(0)
    tm = xs_ref.shape[0]
    e = exp_ref[w]
    flags = flag_ref[w]

    @pl.when(w == 0)
    def _():
        acc_ref[...] = jnp.zeros_like(acc_ref)

    @pl.when((flags & 4) != 0)
    def _():
        wgu_ref[:, :EXPERT_FF] = wg_ref[0, 0].astype(BF16)
        wgu_ref[:, EXPERT_FF:] = wu_ref[0, 0].astype(BF16)
        wdb_ref[...] = wd_ref[0, 0].astype(BF16)

    @pl.when((flags & 1) != 0)
    def _():
        xa, xb = _unpack_halves(xs_ref[...])
        xa = xa.astype(BF16)
        xb = xb.astype(BF16)
        half = D_MODEL // 2
        gate_up = _dot(xa, wgu_ref[:half, :]) + _dot(xb, wgu_ref[half:, :])
        hid = (jax.nn.silu(gate_up[:, :EXPERT_FF]) * gate_up[:, EXPERT_FF:]).astype(BF16)
        y = _dot(hid, wdb_ref[...])
        rows = tile_ref[w] * tm + lax.broadcasted_iota(I32, (tm, 1), 0)
        mine = (rows >= off_ref[e]) & (rows < off_ref[e + 1])
        y = jnp.where(mine, y, 0.0)
        acc = jnp.where((flags & 2) != 0, y, acc_ref[...] + y)
        acc_ref[...] = acc
        ys_ref[...] = _pack_halves(acc)


def _moe_plan(off, n_rows, tm):
    nt = n_rows // tm
    n_items = nt + N_EXPERTS - 1
    row_lo = jnp.arange(nt, dtype=I32) * tm
    ends = off[1:N_EXPERTS][None, :]
    e_lo = jnp.sum((ends <= row_lo[:, None]).astype(I32), axis=1)
    e_hi = jnp.sum((ends <= (row_lo + tm - 1)[:, None]).astype(I32), axis=1)
    span = e_hi - e_lo + 1
    start = jnp.cumsum(span) - span
    total = jnp.sum(span)
    w = jnp.arange(n_items, dtype=I32)
    valid = w < total
    tile = jnp.clip(jnp.sum((start[None, :] <= w[:, None]).astype(I32), axis=1) - 1, 0, nt - 1)
    expert = jnp.where(valid, e_lo[tile] + (w - start[tile]), e_hi[nt - 1])
    first = valid & (w == start[tile])
    new_e = jnp.concatenate([jnp.ones((1,), bool), expert[1:] != expert[:-1]])
    flags = valid.astype(I32) + 2 * first.astype(I32) + 4 * new_e.astype(I32)
    return tile, expert, flags


def _moe(xs, tile, expert, flags, off, wg, wu, wd, layer, tm):
    n_rows, half = xs.shape
    n_items = tile.shape[0]
    grid_spec = pltpu.PrefetchScalarGridSpec(
        num_scalar_prefetch=4,
        grid=(n_items,),
        in_specs=[pl.BlockSpec((tm, half), lambda w, t, e, f, o: (t[w], 0)),
                  pl.BlockSpec((1, 1, D_MODEL, EXPERT_FF), lambda w, t, e, f, o: (layer, e[w], 0, 0)),
                  pl.BlockSpec((1, 1, D_MODEL, EXPERT_FF), lambda w, t, e, f, o: (layer, e[w], 0, 0)),
                  pl.BlockSpec((1, 1, EXPERT_FF, D_MODEL), lambda w, t, e, f, o: (layer, e[w], 0, 0))],
        out_specs=pl.BlockSpec((tm, half), lambda w, t, e, f, o: (t[w], 0)),
        scratch_shapes=[pltpu.VMEM((D_MODEL, 2 * EXPERT_FF), BF16),
                        pltpu.VMEM((EXPERT_FF, D_MODEL), BF16), pltpu.VMEM((tm, D_MODEL), F32)])
    return pl.pallas_call(
        _moe_kernel,
        grid_spec=grid_spec,
        out_shape=jax.ShapeDtypeStruct((n_rows, half), I32),
        compiler_params=_cparams(("arbitrary",)),
        name="moe_experts",
    )(tile, expert, flags, off, xs, wg, wu, wd)


def _moe_sparse(xn_packed, route, wg, wu, wd, layer, tm=256):
    n = xn_packed.shape[0]
    eid = jnp.concatenate([route[:, 0], route[:, 1]]).astype(I32)
    _, sorted_pair = lax.sort_key_val(eid, jnp.arange(2 * n, dtype=I32))
    token = jnp.where(sorted_pair >= n, sorted_pair - n, sorted_pair)
    bounds = jnp.arange(N_EXPERTS + 1, dtype=I32)
    off = jnp.sum((eid[None, :] < bounds[:, None]).astype(I32), axis=1)
    tile, expert, flags = _moe_plan(off, 2 * n, tm)
    xs = _sc_rows(xn_packed, token, scatter=False)
    ys = _moe(xs, tile, expert, flags, off, wg, wu, wd, layer, tm)
    return _sc_rows(ys, sorted_pair, scatter=True)


def _ple_kernel(h1_ref, y0_ref, y1_ref, route_ref, p_ref, wup_ref, wgate_ref, g_ref, o_ref):
    w1 = route_ref[:, 2:3]
    w2 = route_ref[:, 3:4]
    a0, b0 = _unpack_halves(y0_ref[...])
    a1, b1 = _unpack_halves(y1_ref[...])
    moe = jnp.concatenate([w1 * a0 + w2 * a1, w1 * b0 + w2 * b1], axis=1)
    h2 = h1_ref[...] + moe
    e = _rms(_dot(p_ref[...].astype(BF16), wup_ref[...]), g_ref[...])
    gate = jax.nn.sigmoid(_dot(h2.astype(BF16), wgate_ref[...]))
    o_ref[...] = h2 + e * gate


def _ple(h1, yp, route, p_all, layer, wup, wgate, g, tm=1024):
    n = h1.shape[0]
    nt = n // tm
    row = lambda w: pl.BlockSpec((tm, w), lambda i: (i, 0))
    return pl.pallas_call(
        _ple_kernel,
        grid=(nt,),
        in_specs=[row(D_MODEL), row(D_MODEL // 2),
                  pl.BlockSpec((tm, D_MODEL // 2), lambda i: (i + nt, 0)),
                  row(LANES), pl.BlockSpec((tm, PLE_DIM), lambda i: (i + layer * nt, 0)),
                  _full(wup.shape), _full(wgate.shape), _full(g.shape)],
        out_specs=row(D_MODEL),
        out_shape=jax.ShapeDtypeStruct((n, D_MODEL), F32),
        compiler_params=_cparams(("parallel",)),
        name="ple",
    )(h1, yp, yp, route, p_all, wup, wgate, g)


def _odd_in_kernel(h_ref, g_ref, wrow_ref, wkt_ref, cos_ref, sin_ref, cost_ref, sint_ref,
                   q_ref, v_ref, gate_ref, kt_ref):
    hn = _rms(h_ref[...], g_ref[...]).astype(BF16)
    cos = cos_ref[...]
    sin = sin_ref[...]
    half = RET_QK_DIM // 2
    q = _dot(hn, wrow_ref[:, :RET_QK_WIDTH])
    for hd in range(RET_HEADS):
        sl = slice(hd * RET_QK_DIM, (hd + 1) * RET_QK_DIM)
        qh = q[:, sl]
        q_ref[:, sl] = (qh * cos + pltpu.roll(qh, half, 1) * sin).astype(BF16)
    v_ref[...] = _dot(hn, wrow_ref[:, RET_QK_WIDTH:RET_QK_WIDTH + RET_V_WIDTH]).astype(BF16)
    gate_ref[...] = _dot(hn, wrow_ref[:, RET_QK_WIDTH + RET_V_WIDTH:]).astype(BF16)

    cost = cost_ref[...] * (RET_QK_DIM ** -0.5)
    sint = sint_ref[...] * (RET_QK_DIM ** -0.5)
    kt = _dot_nt(wkt_ref[...], hn)
    for hd in range(RET_HEADS):
        sl = slice(hd * RET_QK_DIM, (hd + 1) * RET_QK_DIM)
        kh = kt[sl, :]
        swapped = jnp.concatenate([kh[half:], kh[:half]], axis=0)
        kt_ref[sl, :] = (kh * cost + swapped * sint).astype(BF16)


def _odd_in(h, g, wrow, wkt, cos, sin, cost, sint, seq, tm=512):
    n = h.shape[0]
    nt = seq // tm
    row = lambda w: pl.BlockSpec((tm, w), lambda i: (i, 0))
    return pl.pallas_call(
        _odd_in_kernel,
        grid=(n // tm,),
        in_specs=[row(D_MODEL), _full(g.shape), _full(wrow.shape), _full(wkt.shape),
                  pl.BlockSpec((tm, RET_QK_DIM), lambda i: (i % nt, 0)),
                  pl.BlockSpec((tm, RET_QK_DIM), lambda i: (i % nt, 0)),
                  pl.BlockSpec((RET_QK_DIM, tm), lambda i: (0, i % nt)),
                  pl.BlockSpec((RET_QK_DIM, tm), lambda i: (0, i % nt))],
        out_specs=[row(RET_QK_WIDTH), row(RET_V_WIDTH), row(RET_V_WIDTH),
                   pl.BlockSpec((RET_QK_WIDTH, tm), lambda i: (0, i))],
        out_shape=[jax.ShapeDtypeStruct((n, RET_QK_WIDTH), BF16), jax.ShapeDtypeStruct((n, RET_V_WIDTH), BF16),
                   jax.ShapeDtypeStruct((n, RET_V_WIDTH), BF16), jax.ShapeDtypeStruct((RET_QK_WIDTH, n), BF16)],
        compiler_params=_cparams(("parallel",)),
        name="odd_in_proj",
    )(h, g, wrow, wkt, cos, sin, cost, sint)


def _ret_kernel(cdec_ref, q_ref, kt_ref, v_ref, gate_ref, dintra_ref, qdec_ref, kdec_ref, gn_ref,
                y_ref, state_ref):
    @pl.when(pl.program_id(1) == 0)
    def _():
        state_ref[...] = jnp.zeros_like(state_ref)

    for h in range(RET_HEADS):
        ks = slice(h * RET_QK_DIM, (h + 1) * RET_QK_DIM)
        vs = slice(h * RET_V_DIM, (h + 1) * RET_V_DIM)
        qh = q_ref[:, ks]
        kth = kt_ref[ks, :]
        vh = v_ref[:, vs]
        state = state_ref[h]
        inner = (_dot(qh, kth) * dintra_ref[h]).astype(BF16)
        o = _dot(inner, vh) + _dot(qh, state.astype(BF16)) * qdec_ref[h]
        kd = (kth.astype(F32) * kdec_ref[h]).astype(BF16)
        state_ref[h] = state * cdec_ref[h] + _dot(kd, vh)
        mu = jnp.mean(o, axis=-1, keepdims=True)
        oc = o - mu
        var = jnp.mean(oc * oc, axis=-1, keepdims=True)
        on = oc * lax.rsqrt(var + GN_EPS) * gn_ref[:, vs]
        y_ref[:, vs] = (jax.nn.silu(gate_ref[:, vs].astype(F32)) * on).astype(BF16)


def _retention(cdec, q, kt, v, gate, dintra, qdec, kdec, gn, batch, seq):
    n = q.shape[0]
    c = RET_TILE
    nc = seq // c
    row = lambda w: pl.BlockSpec((c, w), lambda b, t: (b * nc + t, 0))
    return pl.pallas_call(
        _ret_kernel,
        grid=(batch, nc),
        in_specs=[pl.BlockSpec(memory_space=pltpu.SMEM),
                  row(RET_QK_WIDTH),
                  pl.BlockSpec((RET_QK_WIDTH, c), lambda b, t: (0, b * nc + t)),
                  row(RET_V_WIDTH), row(RET_V_WIDTH),
                  _full(dintra.shape), _full(qdec.shape), _full(kdec.shape), _full(gn.shape)],
        out_specs=row(RET_V_WIDTH),
        out_shape=jax.ShapeDtypeStruct((n, RET_V_WIDTH), BF16),
        scratch_shapes=[pltpu.VMEM((RET_HEADS, RET_QK_DIM, RET_V_DIM), F32)],
        compiler_params=_cparams(("parallel", "arbitrary")),
        name="retention",
    )(cdec, q, kt, v, gate, dintra, qdec, kdec, gn)


def _t5_bucket(rel):
    half = N_BUCKETS // 2
    max_exact = half // 2
    ret = (rel > 0).astype(I32) * half
    n = jnp.abs(rel)
    nf = jnp.maximum(n, 1).astype(F32)
    large = max_exact + (jnp.log(nf / max_exact) / math.log(MAX_DISTANCE / max_exact)
                         * (half - max_exact)).astype(I32)
    large = jnp.minimum(large, half - 1)
    return ret + jnp.where(n < max_exact, n, large)


def _bias_tables(rel_bias):
    c = jnp.arange(BIAS_BLOCK, dtype=I32)[:, None]
    r = jnp.arange(Q_TILE, dtype=I32)[None, :]
    rels = jnp.stack([c + (b - 1) * BIAS_BLOCK - r for b in range(Q_TILE // BIAS_BLOCK + 1)])
    onehot = jax.nn.one_hot(_t5_bucket(rels), N_BUCKETS, dtype=F32)
    tab = jnp.einsum("abcn,nh->ahbc", onehot, rel_bias, precision=lax.Precision.HIGHEST)
    far = rel_bias[_t5_bucket(jnp.int32(-2 * MAX_DISTANCE))]
    return (tab * LOG2E).astype(F32), (far * LOG2E).astype(F32)


def _transpose_bf16(w):
    wb = w.astype(BF16)
    eye = jnp.eye(w.shape[0], dtype=BF16)
    return lax.dot_general(wb, eye, (((0,), (0,)), ((), ())), preferred_element_type=BF16)


def _block_diag(w):
    nb, bs, _ = w.shape
    eye = jnp.eye(nb, dtype=w.dtype)
    return jnp.einsum("hij,hg->higj", w, eye).reshape(nb * bs, nb * bs)


def _router_weights(w_group, w_expert):
    wr = jnp.concatenate([w_expert, w_group, jnp.zeros((D_MODEL, LANES - N_EXPERTS - N_GROUPS), F32)], axis=1)
    hi = wr.astype(BF16)
    lo = (wr - hi.astype(F32)).astype(BF16)
    return hi, lo


def _rotary_tables(seq):
    half = RET_QK_DIM // 2
    inv = ROPE_BASE ** (-jnp.arange(half, dtype=F32) / half)
    ang = jnp.arange(seq, dtype=F32)[:, None] * inv[None, :]
    cos, sin = jnp.cos(ang), jnp.sin(ang)
    cos_row = jnp.concatenate([cos, cos], axis=1)
    sin_row = jnp.concatenate([-sin, sin], axis=1)
    cos_t = jnp.concatenate([cos.T, cos.T], axis=0)
    sin_t = jnp.concatenate([-sin.T, sin.T], axis=0)
    return cos_row, sin_row, cos_t, sin_t


def _retention_tables():
    c = RET_TILE
    log_g = jnp.log(1.0 - 2.0 ** (-5.0 - jnp.arange(RET_HEADS, dtype=F32)))
    pos = jnp.arange(c, dtype=F32)
    diff = pos[:, None] - pos[None, :]
    causal = diff >= 0
    dintra = jnp.where(causal[None], jnp.exp(jnp.where(causal, diff, 0.0)[None] * log_g[:, None, None]), 0.0)
    qdec = jnp.exp((pos + 1.0)[None, :, None] * log_g[:, None, None])
    kdec = jnp.exp((c - 1.0 - pos)[None, None, :] * log_g[:, None, None])
    cdec = jnp.exp(c * log_g)
    return dintra, qdec, kdec, cdec


def kernel(x, p, rel_bias, mix_norm_g, ffn_norm_g, ple_norm_g, ev_w_in, ev_conv_w, ev_conv_b, ev_lru_wa, ev_lru_ba, ev_lru_wx, ev_lru_bx, ev_lru_lambda, ev_q_norm_g, ev_k_norm_g, ev_w_out, od_w_in, od_gn_g, od_w_out, moe_w_group, moe_w_expert, moe_w_gate, moe_w_up, moe_w_down, ple_w_up, ple_w_gate):
    batch, seq, _ = x.shape
    n = batch * seq
    depth = p.shape[0]
    h = x.reshape(n, D_MODEL)
    row = lambda a: a.reshape(1, -1)

    tab, far = _bias_tables(rel_bias)
    cos_row, sin_row, cos_t, sin_t = _rotary_tables(seq)
    dintra, qdec, kdec, cdec = _retention_tables()
    head_ones = _block_diag(jnp.ones((ATT_HEADS, ATT_HEAD_DIM, ATT_HEAD_DIM), BF16))

    for i in range(depth):
        jdx = i // 2
        if i % 2 == 0:
            w = ev_w_in[jdx]
            widths = (LRU_WIDTH, LRU_WIDTH, ATT_WIDTH, ATT_WIDTH, ATT_WIDTH, IDX_HEADS * IDX_DIM, IDX_DIM, IDX_HEADS)
            o = [sum(widths[:t]) for t in range(len(widths) + 1)]
            xa_w, ga_w, q_w, k_w, v_w, iq_w, ik_w, iw_w = [w[:, o[t]:o[t + 1]] for t in range(8)]
            wrow = jnp.concatenate([xa_w, ga_w, k_w, ik_w, ik_w], axis=1).astype(BF16)
            wt = _transpose_bf16(jnp.concatenate(
                [q_w, iq_w, v_w, iw_w, jnp.zeros((D_MODEL, 16 - IDX_HEADS), F32)], axis=1))
            qg = ev_q_norm_g[jdx].reshape(-1, 1) * (ATT_HEAD_DIM ** -0.5 * LOG2E)
            kg = jnp.tile(ev_k_norm_g[jdx], ATT_HEADS).reshape(1, -1)
            xg, k, ik, qt, iqt, vt, iwt = _even_in(h, row(mix_norm_g[i]), wrow, wt, head_ones, qg, kg)
            ya = _rglru(xg, ev_conv_w[jdx], row(ev_conv_b[jdx]),
                        _block_diag(ev_lru_wa[jdx]).astype(BF16), row(ev_lru_ba[jdx]),
                        _block_diag(ev_lru_wx[jdx]).astype(BF16), row(ev_lru_bx[jdx]),
                        row(ev_lru_lambda[jdx]), batch, seq)
            yb = _dsa(far, qt, iqt, iwt, k, ik, vt, tab, batch, seq)
            wo = ev_w_out[jdx].astype(BF16)
            ys, ws = [ya, yb], [wo[:LRU_WIDTH], wo[LRU_WIDTH:]]
        else:
            w = od_w_in[jdx]
            wrow = jnp.concatenate([w[:, :RET_QK_WIDTH], w[:, 2 * RET_QK_WIDTH:]], axis=1).astype(BF16)
            wkt = _transpose_bf16(w[:, RET_QK_WIDTH:2 * RET_QK_WIDTH])
            q, v, gate, kt = _odd_in(h, row(mix_norm_g[i]), wrow, wkt, cos_row, sin_row, cos_t, sin_t, seq)
            yc = _retention(cdec, q, kt, v, gate, dintra, qdec, kdec, row(od_gn_g[jdx]), batch, seq)
            ys, ws = [yc], [od_w_out[jdx].astype(BF16)]
        wr_hi, wr_lo = _router_weights(moe_w_group[i], moe_w_expert[i])
        h1, xn, route = _mix_out(h, ys, ws, row(ffn_norm_g[i]), wr_hi, wr_lo)
        yp = _moe_sparse(xn, route, moe_w_gate, moe_w_up, moe_w_down, i)
        h = _ple(h1, yp, route, p.reshape(depth * n, PLE_DIM), i, ple_w_up[i].astype(BF16),
                 ple_w_gate[i].astype(BF16), row(ple_norm_g[i]))
    return h.reshape(batch, seq, D_MODEL)
```

```python
import functools
import math

import jax
import jax.numpy as jnp
from jax import lax
from jax.experimental import pallas as pl
from jax.experimental.pallas import tpu as pltpu
from jax.experimental.pallas import tpu_sc as plsc

F32 = jnp.float32
BF16 = jnp.bfloat16
I32 = jnp.int32

D_MODEL = 1024
CHUNK = 64
PLE_DIM = 256
RMS_EPS = 1e-6

LRU_WIDTH = 512
LRU_BLOCKS = 8
LRU_C = 8.0

ATT_HEADS = 8
ATT_HEAD_DIM = 64
ATT_WIDTH = 512
IDX_HEADS = 8
IDX_DIM = 64
MAX_TOP_K = 256
N_BUCKETS = 32
MAX_DISTANCE = 128

RET_HEADS = 8
RET_QK_DIM = 128
RET_V_DIM = 256
RET_QK_WIDTH = 1024
RET_V_WIDTH = 2048
ROPE_BASE = 10000.0
GN_EPS = 1e-5

N_GROUPS = 4
EXPERTS_PER_GROUP = 8
N_EXPERTS = 32
EXPERT_FF = 512

LANES = 128
INT_MIN = -(2 ** 31)
NEG_BIG = -1e30
LOG2E = 1.4426950408889634
VMEM_LIMIT = 56 * 1024 * 1024

SC_CORES = 2
SC_SUBCORES = 16
SC_CHUNK = 64

MIX_SUB = 256
Q_TILE = 128
BIAS_BLOCK = 128
K_TILE = 512
RET_TILE = 256


def _cparams(sem):
    return pltpu.CompilerParams(dimension_semantics=sem, vmem_limit_bytes=VMEM_LIMIT)


def _full(shape):
    nd = len(shape)
    return pl.BlockSpec(shape, lambda *_: (0,) * nd)


def _rms(xf, g):
    return xf * lax.rsqrt(jnp.mean(xf * xf, axis=-1, keepdims=True) + RMS_EPS) * g


def _dot(a, b):
    return jnp.dot(a, b, preferred_element_type=F32)


def _dot_nt(a, b):
    return lax.dot_general(a, b, (((1,), (1,)), ((), ())), preferred_element_type=F32)


def _neg_expm1(x):
    series = -x * (1.0 + x * (0.5 + x * (1.0 / 6.0 + x * (1.0 / 24.0))))
    return jnp.where(x > -0.03, series, 1.0 - jnp.exp(x))


def _pack_halves(x):
    w = x.shape[1] // 2
    hi = pltpu.bitcast(x[:, :w].astype(BF16).astype(F32), I32)
    lo = pltpu.bitcast(x[:, w:].astype(BF16).astype(F32), I32)
    return hi | lax.shift_right_logical(lo, 16)


def _unpack_halves(p):
    hi = pltpu.bitcast(p & jnp.int32(-65536), F32)
    lo = pltpu.bitcast(lax.shift_left(p, 16), F32)
    return hi, lo


def _split_bf16(x):
    hi = x.astype(BF16)
    lo = (x - hi.astype(F32)).astype(BF16)
    return hi, lo


def _even_in_kernel(h_ref, g_ref, wrow_ref, wt_ref, bd_ref, qg_ref, kg_ref,
                    xg_ref, k_ref, ik_ref, qt_ref, iqt_ref, vt_ref, iwt_ref):
    tm = h_ref.shape[0]
    w = ATT_WIDTH
    hn = _rms(h_ref[...], g_ref[...]).astype(BF16)

    xg_ref[...] = _dot(hn, wrow_ref[:, :2 * LRU_WIDTH])
    kf = _dot(hn, wrow_ref[:, 2 * LRU_WIDTH:2 * LRU_WIDTH + w])
    hi, lo = _split_bf16(kf * kf)
    ss = _dot(hi, bd_ref[...]) + _dot(lo, bd_ref[...])
    k_ref[...] = (kf * lax.rsqrt(ss * (1.0 / ATT_HEAD_DIM) + RMS_EPS) * kg_ref[...]).astype(BF16)
    ik_ref[...] = _dot(hn, wrow_ref[:, 2 * LRU_WIDTH + w:]).astype(BF16)

    feat = _dot_nt(wt_ref[...], hn)
    q3 = feat[0:w].reshape(ATT_HEADS, ATT_HEAD_DIM, tm)
    ssq = jnp.sum(q3 * q3, axis=1, keepdims=True)
    qn = q3 * lax.rsqrt(ssq * (1.0 / ATT_HEAD_DIM) + RMS_EPS) * qg_ref[...][None]
    qt_ref[...] = qn.reshape(w, tm).astype(BF16)
    iqt_ref[...] = feat[w:2 * w].astype(BF16)
    vt_ref[...] = feat[2 * w:3 * w].astype(BF16)
    iwt_ref[...] = feat[3 * w:3 * w + IDX_HEADS] * (IDX_HEADS ** -0.5 * IDX_DIM ** -0.5)


def _even_in(h, g, wrow, wt, bd, qg, kg, tm=512):
    n = h.shape[0]
    row = lambda w: pl.BlockSpec((tm, w), lambda i: (i, 0))
    col = lambda r: pl.BlockSpec((r, tm), lambda i: (0, i))
    return pl.pallas_call(
        _even_in_kernel,
        grid=(n // tm,),
        in_specs=[row(D_MODEL), _full(g.shape), _full(wrow.shape), _full(wt.shape),
                  _full(bd.shape), _full(qg.shape), _full(kg.shape)],
        out_specs=[row(2 * LRU_WIDTH), row(ATT_WIDTH), row(LANES), col(ATT_WIDTH), col(ATT_WIDTH),
                   col(ATT_WIDTH), col(IDX_HEADS)],
        out_shape=[jax.ShapeDtypeStruct((n, 2 * LRU_WIDTH), F32), jax.ShapeDtypeStruct((n, ATT_WIDTH), BF16),
                   jax.ShapeDtypeStruct((n, LANES), BF16), jax.ShapeDtypeStruct((ATT_WIDTH, n), BF16),
                   jax.ShapeDtypeStruct((ATT_WIDTH, n), BF16), jax.ShapeDtypeStruct((ATT_WIDTH, n), BF16),
                   jax.ShapeDtypeStruct((IDX_HEADS, n), F32)],
        compiler_params=_cparams(("parallel",)),
        name="even_in_proj",
    )(h, g, wrow, wt, bd, qg, kg)


def _rglru_kernel(xg_ref, cw_ref, cb_ref, wa_ref, ba_ref, wx_ref, bx_ref, lam_ref, ya_ref,
                  ext_ref, hst_ref):
    ts = xg_ref.shape[0]
    w = LRU_WIDTH

    @pl.when(pl.program_id(1) == 0)
    def _():
        ext_ref[0:8, :] = jnp.zeros((8, w), F32)
        hst_ref[...] = jnp.zeros_like(hst_ref)

    xa = xg_ref[:, :w]
    ga = xg_ref[:, w:]
    ext_ref[8:, :] = xa
    xc = xa * cw_ref[3:4, :] + cb_ref[...]
    for d in (1, 2, 3):
        xc = xc + ext_ref[8 - d:8 - d + ts, :] * cw_ref[3 - d:4 - d, :]
    ext_ref[0:8, :] = xa[ts - 8:, :]

    xcb = xc.astype(BF16)
    r = jax.nn.sigmoid(_dot(xcb, wa_ref[...]) + ba_ref[...])
    gi = jax.nn.sigmoid(_dot(xcb, wx_ref[...]) + bx_ref[...])
    nl = -lam_ref[...]
    softplus = jnp.maximum(nl, 0.0) + jnp.log1p(jnp.exp(-jnp.abs(nl)))
    log_a = (-LRU_C) * r * softplus
    a = jnp.exp(log_a)
    u = jnp.sqrt(_neg_expm1(2.0 * log_a)) * (gi * xc)

    groups = ts // 8
    a3 = a.reshape(groups, 8, w)
    u3 = u.reshape(groups, 8, w)
    pos = lax.broadcasted_iota(I32, (groups, 8, w), 1)
    for d in (1, 2, 4):
        keep = pos >= d
        a_sh = jnp.where(keep, pltpu.roll(a3, d, 1), 1.0)
        u_sh = jnp.where(keep, pltpu.roll(u3, d, 1), 0.0)
        u3 = a3 * u_sh + u3
        a3 = a3 * a_sh
    carry = hst_ref[0:1, :]
    hs = []
    for g in range(groups):
        hg = a3[g] * carry + u3[g]
        carry = hg[7:8, :]
        hs.append(hg)
    hst_ref[0:1, :] = carry
    hseq = jnp.concatenate(hs, axis=0)
    ya_ref[...] = (hseq * jax.nn.gelu(ga)).astype(BF16)


def _rglru(xg, cw, cb, wa, ba, wx, bx, lam, batch, seq, ts=512):
    n = xg.shape[0]
    nt = seq // ts
    return pl.pallas_call(
        _rglru_kernel,
        grid=(batch, nt),
        in_specs=[pl.BlockSpec((ts, 2 * LRU_WIDTH), lambda b, t: (b * nt + t, 0)),
                  _full(cw.shape), _full(cb.shape), _full(wa.shape), _full(ba.shape),
                  _full(wx.shape), _full(bx.shape), _full(lam.shape)],
        out_specs=pl.BlockSpec((ts, LRU_WIDTH), lambda b, t: (b * nt + t, 0)),
        out_shape=jax.ShapeDtypeStruct((n, LRU_WIDTH), BF16),
        scratch_shapes=[pltpu.VMEM((ts + 8, LRU_WIDTH), F32), pltpu.VMEM((8, LRU_WIDTH), F32)],
        compiler_params=_cparams(("parallel", "arbitrary")),
        name="rglru",
    )(xg, cw, cb, wa, ba, wx, bx, lam)


def _dsa_kernel(far_ref, qt_ref, iqt_ref, iwt_ref, k_ref, ik_ref, vt_ref, tab_ref, o_ref,
                keys_ref, sel_ref, cut_ref, seln_ref, *, seq, top_k):
    tq = Q_TILE
    hd = ATT_HEAD_DIM
    j = pl.program_id(1)
    q0 = j * tq
    nkt = (q0 + tq + K_TILE - 1) // K_TILE
    qcol = q0 + lax.broadcasted_iota(I32, (1, tq), 1)
    qlim = (qcol // CHUNK + 1) * CHUNK
    zero_rows = jnp.zeros((hd, tq), BF16)

    def head_rows(ref, h):
        blk = ref[h * hd:(h + 1) * hd, :]
        return jnp.concatenate([blk, zero_rows] if h % 2 == 0 else [zero_rows, blk], axis=0)

    def ktile(kt):
        return pl.multiple_of(kt * K_TILE, K_TILE)

    def key_pos(off, rows):
        return off + lax.broadcasted_iota(I32, (rows, tq), 0)

    iq_m = [head_rows(iqt_ref, h) for h in range(IDX_HEADS)]

    def score_tile(kt, mask_inadmissible):
        off = ktile(kt)
        ik = ik_ref[pl.ds(off, K_TILE), :]
        s = jnp.zeros((K_TILE, tq), F32)
        for h in range(IDX_HEADS):
            s = s + jnp.maximum(_dot(ik, iq_m[h]), 0.0) * iwt_ref[h:h + 1, :]
        bits = pltpu.bitcast(s, I32)
        neg = bits >> 31
        key = (bits ^ (neg & 0x7FFFFFFF)) - neg
        if mask_inadmissible:
            key = jnp.where(key_pos(off, K_TILE) < qlim, key, INT_MIN)
        keys_ref[pl.ds(off, K_TILE), :] = key

    lax.fori_loop(0, nkt - 1, lambda kt, c: (score_tile(kt, False), c)[1], 0)
    score_tile(nkt - 1, True)

    def count(pred):
        def body(kt, acc):
            off = ktile(kt)
            ind = pred(keys_ref[pl.ds(off, K_TILE), :], key_pos(off, K_TILE)).reshape(K_TILE // 8, 8, tq)
            while ind.shape[0] > 1:
                half = ind.shape[0] // 2
                ind = ind[:half] + ind[half:]
            return acc + ind[0]
        acc = lax.fori_loop(0, nkt, body, jnp.zeros((8, tq), F32))
        return jnp.sum(acc, axis=0, keepdims=True)

    kf = float(top_k)

    def search_body(i, carry):
        ans, cnt_ans = carry
        cand = ans + jnp.left_shift(jnp.int32(1), 31 - i)
        cnt = count(lambda kk, kpos: jnp.where(kk >= cand, 1.0, 0.0))
        take = cnt >= kf
        return jnp.where(take, cand, ans), jnp.where(take, cnt, cnt_ans)

    covered = (nkt * K_TILE).astype(F32)
    ans, cnt_ge = lax.fori_loop(0, 32, search_body,
                                (jnp.full((1, tq), INT_MIN, I32), jnp.full((1, tq), covered, F32)))

    excess = jnp.where(ans > INT_MIN, cnt_ge - kf, 0.0)
    has_ties = jnp.max(excess) > 0.0

    @pl.when(has_ties)
    def _():
        need = kf - count(lambda kk, kpos: jnp.where(kk > ans, 1.0, 0.0))

        def idx_body(i, pos):
            cand = pos + jnp.left_shift(jnp.int32(1), (seq.bit_length() - 2) - i)
            c = count(lambda kk, kpos: jnp.where(kk == ans, jnp.where(kpos < cand, 1.0, 0.0), 0.0))
            return jnp.where(c < need, cand, pos)

        pos = lax.fori_loop(0, seq.bit_length() - 1, idx_body, jnp.zeros((1, tq), I32))
        cut_ref[...] = jnp.broadcast_to(pos, cut_ref.shape)

    thr = jnp.maximum(ans, INT_MIN + 1)

    near_w = tq + BIAS_BLOCK
    near_start = pl.multiple_of(jnp.maximum(q0 - BIAS_BLOCK, 0), BIAS_BLOCK)
    n_far = (near_start + K_TILE - 1) // K_TILE

    def write_selection(selection):
        def sel_body(kt, c):
            off = ktile(kt)
            kpos = key_pos(off, K_TILE)
            s = selection(keys_ref[pl.ds(off, K_TILE), :], kpos)
            sel_ref[pl.ds(off, K_TILE), :] = jnp.where(kpos < near_start, s, NEG_BIG)
            return c
        lax.fori_loop(0, n_far, sel_body, 0)
        seln_ref[...] = selection(keys_ref[pl.ds(near_start, near_w), :], key_pos(near_start, near_w))

    @pl.when(has_ties)
    def _():
        cut = cut_ref[0:1, :]
        write_selection(lambda kk, kpos: jnp.where(
            kk > thr, 0.0, jnp.where(kk == thr, jnp.where(kpos <= cut, 0.0, NEG_BIG), NEG_BIG)))

    @pl.when(jnp.logical_not(has_ties))
    def _():
        write_selection(lambda kk, kpos: jnp.where(kk >= thr, 0.0, NEG_BIG))

    sel_near = seln_ref[...]

    def step(qm, kb, vte, add, carry):
        m, acc = carry
        s = _dot(kb, qm) + add
        m_new = jnp.maximum(m, jnp.max(s, axis=0, keepdims=True))
        p = jnp.exp2(s - m_new).astype(BF16)
        acc = jnp.exp2(m - m_new) * acc + _dot(vte, p)
        return m_new, acc

    qms = [head_rows(qt_ref, h) for h in range(ATT_HEADS)]
    pairs = [slice((h // 2) * LANES, (h // 2 + 1) * LANES) for h in range(ATT_HEADS)]

    def values(h, off, width):
        return jnp.concatenate([vt_ref[h * hd:(h + 1) * hd, pl.ds(off, width)],
                                jnp.ones((hd, width), BF16)], axis=0)

    def far_body(kt, carries):
        off = ktile(kt)
        sel = sel_ref[pl.ds(off, K_TILE), :]
        return tuple(step(qms[h], k_ref[pl.ds(off, K_TILE), pairs[h]], values(h, off, K_TILE), sel, carries[h])
                     for h in range(ATT_HEADS))

    init = (jnp.full((1, tq), NEG_BIG, F32), jnp.zeros((2 * hd, tq), F32))
    carries = lax.fori_loop(0, n_far, far_body, (init,) * ATT_HEADS)

    n_blocks = near_w // BIAS_BLOCK
    first = jnp.where(j == 0, 1, 0)
    res = []
    for h in range(ATT_HEADS):
        m, acc = carries[h]
        bias = jnp.concatenate([tab_ref[jnp.minimum(first + b, n_blocks - 1), h] for b in range(n_blocks)],
                               axis=0)
        m, acc = step(qms[h], k_ref[pl.ds(near_start, near_w), pairs[h]], values(h, near_start, near_w),
                      sel_near + bias, (m + far_ref[h], acc))
        res.append(acc[:hd, :] / acc[hd:hd + 1, :])

    for p in range(ATT_HEADS // 2):
        pair_t = jnp.concatenate([res[2 * p], res[2 * p + 1]], axis=0)
        o_ref[:, p * LANES:(p + 1) * LANES] = pair_t.T.astype(BF16)


def _dsa(far, qt, iqt, iwt, k, ik, vt, tab, batch, seq):
    n = k.shape[0]
    nb = seq // Q_TILE
    top_k = min(MAX_TOP_K, seq // 4)
    qspec = lambda r: pl.BlockSpec((r, Q_TILE), lambda b, j: (0, b * nb + j))
    return pl.pallas_call(
        functools.partial(_dsa_kernel, seq=seq, top_k=top_k),
        grid=(batch, nb),
        in_specs=[pl.BlockSpec(memory_space=pltpu.SMEM),
                  qspec(ATT_WIDTH), qspec(ATT_WIDTH), qspec(IDX_HEADS),
                  pl.BlockSpec((seq, ATT_WIDTH), lambda b, j: (b, 0)),
                  pl.BlockSpec((seq, LANES), lambda b, j: (b, 0)),
                  pl.BlockSpec((ATT_WIDTH, seq), lambda b, j: (0, b)),
                  _full(tab.shape)],
        out_specs=pl.BlockSpec((Q_TILE, ATT_WIDTH), lambda b, j: (b * nb + j, 0)),
        out_shape=jax.ShapeDtypeStruct((n, ATT_WIDTH), BF16),
        scratch_shapes=[pltpu.VMEM((seq, Q_TILE), I32), pltpu.VMEM((seq, Q_TILE), F32),
                        pltpu.VMEM((8, Q_TILE), I32), pltpu.VMEM((Q_TILE + BIAS_BLOCK, Q_TILE), F32)],
        compiler_params=_cparams(("parallel", "arbitrary")),
        name="dsa",
    )(far, qt, iqt, iwt, k, ik, vt, tab)


def _mix_out_kernel(*refs, n_y):
    h_ref = refs[0]
    y_refs = refs[1:1 + n_y]
    w_refs = refs[1 + n_y:1 + 2 * n_y]
    g_ref, wr_hi_ref, wr_lo_ref = refs[1 + 2 * n_y:4 + 2 * n_y]
    h1_ref, xn_ref, route_ref = refs[4 + 2 * n_y:]
    for sub in range(h_ref.shape[0] // MIX_SUB):
        rs = slice(sub * MIX_SUB, (sub + 1) * MIX_SUB)
        _mix_out_rows(rs, h_ref, y_refs, w_refs, g_ref, wr_hi_ref, wr_lo_ref, h1_ref, xn_ref, route_ref)


def _mix_out_rows(rs, h_ref, y_refs, w_refs, g_ref, wr_hi_ref, wr_lo_ref, h1_ref, xn_ref, route_ref):
    tm = rs.stop - rs.start
    mix = _dot(y_refs[0][rs, :], w_refs[0][...])
    for y_ref, w_ref in zip(y_refs[1:], w_refs[1:]):
        mix = mix + _dot(y_ref[rs, :], w_ref[...])
    h1 = h_ref[rs, :] + mix
    h1_ref[rs, :] = h1
    xn = _rms(h1, g_ref[...])
    hi, lo = _split_bf16(xn)
    xn_ref[rs, :] = _pack_halves(xn)
    logits = _dot(hi, wr_hi_ref[...]) + _dot(lo, wr_hi_ref[...]) + _dot(hi, wr_lo_ref[...])

    lane = lax.broadcasted_iota(I32, (tm, LANES), 1).astype(F32)
    big = float(LANES)
    ninf = -jnp.inf
    is_g = (lane >= N_EXPERTS) & (lane < N_EXPERTS + N_GROUPS)
    glog = jnp.where(is_g, logits, ninf)
    gmax = jnp.max(glog, axis=1, keepdims=True)
    gsel = jnp.min(jnp.where(glog == gmax, lane, big), axis=1, keepdims=True) - N_EXPERTS
    gprob = 1.0 / jnp.sum(jnp.exp(glog - gmax), axis=1, keepdims=True)
    lo_l = gsel * EXPERTS_PER_GROUP
    within = jnp.where((lane >= lo_l) & (lane < lo_l + EXPERTS_PER_GROUP), logits, ninf)
    v1 = jnp.max(within, axis=1, keepdims=True)
    i1 = jnp.min(jnp.where(within == v1, lane, big), axis=1, keepdims=True)
    rest = jnp.where(lane == i1, ninf, within)
    v2 = jnp.max(rest, axis=1, keepdims=True)
    i2 = jnp.min(jnp.where(rest == v2, lane, big), axis=1, keepdims=True)
    e2 = jnp.exp(v2 - v1)
    w1 = gprob / (1.0 + e2)
    w2 = gprob * e2 / (1.0 + e2)
    route_ref[rs, :] = jnp.where(lane == 0, i1, jnp.where(lane == 1, i2,
                                 jnp.where(lane == 2, w1, jnp.where(lane == 3, w2, 0.0))))


def _mix_out(h, ys, ws, g, wr_hi, wr_lo, tm=1024):
    n = h.shape[0]
    n_y = len(ys)
    row = lambda w: pl.BlockSpec((tm, w), lambda i: (i, 0))
    return pl.pallas_call(
        functools.partial(_mix_out_kernel, n_y=n_y),
        grid=(n // tm,),
        in_specs=[row(D_MODEL)] + [row(y.shape[1]) for y in ys] + [_full(w.shape) for w in ws]
                 + [_full(g.shape), _full(wr_hi.shape), _full(wr_lo.shape)],
        out_specs=[row(D_MODEL), row(D_MODEL // 2), row(LANES)],
        out_shape=[jax.ShapeDtypeStruct((n, D_MODEL), F32), jax.ShapeDtypeStruct((n, D_MODEL // 2), I32),
                   jax.ShapeDtypeStruct((n, LANES), F32)],
        compiler_params=_cparams(("parallel",)),
        name="mix_out_router",
    )(h, *ys, *ws, g, wr_hi, wr_lo)


def _sc_rows(src, idx, scatter):
    r, d = idx.shape[0], src.shape[1]
    n_src = src.shape[0]
    nw = SC_CORES * SC_SUBCORES
    per_w = r // nw
    n_chunks = per_w // SC_CHUNK
    assert r % nw == 0 and per_w % SC_CHUNK == 0 and (not scatter or n_src % per_w == 0)
    idx3 = idx.reshape(nw, n_chunks, SC_CHUNK)
    mesh = plsc.VectorSubcoreMesh(core_axis_name="c", subcore_axis_name="s")

    row_buf = pltpu.VMEM((SC_CHUNK, d), src.dtype)

    @functools.partial(
        pl.kernel, mesh=mesh, out_type=jax.ShapeDtypeStruct((r, d), src.dtype),
        scratch_types=[pltpu.VMEM((n_chunks, SC_CHUNK), I32), row_buf, row_buf] + [pltpu.SemaphoreType.DMA] * 4)
    def permute(src_hbm, idx_hbm, out_hbm, idx_v, rows0, rows1, in0, in1, out0, out1):
        wid = lax.axis_index("s") * SC_CORES + lax.axis_index("c")
        pltpu.sync_copy(idx_hbm.at[wid], idx_v)
        bufs, in_sems, out_sems = (rows0, rows1), (in0, in1), (out0, out1)

        def linear(c, n_rows):
            start = lax.rem(wid * per_w + c * SC_CHUNK, n_rows)
            return pl.ds(pl.multiple_of(start, SC_CHUNK), SC_CHUNK)

        def load(c):
            ref = src_hbm.at[linear(c, n_src)] if scatter else src_hbm.at[idx_v.at[c]]
            return pltpu.async_copy(ref, bufs[c % 2], in_sems[c % 2])

        def store(c):
            ref = out_hbm.at[idx_v.at[c]] if scatter else out_hbm.at[linear(c, r)]
            return pltpu.async_copy(bufs[c % 2], ref, out_sems[c % 2])

        loads, stores = {0: load(0)}, {}
        for c in range(n_chunks):
            if c + 1 < n_chunks:
                if c >= 1:
                    stores[c - 1].wait()
                loads[c + 1] = load(c + 1)
            loads[c].wait()
            stores[c] = store(c)
        for c in range(max(n_chunks - 2, 0), n_chunks):
            stores[c].wait()

    return permute(src, idx3)


def _moe_kernel(tile_ref, exp_ref, flag_ref, off_ref, xs_ref, wg_ref, wu_ref, wd_ref, ys_ref,
                wgu_ref, wdb_ref, acc_ref):
    w = pl.program_id(0)
    tm = xs_ref.shape[0]
    e = exp_ref[w]
    flags = flag_ref[w]

    @pl.when(w == 0)
    def _():
        acc_ref[...] = jnp.zeros_like(acc_ref)

    @pl.when((flags & 4) != 0)
    def _():
        wgu_ref[:, :EXPERT_FF] = wg_ref[0, 0].astype(BF16)
        wgu_ref[:, EXPERT_FF:] = wu_ref[0, 0].astype(BF16)
        wdb_ref[...] = wd_ref[0, 0].astype(BF16)

    @pl.when((flags & 1) != 0)
    def _():
        xa, xb = _unpack_halves(xs_ref[...])
        xa = xa.astype(BF16)
        xb = xb.astype(BF16)
        half = D_MODEL // 2
        gate_up = _dot(xa, wgu_ref[:half, :]) + _dot(xb, wgu_ref[half:, :])
        hid = (jax.nn.silu(gate_up[:, :EXPERT_FF]) * gate_up[:, EXPERT_FF:]).astype(BF16)
        y = _dot(hid, wdb_ref[...])
        rows = tile_ref[w] * tm + lax.broadcasted_iota(I32, (tm, 1), 0)
        mine = (rows >= off_ref[e]) & (rows < off_ref[e + 1])
        y = jnp.where(mine, y, 0.0)
        acc = jnp.where((flags & 2) != 0, y, acc_ref[...] + y)
        acc_ref[...] = acc
        ys_ref[...] = _pack_halves(acc)


def _moe_plan(off, n_rows, tm):
    nt = n_rows // tm
    n_items = nt + N_EXPERTS - 1
    row_lo = jnp.arange(nt, dtype=I32) * tm
    ends = off[1:N_EXPERTS][None, :]
    e_lo = jnp.sum((ends <= row_lo[:, None]).astype(I32), axis=1)
    e_hi = jnp.sum((ends <= (row_lo + tm - 1)[:, None]).astype(I32), axis=1)
    span = e_hi - e_lo + 1
    start = jnp.cumsum(span) - span
    total = jnp.sum(span)
    w = jnp.arange(n_items, dtype=I32)
    valid = w < total
    tile = jnp.clip(jnp.sum((start[None, :] <= w[:, None]).astype(I32), axis=1) - 1, 0, nt - 1)
    expert = jnp.where(valid, e_lo[tile] + (w - start[tile]), e_hi[nt - 1])
    first = valid & (w == start[tile])
    new_e = jnp.concatenate([jnp.ones((1,), bool), expert[1:] != expert[:-1]])
    flags = valid.astype(I32) + 2 * first.astype(I32) + 4 * new_e.astype(I32)
    return tile, expert, flags


def _moe(xs, tile, expert, flags, off, wg, wu, wd, layer, tm):
    n_rows, half = xs.shape
    n_items = tile.shape[0]
    grid_spec = pltpu.PrefetchScalarGridSpec(
        num_scalar_prefetch=4,
        grid=(n_items,),
        in_specs=[pl.BlockSpec((tm, half), lambda w, t, e, f, o: (t[w], 0)),
                  pl.BlockSpec((1, 1, D_MODEL, EXPERT_FF), lambda w, t, e, f, o: (layer, e[w], 0, 0)),
                  pl.BlockSpec((1, 1, D_MODEL, EXPERT_FF), lambda w, t, e, f, o: (layer, e[w], 0, 0)),
                  pl.BlockSpec((1, 1, EXPERT_FF, D_MODEL), lambda w, t, e, f, o: (layer, e[w], 0, 0))],
        out_specs=pl.BlockSpec((tm, half), lambda w, t, e, f, o: (t[w], 0)),
        scratch_shapes=[pltpu.VMEM((D_MODEL, 2 * EXPERT_FF), BF16),
                        pltpu.VMEM((EXPERT_FF, D_MODEL), BF16), pltpu.VMEM((tm, D_MODEL), F32)])
    return pl.pallas_call(
        _moe_kernel,
        grid_spec=grid_spec,
        out_shape=jax.ShapeDtypeStruct((n_rows, half), I32),
        compiler_params=_cparams(("arbitrary",)),
        name="moe_experts",
    )(tile, expert, flags, off, xs, wg, wu, wd)


def _moe_sparse(xn_packed, route, wg, wu, wd, layer, tm=256):
    n = xn_packed.shape[0]
    eid = jnp.concatenate([route[:, 0], route[:, 1]]).astype(I32)
    _, sorted_pair = lax.sort_key_val(eid, jnp.arange(2 * n, dtype=I32))
    token = jnp.where(sorted_pair >= n, sorted_pair - n, sorted_pair)
    bounds = jnp.arange(N_EXPERTS + 1, dtype=I32)
    off = jnp.sum((eid[None, :] < bounds[:, None]).astype(I32), axis=1)
    tile, expert, flags = _moe_plan(off, 2 * n, tm)
    xs = _sc_rows(xn_packed, token, scatter=False)
    ys = _moe(xs, tile, expert, flags, off, wg, wu, wd, layer, tm)
    return _sc_rows(ys, sorted_pair, scatter=True)


def _ple_kernel(h1_ref, y0_ref, y1_ref, route_ref, p_ref, wup_ref, wgate_ref, g_ref, o_ref):
    w1 = route_ref[:, 2:3]
    w2 = route_ref[:, 3:4]
    a0, b0 = _unpack_halves(y0_ref[...])
    a1, b1 = _unpack_halves(y1_ref[...])
    moe = jnp.concatenate([w1 * a0 + w2 * a1, w1 * b0 + w2 * b1], axis=1)
    h2 = h1_ref[...] + moe
    e = _rms(_dot(p_ref[...].astype(BF16), wup_ref[...]), g_ref[...])
    gate = jax.nn.sigmoid(_dot(h2.astype(BF16), wgate_ref[...]))
    o_ref[...] = h2 + e * gate


def _ple(h1, yp, route, p_all, layer, wup, wgate, g, tm=1024):
    n = h1.shape[0]
    nt = n // tm
    row = lambda w: pl.BlockSpec((tm, w), lambda i: (i, 0))
    return pl.pallas_call(
        _ple_kernel,
        grid=(nt,),
        in_specs=[row(D_MODEL), row(D_MODEL // 2),
                  pl.BlockSpec((tm, D_MODEL // 2), lambda i: (i + nt, 0)),
                  row(LANES), pl.BlockSpec((tm, PLE_DIM), lambda i: (i + layer * nt, 0)),
                  _full(wup.shape), _full(wgate.shape), _full(g.shape)],
        out_specs=row(D_MODEL),
        out_shape=jax.ShapeDtypeStruct((n, D_MODEL), F32),
        compiler_params=_cparams(("parallel",)),
        name="ple",
    )(h1, yp, yp, route, p_all, wup, wgate, g)


def _odd_in_kernel(h_ref, g_ref, wrow_ref, wkt_ref, cos_ref, sin_ref, cost_ref, sint_ref,
                   q_ref, v_ref, gate_ref, kt_ref):
    hn = _rms(h_ref[...], g_ref[...]).astype(BF16)
    cos = cos_ref[...]
    sin = sin_ref[...]
    half = RET_QK_DIM // 2
    q = _dot(hn, wrow_ref[:, :RET_QK_WIDTH])
    for hd in range(RET_HEADS):
        sl = slice(hd * RET_QK_DIM, (hd + 1) * RET_QK_DIM)
        qh = q[:, sl]
        q_ref[:, sl] = (qh * cos + pltpu.roll(qh, half, 1) * sin).astype(BF16)
    v_ref[...] = _dot(hn, wrow_ref[:, RET_QK_WIDTH:RET_QK_WIDTH + RET_V_WIDTH]).astype(BF16)
    gate_ref[...] = _dot(hn, wrow_ref[:, RET_QK_WIDTH + RET_V_WIDTH:]).astype(BF16)

    cost = cost_ref[...] * (RET_QK_DIM ** -0.5)
    sint = sint_ref[...] * (RET_QK_DIM ** -0.5)
    kt = _dot_nt(wkt_ref[...], hn)
    for hd in range(RET_HEADS):
        sl = slice(hd * RET_QK_DIM, (hd + 1) * RET_QK_DIM)
        kh = kt[sl, :]
        swapped = jnp.concatenate([kh[half:], kh[:half]], axis=0)
        kt_ref[sl, :] = (kh * cost + swapped * sint).astype(BF16)


def _odd_in(h, g, wrow, wkt, cos, sin, cost, sint, seq, tm=512):
    n = h.shape[0]
    nt = seq // tm
    row = lambda w: pl.BlockSpec((tm, w), lambda i: (i, 0))
    return pl.pallas_call(
        _odd_in_kernel,
        grid=(n // tm,),
        in_specs=[row(D_MODEL), _full(g.shape), _full(wrow.shape), _full(wkt.shape),
                  pl.BlockSpec((tm, RET_QK_DIM), lambda i: (i % nt, 0)),
                  pl.BlockSpec((tm, RET_QK_DIM), lambda i: (i % nt, 0)),
                  pl.BlockSpec((RET_QK_DIM, tm), lambda i: (0, i % nt)),
                  pl.BlockSpec((RET_QK_DIM, tm), lambda i: (0, i % nt))],
        out_specs=[row(RET_QK_WIDTH), row(RET_V_WIDTH), row(RET_V_WIDTH),
                   pl.BlockSpec((RET_QK_WIDTH, tm), lambda i: (0, i))],
        out_shape=[jax.ShapeDtypeStruct((n, RET_QK_WIDTH), BF16), jax.ShapeDtypeStruct((n, RET_V_WIDTH), BF16),
                   jax.ShapeDtypeStruct((n, RET_V_WIDTH), BF16), jax.ShapeDtypeStruct((RET_QK_WIDTH, n), BF16)],
        compiler_params=_cparams(("parallel",)),
        name="odd_in_proj",
    )(h, g, wrow, wkt, cos, sin, cost, sint)


def _ret_kernel(cdec_ref, q_ref, kt_ref, v_ref, gate_ref, dintra_ref, qdec_ref, kdec_ref, gn_ref,
                y_ref, state_ref):
    @pl.when(pl.program_id(1) == 0)
    def _():
        state_ref[...] = jnp.zeros_like(state_ref)

    for h in range(RET_HEADS):
        ks = slice(h * RET_QK_DIM, (h + 1) * RET_QK_DIM)
        vs = slice(h * RET_V_DIM, (h + 1) * RET_V_DIM)
        qh = q_ref[:, ks]
        kth = kt_ref[ks, :]
        vh = v_ref[:, vs]
        state = state_ref[h]
        inner = (_dot(qh, kth) * dintra_ref[h]).astype(BF16)
        o = _dot(inner, vh) + _dot(qh, state.astype(BF16)) * qdec_ref[h]
        kd = (kth.astype(F32) * kdec_ref[h]).astype(BF16)
        state_ref[h] = state * cdec_ref[h] + _dot(kd, vh)
        mu = jnp.mean(o, axis=-1, keepdims=True)
        oc = o - mu
        var = jnp.mean(oc * oc, axis=-1, keepdims=True)
        on = oc * lax.rsqrt(var + GN_EPS) * gn_ref[:, vs]
        y_ref[:, vs] = (jax.nn.silu(gate_ref[:, vs].astype(F32)) * on).astype(BF16)


def _retention(cdec, q, kt, v, gate, dintra, qdec, kdec, gn, batch, seq):
    n = q.shape[0]
    c = RET_TILE
    nc = seq // c
    row = lambda w: pl.BlockSpec((c, w), lambda b, t: (b * nc + t, 0))
    return pl.pallas_call(
        _ret_kernel,
        grid=(batch, nc),
        in_specs=[pl.BlockSpec(memory_space=pltpu.SMEM),
                  row(RET_QK_WIDTH),
                  pl.BlockSpec((RET_QK_WIDTH, c), lambda b, t: (0, b * nc + t)),
                  row(RET_V_WIDTH), row(RET_V_WIDTH),
                  _full(dintra.shape), _full(qdec.shape), _full(kdec.shape), _full(gn.shape)],
        out_specs=row(RET_V_WIDTH),
        out_shape=jax.ShapeDtypeStruct((n, RET_V_WIDTH), BF16),
        scratch_shapes=[pltpu.VMEM((RET_HEADS, RET_QK_DIM, RET_V_DIM), F32)],
        compiler_params=_cparams(("parallel", "arbitrary")),
        name="retention",
    )(cdec, q, kt, v, gate, dintra, qdec, kdec, gn)


def _t5_bucket(rel):
    half = N_BUCKETS // 2
    max_exact = half // 2
    ret = (rel > 0).astype(I32) * half
    n = jnp.abs(rel)
    nf = jnp.maximum(n, 1).astype(F32)
    large = max_exact + (jnp.log(nf / max_exact) / math.log(MAX_DISTANCE / max_exact)
                         * (half - max_exact)).astype(I32)
    large = jnp.minimum(large, half - 1)
    return ret + jnp.where(n < max_exact, n, large)


def _bias_tables(rel_bias):
    c = jnp.arange(BIAS_BLOCK, dtype=I32)[:, None]
    r = jnp.arange(Q_TILE, dtype=I32)[None, :]
    rels = jnp.stack([c + (b - 1) * BIAS_BLOCK - r for b in range(Q_TILE // BIAS_BLOCK + 1)])
    onehot = jax.nn.one_hot(_t5_bucket(rels), N_BUCKETS, dtype=F32)
    tab = jnp.einsum("abcn,nh->ahbc", onehot, rel_bias, precision=lax.Precision.HIGHEST)
    far = rel_bias[_t5_bucket(jnp.int32(-2 * MAX_DISTANCE))]
    return (tab * LOG2E).astype(F32), (far * LOG2E).astype(F32)


def _transpose_bf16(w):
    wb = w.astype(BF16)
    eye = jnp.eye(w.shape[0], dtype=BF16)
    return lax.dot_general(wb, eye, (((0,), (0,)), ((), ())), preferred_element_type=BF16)


def _block_diag(w):
    nb, bs, _ = w.shape
    eye = jnp.eye(nb, dtype=w.dtype)
    return jnp.einsum("hij,hg->higj", w, eye).reshape(nb * bs, nb * bs)


def _router_weights(w_group, w_expert):
    wr = jnp.concatenate([w_expert, w_group, jnp.zeros((D_MODEL, LANES - N_EXPERTS - N_GROUPS), F32)], axis=1)
    hi = wr.astype(BF16)
    lo = (wr - hi.astype(F32)).astype(BF16)
    return hi, lo


def _rotary_tables(seq):
    half = RET_QK_DIM // 2
    inv = ROPE_BASE ** (-jnp.arange(half, dtype=F32) / half)
    ang = jnp.arange(seq, dtype=F32)[:, None] * inv[None, :]
    cos, sin = jnp.cos(ang), jnp.sin(ang)
    cos_row = jnp.concatenate([cos, cos], axis=1)
    sin_row = jnp.concatenate([-sin, sin], axis=1)
    cos_t = jnp.concatenate([cos.T, cos.T], axis=0)
    sin_t = jnp.concatenate([-sin.T, sin.T], axis=0)
    return cos_row, sin_row, cos_t, sin_t


def _retention_tables():
    c = RET_TILE
    log_g = jnp.log(1.0 - 2.0 ** (-5.0 - jnp.arange(RET_HEADS, dtype=F32)))
    pos = jnp.arange(c, dtype=F32)
    diff = pos[:, None] - pos[None, :]
    causal = diff >= 0
    dintra = jnp.where(causal[None], jnp.exp(jnp.where(causal, diff, 0.0)[None] * log_g[:, None, None]), 0.0)
    qdec = jnp.exp((pos + 1.0)[None, :, None] * log_g[:, None, None])
    kdec = jnp.exp((c - 1.0 - pos)[None, None, :] * log_g[:, None, None])
    cdec = jnp.exp(c * log_g)
    return dintra, qdec, kdec, cdec


def kernel(x, p, rel_bias, mix_norm_g, ffn_norm_g, ple_norm_g, ev_w_in, ev_conv_w, ev_conv_b, ev_lru_wa, ev_lru_ba, ev_lru_wx, ev_lru_bx, ev_lru_lambda, ev_q_norm_g, ev_k_norm_g, ev_w_out, od_w_in, od_gn_g, od_w_out, moe_w_group, moe_w_expert, moe_w_gate, moe_w_up, moe_w_down, ple_w_up, ple_w_gate):
    batch, seq, _ = x.shape
    n = batch * seq
    depth = p.shape[0]
    h = x.reshape(n, D_MODEL)
    row = lambda a: a.reshape(1, -1)

    tab, far = _bias_tables(rel_bias)
    cos_row, sin_row, cos_t, sin_t = _rotary_tables(seq)
    dintra, qdec, kdec, cdec = _retention_tables()
    head_ones = _block_diag(jnp.ones((ATT_HEADS, ATT_HEAD_DIM, ATT_HEAD_DIM), BF16))

    for i in range(depth):
        jdx = i // 2
        if i % 2 == 0:
            w = ev_w_in[jdx]
            widths = (LRU_WIDTH, LRU_WIDTH, ATT_WIDTH, ATT_WIDTH, ATT_WIDTH, IDX_HEADS * IDX_DIM, IDX_DIM, IDX_HEADS)
            o = [sum(widths[:t]) for t in range(len(widths) + 1)]
            xa_w, ga_w, q_w, k_w, v_w, iq_w, ik_w, iw_w = [w[:, o[t]:o[t + 1]] for t in range(8)]
            wrow = jnp.concatenate([xa_w, ga_w, k_w, ik_w, ik_w], axis=1).astype(BF16)
            wt = _transpose_bf16(jnp.concatenate(
                [q_w, iq_w, v_w, iw_w, jnp.zeros((D_MODEL, 16 - IDX_HEADS), F32)], axis=1))
            qg = ev_q_norm_g[jdx].reshape(-1, 1) * (ATT_HEAD_DIM ** -0.5 * LOG2E)
            kg = jnp.tile(ev_k_norm_g[jdx], ATT_HEADS).reshape(1, -1)
            xg, k, ik, qt, iqt, vt, iwt = _even_in(h, row(mix_norm_g[i]), wrow, wt, head_ones, qg, kg)
            ya = _rglru(xg, ev_conv_w[jdx], row(ev_conv_b[jdx]),
                        _block_diag(ev_lru_wa[jdx]).astype(BF16), row(ev_lru_ba[jdx]),
                        _block_diag(ev_lru_wx[jdx]).astype(BF16), row(ev_lru_bx[jdx]),
                        row(ev_lru_lambda[jdx]), batch, seq)
            yb = _dsa(far, qt, iqt, iwt, k, ik, vt, tab, batch, seq)
            wo = ev_w_out[jdx].astype(BF16)
            ys, ws = [ya, yb], [wo[:LRU_WIDTH], wo[LRU_WIDTH:]]
        else:
            w = od_w_in[jdx]
            wrow = jnp.concatenate([w[:, :RET_QK_WIDTH], w[:, 2 * RET_QK_WIDTH:]], axis=1).astype(BF16)
            wkt = _transpose_bf16(w[:, RET_QK_WIDTH:2 * RET_QK_WIDTH])
            q, v, gate, kt = _odd_in(h, row(mix_norm_g[i]), wrow, wkt, cos_row, sin_row, cos_t, sin_t, seq)
            yc = _retention(cdec, q, kt, v, gate, dintra, qdec, kdec, row(od_gn_g[jdx]), batch, seq)
            ys, ws = [yc], [od_w_out[jdx].astype(BF16)]
        wr_hi, wr_lo = _router_weights(moe_w_group[i], moe_w_expert[i])
        h1, xn, route = _mix_out(h, ys, ws, row(ffn_norm_g[i]), wr_hi, wr_lo)
        yp = _moe_sparse(xn, route, moe_w_gate, moe_w_up, moe_w_down, i)
        h = _ple(h1, yp, route, p.reshape(depth * n, PLE_DIM), i, ple_w_up[i].astype(BF16),
                 ple_w_gate[i].astype(BF16), row(ple_norm_g[i]))
    return h.reshape(batch, seq, D_MODEL)
```

```python
import functools
import math

import jax
import jax.numpy as jnp
from jax import lax
from jax.experimental import pallas as pl
from jax.experimental.pallas import tpu as pltpu
from jax.experimental.pallas import tpu_sc as plsc

F32 = jnp.float32
BF16 = jnp.bfloat16
I32 = jnp.int32

D_MODEL = 1024
CHUNK = 64
PLE_DIM = 256
RMS_EPS = 1e-6

LRU_WIDTH = 512
LRU_BLOCKS = 8
LRU_C = 8.0

ATT_HEADS = 8
ATT_HEAD_DIM = 64
ATT_WIDTH = 512
IDX_HEADS = 8
IDX_DIM = 64
MAX_TOP_K = 256
N_BUCKETS = 32
MAX_DISTANCE = 128

RET_HEADS = 8
RET_QK_DIM = 128
RET_V_DIM = 256
RET_QK_WIDTH = 1024
RET_V_WIDTH = 2048
ROPE_BASE = 10000.0
GN_EPS = 1e-5

N_GROUPS = 4
EXPERTS_PER_GROUP = 8
N_EXPERTS = 32
EXPERT_FF = 512

LANES = 128
INT_MIN = -(2 ** 31)
NEG_BIG = -1e30
LOG2E = 1.4426950408889634
VMEM_LIMIT = 56 * 1024 * 1024

SC_CORES = 2
SC_SUBCORES = 16
SC_CHUNK = 64

MIX_SUB = 256
Q_TILE = 128
BIAS_BLOCK = 128
K_TILE = 512
RET_TILE = 256


def _cparams(sem):
    return pltpu.CompilerParams(dimension_semantics=sem, vmem_limit_bytes=VMEM_LIMIT)


def _full(shape):
    nd = len(shape)
    return pl.BlockSpec(shape, lambda *_: (0,) * nd)


def _rms(xf, g):
    return xf * lax.rsqrt(jnp.mean(xf * xf, axis=-1, keepdims=True) + RMS_EPS) * g


def _dot(a, b):
    return jnp.dot(a, b, preferred_element_type=F32)


def _dot_nt(a, b):
    return lax.dot_general(a, b, (((1,), (1,)), ((), ())), preferred_element_type=F32)


def _neg_expm1(x):
    series = -x * (1.0 + x * (0.5 + x * (1.0 / 6.0 + x * (1.0 / 24.0))))
    return jnp.where(x > -0.03, series, 1.0 - jnp.exp(x))


def _pack_halves(x):
    w = x.shape[1] // 2
    hi = pltpu.bitcast(x[:, :w].astype(BF16).astype(F32), I32)
    lo = pltpu.bitcast(x[:, w:].astype(BF16).astype(F32), I32)
    return hi | lax.shift_right_logical(lo, 16)


def _unpack_halves(p):
    hi = pltpu.bitcast(p & jnp.int32(-65536), F32)
    lo = pltpu.bitcast(lax.shift_left(p, 16), F32)
    return hi, lo


def _split_bf16(x):
    hi = x.astype(BF16)
    lo = (x - hi.astype(F32)).astype(BF16)
    return hi, lo


def _even_in_kernel(h_ref, g_ref, wrow_ref, wt_ref, bd_ref, qg_ref, kg_ref,
                    xg_ref, k_ref, ik_ref, qt_ref, iqt_ref, vt_ref, iwt_ref):
    tm = h_ref.shape[0]
    w = ATT_WIDTH
    hn = _rms(h_ref[...], g_ref[...]).astype(BF16)

    xg_ref[...] = _dot(hn, wrow_ref[:, :2 * LRU_WIDTH])
    kf = _dot(hn, wrow_ref[:, 2 * LRU_WIDTH:2 * LRU_WIDTH + w])
    hi, lo = _split_bf16(kf * kf)
    ss = _dot(hi, bd_ref[...]) + _dot(lo, bd_ref[...])
    k_ref[...] = (kf * lax.rsqrt(ss * (1.0 / ATT_HEAD_DIM) + RMS_EPS) * kg_ref[...]).astype(BF16)
    ik_ref[...] = _dot(hn, wrow_ref[:, 2 * LRU_WIDTH + w:]).astype(BF16)

    feat = _dot_nt(wt_ref[...], hn)
    q3 = feat[0:w].reshape(ATT_HEADS, ATT_HEAD_DIM, tm)
    ssq = jnp.sum(q3 * q3, axis=1, keepdims=True)
    qn = q3 * lax.rsqrt(ssq * (1.0 / ATT_HEAD_DIM) + RMS_EPS) * qg_ref[...][None]
    qt_ref[...] = qn.reshape(w, tm).astype(BF16)
    iqt_ref[...] = feat[w:2 * w].astype(BF16)
    vt_ref[...] = feat[2 * w:3 * w].astype(BF16)
    iwt_ref[...] = feat[3 * w:3 * w + IDX_HEADS] * (IDX_HEADS ** -0.5 * IDX_DIM ** -0.5)


def _even_in(h, g, wrow, wt, bd, qg, kg, tm=512):
    n = h.shape[0]
    row = lambda w: pl.BlockSpec((tm, w), lambda i: (i, 0))
    col = lambda r: pl.BlockSpec((r, tm), lambda i: (0, i))
    return pl.pallas_call(
        _even_in_kernel,
        grid=(n // tm,),
        in_specs=[row(D_MODEL), _full(g.shape), _full(wrow.shape), _full(wt.shape),
                  _full(bd.shape), _full(qg.shape), _full(kg.shape)],
        out_specs=[row(2 * LRU_WIDTH), row(ATT_WIDTH), row(LANES), col(ATT_WIDTH), col(ATT_WIDTH),
                   col(ATT_WIDTH), col(IDX_HEADS)],
        out_shape=[jax.ShapeDtypeStruct((n, 2 * LRU_WIDTH), F32), jax.ShapeDtypeStruct((n, ATT_WIDTH), BF16),
                   jax.ShapeDtypeStruct((n, LANES), BF16), jax.ShapeDtypeStruct((ATT_WIDTH, n), BF16),
                   jax.ShapeDtypeStruct((ATT_WIDTH, n), BF16), jax.ShapeDtypeStruct((ATT_WIDTH, n), BF16),
                   jax.ShapeDtypeStruct((IDX_HEADS, n), F32)],
        compiler_params=_cparams(("parallel",)),
        name="even_in_proj",
    )(h, g, wrow, wt, bd, qg, kg)


def _rglru_kernel(xg_ref, cw_ref, cb_ref, wa_ref, ba_ref, wx_ref, bx_ref, lam_ref, ya_ref,
                  ext_ref, hst_ref):
    ts = xg_ref.shape[0]
    w = LRU_WIDTH

    @pl.when(pl.program_id(1) == 0)
    def _():
        ext_ref[0:8, :] = jnp.zeros((8, w), F32)
        hst_ref[...] = jnp.zeros_like(hst_ref)

    xa = xg_ref[:, :w]
    ga = xg_ref[:, w:]
    ext_ref[8:, :] = xa
    xc = xa * cw_ref[3:4, :] + cb_ref[...]
    for d in (1, 2, 3):
        xc = xc + ext_ref[8 - d:8 - d + ts, :] * cw_ref[3 - d:4 - d, :]
    ext_ref[0:8, :] = xa[ts - 8:, :]

    xcb = xc.astype(BF16)
    r = jax.nn.sigmoid(_dot(xcb, wa_ref[...]) + ba_ref[...])
    gi = jax.nn.sigmoid(_dot(xcb, wx_ref[...]) + bx_ref[...])
    nl = -lam_ref[...]
    softplus = jnp.maximum(nl, 0.0) + jnp.log1p(jnp.exp(-jnp.abs(nl)))
    log_a = (-LRU_C) * r * softplus
    a = jnp.exp(log_a)
    u = jnp.sqrt(_neg_expm1(2.0 * log_a)) * (gi * xc)

    groups = ts // 8
    a3 = a.reshape(groups, 8, w)
    u3 = u.reshape(groups, 8, w)
    pos = lax.broadcasted_iota(I32, (groups, 8, w), 1)
    for d in (1, 2, 4):
        keep = pos >= d
        a_sh = jnp.where(keep, pltpu.roll(a3, d, 1), 1.0)
        u_sh = jnp.where(keep, pltpu.roll(u3, d, 1), 0.0)
        u3 = a3 * u_sh + u3
        a3 = a3 * a_sh
    carry = hst_ref[0:1, :]
    hs = []
    for g in range(groups):
        hg = a3[g] * carry + u3[g]
        carry = hg[7:8, :]
        hs.append(hg)
    hst_ref[0:1, :] = carry
    hseq = jnp.concatenate(hs, axis=0)
    ya_ref[...] = (hseq * jax.nn.gelu(ga)).astype(BF16)


def _rglru(xg, cw, cb, wa, ba, wx, bx, lam, batch, seq, ts=512):
    n = xg.shape[0]
    nt = seq // ts
    return pl.pallas_call(
        _rglru_kernel,
        grid=(batch, nt),
        in_specs=[pl.BlockSpec((ts, 2 * LRU_WIDTH), lambda b, t: (b * nt + t, 0)),
                  _full(cw.shape), _full(cb.shape), _full(wa.shape), _full(ba.shape),
                  _full(wx.shape), _full(bx.shape), _full(lam.shape)],
        out_specs=pl.BlockSpec((ts, LRU_WIDTH), lambda b, t: (b * nt + t, 0)),
        out_shape=jax.ShapeDtypeStruct((n, LRU_WIDTH), BF16),
        scratch_shapes=[pltpu.VMEM((ts + 8, LRU_WIDTH), F32), pltpu.VMEM((8, LRU_WIDTH), F32)],
        compiler_params=_cparams(("parallel", "arbitrary")),
        name="rglru",
    )(xg, cw, cb, wa, ba, wx, bx, lam)


def _dsa_kernel(far_ref, qt_ref, iqt_ref, iwt_ref, k_ref, ik_ref, vt_ref, tab_ref, o_ref,
                keys_ref, sel_ref, cut_ref, seln_ref, *, seq, top_k):
    tq = Q_TILE
    hd = ATT_HEAD_DIM
    j = pl.program_id(1)
    q0 = j * tq
    nkt = (q0 + tq + K_TILE - 1) // K_TILE
    qcol = q0 + lax.broadcasted_iota(I32, (1, tq), 1)
    qlim = (qcol // CHUNK + 1) * CHUNK
    zero_rows = jnp.zeros((hd, tq), BF16)

    def head_rows(ref, h):
        blk = ref[h * hd:(h + 1) * hd, :]
        return jnp.concatenate([blk, zero_rows] if h % 2 == 0 else [zero_rows, blk], axis=0)

    def ktile(kt):
        return pl.multiple_of(kt * K_TILE, K_TILE)

    def key_pos(off, rows):
        return off + lax.broadcasted_iota(I32, (rows, tq), 0)


    def score_tile(kt, mask_inadmissible):
        off = ktile(kt)
        ik = ik_ref[pl.ds(off, K_TILE), :]
        s = jnp.zeros((K_TILE, tq), F32)
        for h in range(IDX_HEADS):
            s = s + jnp.maximum(_dot(ik, head_rows(iqt_ref, h)), 0.0) * iwt_ref[h:h + 1, :]
        bits = pltpu.bitcast(s, I32)
        neg = bits >> 31
        key = (bits ^ (neg & 0x7FFFFFFF)) - neg
        if mask_inadmissible:
            key = jnp.where(key_pos(off, K_TILE) < qlim, key, INT_MIN)
        keys_ref[pl.ds(off, K_TILE), :] = key

    lax.fori_loop(0, nkt - 1, lambda kt, c: (score_tile(kt, False), c)[1], 0)
    score_tile(nkt - 1, True)

    def count(pred):
        def body(kt, acc):
            off = ktile(kt)
            ind = pred(keys_ref[pl.ds(off, K_TILE), :], key_pos(off, K_TILE)).reshape(K_TILE // 8, 8, tq)
            while ind.shape[0] > 1:
                half = ind.shape[0] // 2
                ind = ind[:half] + ind[half:]
            return acc + ind[0]
        acc = lax.fori_loop(0, nkt, body, jnp.zeros((8, tq), F32))
        return jnp.sum(acc, axis=0, keepdims=True)

    kf = float(top_k)

    def search_body(i, carry):
        ans, cnt_ans = carry
        cand = ans + jnp.left_shift(jnp.int32(1), 31 - i)
        cnt = count(lambda kk, kpos: jnp.where(kk >= cand, 1.0, 0.0))
        take = cnt >= kf
        return jnp.where(take, cand, ans), jnp.where(take, cnt, cnt_ans)

    covered = (nkt * K_TILE).astype(F32)
    ans, cnt_ge = lax.fori_loop(0, 32, search_body,
                                (jnp.full((1, tq), INT_MIN, I32), jnp.full((1, tq), covered, F32)))

    excess = jnp.where(ans > INT_MIN, cnt_ge - kf, 0.0)
    has_ties = jnp.max(excess) > 0.0

    @pl.when(has_ties)
    def _():
        need = kf - count(lambda kk, kpos: jnp.where(kk > ans, 1.0, 0.0))

        def idx_body(i, pos):
            cand = pos + jnp.left_shift(jnp.int32(1), (seq.bit_length() - 2) - i)
            c = count(lambda kk, kpos: jnp.where(kk == ans, jnp.where(kpos < cand, 1.0, 0.0), 0.0))
            return jnp.where(c < need, cand, pos)

        pos = lax.fori_loop(0, seq.bit_length() - 1, idx_body, jnp.zeros((1, tq), I32))
        cut_ref[...] = jnp.broadcast_to(pos, cut_ref.shape)

    thr = jnp.maximum(ans, INT_MIN + 1)

    near_w = tq + BIAS_BLOCK
    near_start = pl.multiple_of(jnp.maximum(q0 - BIAS_BLOCK, 0), BIAS_BLOCK)
    n_far = (near_start + K_TILE - 1) // K_TILE

    def write_selection(selection):
        def sel_body(kt, c):
            off = ktile(kt)
            kpos = key_pos(off, K_TILE)
            s = selection(keys_ref[pl.ds(off, K_TILE), :], kpos)
            sel_ref[pl.ds(off, K_TILE), :] = jnp.where(kpos < near_start, s, NEG_BIG)
            return c
        lax.fori_loop(0, n_far, sel_body, 0)
        seln_ref[...] = selection(keys_ref[pl.ds(near_start, near_w), :], key_pos(near_start, near_w))

    @pl.when(has_ties)
    def _():
        cut = cut_ref[0:1, :]
        write_selection(lambda kk, kpos: jnp.where(
            kk > thr, 0.0, jnp.where(kk == thr, jnp.where(kpos <= cut, 0.0, NEG_BIG), NEG_BIG)))

    @pl.when(jnp.logical_not(has_ties))
    def _():
        write_selection(lambda kk, kpos: jnp.where(kk >= thr, 0.0, NEG_BIG))

    sel_near = seln_ref[...]

    def step(qm, kb, vte, add, carry):
        m, acc = carry
        s = _dot(kb, qm) + add
        m_new = jnp.maximum(m, jnp.max(s, axis=0, keepdims=True))
        p = jnp.exp2(s - m_new).astype(BF16)
        acc = jnp.exp2(m - m_new) * acc + _dot(vte, p)
        return m_new, acc

    pairs = [slice((h // 2) * LANES, (h // 2 + 1) * LANES) for h in range(ATT_HEADS)]

    def values(h, off, width):
        return jnp.concatenate([vt_ref[h * hd:(h + 1) * hd, pl.ds(off, width)],
                                jnp.ones((hd, width), BF16)], axis=0)

    def far_body(kt, carries):
        off = ktile(kt)
        sel = sel_ref[pl.ds(off, K_TILE), :]
        return tuple(step(head_rows(qt_ref, h),k_ref[pl.ds(off, K_TILE), pairs[h]], values(h, off, K_TILE), sel, carries[h])
                     for h in range(ATT_HEADS))

    init = (jnp.full((1, tq), NEG_BIG, F32), jnp.zeros((2 * hd, tq), F32))
    carries = lax.fori_loop(0, n_far, far_body, (init,) * ATT_HEADS)

    n_blocks = near_w // BIAS_BLOCK
    first = jnp.where(j == 0, 1, 0)
    res = []
    for h in range(ATT_HEADS):
        m, acc = carries[h]
        bias = jnp.concatenate([tab_ref[jnp.minimum(first + b, n_blocks - 1), h] for b in range(n_blocks)],
                               axis=0)
        m, acc = step(head_rows(qt_ref, h),k_ref[pl.ds(near_start, near_w), pairs[h]], values(h, near_start, near_w),
                      sel_near + bias, (m + far_ref[h], acc))
        res.append(acc[:hd, :] / acc[hd:hd + 1, :])

    for p in range(ATT_HEADS // 2):
        pair_t = jnp.concatenate([res[2 * p], res[2 * p + 1]], axis=0)
        o_ref[:, p * LANES:(p + 1) * LANES] = pair_t.T.astype(BF16)


def _dsa(far, qt, iqt, iwt, k, ik, vt, tab, batch, seq):
    n = k.shape[0]
    nb = seq // Q_TILE
    top_k = min(MAX_TOP_K, seq // 4)
    qspec = lambda r: pl.BlockSpec((r, Q_TILE), lambda b, j: (0, b * nb + j))
    return pl.pallas_call(
        functools.partial(_dsa_kernel, seq=seq, top_k=top_k),
        grid=(batch, nb),
        in_specs=[pl.BlockSpec(memory_space=pltpu.SMEM),
                  qspec(ATT_WIDTH), qspec(ATT_WIDTH), qspec(IDX_HEADS),
                  pl.BlockSpec((seq, ATT_WIDTH), lambda b, j: (b, 0)),
                  pl.BlockSpec((seq, LANES), lambda b, j: (b, 0)),
                  pl.BlockSpec((ATT_WIDTH, seq), lambda b, j: (0, b)),
                  _full(tab.shape)],
        out_specs=pl.BlockSpec((Q_TILE, ATT_WIDTH), lambda b, j: (b * nb + j, 0)),
        out_shape=jax.ShapeDtypeStruct((n, ATT_WIDTH), BF16),
        scratch_shapes=[pltpu.VMEM((seq, Q_TILE), I32), pltpu.VMEM((seq, Q_TILE), F32),
                        pltpu.VMEM((8, Q_TILE), I32), pltpu.VMEM((Q_TILE + BIAS_BLOCK, Q_TILE), F32)],
        compiler_params=_cparams(("parallel", "arbitrary")),
        name="dsa",
    )(far, qt, iqt, iwt, k, ik, vt, tab)


def _mix_out_kernel(*refs, n_y):
    h_ref = refs[0]
    y_refs = refs[1:1 + n_y]
    w_refs = refs[1 + n_y:1 + 2 * n_y]
    g_ref, wr_hi_ref, wr_lo_ref = refs[1 + 2 * n_y:4 + 2 * n_y]
    h1_ref, xn_ref, route_ref = refs[4 + 2 * n_y:]
    for sub in range(h_ref.shape[0] // MIX_SUB):
        rs = slice(sub * MIX_SUB, (sub + 1) * MIX_SUB)
        _mix_out_rows(rs, h_ref, y_refs, w_refs, g_ref, wr_hi_ref, wr_lo_ref, h1_ref, xn_ref, route_ref)


def _mix_out_rows(rs, h_ref, y_refs, w_refs, g_ref, wr_hi_ref, wr_lo_ref, h1_ref, xn_ref, route_ref):
    tm = rs.stop - rs.start
    mix = _dot(y_refs[0][rs, :], w_refs[0][...])
    for y_ref, w_ref in zip(y_refs[1:], w_refs[1:]):
        mix = mix + _dot(y_ref[rs, :], w_ref[...])
    h1 = h_ref[rs, :] + mix
    h1_ref[rs, :] = h1
    xn = _rms(h1, g_ref[...])
    hi, lo = _split_bf16(xn)
    xn_ref[rs, :] = _pack_halves(xn)
    logits = _dot(hi, wr_hi_ref[...]) + _dot(lo, wr_hi_ref[...]) + _dot(hi, wr_lo_ref[...])

    lane = lax.broadcasted_iota(I32, (tm, LANES), 1).astype(F32)
    big = float(LANES)
    ninf = -jnp.inf
    is_g = (lane >= N_EXPERTS) & (lane < N_EXPERTS + N_GROUPS)
    glog = jnp.where(is_g, logits, ninf)
    gmax = jnp.max(glog, axis=1, keepdims=True)
    gsel = jnp.min(jnp.where(glog == gmax, lane, big), axis=1, keepdims=True) - N_EXPERTS
    gprob = 1.0 / jnp.sum(jnp.exp(glog - gmax), axis=1, keepdims=True)
    lo_l = gsel * EXPERTS_PER_GROUP
    within = jnp.where((lane >= lo_l) & (lane < lo_l + EXPERTS_PER_GROUP), logits, ninf)
    v1 = jnp.max(within, axis=1, keepdims=True)
    i1 = jnp.min(jnp.where(within == v1, lane, big), axis=1, keepdims=True)
    rest = jnp.where(lane == i1, ninf, within)
    v2 = jnp.max(rest, axis=1, keepdims=True)
    i2 = jnp.min(jnp.where(rest == v2, lane, big), axis=1, keepdims=True)
    e2 = jnp.exp(v2 - v1)
    w1 = gprob / (1.0 + e2)
    w2 = gprob * e2 / (1.0 + e2)
    route_ref[rs, :] = jnp.where(lane == 0, i1, jnp.where(lane == 1, i2,
                                 jnp.where(lane == 2, w1, jnp.where(lane == 3, w2, 0.0))))


def _mix_out(h, ys, ws, g, wr_hi, wr_lo, tm=1024):
    n = h.shape[0]
    n_y = len(ys)
    row = lambda w: pl.BlockSpec((tm, w), lambda i: (i, 0))
    return pl.pallas_call(
        functools.partial(_mix_out_kernel, n_y=n_y),
        grid=(n // tm,),
        in_specs=[row(D_MODEL)] + [row(y.shape[1]) for y in ys] + [_full(w.shape) for w in ws]
                 + [_full(g.shape), _full(wr_hi.shape), _full(wr_lo.shape)],
        out_specs=[row(D_MODEL), row(D_MODEL // 2), row(LANES)],
        out_shape=[jax.ShapeDtypeStruct((n, D_MODEL), F32), jax.ShapeDtypeStruct((n, D_MODEL // 2), I32),
                   jax.ShapeDtypeStruct((n, LANES), F32)],
        compiler_params=_cparams(("parallel",)),
        name="mix_out_router",
    )(h, *ys, *ws, g, wr_hi, wr_lo)


def _sc_rows(src, idx, scatter):
    r, d = idx.shape[0], src.shape[1]
    n_src = src.shape[0]
    nw = SC_CORES * SC_SUBCORES
    per_w = r // nw
    n_chunks = per_w // SC_CHUNK
    assert r % nw == 0 and per_w % SC_CHUNK == 0 and (not scatter or n_src % per_w == 0)
    idx3 = idx.reshape(nw, n_chunks, SC_CHUNK)
    mesh = plsc.VectorSubcoreMesh(core_axis_name="c", subcore_axis_name="s")

    row_buf = pltpu.VMEM((SC_CHUNK, d), src.dtype)

    @functools.partial(
        pl.kernel, mesh=mesh, out_type=jax.ShapeDtypeStruct((r, d), src.dtype),
        scratch_types=[pltpu.VMEM((n_chunks, SC_CHUNK), I32), row_buf, row_buf] + [pltpu.SemaphoreType.DMA] * 4)
    def permute(src_hbm, idx_hbm, out_hbm, idx_v, rows0, rows1, in0, in1, out0, out1):
        wid = lax.axis_index("s") * SC_CORES + lax.axis_index("c")
        pltpu.sync_copy(idx_hbm.at[wid], idx_v)
        bufs, in_sems, out_sems = (rows0, rows1), (in0, in1), (out0, out1)

        def linear(c, n_rows):
            start = lax.rem(wid * per_w + c * SC_CHUNK, n_rows)
            return pl.ds(pl.multiple_of(start, SC_CHUNK), SC_CHUNK)

        def load(c):
            ref = src_hbm.at[linear(c, n_src)] if scatter else src_hbm.at[idx_v.at[c]]
            return pltpu.async_copy(ref, bufs[c % 2], in_sems[c % 2])

        def store(c):
            ref = out_hbm.at[idx_v.at[c]] if scatter else out_hbm.at[linear(c, r)]
            return pltpu.async_copy(bufs[c % 2], ref, out_sems[c % 2])

        loads, stores = {0: load(0)}, {}
        for c in range(n_chunks):
            if c + 1 < n_chunks:
                if c >= 1:
                    stores[c - 1].wait()
                loads[c + 1] = load(c + 1)
            loads[c].wait()
            stores[c] = store(c)
        for c in range(max(n_chunks - 2, 0), n_chunks):
            stores[c].wait()

    return permute(src, idx3)


def _moe_kernel(tile_ref, exp_ref, flag_ref, off_ref, xs_ref, wg_ref, wu_ref, wd_ref, ys_ref,
                wgu_ref, wdb_ref, acc_ref):
    w = pl.program_id(0)
    tm = xs_ref.shape[0]
    e = exp_ref[w]
    flags = flag_ref[w]

    @pl.when(w == 0)
    def _():
        acc_ref[...] = jnp.zeros_like(acc_ref)

    @pl.when((flags & 4) != 0)
    def _():
        wgu_ref[:, :EXPERT_FF] = wg_ref[0, 0].astype(BF16)
        wgu_ref[:, EXPERT_FF:] = wu_ref[0, 0].astype(BF16)
        wdb_ref[...] = wd_ref[0, 0].astype(BF16)

    @pl.when((flags & 1) != 0)
    def _():
        xa, xb = _unpack_halves(xs_ref[...])
        xa = xa.astype(BF16)
        xb = xb.astype(BF16)
        half = D_MODEL // 2
        gate_up = _dot(xa, wgu_ref[:half, :]) + _dot(xb, wgu_ref[half:, :])
        hid = (jax.nn.silu(gate_up[:, :EXPERT_FF]) * gate_up[:, EXPERT_FF:]).astype(BF16)
        y = _dot(hid, wdb_ref[...])
        rows = tile_ref[w] * tm + lax.broadcasted_iota(I32, (tm, 1), 0)
        mine = (rows >= off_ref[e]) & (rows < off_ref[e + 1])
        y = jnp.where(mine, y, 0.0)
        acc = jnp.where((flags & 2) != 0, y, acc_ref[...] + y)
        acc_ref[...] = acc
        ys_ref[...] = _pack_halves(acc)


def _moe_plan(off, n_rows, tm):
    nt = n_rows // tm
    n_items = nt + N_EXPERTS - 1
    row_lo = jnp.arange(nt, dtype=I32) * tm
    ends = off[1:N_EXPERTS][None, :]
    e_lo = jnp.sum((ends <= row_lo[:, None]).astype(I32), axis=1)
    e_hi = jnp.sum((ends <= (row_lo + tm - 1)[:, None]).astype(I32), axis=1)
    span = e_hi - e_lo + 1
    start = jnp.cumsum(span) - span
    total = jnp.sum(span)
    w = jnp.arange(n_items, dtype=I32)
    valid = w < total
    tile = jnp.clip(jnp.sum((start[None, :] <= w[:, None]).astype(I32), axis=1) - 1, 0, nt - 1)
    expert = jnp.where(valid, e_lo[tile] + (w - start[tile]), e_hi[nt - 1])
    first = valid & (w == start[tile])
    new_e = jnp.concatenate([jnp.ones((1,), bool), expert[1:] != expert[:-1]])
    flags = valid.astype(I32) + 2 * first.astype(I32) + 4 * new_e.astype(I32)
    return tile, expert, flags


def _moe(xs, tile, expert, flags, off, wg, wu, wd, layer, tm):
    n_rows, half = xs.shape
    n_items = tile.shape[0]
    grid_spec = pltpu.PrefetchScalarGridSpec(
        num_scalar_prefetch=4,
        grid=(n_items,),
        in_specs=[pl.BlockSpec((tm, half), lambda w, t, e, f, o: (t[w], 0)),
                  pl.BlockSpec((1, 1, D_MODEL, EXPERT_FF), lambda w, t, e, f, o: (layer, e[w], 0, 0)),
                  pl.BlockSpec((1, 1, D_MODEL, EXPERT_FF), lambda w, t, e, f, o: (layer, e[w], 0, 0)),
                  pl.BlockSpec((1, 1, EXPERT_FF, D_MODEL), lambda w, t, e, f, o: (layer, e[w], 0, 0))],
        out_specs=pl.BlockSpec((tm, half), lambda w, t, e, f, o: (t[w], 0)),
        scratch_shapes=[pltpu.VMEM((D_MODEL, 2 * EXPERT_FF), BF16),
                        pltpu.VMEM((EXPERT_FF, D_MODEL), BF16), pltpu.VMEM((tm, D_MODEL), F32)])
    return pl.pallas_call(
        _moe_kernel,
        grid_spec=grid_spec,
        out_shape=jax.ShapeDtypeStruct((n_rows, half), I32),
        compiler_params=_cparams(("arbitrary",)),
        name="moe_experts",
    )(tile, expert, flags, off, xs, wg, wu, wd)


def _moe_sparse(xn_packed, route, wg, wu, wd, layer, tm=256):
    n = xn_packed.shape[0]
    eid = jnp.concatenate([route[:, 0], route[:, 1]]).astype(I32)
    _, sorted_pair = lax.sort_key_val(eid, jnp.arange(2 * n, dtype=I32))
    token = jnp.where(sorted_pair >= n, sorted_pair - n, sorted_pair)
    bounds = jnp.arange(N_EXPERTS + 1, dtype=I32)
    off = jnp.sum((eid[None, :] < bounds[:, None]).astype(I32), axis=1)
    tile, expert, flags = _moe_plan(off, 2 * n, tm)
    xs = _sc_rows(xn_packed, token, scatter=False)
    ys = _moe(xs, tile, expert, flags, off, wg, wu, wd, layer, tm)
    return _sc_rows(ys, sorted_pair, scatter=True)


def _ple_kernel(h1_ref, y0_ref, y1_ref, route_ref, p_ref, wup_ref, wgate_ref, g_ref, o_ref):
    w1 = route_ref[:, 2:3]
    w2 = route_ref[:, 3:4]
    a0, b0 = _unpack_halves(y0_ref[...])
    a1, b1 = _unpack_halves(y1_ref[...])
    moe = jnp.concatenate([w1 * a0 + w2 * a1, w1 * b0 + w2 * b1], axis=1)
    h2 = h1_ref[...] + moe
    e = _rms(_dot(p_ref[...].astype(BF16), wup_ref[...]), g_ref[...])
    gate = jax.nn.sigmoid(_dot(h2.astype(BF16), wgate_ref[...]))
    o_ref[...] = h2 + e * gate


def _ple(h1, yp, route, p_all, layer, wup, wgate, g, tm=1024):
    n = h1.shape[0]
    nt = n // tm
    row = lambda w: pl.BlockSpec((tm, w), lambda i: (i, 0))
    return pl.pallas_call(
        _ple_kernel,
        grid=(nt,),
        in_specs=[row(D_MODEL), row(D_MODEL // 2),
                  pl.BlockSpec((tm, D_MODEL // 2), lambda i: (i + nt, 0)),
                  row(LANES), pl.BlockSpec((tm, PLE_DIM), lambda i: (i + layer * nt, 0)),
                  _full(wup.shape), _full(wgate.shape), _full(g.shape)],
        out_specs=row(D_MODEL),
        out_shape=jax.ShapeDtypeStruct((n, D_MODEL), F32),
        compiler_params=_cparams(("parallel",)),
        name="ple",
    )(h1, yp, yp, route, p_all, wup, wgate, g)


def _odd_in_kernel(h_ref, g_ref, wrow_ref, wkt_ref, cos_ref, sin_ref, cost_ref, sint_ref,
                   q_ref, v_ref, gate_ref, kt_ref):
    hn = _rms(h_ref[...], g_ref[...]).astype(BF16)
    cos = cos_ref[...]
    sin = sin_ref[...]
    half = RET_QK_DIM // 2
    q = _dot(hn, wrow_ref[:, :RET_QK_WIDTH])
    for hd in range(RET_HEADS):
        sl = slice(hd * RET_QK_DIM, (hd + 1) * RET_QK_DIM)
        qh = q[:, sl]
        q_ref[:, sl] = (qh * cos + pltpu.roll(qh, half, 1) * sin).astype(BF16)
    v_ref[...] = _dot(hn, wrow_ref[:, RET_QK_WIDTH:RET_QK_WIDTH + RET_V_WIDTH]).astype(BF16)
    gate_ref[...] = _dot(hn, wrow_ref[:, RET_QK_WIDTH + RET_V_WIDTH:]).astype(BF16)

    cost = cost_ref[...] * (RET_QK_DIM ** -0.5)
    sint = sint_ref[...] * (RET_QK_DIM ** -0.5)
    kt = _dot_nt(wkt_ref[...], hn)
    for hd in range(RET_HEADS):
        sl = slice(hd * RET_QK_DIM, (hd + 1) * RET_QK_DIM)
        kh = kt[sl, :]
        swapped = jnp.concatenate([kh[half:], kh[:half]], axis=0)
        kt_ref[sl, :] = (kh * cost + swapped * sint).astype(BF16)


def _odd_in(h, g, wrow, wkt, cos, sin, cost, sint, seq, tm=512):
    n = h.shape[0]
    nt = seq // tm
    row = lambda w: pl.BlockSpec((tm, w), lambda i: (i, 0))
    return pl.pallas_call(
        _odd_in_kernel,
        grid=(n // tm,),
        in_specs=[row(D_MODEL), _full(g.shape), _full(wrow.shape), _full(wkt.shape),
                  pl.BlockSpec((tm, RET_QK_DIM), lambda i: (i % nt, 0)),
                  pl.BlockSpec((tm, RET_QK_DIM), lambda i: (i % nt, 0)),
                  pl.BlockSpec((RET_QK_DIM, tm), lambda i: (0, i % nt)),
                  pl.BlockSpec((RET_QK_DIM, tm), lambda i: (0, i % nt))],
        out_specs=[row(RET_QK_WIDTH), row(RET_V_WIDTH), row(RET_V_WIDTH),
                   pl.BlockSpec((RET_QK_WIDTH, tm), lambda i: (0, i))],
        out_shape=[jax.ShapeDtypeStruct((n, RET_QK_WIDTH), BF16), jax.ShapeDtypeStruct((n, RET_V_WIDTH), BF16),
                   jax.ShapeDtypeStruct((n, RET_V_WIDTH), BF16), jax.ShapeDtypeStruct((RET_QK_WIDTH, n), BF16)],
        compiler_params=_cparams(("parallel",)),
        name="odd_in_proj",
    )(h, g, wrow, wkt, cos, sin, cost, sint)


def _ret_kernel(cdec_ref, q_ref, kt_ref, v_ref, gate_ref, dintra_ref, qdec_ref, kdec_ref, gn_ref,
                y_ref, state_ref):
    @pl.when(pl.program_id(1) == 0)
    def _():
        state_ref[...] = jnp.zeros_like(state_ref)

    for h in range(RET_HEADS):
        ks = slice(h * RET_QK_DIM, (h + 1) * RET_QK_DIM)
        vs = slice(h * RET_V_DIM, (h + 1) * RET_V_DIM)
        qh = q_ref[:, ks]
        kth = kt_ref[ks, :]
        vh = v_ref[:, vs]
        state = state_ref[h]
        inner = (_dot(qh, kth) * dintra_ref[h]).astype(BF16)
        o = _dot(inner, vh) + _dot(qh, state.astype(BF16)) * qdec_ref[h]
        kd = (kth.astype(F32) * kdec_ref[h]).astype(BF16)
        state_ref[h] = state * cdec_ref[h] + _dot(kd, vh)
        mu = jnp.mean(o, axis=-1, keepdims=True)
        oc = o - mu
        var = jnp.mean(oc * oc, axis=-1, keepdims=True)
        on = oc * lax.rsqrt(var + GN_EPS) * gn_ref[:, vs]
        y_ref[:, vs] = (jax.nn.silu(gate_ref[:, vs].astype(F32)) * on).astype(BF16)


def _retention(cdec, q, kt, v, gate, dintra, qdec, kdec, gn, batch, seq):
    n = q.shape[0]
    c = RET_TILE
    nc = seq // c
    row = lambda w: pl.BlockSpec((c, w), lambda b, t: (b * nc + t, 0))
    return pl.pallas_call(
        _ret_kernel,
        grid=(batch, nc),
        in_specs=[pl.BlockSpec(memory_space=pltpu.SMEM),
                  row(RET_QK_WIDTH),
                  pl.BlockSpec((RET_QK_WIDTH, c), lambda b, t: (0, b * nc + t)),
                  row(RET_V_WIDTH), row(RET_V_WIDTH),
                  _full(dintra.shape), _full(qdec.shape), _full(kdec.shape), _full(gn.shape)],
        out_specs=row(RET_V_WIDTH),
        out_shape=jax.ShapeDtypeStruct((n, RET_V_WIDTH), BF16),
        scratch_shapes=[pltpu.VMEM((RET_HEADS, RET_QK_DIM, RET_V_DIM), F32)],
        compiler_params=_cparams(("parallel", "arbitrary")),
        name="retention",
    )(cdec, q, kt, v, gate, dintra, qdec, kdec, gn)


def _t5_bucket(rel):
    half = N_BUCKETS // 2
    max_exact = half // 2
    ret = (rel > 0).astype(I32) * half
    n = jnp.abs(rel)
    nf = jnp.maximum(n, 1).astype(F32)
    large = max_exact + (jnp.log(nf / max_exact) / math.log(MAX_DISTANCE / max_exact)
                         * (half - max_exact)).astype(I32)
    large = jnp.minimum(large, half - 1)
    return ret + jnp.where(n < max_exact, n, large)


def _bias_tables(rel_bias):
    c = jnp.arange(BIAS_BLOCK, dtype=I32)[:, None]
    r = jnp.arange(Q_TILE, dtype=I32)[None, :]
    rels = jnp.stack([c + (b - 1) * BIAS_BLOCK - r for b in range(Q_TILE // BIAS_BLOCK + 1)])
    onehot = jax.nn.one_hot(_t5_bucket(rels), N_BUCKETS, dtype=F32)
    tab = jnp.einsum("abcn,nh->ahbc", onehot, rel_bias, precision=lax.Precision.HIGHEST)
    far = rel_bias[_t5_bucket(jnp.int32(-2 * MAX_DISTANCE))]
    return (tab * LOG2E).astype(F32), (far * LOG2E).astype(F32)


def _transpose_bf16(w):
    wb = w.astype(BF16)
    eye = jnp.eye(w.shape[0], dtype=BF16)
    return lax.dot_general(wb, eye, (((0,), (0,)), ((), ())), preferred_element_type=BF16)


def _block_diag(w):
    nb, bs, _ = w.shape
    eye = jnp.eye(nb, dtype=w.dtype)
    return jnp.einsum("hij,hg->higj", w, eye).reshape(nb * bs, nb * bs)


def _router_weights(w_group, w_expert):
    wr = jnp.concatenate([w_expert, w_group, jnp.zeros((D_MODEL, LANES - N_EXPERTS - N_GROUPS), F32)], axis=1)
    hi = wr.astype(BF16)
    lo = (wr - hi.astype(F32)).astype(BF16)
    return hi, lo


def _rotary_tables(seq):
    half = RET_QK_DIM // 2
    inv = ROPE_BASE ** (-jnp.arange(half, dtype=F32) / half)
    ang = jnp.arange(seq, dtype=F32)[:, None] * inv[None, :]
    cos, sin = jnp.cos(ang), jnp.sin(ang)
    cos_row = jnp.concatenate([cos, cos], axis=1)
    sin_row = jnp.concatenate([-sin, sin], axis=1)
    cos_t = jnp.concatenate([cos.T, cos.T], axis=0)
    sin_t = jnp.concatenate([-sin.T, sin.T], axis=0)
    return cos_row, sin_row, cos_t, sin_t


def _retention_tables():
    c = RET_TILE
    log_g = jnp.log(1.0 - 2.0 ** (-5.0 - jnp.arange(RET_HEADS, dtype=F32)))
    pos = jnp.arange(c, dtype=F32)
    diff = pos[:, None] - pos[None, :]
    causal = diff >= 0
    dintra = jnp.where(causal[None], jnp.exp(jnp.where(causal, diff, 0.0)[None] * log_g[:, None, None]), 0.0)
    qdec = jnp.exp((pos + 1.0)[None, :, None] * log_g[:, None, None])
    kdec = jnp.exp((c - 1.0 - pos)[None, None, :] * log_g[:, None, None])
    cdec = jnp.exp(c * log_g)
    return dintra, qdec, kdec, cdec


def kernel(x, p, rel_bias, mix_norm_g, ffn_norm_g, ple_norm_g, ev_w_in, ev_conv_w, ev_conv_b, ev_lru_wa, ev_lru_ba, ev_lru_wx, ev_lru_bx, ev_lru_lambda, ev_q_norm_g, ev_k_norm_g, ev_w_out, od_w_in, od_gn_g, od_w_out, moe_w_group, moe_w_expert, moe_w_gate, moe_w_up, moe_w_down, ple_w_up, ple_w_gate):
    batch, seq, _ = x.shape
    n = batch * seq
    depth = p.shape[0]
    h = x.reshape(n, D_MODEL)
    row = lambda a: a.reshape(1, -1)

    tab, far = _bias_tables(rel_bias)
    cos_row, sin_row, cos_t, sin_t = _rotary_tables(seq)
    dintra, qdec, kdec, cdec = _retention_tables()
    head_ones = _block_diag(jnp.ones((ATT_HEADS, ATT_HEAD_DIM, ATT_HEAD_DIM), BF16))

    for i in range(depth):
        jdx = i // 2
        if i % 2 == 0:
            w = ev_w_in[jdx]
            widths = (LRU_WIDTH, LRU_WIDTH, ATT_WIDTH, ATT_WIDTH, ATT_WIDTH, IDX_HEADS * IDX_DIM, IDX_DIM, IDX_HEADS)
            o = [sum(widths[:t]) for t in range(len(widths) + 1)]
            xa_w, ga_w, q_w, k_w, v_w, iq_w, ik_w, iw_w = [w[:, o[t]:o[t + 1]] for t in range(8)]
            wrow = jnp.concatenate([xa_w, ga_w, k_w, ik_w, ik_w], axis=1).astype(BF16)
            wt = _transpose_bf16(jnp.concatenate(
                [q_w, iq_w, v_w, iw_w, jnp.zeros((D_MODEL, 16 - IDX_HEADS), F32)], axis=1))
            qg = ev_q_norm_g[jdx].reshape(-1, 1) * (ATT_HEAD_DIM ** -0.5 * LOG2E)
            kg = jnp.tile(ev_k_norm_g[jdx], ATT_HEADS).reshape(1, -1)
            xg, k, ik, qt, iqt, vt, iwt = _even_in(h, row(mix_norm_g[i]), wrow, wt, head_ones, qg, kg)
            ya = _rglru(xg, ev_conv_w[jdx], row(ev_conv_b[jdx]),
                        _block_diag(ev_lru_wa[jdx]).astype(BF16), row(ev_lru_ba[jdx]),
                        _block_diag(ev_lru_wx[jdx]).astype(BF16), row(ev_lru_bx[jdx]),
                        row(ev_lru_lambda[jdx]), batch, seq)
            yb = _dsa(far, qt, iqt, iwt, k, ik, vt, tab, batch, seq)
            wo = ev_w_out[jdx].astype(BF16)
            ys, ws = [ya, yb], [wo[:LRU_WIDTH], wo[LRU_WIDTH:]]
        else:
            w = od_w_in[jdx]
            wrow = jnp.concatenate([w[:, :RET_QK_WIDTH], w[:, 2 * RET_QK_WIDTH:]], axis=1).astype(BF16)
            wkt = _transpose_bf16(w[:, RET_QK_WIDTH:2 * RET_QK_WIDTH])
            q, v, gate, kt = _odd_in(h, row(mix_norm_g[i]), wrow, wkt, cos_row, sin_row, cos_t, sin_t, seq)
            yc = _retention(cdec, q, kt, v, gate, dintra, qdec, kdec, row(od_gn_g[jdx]), batch, seq)
            ys, ws = [yc], [od_w_out[jdx].astype(BF16)]
        wr_hi, wr_lo = _router_weights(moe_w_group[i], moe_w_expert[i])
        h1, xn, route = _mix_out(h, ys, ws, row(ffn_norm_g[i]), wr_hi, wr_lo)
        yp = _moe_sparse(xn, route, moe_w_gate, moe_w_up, moe_w_down, i)
        h = _ple(h1, yp, route, p.reshape(depth * n, PLE_DIM), i, ple_w_up[i].astype(BF16),
                 ple_w_gate[i].astype(BF16), row(ple_norm_g[i]))
    return h.reshape(batch, seq, D_MODEL)
```
